```python
import jax, jax.numpy as jnp
from jax import lax
import numpy as np

D_MODEL = 1024
BATCH = 8
SEQ = 4096
DEPTH = 1
DEC_BATCH = 16
DEC_SEQ = 32
PAST_LEN = 1024

CHUNK = 64
Q_BLOCK = 128
N_ATTN_HEADS = 8
ATTN_HEAD_DIM = 64
D_ATTN = N_ATTN_HEADS * ATTN_HEAD_DIM
ATTN_SCALE = ATTN_HEAD_DIM ** -0.5
D_RNN = 512
N_RNN_BLOCKS = 8
RNN_BLOCK = D_RNN // N_RNN_BLOCKS
CONV_WIDTH = 4
LRU_C = 8.0
N_KEYS = 128
N_EXPERTS = N_KEYS * N_KEYS
PEER_HEADS = 8
PEER_TOPK = 16
PEER_KEY_DIM = 256
PEER_HALF = PEER_KEY_DIM // 2
PEER_TOKEN_BLOCK = 128
DN_ALPHA = (2 * DEPTH) ** 0.25
DN_BETA = (8 * DEPTH) ** -0.25
LN_EPS = 1e-5
IN_WIDTHS = (D_ATTN, D_ATTN, D_ATTN, N_ATTN_HEADS, D_RNN, D_RNN, D_MODEL, D_MODEL)
D_IN = sum(IN_WIDTHS)
IN_SPLITS = tuple(int(s) for s in np.cumsum(IN_WIDTHS)[:-1])

kernel_name = "fox_rglru_peer_streaming_step"


def layer_norm(x, g, b):
    xf = x.astype(jnp.float32)
    mu = jnp.mean(xf, axis=-1, keepdims=True)
    var = jnp.mean(jnp.square(xf - mu), axis=-1, keepdims=True)
    return ((xf - mu) * lax.rsqrt(var + LN_EPS) * g + b).astype(x.dtype)


def fox_block(q, k, v, f_q, f_k, q_pos, k_pos):
    s = jnp.einsum("bqhd,bkhd->bhqk", q, k).astype(jnp.float32) * ATTN_SCALE
    s = s + jnp.transpose(f_q, (0, 2, 1))[..., :, None] - jnp.transpose(f_k, (0, 2, 1))[..., None, :]
    mask = k_pos[None, :] <= q_pos[:, None]
    p = jax.nn.softmax(jnp.where(mask, s, -jnp.inf), axis=-1)
    return jnp.einsum("bhqk,bkhd->bqhd", p.astype(v.dtype), v)


def fox_attention(q, k, v, f_q, f_k, q_pos, k_pos):
    t_q = q.shape[1]
    if t_q <= Q_BLOCK:
        return fox_block(q, k, v, f_q, f_k, q_pos, k_pos)
    nb = t_q // Q_BLOCK
    b, _, h, d = q.shape
    qb = q.reshape(b, nb, Q_BLOCK, h, d).swapaxes(0, 1)
    fb = f_q.reshape(b, nb, Q_BLOCK, h).swapaxes(0, 1)
    pb = q_pos.reshape(nb, Q_BLOCK)
    out = lax.map(lambda a: fox_block(a[0], k, v, a[1], f_k, a[2], k_pos), (qb, fb, pb))
    return out.swapaxes(0, 1).reshape(b, t_q, h, d)


def _combine(c1, c2):
    a1, b1 = c1
    a2, b2 = c2
    return a1 * a2, a2 * b1 + b2


def linear_recurrence(a, b, h0):
    bsz, t, c = a.shape
    nc = -(-t // CHUNK)
    pad = nc * CHUNK - t
    a = jnp.pad(a, ((0, 0), (0, pad), (0, 0)), constant_values=1.0)
    b = jnp.pad(b, ((0, 0), (0, pad), (0, 0)))
    a = a.reshape(bsz, nc, CHUNK, c).swapaxes(0, 1)
    b = b.reshape(bsz, nc, CHUNK, c).swapaxes(0, 1)

    def step(h, ab):
        a_c, b_c = ab
        b_c = b_c.at[:, 0].add(a_c[:, 0] * h)
        _, h_c = lax.associative_scan(_combine, (a_c, b_c), axis=1)
        return h_c[:, -1], h_c

    h_last, hs = lax.scan(step, h0, (a, b))
    hs = hs.swapaxes(0, 1).reshape(bsz, nc * CHUNK, c)[:, :t]
    return hs, h_last


def recurrent_branch(xr, gate, conv_hist, h0, conv_w, conv_b, w_rg_a, b_rg_a, w_rg_x, b_rg_x, lru_lambda):
    bsz, t, _ = xr.shape
    xp = jnp.concatenate([conv_hist.astype(xr.dtype), xr], axis=1)
    win = jnp.stack([xp[:, i:i + t] for i in range(CONV_WIDTH)], axis=2)
    xc = jnp.einsum("btwc,wc->btc", win, conv_w) + conv_b
    new_hist = xp[:, xp.shape[1] - (CONV_WIDTH - 1):]
    xblk = xc.reshape(bsz, t, N_RNN_BLOCKS, RNN_BLOCK)
    r = jax.nn.sigmoid((jnp.einsum("btni,nij->btnj", xblk, w_rg_a).reshape(bsz, t, D_RNN) + b_rg_a).astype(jnp.float32))
    i_g = jax.nn.sigmoid((jnp.einsum("btni,nij->btnj", xblk, w_rg_x).reshape(bsz, t, D_RNN) + b_rg_x).astype(jnp.float32))
    log_a = -LRU_C * r * jax.nn.softplus(-lru_lambda.astype(jnp.float32))
    a = jnp.exp(log_a)
    mult = jnp.sqrt(-jnp.expm1(2.0 * log_a))
    bterm = mult * i_g * xc.astype(jnp.float32)
    hs, h_last = linear_recurrence(a, bterm, h0.astype(jnp.float32))
    out = hs.astype(xr.dtype) * jax.nn.gelu(gate)
    return out, new_hist, h_last


def token_mixer(x, past_k, past_v, past_logf, conv_hist, h0, w_in, b_forget, conv_w, conv_b,
                w_rg_a, b_rg_a, w_rg_x, b_rg_x, lru_lambda, w_attn_up, w_rnn_up, w_out):
    bsz, t, _ = x.shape
    z = x @ w_in
    q, k, v, f_logit, xr, gate, g_attn, g_rnn = jnp.split(z, IN_SPLITS, axis=-1)
    q = q.reshape(bsz, t, N_ATTN_HEADS, ATTN_HEAD_DIM)
    k = k.reshape(bsz, t, N_ATTN_HEADS, ATTN_HEAD_DIM)
    v = v.reshape(bsz, t, N_ATTN_HEADS, ATTN_HEAD_DIM)
    logf = jax.nn.log_sigmoid(f_logit.astype(jnp.float32) + b_forget.astype(jnp.float32))
    if past_k is None:
        k_all, v_all, logf_all, p = k, v, logf, 0
    else:
        k_all = jnp.concatenate([past_k.astype(k.dtype), k], axis=1)
        v_all = jnp.concatenate([past_v.astype(v.dtype), v], axis=1)
        logf_all = jnp.concatenate([past_logf.astype(jnp.float32), logf], axis=1)
        p = past_k.shape[1]
    f_all = jnp.cumsum(logf_all, axis=1)
    o = fox_attention(q, k_all, v_all, f_all[:, p:], f_all, p + jnp.arange(t), jnp.arange(p + t))
    rnn_out, new_hist, h_last = recurrent_branch(xr, gate, conv_hist, h0, conv_w, conv_b,
                                                 w_rg_a, b_rg_a, w_rg_x, b_rg_x, lru_lambda)
    merged = (jax.nn.sigmoid(g_attn) * (o.reshape(bsz, t, D_ATTN) @ w_attn_up)
              + jax.nn.sigmoid(g_rnn) * (rnn_out @ w_rnn_up))
    return merged @ w_out, k, v, logf, new_hist, h_last


def peer_block(xb, w_query, keys_1, keys_2, u_tab, v_tab):
    n = xb.shape[0]
    qy = (xb @ w_query).reshape(n, PEER_HEADS, 2, PEER_HALF)
    s1 = jnp.einsum("nhd,kd->nhk", qy[:, :, 0], keys_1).astype(jnp.float32)
    s2 = jnp.einsum("nhd,kd->nhk", qy[:, :, 1], keys_2).astype(jnp.float32)
    t1, i1 = lax.top_k(s1, PEER_TOPK)
    t2, i2 = lax.top_k(s2, PEER_TOPK)
    cand = (t1[..., :, None] + t2[..., None, :]).reshape(n, PEER_HEADS, PEER_TOPK * PEER_TOPK)
    cidx = (i1[..., :, None] * N_KEYS + i2[..., None, :]).reshape(n, PEER_HEADS, PEER_TOPK * PEER_TOPK)
    top, sel = lax.top_k(cand, PEER_TOPK)
    idx = jnp.take_along_axis(cidx, sel, axis=-1)
    g = jax.nn.softmax(top, axis=-1)
    u = jnp.take(u_tab, idx, axis=0)
    act = jax.nn.gelu(jnp.einsum("nhkd,nd->nhk", u, xb).astype(jnp.float32))
    vv = jnp.take(v_tab, idx, axis=0)
    return jnp.einsum("nhk,nhkd->nd", (g * act).astype(xb.dtype), vv)


def peer_ffn(x, w_query, keys_1, keys_2, u_tab, v_tab):
    bsz, t, d = x.shape
    n = bsz * t
    nb = -(-n // PEER_TOKEN_BLOCK)
    flat = jnp.pad(x.reshape(n, d), ((0, nb * PEER_TOKEN_BLOCK - n), (0, 0)))
    out = lax.map(lambda xb: peer_block(xb, w_query, keys_1, keys_2, u_tab, v_tab),
                  flat.reshape(nb, PEER_TOKEN_BLOCK, d))
    return out.reshape(nb * PEER_TOKEN_BLOCK, d)[:n].reshape(bsz, t, d)


def trunk_layer(x, past_k, past_v, past_logf, conv_hist, h0, w_in, b_forget, conv_w, conv_b,
                w_rg_a, b_rg_a, w_rg_x, b_rg_x, lru_lambda, w_attn_up, w_rnn_up, w_out, ln1_g, ln1_b,
                peer_w_query, peer_keys_1, peer_keys_2, peer_u, peer_v, ln2_g, ln2_b):
    mix, k, v, logf, conv_state, rnn_state = token_mixer(
        x, past_k, past_v, past_logf, conv_hist, h0, w_in, b_forget, conv_w, conv_b,
        w_rg_a, b_rg_a, w_rg_x, b_rg_x, lru_lambda, w_attn_up, w_rnn_up, w_out)
    h = layer_norm(DN_ALPHA * x + mix, ln1_g, ln1_b)
    y = layer_norm(DN_ALPHA * h + peer_ffn(h, peer_w_query, peer_keys_1, peer_keys_2, peer_u, peer_v), ln2_g, ln2_b)
    return y, k, v, logf, conv_state, rnn_state


def setup_inputs(seed: int = 0) -> dict:
    key = jax.random.key(seed)
    ks = jax.random.split(key, 32)
    nrm = jax.random.normal
    f32 = jnp.float32
    a0c = jax.random.uniform(ks[12], (DEPTH, D_RNN), f32, minval=0.9, maxval=0.999)
    s = a0c ** (1.0 / LRU_C)
    return {
        "x_prompt": nrm(ks[0], (BATCH, SEQ, D_MODEL), f32),
        "x_sample": nrm(ks[1], (DEC_BATCH, DEC_SEQ, D_MODEL), f32),
        "cache_k": nrm(ks[2], (DEPTH, DEC_BATCH, PAST_LEN, N_ATTN_HEADS, ATTN_HEAD_DIM), f32),
        "cache_v": nrm(ks[3], (DEPTH, DEC_BATCH, PAST_LEN, N_ATTN_HEADS, ATTN_HEAD_DIM), f32),
        "cache_logf": jax.nn.log_sigmoid(nrm(ks[4], (DEPTH, DEC_BATCH, PAST_LEN, N_ATTN_HEADS), f32) + 3.0),
        "state_conv": nrm(ks[5], (DEPTH, DEC_BATCH, CONV_WIDTH - 1, D_RNN), f32),
        "state_rnn": 0.5 * nrm(ks[6], (DEPTH, DEC_BATCH, D_RNN), f32),
        "w_in": nrm(ks[7], (DEPTH, D_MODEL, D_IN), f32) * D_MODEL ** -0.5,
        "b_forget": 3.0 + 0.5 * nrm(ks[8], (DEPTH, N_ATTN_HEADS), f32),
        "conv_w": nrm(ks[9], (DEPTH, CONV_WIDTH, D_RNN), f32) * CONV_WIDTH ** -0.5,
        "conv_b": 0.01 * nrm(ks[10], (DEPTH, D_RNN), f32),
        "w_rg_a": nrm(ks[11], (DEPTH, N_RNN_BLOCKS, RNN_BLOCK, RNN_BLOCK), f32) * RNN_BLOCK ** -0.5,
        "b_rg_a": 0.01 * nrm(ks[13], (DEPTH, D_RNN), f32),
        "w_rg_x": nrm(ks[14], (DEPTH, N_RNN_BLOCKS, RNN_BLOCK, RNN_BLOCK), f32) * RNN_BLOCK ** -0.5,
        "b_rg_x": 0.01 * nrm(ks[15], (DEPTH, D_RNN), f32),
        "lru_lambda": jnp.log(s) - jnp.log1p(-s),
        "w_attn_up": nrm(ks[16], (DEPTH, D_ATTN, D_MODEL), f32) * (D_ATTN ** -0.5 * DN_BETA),
        "w_rnn_up": nrm(ks[17], (DEPTH, D_RNN, D_MODEL), f32) * (D_RNN ** -0.5 * DN_BETA),
        "w_out": nrm(ks[18], (DEPTH, D_MODEL, D_MODEL), f32) * (D_MODEL ** -0.5 * DN_BETA),
        "ln1_g": 1.0 + 0.02 * nrm(ks[19], (DEPTH, D_MODEL), f32),
        "ln1_b": 0.02 * nrm(ks[20], (DEPTH, D_MODEL), f32),
        "peer_w_query": nrm(ks[21], (DEPTH, D_MODEL, PEER_HEADS * PEER_KEY_DIM), f32) * D_MODEL ** -0.5,
        "peer_keys_1": nrm(ks[22], (DEPTH, N_KEYS, PEER_HALF), f32) * PEER_HALF ** -0.5,
        "peer_keys_2": nrm(ks[23], (DEPTH, N_KEYS, PEER_HALF), f32) * PEER_HALF ** -0.5,
        "peer_u": nrm(ks[24], (DEPTH, N_EXPERTS, D_MODEL), f32) * D_MODEL ** -0.5,
        "peer_v": nrm(ks[25], (DEPTH, N_EXPERTS, D_MODEL), f32) * (DN_BETA * PEER_HEADS ** -0.5),
        "ln2_g": 1.0 + 0.02 * nrm(ks[26], (DEPTH, D_MODEL), f32),
        "ln2_b": 0.02 * nrm(ks[27], (DEPTH, D_MODEL), f32),
    }


def reference(x_prompt, x_sample, cache_k, cache_v, cache_logf, state_conv, state_rnn,
              w_in, b_forget, conv_w, conv_b, w_rg_a, b_rg_a, w_rg_x, b_rg_x, lru_lambda,
              w_attn_up, w_rnn_up, w_out, ln1_g, ln1_b, peer_w_query, peer_keys_1, peer_keys_2,
              peer_u, peer_v, ln2_g, ln2_b):
    hp, hs = x_prompt, x_sample
    kp_l, vp_l, fp_l, cp_l, rp_l = [], [], [], [], []
    ks_l, vs_l, fs_l, cs_l, rs_l = [], [], [], [], []
    for l in range(DEPTH):
        lw = (w_in[l], b_forget[l], conv_w[l], conv_b[l], w_rg_a[l], b_rg_a[l], w_rg_x[l], b_rg_x[l],
              lru_lambda[l], w_attn_up[l], w_rnn_up[l], w_out[l], ln1_g[l], ln1_b[l],
              peer_w_query[l], peer_keys_1[l], peer_keys_2[l], peer_u[l], peer_v[l], ln2_g[l], ln2_b[l])
        zero_hist = jnp.zeros((hp.shape[0], CONV_WIDTH - 1, D_RNN), hp.dtype)
        zero_h = jnp.zeros((hp.shape[0], D_RNN), jnp.float32)
        hp, kp, vp, fp, cp, rp = trunk_layer(hp, None, None, None, zero_hist, zero_h, *lw)
        hs, ks_, vs_, fs_, cs_, rs_ = trunk_layer(hs, cache_k[l], cache_v[l], cache_logf[l],
                                                  state_conv[l], state_rnn[l], *lw)
        kp_l.append(kp); vp_l.append(vp); fp_l.append(fp); cp_l.append(cp); rp_l.append(rp)
        ks_l.append(ks_); vs_l.append(vs_); fs_l.append(fs_); cs_l.append(cs_); rs_l.append(rs_)
    return (hp, hs,
            jnp.stack(kp_l), jnp.stack(vp_l), jnp.stack(fp_l), jnp.stack(cp_l), jnp.stack(rp_l),
            jnp.stack(ks_l), jnp.stack(vs_l), jnp.stack(fs_l), jnp.stack(cs_l), jnp.stack(rs_l))
```

```python
import functools
import math

import jax
import jax.numpy as jnp
from jax import lax
from jax.experimental import pallas as pl
from jax.experimental.pallas import tpu as pltpu

F32 = jnp.float32
BF16 = jnp.bfloat16

D_MODEL = 1024
N_ATTN_HEADS = 8
ATTN_HEAD_DIM = 64
D_ATTN = N_ATTN_HEADS * ATTN_HEAD_DIM
ATTN_SCALE = ATTN_HEAD_DIM ** -0.5
D_RNN = 512
N_RNN_BLOCKS = 8
RNN_BLOCK = D_RNN // N_RNN_BLOCKS
CONV_WIDTH = 4
LRU_C = 8.0
N_KEYS = 128
N_EXPERTS = N_KEYS * N_KEYS
PEER_HEADS = 8
PEER_TOPK = 16
PEER_HALF = 128
LN_EPS = 1e-5

LANES = 128
SUBLANES = 8
VMEM_LIMIT_BYTES = 56 * 1024 * 1024

HEADS_PER_GROUP = 4
GROUP_LANES = HEADS_PER_GROUP * ATTN_HEAD_DIM
N_HEAD_GROUPS = N_ATTN_HEADS // HEADS_PER_GROUP
F_PAD = LANES
NEG_BIG = -1e30
NOT_SELECTED_RANK = 99.0

_C_Q = 0
_C_K = _C_Q + D_ATTN
_C_V = _C_K + D_ATTN
_C_F = _C_V + D_ATTN
_C_XR = _C_F + F_PAD
_C_GATE = _C_XR + D_RNN
_C_GA = _C_GATE + D_RNN
_C_GR = _C_GA + D_MODEL
_C_END = _C_GR + D_MODEL


def _params(*sem):
    return pltpu.CompilerParams(dimension_semantics=sem, vmem_limit_bytes=VMEM_LIMIT_BYTES)


def _sigmoid(x):
    return 1.0 / (1.0 + jnp.exp(-x))


def _gelu_tanh(x):
    return 0.5 * x * (1.0 + jnp.tanh(0.7978845608028654 * (x + 0.044715 * (x * x * x))))


def _softplus(x):
    return jnp.maximum(x, 0.0) + jnp.log1p(jnp.exp(-jnp.abs(x)))


def _layer_norm(x, g, b):
    mu = jnp.mean(x, axis=-1, keepdims=True)
    xc = x - mu
    var = jnp.mean(xc * xc, axis=-1, keepdims=True)
    return xc * lax.rsqrt(var + LN_EPS) * g + b


def _inproj_kernel(x_ref, w_ref, bf_ref, q_ref, k_ref, v_ref, kb_ref, vb_ref, lf_ref, xr_ref,
                   gg_ref, sa_ref, sr_ref):
    xb = x_ref[...].astype(BF16)

    def mm(lo, hi):
        return jnp.dot(xb, w_ref[:, lo:hi], preferred_element_type=F32)

    q_ref[...] = (mm(_C_Q, _C_K) * ATTN_SCALE).astype(BF16)
    k = mm(_C_K, _C_V)
    k_ref[...] = k
    kb_ref[...] = k.astype(BF16)
    v = mm(_C_V, _C_F)
    v_ref[...] = v
    vb_ref[...] = v.astype(BF16)
    f = mm(_C_F, _C_XR)[:, :N_ATTN_HEADS] + bf_ref[...]
    lf_ref[...] = -_softplus(-f)
    xr_ref[...] = mm(_C_XR, _C_GATE)
    gg_ref[...] = _gelu_tanh(mm(_C_GATE, _C_GA)).astype(BF16)
    sa_ref[...] = _sigmoid(mm(_C_GA, _C_GR)).astype(BF16)
    sr_ref[...] = _sigmoid(mm(_C_GR, _C_END)).astype(BF16)


def _inproj(x2, w_in_p, b_forget, tm):
    n = x2.shape[0]
    row = lambda w: pl.BlockSpec((tm, w), lambda i: (i, 0))
    full = lambda a: pl.BlockSpec(a.shape, lambda i: (0,) * a.ndim)
    out_shape = (
        jax.ShapeDtypeStruct((n, D_ATTN), BF16),
        jax.ShapeDtypeStruct((n, D_ATTN), F32),
        jax.ShapeDtypeStruct((n, D_ATTN), F32),
        jax.ShapeDtypeStruct((n, D_ATTN), BF16),
        jax.ShapeDtypeStruct((n, D_ATTN), BF16),
        jax.ShapeDtypeStruct((n, N_ATTN_HEADS), F32),
        jax.ShapeDtypeStruct((n, D_RNN), F32),
        jax.ShapeDtypeStruct((n, D_RNN), BF16),
        jax.ShapeDtypeStruct((n, D_MODEL), BF16),
        jax.ShapeDtypeStruct((n, D_MODEL), BF16),
    )
    out_specs = (row(D_ATTN), row(D_ATTN), row(D_ATTN), row(D_ATTN), row(D_ATTN), row(N_ATTN_HEADS),
                 row(D_RNN), row(D_RNN), row(D_MODEL), row(D_MODEL))
    return pl.pallas_call(
        _inproj_kernel,
        grid=(n // tm,),
        in_specs=[row(D_MODEL), full(w_in_p), full(b_forget)],
        out_specs=out_specs,
        out_shape=out_shape,
        compiler_params=_params("parallel"),
        name="inproj",
    )(x2, w_in_p, b_forget)


def _cumsum_kernel(x_ref, o_ref):
    x = x_ref[0]
    t = x.shape[1]
    lane = lax.broadcasted_iota(jnp.int32, x.shape, 1)
    d = 1
    while d < t:
        x = x + jnp.where(lane >= d, pltpu.roll(x, d, axis=1), 0.0)
        d *= 2
    o_ref[0] = x


def _cumsum_time(lf_t):
    b, h, t = lf_t.shape
    spec = pl.BlockSpec((1, h, t), lambda i: (i, 0, 0))
    return pl.pallas_call(
        _cumsum_kernel, grid=(b,), in_specs=[spec], out_specs=spec,
        out_shape=jax.ShapeDtypeStruct(lf_t.shape, F32),
        compiler_params=_params("parallel"), name="logf_cumsum",
    )(lf_t)


def _attn_kernel(q_ref, k_ref, v_ref, fq_ref, fk_ref, o_ref, *, tq, tk, q_off, n_kv):
    qi = pl.program_id(2)
    q = q_ref[0]
    lane = lax.broadcasted_iota(jnp.int32, (tq, GROUP_LANES), 1)
    row0 = q_off + qi * tq
    n_full = (row0 + 1) // tk
    n_all = jnp.minimum((row0 + tq + tk - 1) // tk, n_kv)
    q_pos = row0 + lax.broadcasted_iota(jnp.int32, (tq, tk), 0)
    k_iota = lax.broadcasted_iota(jnp.int32, (tq, tk), 1)
    out = jnp.zeros((tq, GROUP_LANES), F32)
    for j in range(HEADS_PER_GROUP):
        in_head = (lane >= j * ATTN_HEAD_DIM) & (lane < (j + 1) * ATTN_HEAD_DIM)
        qj = jnp.where(in_head, q, jnp.zeros_like(q))
        fq = fq_ref[0, 0][:, j:j + 1]

        def step(c, carry, masked):
            m, l, acc = carry
            start = pl.multiple_of(c * tk, tk)
            ks = k_ref[0, pl.ds(start, tk), :]
            vs = v_ref[0, pl.ds(start, tk), :]
            s = lax.dot_general(qj, ks, (((1,), (1,)), ((), ())), preferred_element_type=F32)
            fk = fk_ref[0, 0, c][j:j + 1, :]
            s = s + (fq - fk)
            if masked:
                s = jnp.where(c * tk + k_iota <= q_pos, s, NEG_BIG)
            m_new = jnp.maximum(m, jnp.max(s, axis=1, keepdims=True))
            alpha = jnp.exp(m - m_new)
            p = jnp.exp(s - m_new)
            l = alpha * l + jnp.sum(p, axis=1, keepdims=True)
            acc = alpha * acc + jnp.dot(p.astype(BF16), vs, preferred_element_type=F32)
            return m_new, l, acc

        init = (jnp.full((tq, 1), NEG_BIG, F32), jnp.zeros((tq, 1), F32),
                jnp.zeros((tq, GROUP_LANES), F32))
        carry = lax.fori_loop(0, n_full, functools.partial(step, masked=False), init)
        _, l, acc = lax.fori_loop(n_full, n_all, functools.partial(step, masked=True), carry)
        out = jnp.where(in_head, acc / l, out)
    o_ref[0] = out.astype(BF16)


def _attention(q, k_all, v_all, fq, fk, *, tq, tk, q_off):
    b, t, _ = q.shape
    t_k = k_all.shape[1]
    n_kv = t_k // tk
    kern = functools.partial(_attn_kernel, tq=tq, tk=tk, q_off=q_off, n_kv=n_kv)
    return pl.pallas_call(
        kern,
        grid=(b, N_HEAD_GROUPS, t // tq),
        in_specs=[
            pl.BlockSpec((1, tq, GROUP_LANES), lambda bi, g, i: (bi, i, g)),
            pl.BlockSpec((1, t_k, GROUP_LANES), lambda bi, g, i: (bi, 0, g)),
            pl.BlockSpec((1, t_k, GROUP_LANES), lambda bi, g, i: (bi, 0, g)),
            pl.BlockSpec((1, 1, tq, HEADS_PER_GROUP), lambda bi, g, i: (bi, g, i, 0)),
            pl.BlockSpec((1, 1, n_kv, HEADS_PER_GROUP, tk), lambda bi, g, i: (bi, g, 0, 0, 0)),
        ],
        out_specs=pl.BlockSpec((1, tq, GROUP_LANES), lambda bi, g, i: (bi, i, g)),
        out_shape=jax.ShapeDtypeStruct((b, t, D_ATTN), BF16),
        compiler_params=_params("parallel", "parallel", "arbitrary"),
        name="fox_attention",
    )(q, k_all, v_all, fq, fk)


def _rnn_kernel(xr_ref, gg_ref, hist_ref, h0_ref, cw_ref, cb_ref, wa_ref, ba_ref, wx_ref, bx_ref,
                lam_ref, out_ref, hl_ref, prev_sc, h_sc, *, tb):
    t = pl.program_id(1)

    @pl.when(t == 0)
    def _():
        prev_sc[...] = hist_ref[0]
        h_sc[...] = h0_ref[0]

    x = xr_ref[0]
    prev = prev_sc[...]
    row8 = lax.broadcasted_iota(jnp.int32, (SUBLANES, D_RNN), 0)
    xc = x * cw_ref[CONV_WIDTH - 1:CONV_WIDTH, :] + cb_ref[...]
    for s in range(1, CONV_WIDTH):
        xs = pltpu.roll(x, s, axis=0)
        head = jnp.where(row8 < s, pltpu.roll(prev, s, axis=0), xs[:SUBLANES])
        xs = jnp.concatenate([head, xs[SUBLANES:]], axis=0)
        xc = xc + xs * cw_ref[CONV_WIDTH - 1 - s:CONV_WIDTH - s, :]
    prev_sc[...] = x[tb - SUBLANES:tb]

    xcb = xc.astype(BF16)
    r = _sigmoid(jnp.dot(xcb, wa_ref[...], preferred_element_type=F32) + ba_ref[...])
    ig = _sigmoid(jnp.dot(xcb, wx_ref[...], preferred_element_type=F32) + bx_ref[...])
    log_a = (-LRU_C) * r * _softplus(-lam_ref[...])
    a = jnp.exp(log_a)
    bterm = jnp.sqrt(1.0 - a * a) * ig * xc

    row = lax.broadcasted_iota(jnp.int32, (tb, D_RNN), 0)
    d = 1
    while d < tb:
        valid = row >= d
        a_s = pltpu.roll(a, d, axis=0)
        b_s = pltpu.roll(bterm, d, axis=0)
        bterm = jnp.where(valid, a * b_s + bterm, bterm)
        a = jnp.where(valid, a * a_s, a)
        d *= 2
    h = bterm + a * h_sc[...]
    h_last = h[tb - 1:tb]
    h_sc[...] = h_last
    hl_ref[0] = h_last
    out_ref[0] = (h * gg_ref[0].astype(F32)).astype(BF16)


def _rnn(xr, gg, hist8, h0, conv_w, conv_b, wa, ba, wx, bx, lam, tb):
    b, t, _ = xr.shape
    blk = pl.BlockSpec((1, tb, D_RNN), lambda bi, ti: (bi, ti, 0))
    full = lambda a: pl.BlockSpec(a.shape, lambda bi, ti: (0,) * a.ndim)
    per_b = lambda r: pl.BlockSpec((1, r, D_RNN), lambda bi, ti: (bi, 0, 0))
    return pl.pallas_call(
        functools.partial(_rnn_kernel, tb=tb),
        grid=(b, t // tb),
        in_specs=[blk, blk, per_b(SUBLANES), per_b(1), full(conv_w), full(conv_b), full(wa), full(ba),
                  full(wx), full(bx), full(lam)],
        out_specs=(blk, per_b(1)),
        out_shape=(jax.ShapeDtypeStruct((b, t, D_RNN), BF16), jax.ShapeDtypeStruct((b, 1, D_RNN), F32)),
        scratch_shapes=[pltpu.VMEM((SUBLANES, D_RNN), F32), pltpu.VMEM((1, D_RNN), F32)],
        compiler_params=_params("parallel", "arbitrary"),
        name="conv_rglru",
    )(xr, gg, hist8, h0, conv_w, conv_b, wa, ba, wx, bx, lam)


def _outproj_kernel(o_ref, r_ref, sa_ref, sr_ref, x_ref, wau_ref, wru_ref, wo_ref, g_ref, b_ref,
                    h_ref, *, alpha):
    up_a = jnp.dot(o_ref[...], wau_ref[...], preferred_element_type=F32)
    up_r = jnp.dot(r_ref[...], wru_ref[...], preferred_element_type=F32)
    merged = sa_ref[...].astype(F32) * up_a + sr_ref[...].astype(F32) * up_r
    mix = jnp.dot(merged.astype(BF16), wo_ref[...], preferred_element_type=F32)
    h_ref[...] = _layer_norm(alpha * x_ref[...] + mix, g_ref[...], b_ref[...])


def _outproj(o2, r2, sa, sr, x2, wau, wru, wo, g, bta, alpha, tm):
    n = x2.shape[0]
    row = lambda w: pl.BlockSpec((tm, w), lambda i: (i, 0))
    full = lambda a: pl.BlockSpec(a.shape, lambda i: (0,) * a.ndim)
    return pl.pallas_call(
        functools.partial(_outproj_kernel, alpha=alpha),
        grid=(n // tm,),
        in_specs=[row(D_ATTN), row(D_RNN), row(D_MODEL), row(D_MODEL), row(D_MODEL), full(wau), full(wru),
                  full(wo), full(g), full(bta)],
        out_specs=row(D_MODEL),
        out_shape=jax.ShapeDtypeStruct((n, D_MODEL), F32),
        compiler_params=_params("parallel"),
        name="outproj_ln1",
    )(o2, r2, sa, sr, x2, wau, wru, wo, g, bta)


def _route_kernel(h_ref, wq_ref, k1_ref, k2_ref, ht_ref, cnt_ref, e1_ref, r2_ref, e2_ref,
                  t1_sc, t2_sc, *, tm):
    ht = jnp.transpose(h_ref[...]).astype(BF16)
    ht_ref[...] = ht
    qt = jnp.dot(wq_ref[...], ht, preferred_element_type=F32)
    key_iota = lax.broadcasted_iota(jnp.int32, (N_KEYS, tm), 0).astype(F32)
    top_iota = lax.broadcasted_iota(jnp.int32, (PEER_TOPK, tm), 0).astype(F32)

    def top_k_ranks(s, t_sc):
        def body(a, carry):
            v, rank = carry
            m = jnp.max(v, axis=0, keepdims=True)
            idx = jnp.min(jnp.where(v == m, key_iota, float(N_KEYS)), axis=0, keepdims=True)
            sel = key_iota == idx
            t_sc[pl.ds(a, 1), :] = m
            return jnp.where(sel, -jnp.inf, v), jnp.where(sel, a.astype(F32), rank)

        _, rank = lax.fori_loop(0, PEER_TOPK, body, (s, jnp.full(s.shape, NOT_SELECTED_RANK, F32)))
        return rank

    for hd in range(PEER_HEADS):
        base = hd * 2 * PEER_HALF
        q1 = qt[base:base + PEER_HALF].astype(BF16)
        q2 = qt[base + PEER_HALF:base + 2 * PEER_HALF].astype(BF16)
        s1 = jnp.dot(k1_ref[...], q1, preferred_element_type=F32)
        s2 = jnp.dot(k2_ref[...], q2, preferred_element_type=F32)
        rank1 = top_k_ranks(s1, t1_sc)
        rank2 = top_k_ranks(s2, t2_sc)
        t1 = t1_sc[...]
        t2 = t2_sc[...]
        top0 = t1[0:1] + t2[0:1]

        def pick(_, carry):
            ptr, front, z = carry
            m = jnp.max(front, axis=0, keepdims=True)
            a_idx = jnp.min(jnp.where(front == m, top_iota, float(PEER_TOPK)), axis=0, keepdims=True)
            ptr = ptr + jnp.where(top_iota == a_idx, 1.0, 0.0)
            nxt = jnp.full(ptr.shape, -jnp.inf, F32)
            for b in range(PEER_TOPK):
                nxt = jnp.where(ptr == float(b), t2_sc[b:b + 1, :], nxt)
            return ptr, t1 + nxt, z + jnp.exp(m - top0)

        cnt, _, z = lax.fori_loop(
            0, PEER_TOPK, pick,
            (jnp.zeros((PEER_TOPK, tm), F32), t1 + t2[0:1], jnp.zeros((1, tm), F32)))

        cnt1 = jnp.zeros((N_KEYS, tm), F32)
        for a in range(PEER_TOPK):
            cnt1 = jnp.where(rank1 == float(a), cnt[a:a + 1], cnt1)
        cnt_ref[hd] = cnt1
        e1_ref[hd] = jnp.exp(s1 - t1[0:1]) / z
        r2_ref[hd] = rank2.astype(BF16)
        e2_ref[hd] = jnp.exp(s2 - t2[0:1]).astype(BF16)


def _route(h2, wq_t, k1, k2, tm):
    n = h2.shape[0]
    full = lambda a: pl.BlockSpec(a.shape, lambda i: (0,) * a.ndim)
    per_head = pl.BlockSpec((PEER_HEADS, N_KEYS, tm), lambda i: (0, 0, i))
    hshape = lambda dt: jax.ShapeDtypeStruct((PEER_HEADS, N_KEYS, n), dt)
    return pl.pallas_call(
        functools.partial(_route_kernel, tm=tm),
        grid=(n // tm,),
        in_specs=[pl.BlockSpec((tm, D_MODEL), lambda i: (i, 0)), full(wq_t), full(k1), full(k2)],
        out_specs=(pl.BlockSpec((D_MODEL, tm), lambda i: (0, i)), per_head, per_head, per_head, per_head),
        out_shape=(jax.ShapeDtypeStruct((D_MODEL, n), BF16), hshape(F32), hshape(F32), hshape(BF16),
                   hshape(BF16)),
        scratch_shapes=[pltpu.VMEM((PEER_TOPK, tm), F32), pltpu.VMEM((PEER_TOPK, tm), F32)],
        compiler_params=_params("parallel"),
        name="peer_route",
    )(h2, wq_t, k1, k2)


ROWS_PER_STEP = 8
EXPERTS_PER_STEP = ROWS_PER_STEP * N_KEYS


def _dense_kernel(ht_ref, u_ref, vt_ref, cnt_ref, e1_ref, r2_ref, e2_ref, h_ref, g_ref, b_ref,
                  y_ref, acc_sc, coef_sc, *, alpha):
    e = pl.program_id(1)

    @pl.when(e == 0)
    def _():
        acc_sc[...] = jnp.zeros_like(acc_sc)

    act = jnp.dot(u_ref[...], ht_ref[...], preferred_element_type=F32)
    for il in range(ROWS_PER_STEP):
        w = None
        for hd in range(PEER_HEADS):
            cnt = cnt_ref[hd, il:il + 1, :].astype(BF16)
            e1 = e1_ref[hd, il:il + 1, :].astype(BF16)
            term = jnp.where(r2_ref[hd] < cnt, e2_ref[hd] * e1, jnp.zeros((), BF16))
            w = term if w is None else w + term
        rows = slice(il * N_KEYS, (il + 1) * N_KEYS)
        coef_sc[rows, :] = (w.astype(F32) * _gelu_tanh(act[rows, :])).astype(BF16)
    acc_sc[...] += jnp.dot(vt_ref[...], coef_sc[...], preferred_element_type=F32)

    @pl.when(e == pl.num_programs(1) - 1)
    def _():
        peer = jnp.transpose(acc_sc[...])
        y_ref[...] = _layer_norm(alpha * h_ref[...] + peer, g_ref[...], b_ref[...])


def _dense(ht, u_b, vt_b, cnt1, e1n, rank2, e2, h2, g, bta, alpha, tm):
    n = h2.shape[0]
    n_e = N_EXPERTS // EXPERTS_PER_STEP
    full = lambda a: pl.BlockSpec(a.shape, lambda t, e: (0,) * a.ndim)
    rows_blk = pl.BlockSpec((PEER_HEADS, ROWS_PER_STEP, tm), lambda t, e: (0, e, t))
    cols_blk = pl.BlockSpec((PEER_HEADS, N_KEYS, tm), lambda t, e: (0, 0, t))
    return pl.pallas_call(
        functools.partial(_dense_kernel, alpha=alpha),
        grid=(n // tm, n_e),
        in_specs=[
            pl.BlockSpec((D_MODEL, tm), lambda t, e: (0, t)),
            pl.BlockSpec((EXPERTS_PER_STEP, D_MODEL), lambda t, e: (e, 0)),
            pl.BlockSpec((D_MODEL, EXPERTS_PER_STEP), lambda t, e: (0, e)),
            rows_blk, rows_blk, cols_blk, cols_blk,
            pl.BlockSpec((tm, D_MODEL), lambda t, e: (t, 0)),
            full(g), full(bta),
        ],
        out_specs=pl.BlockSpec((tm, D_MODEL), lambda t, e: (t, 0)),
        out_shape=jax.ShapeDtypeStruct((n, D_MODEL), F32),
        scratch_shapes=[pltpu.VMEM((D_MODEL, tm), F32), pltpu.VMEM((EXPERTS_PER_STEP, tm), BF16)],
        compiler_params=_params("parallel", "arbitrary"),
        name="peer_dense",
    )(ht, u_b, vt_b, cnt1, e1n, rank2, e2, h2, g, bta)


def _block_diag(w):
    nb, bi, bo = w.shape
    eye = jnp.eye(nb, dtype=w.dtype)
    return (eye[:, None, :, None] * w[:, :, None, :]).reshape(nb * bi, nb * bo)


def _prep_weights(w_in, b_forget, conv_w, conv_b, w_rg_a, b_rg_a, w_rg_x, b_rg_x, lru_lambda,
                  w_attn_up, w_rnn_up, w_out, ln1_g, ln1_b, peer_w_query, peer_keys_1, peer_keys_2,
                  peer_u, peer_v, ln2_g, ln2_b):
    c_f = 3 * D_ATTN
    w_in_p = jnp.concatenate(
        [w_in[:, :c_f], jnp.pad(w_in[:, c_f:c_f + N_ATTN_HEADS], ((0, 0), (0, F_PAD - N_ATTN_HEADS))),
         w_in[:, c_f + N_ATTN_HEADS:]], axis=1).astype(BF16)
    row = lambda a: a.reshape(1, -1).astype(F32)
    return dict(
        w_in=w_in_p, b_forget=row(b_forget), conv_w=conv_w.astype(F32), conv_b=row(conv_b),
        wa=_block_diag(w_rg_a).astype(BF16), ba=row(b_rg_a), wx=_block_diag(w_rg_x).astype(BF16),
        bx=row(b_rg_x), lam=row(lru_lambda),
        wau=w_attn_up.astype(BF16), wru=w_rnn_up.astype(BF16), wo=w_out.astype(BF16),
        ln1_g=row(ln1_g), ln1_b=row(ln1_b),
        wq_t=jnp.transpose(peer_w_query).astype(BF16), k1=peer_keys_1.astype(BF16),
        k2=peer_keys_2.astype(BF16), u=peer_u.astype(BF16), vt=jnp.transpose(peer_v).astype(BF16),
        ln2_g=row(ln2_g), ln2_b=row(ln2_b),
    )


def _pick_block(n, target):
    blk = min(n, target)
    assert n % blk == 0, (n, blk)
    return blk


def _trunk_layer(x, past_k, past_v, past_logf, conv_hist, h0, p, alpha):
    bsz, t, _ = x.shape
    n = bsz * t
    x2 = x.reshape(n, D_MODEL)
    q, k, v, kb, vb, lf, xr, gg, sa, sr = _inproj(x2, p["w_in"], p["b_forget"], _pick_block(n, 256))

    tq = _pick_block(t, 256)
    tk = 512 if past_k is None else LANES
    n_past = 0 if past_k is None else past_k.shape[1]
    t_all = n_past + t
    t_pad = -(-t_all // tk) * tk
    lf3 = lf.reshape(bsz, t, N_ATTN_HEADS)
    kb3 = kb.reshape(bsz, t, D_ATTN)
    vb3 = vb.reshape(bsz, t, D_ATTN)
    if past_k is not None:
        lf3 = jnp.concatenate([past_logf.astype(F32), lf3], axis=1)
        kb3 = jnp.concatenate([past_k.reshape(bsz, n_past, D_ATTN).astype(BF16), kb3], axis=1)
        vb3 = jnp.concatenate([past_v.reshape(bsz, n_past, D_ATTN).astype(BF16), vb3], axis=1)
    pad = ((0, 0), (0, t_pad - t_all), (0, 0))
    lf3, kb3, vb3 = jnp.pad(lf3, pad), jnp.pad(kb3, pad), jnp.pad(vb3, pad)
    f_t = _cumsum_time(jnp.transpose(lf3, (0, 2, 1)))
    f_g = f_t.reshape(bsz, N_HEAD_GROUPS, HEADS_PER_GROUP, t_pad)
    fq = jnp.transpose(f_g[:, :, :, n_past:n_past + t], (0, 1, 3, 2))
    fk = jnp.transpose(f_g.reshape(bsz, N_HEAD_GROUPS, HEADS_PER_GROUP, t_pad // tk, tk), (0, 1, 3, 2, 4))
    o = _attention(q.reshape(bsz, t, D_ATTN), kb3, vb3, fq, fk, tq=tq, tk=tk, q_off=n_past)

    hist8 = jnp.pad(conv_hist.astype(F32), ((0, 0), (SUBLANES - (CONV_WIDTH - 1), 0), (0, 0)))
    xr3 = xr.reshape(bsz, t, D_RNN)
    rnn_out, h_last = _rnn(xr3, gg.reshape(bsz, t, D_RNN), hist8, h0.astype(F32).reshape(bsz, 1, D_RNN),
                           p["conv_w"], p["conv_b"], p["wa"], p["ba"], p["wx"], p["bx"], p["lam"],
                           _pick_block(t, 256))
    new_hist = jnp.concatenate([conv_hist.astype(F32), xr3], axis=1)[:, -(CONV_WIDTH - 1):]

    h = _outproj(o.reshape(n, D_ATTN), rnn_out.reshape(n, D_RNN), sa, sr, x2, p["wau"], p["wru"], p["wo"],
                 p["ln1_g"], p["ln1_b"], alpha, _pick_block(n, 256))
    ht, cnt1, e1n, rank2, e2 = _route(h, p["wq_t"], p["k1"], p["k2"], LANES)
    y = _dense(ht, p["u"], p["vt"], cnt1, e1n, rank2, e2, h, p["ln2_g"], p["ln2_b"], alpha,
               _pick_block(n, 512))
    return (y.reshape(bsz, t, D_MODEL), k.reshape(bsz, t, N_ATTN_HEADS, ATTN_HEAD_DIM),
            v.reshape(bsz, t, N_ATTN_HEADS, ATTN_HEAD_DIM), lf.reshape(bsz, t, N_ATTN_HEADS), new_hist,
            h_last.reshape(bsz, D_RNN))


def kernel(x_prompt, x_sample, cache_k, cache_v, cache_logf, state_conv, state_rnn, w_in, b_forget, conv_w, conv_b, w_rg_a, b_rg_a, w_rg_x, b_rg_x, lru_lambda, w_attn_up, w_rnn_up, w_out, ln1_g, ln1_b, peer_w_query, peer_keys_1, peer_keys_2, peer_u, peer_v, ln2_g, ln2_b):
    depth = w_in.shape[0]
    alpha = (2 * depth) ** 0.25
    layer_weights = (w_in, b_forget, conv_w, conv_b, w_rg_a, b_rg_a, w_rg_x, b_rg_x, lru_lambda, w_attn_up,
                     w_rnn_up, w_out, ln1_g, ln1_b, peer_w_query, peer_keys_1, peer_keys_2, peer_u, peer_v,
                     ln2_g, ln2_b)
    hp, hs = x_prompt, x_sample
    prompt_state, sample_state = [], []
    for l in range(depth):
        p = _prep_weights(*(w[l] for w in layer_weights))
        zero_hist = jnp.zeros((hp.shape[0], CONV_WIDTH - 1, D_RNN), F32)
        zero_h = jnp.zeros((hp.shape[0], D_RNN), F32)
        hp, *st_p = _trunk_layer(hp, None, None, None, zero_hist, zero_h, p, alpha)
        hs, *st_s = _trunk_layer(hs, cache_k[l], cache_v[l], cache_logf[l], state_conv[l], state_rnn[l], p,
                                 alpha)
        prompt_state.append(st_p)
        sample_state.append(st_s)
    stack = lambda states, i: jnp.stack([s[i] for s in states])
    return (hp, hs) + tuple(stack(prompt_state, i) for i in range(5)) + tuple(
        stack(sample_state, i) for i in range(5))
```

```python
import functools
import math

import jax
import jax.numpy as jnp
from jax import lax
from jax.experimental import pallas as pl
from jax.experimental.pallas import tpu as pltpu

F32 = jnp.float32
BF16 = jnp.bfloat16

D_MODEL = 1024
N_ATTN_HEADS = 8
ATTN_HEAD_DIM = 64
D_ATTN = N_ATTN_HEADS * ATTN_HEAD_DIM
ATTN_SCALE = ATTN_HEAD_DIM ** -0.5
D_RNN = 512
N_RNN_BLOCKS = 8
RNN_BLOCK = D_RNN // N_RNN_BLOCKS
CONV_WIDTH = 4
LRU_C = 8.0
N_KEYS = 128
N_EXPERTS = N_KEYS * N_KEYS
PEER_HEADS = 8
PEER_TOPK = 16
PEER_HALF = 128
LN_EPS = 1e-5

LANES = 128
SUBLANES = 8
BF16_ROWS = 2 * SUBLANES
VMEM_LIMIT_BYTES = 56 * 1024 * 1024

HEADS_PER_GROUP = 4
GROUP_LANES = HEADS_PER_GROUP * ATTN_HEAD_DIM
N_HEAD_GROUPS = N_ATTN_HEADS // HEADS_PER_GROUP
F_PAD = LANES
NEG_BIG = -1e30
NOT_SELECTED_RANK = 99.0

_C_Q = 0
_C_K = _C_Q + D_ATTN
_C_V = _C_K + D_ATTN
_C_F = _C_V + D_ATTN
_C_XR = _C_F + F_PAD
_C_GATE = _C_XR + D_RNN
_C_GA = _C_GATE + D_RNN
_C_GR = _C_GA + D_MODEL
_C_END = _C_GR + D_MODEL


def _params(*sem):
    return pltpu.CompilerParams(dimension_semantics=sem, vmem_limit_bytes=VMEM_LIMIT_BYTES)


def _sigmoid(x):
    return 1.0 / (1.0 + jnp.exp(-x))


def _gelu_tanh(x):
    half = 0.5 * x
    return half + half * jnp.tanh(x * (0.7978845608028654 + 0.035677408136300125 * (x * x)))


def _softplus(x):
    return jnp.maximum(x, 0.0) + jnp.log1p(jnp.exp(-jnp.abs(x)))


def _layer_norm(x, g, b):
    mu = jnp.mean(x, axis=-1, keepdims=True)
    xc = x - mu
    var = jnp.mean(xc * xc, axis=-1, keepdims=True)
    return xc * lax.rsqrt(var + LN_EPS) * g + b


def _inproj_kernel(x_ref, w_ref, bf_ref, q_ref, k_ref, v_ref, kb_ref, vb_ref, lf_ref, xr_ref,
                   gg_ref, sa_ref, sr_ref):
    xb = x_ref[...].astype(BF16)

    def mm(lo, hi):
        return jnp.dot(xb, w_ref[:, lo:hi], preferred_element_type=F32)

    q_ref[...] = (mm(_C_Q, _C_K) * ATTN_SCALE).astype(BF16)
    k = mm(_C_K, _C_V)
    k_ref[...] = k
    kb_ref[...] = k.astype(BF16)
    v = mm(_C_V, _C_F)
    v_ref[...] = v
    vb_ref[...] = v.astype(BF16)
    f = mm(_C_F, _C_XR)[:, :N_ATTN_HEADS] + bf_ref[...]
    lf_ref[...] = -_softplus(-f)
    xr_ref[...] = mm(_C_XR, _C_GATE)
    gg_ref[...] = _gelu_tanh(mm(_C_GATE, _C_GA)).astype(BF16)
    sa_ref[...] = _sigmoid(mm(_C_GA, _C_GR)).astype(BF16)
    sr_ref[...] = _sigmoid(mm(_C_GR, _C_END)).astype(BF16)


def _inproj(x2, w_in_p, b_forget, tm):
    n = x2.shape[0]
    row = lambda w: pl.BlockSpec((tm, w), lambda i: (i, 0))
    full = lambda a: pl.BlockSpec(a.shape, lambda i: (0,) * a.ndim)
    out_shape = (
        jax.ShapeDtypeStruct((n, D_ATTN), BF16),
        jax.ShapeDtypeStruct((n, D_ATTN), F32),
        jax.ShapeDtypeStruct((n, D_ATTN), F32),
        jax.ShapeDtypeStruct((n, D_ATTN), BF16),
        jax.ShapeDtypeStruct((n, D_ATTN), BF16),
        jax.ShapeDtypeStruct((n, N_ATTN_HEADS), F32),
        jax.ShapeDtypeStruct((n, D_RNN), F32),
        jax.ShapeDtypeStruct((n, D_RNN), BF16),
        jax.ShapeDtypeStruct((n, D_MODEL), BF16),
        jax.ShapeDtypeStruct((n, D_MODEL), BF16),
    )
    out_specs = (row(D_ATTN), row(D_ATTN), row(D_ATTN), row(D_ATTN), row(D_ATTN), row(N_ATTN_HEADS),
                 row(D_RNN), row(D_RNN), row(D_MODEL), row(D_MODEL))
    return pl.pallas_call(
        _inproj_kernel,
        grid=(n // tm,),
        in_specs=[row(D_MODEL), full(w_in_p), full(b_forget)],
        out_specs=out_specs,
        out_shape=out_shape,
        compiler_params=_params("parallel"),
        name="inproj",
    )(x2, w_in_p, b_forget)


def _cumsum_kernel(x_ref, o_ref):
    x = x_ref[0]
    t = x.shape[1]
    lane = lax.broadcasted_iota(jnp.int32, x.shape, 1)
    d = 1
    while d < t:
        x = x + jnp.where(lane >= d, pltpu.roll(x, d, axis=1), 0.0)
        d *= 2
    o_ref[0] = x


def _cumsum_time(lf_t):
    b, h, t = lf_t.shape
    spec = pl.BlockSpec((1, h, t), lambda i: (i, 0, 0))
    return pl.pallas_call(
        _cumsum_kernel, grid=(b,), in_specs=[spec], out_specs=spec,
        out_shape=jax.ShapeDtypeStruct(lf_t.shape, F32),
        compiler_params=_params("parallel"), name="logf_cumsum",
    )(lf_t)


def _attn_kernel(q_ref, k_ref, v_ref, fq_ref, fk_ref, o_ref, *, tq, tk, q_off, n_kv):
    qi = pl.program_id(2)
    q = q_ref[0]
    lane = lax.broadcasted_iota(jnp.int32, (tq, GROUP_LANES), 1)
    row0 = q_off + qi * tq
    n_full = (row0 + 1) // tk
    n_all = jnp.minimum((row0 + tq + tk - 1) // tk, n_kv)
    q_pos = row0 + lax.broadcasted_iota(jnp.int32, (tq, tk), 0)
    k_iota = lax.broadcasted_iota(jnp.int32, (tq, tk), 1)
    out = jnp.zeros((tq, GROUP_LANES), F32)
    for j in range(HEADS_PER_GROUP):
        in_head = (lane >= j * ATTN_HEAD_DIM) & (lane < (j + 1) * ATTN_HEAD_DIM)
        qj = jnp.where(in_head, q, jnp.zeros_like(q))
        fq = fq_ref[0, 0][:, j:j + 1]

        def step(c, carry, masked):
            m, l, acc = carry
            start = pl.multiple_of(c * tk, tk)
            ks = k_ref[0, pl.ds(start, tk), :]
            vs = v_ref[0, pl.ds(start, tk), :]
            s = lax.dot_general(qj, ks, (((1,), (1,)), ((), ())), preferred_element_type=F32)
            fk = fk_ref[0, 0, c][j:j + 1, :]
            s = s + (fq - fk)
            if masked:
                s = jnp.where(c * tk + k_iota <= q_pos, s, NEG_BIG)
            m_new = jnp.maximum(m, jnp.max(s, axis=1, keepdims=True))
            alpha = jnp.exp(m - m_new)
            p = jnp.exp(s - m_new)
            l = alpha * l + jnp.sum(p, axis=1, keepdims=True)
            acc = alpha * acc + jnp.dot(p.astype(BF16), vs, preferred_element_type=F32)
            return m_new, l, acc

        init = (jnp.full((tq, 1), NEG_BIG, F32), jnp.zeros((tq, 1), F32),
                jnp.zeros((tq, GROUP_LANES), F32))
        carry = lax.fori_loop(0, n_full, functools.partial(step, masked=False), init)
        _, l, acc = lax.fori_loop(n_full, n_all, functools.partial(step, masked=True), carry)
        out = jnp.where(in_head, acc / l, out)
    o_ref[0] = out.astype(BF16)


def _attention(q, k_all, v_all, fq, fk, *, tq, tk, q_off):
    b, t, _ = q.shape
    t_k = k_all.shape[1]
    n_kv = t_k // tk
    kern = functools.partial(_attn_kernel, tq=tq, tk=tk, q_off=q_off, n_kv=n_kv)
    return pl.pallas_call(
        kern,
        grid=(b, N_HEAD_GROUPS, t // tq),
        in_specs=[
            pl.BlockSpec((1, tq, GROUP_LANES), lambda bi, g, i: (bi, i, g)),
            pl.BlockSpec((1, t_k, GROUP_LANES), lambda bi, g, i: (bi, 0, g)),
            pl.BlockSpec((1, t_k, GROUP_LANES), lambda bi, g, i: (bi, 0, g)),
            pl.BlockSpec((1, 1, tq, HEADS_PER_GROUP), lambda bi, g, i: (bi, g, i, 0)),
            pl.BlockSpec((1, 1, n_kv, HEADS_PER_GROUP, tk), lambda bi, g, i: (bi, g, 0, 0, 0)),
        ],
        out_specs=pl.BlockSpec((1, tq, GROUP_LANES), lambda bi, g, i: (bi, i, g)),
        out_shape=jax.ShapeDtypeStruct((b, t, D_ATTN), BF16),
        compiler_params=_params("parallel", "parallel", "arbitrary"),
        name="fox_attention",
    )(q, k_all, v_all, fq, fk)


def _rnn_kernel(xr_ref, gg_ref, hist_ref, h0_ref, cw_ref, cb_ref, wa_ref, ba_ref, wx_ref, bx_ref,
                lam_ref, out_ref, hl_ref, prev_sc, h_sc, *, tb):
    t = pl.program_id(1)

    @pl.when(t == 0)
    def _():
        prev_sc[...] = hist_ref[0]
        h_sc[...] = h0_ref[0]

    x = xr_ref[0]
    prev = prev_sc[...]
    row8 = lax.broadcasted_iota(jnp.int32, (SUBLANES, D_RNN), 0)
    xc = x * cw_ref[CONV_WIDTH - 1:CONV_WIDTH, :] + cb_ref[...]
    for s in range(1, CONV_WIDTH):
        xs = pltpu.roll(x, s, axis=0)
        head = jnp.where(row8 < s, pltpu.roll(prev, s, axis=0), xs[:SUBLANES])
        xs = jnp.concatenate([head, xs[SUBLANES:]], axis=0)
        xc = xc + xs * cw_ref[CONV_WIDTH - 1 - s:CONV_WIDTH - s, :]
    prev_sc[...] = x[tb - SUBLANES:tb]

    xcb = xc.astype(BF16)
    r = _sigmoid(jnp.dot(xcb, wa_ref[...], preferred_element_type=F32) + ba_ref[...])
    ig = _sigmoid(jnp.dot(xcb, wx_ref[...], preferred_element_type=F32) + bx_ref[...])
    log_a = (-LRU_C) * r * _softplus(-lam_ref[...])
    a = jnp.exp(log_a)
    bterm = jnp.sqrt(1.0 - a * a) * ig * xc

    row = lax.broadcasted_iota(jnp.int32, (tb, D_RNN), 0)
    d = 1
    while d < tb:
        valid = row >= d
        a_s = pltpu.roll(a, d, axis=0)
        b_s = pltpu.roll(bterm, d, axis=0)
        bterm = jnp.where(valid, a * b_s + bterm, bterm)
        a = jnp.where(valid, a * a_s, a)
        d *= 2
    h = bterm + a * h_sc[...]
    h_last = h[tb - 1:tb]
    h_sc[...] = h_last
    hl_ref[0] = h_last
    out_ref[0] = (h * gg_ref[0].astype(F32)).astype(BF16)


def _rnn(xr, gg, hist8, h0, conv_w, conv_b, wa, ba, wx, bx, lam, tb):
    b, t, _ = xr.shape
    blk = pl.BlockSpec((1, tb, D_RNN), lambda bi, ti: (bi, ti, 0))
    full = lambda a: pl.BlockSpec(a.shape, lambda bi, ti: (0,) * a.ndim)
    per_b = lambda r: pl.BlockSpec((1, r, D_RNN), lambda bi, ti: (bi, 0, 0))
    return pl.pallas_call(
        functools.partial(_rnn_kernel, tb=tb),
        grid=(b, t // tb),
        in_specs=[blk, blk, per_b(SUBLANES), per_b(1), full(conv_w), full(conv_b), full(wa), full(ba),
                  full(wx), full(bx), full(lam)],
        out_specs=(blk, per_b(1)),
        out_shape=(jax.ShapeDtypeStruct((b, t, D_RNN), BF16), jax.ShapeDtypeStruct((b, 1, D_RNN), F32)),
        scratch_shapes=[pltpu.VMEM((SUBLANES, D_RNN), F32), pltpu.VMEM((1, D_RNN), F32)],
        compiler_params=_params("parallel", "arbitrary"),
        name="conv_rglru",
    )(xr, gg, hist8, h0, conv_w, conv_b, wa, ba, wx, bx, lam)


def _outproj_kernel(o_ref, r_ref, sa_ref, sr_ref, x_ref, wau_ref, wru_ref, wo_ref, g_ref, b_ref,
                    h_ref, *, alpha):
    up_a = jnp.dot(o_ref[...], wau_ref[...], preferred_element_type=F32)
    up_r = jnp.dot(r_ref[...], wru_ref[...], preferred_element_type=F32)
    merged = sa_ref[...].astype(F32) * up_a + sr_ref[...].astype(F32) * up_r
    mix = jnp.dot(merged.astype(BF16), wo_ref[...], preferred_element_type=F32)
    h_ref[...] = _layer_norm(alpha * x_ref[...] + mix, g_ref[...], b_ref[...])


def _outproj(o2, r2, sa, sr, x2, wau, wru, wo, g, bta, alpha, tm):
    n = x2.shape[0]
    row = lambda w: pl.BlockSpec((tm, w), lambda i: (i, 0))
    full = lambda a: pl.BlockSpec(a.shape, lambda i: (0,) * a.ndim)
    return pl.pallas_call(
        functools.partial(_outproj_kernel, alpha=alpha),
        grid=(n // tm,),
        in_specs=[row(D_ATTN), row(D_RNN), row(D_MODEL), row(D_MODEL), row(D_MODEL), full(wau), full(wru),
                  full(wo), full(g), full(bta)],
        out_specs=row(D_MODEL),
        out_shape=jax.ShapeDtypeStruct((n, D_MODEL), F32),
        compiler_params=_params("parallel"),
        name="outproj_ln1",
    )(o2, r2, sa, sr, x2, wau, wru, wo, g, bta)


def _route_kernel(h_ref, wq_ref, k1_ref, k2_ref, ht_ref, cnt_ref, e1_ref, r2_ref, e2_ref,
                  qt_sc, t1_sc, t2_sc, *, tm):
    ht = jnp.transpose(h_ref[...]).astype(BF16)
    ht_ref[...] = ht
    qt_sc[...] = jnp.dot(wq_ref[...], ht, preferred_element_type=F32).astype(BF16)
    key_iota = lax.broadcasted_iota(jnp.int32, (N_KEYS, tm), 0).astype(F32)
    top_iota = lax.broadcasted_iota(jnp.int32, (PEER_TOPK, tm), 0).astype(F32)
    front_rows = SUBLANES

    def route(exact_ties):
        def pick_one(v, iota, n):
            m = jnp.max(v, axis=0, keepdims=True)
            sel = v == m
            if exact_ties:
                sel = iota == jnp.min(jnp.where(sel, iota, float(n)), axis=0, keepdims=True)
            return m, sel

        picked = jnp.zeros((1, tm), F32)
        for hd in range(PEER_HEADS):
            base = hd * 2 * PEER_HALF
            s1 = jnp.dot(k1_ref[...], qt_sc[base:base + PEER_HALF], preferred_element_type=F32)
            s2 = jnp.dot(k2_ref[...], qt_sc[base + PEER_HALF:base + 2 * PEER_HALF], preferred_element_type=F32)

            def extract(a, carry):
                v1, r1, v2, r2 = carry
                m1, sel1 = pick_one(v1, key_iota, N_KEYS)
                m2, sel2 = pick_one(v2, key_iota, N_KEYS)
                t1_sc[pl.ds(a, 1), :] = m1
                t2_sc[pl.ds(a, 1), :] = m2
                af = a.astype(F32)
                return (jnp.where(sel1, -jnp.inf, v1), jnp.where(sel1, af, r1),
                        jnp.where(sel2, -jnp.inf, v2), jnp.where(sel2, af, r2))

            no_rank = jnp.full((N_KEYS, tm), NOT_SELECTED_RANK, F32)
            _, rank1, _, rank2 = lax.fori_loop(0, PEER_TOPK, extract, (s1, no_rank, s2, no_rank))
            t1 = t1_sc[...]
            t2 = t2_sc[...]
            top0 = t1[0:1] + t2[0:1]

            def pick(_, carry):
                ptr, front, z = carry
                m, sel = pick_one(front, top_iota, PEER_TOPK)
                ptr = ptr + jnp.where(sel, 1.0, 0.0)
                lo = ptr[:front_rows]
                nxt = jnp.full(lo.shape, -jnp.inf, F32)
                for b in range(1, PEER_TOPK):
                    nxt = jnp.where(lo == float(b), t2_sc[b:b + 1, :], nxt)
                nxt = jnp.where(lo == 0.0, t2[0:1], nxt)
                front = jnp.concatenate(
                    [t1[:front_rows] + nxt, jnp.where(sel[front_rows:], -jnp.inf, front[front_rows:])], axis=0)
                return ptr, front, z + jnp.exp(m - top0)

            cnt, _, z = lax.fori_loop(
                0, PEER_TOPK, pick,
                (jnp.zeros((PEER_TOPK, tm), F32), t1 + t2[0:1], jnp.zeros((1, tm), F32)))

            cnt1 = jnp.zeros((N_KEYS, tm), F32)
            for a in range(PEER_TOPK):
                cnt1 = jnp.where(rank1 == float(a), cnt[a:a + 1], cnt1)
            cnt_ref[hd] = cnt1
            e1_ref[hd] = jnp.exp(s1 - t1[0:1]) / z
            r2_ref[hd] = rank2.astype(BF16)
            e2_ref[hd] = jnp.exp(s2 - t2[0:1]).astype(BF16)
            n1 = jnp.sum(jnp.where(rank1 < float(PEER_TOPK), 1.0, 0.0), axis=0, keepdims=True)
            n2 = jnp.sum(jnp.where(rank2 < float(PEER_TOPK), 1.0, 0.0), axis=0, keepdims=True)
            n12 = jnp.sum(cnt, axis=0, keepdims=True)
            picked = jnp.maximum(picked, jnp.maximum(jnp.maximum(n1, n2), n12))
        return picked

    picked = route(exact_ties=False)

    @pl.when(jnp.max(picked) > float(PEER_TOPK))
    def _():
        route(exact_ties=True)


def _route(h2, wq_t, k1, k2, tm):
    n = h2.shape[0]
    full = lambda a: pl.BlockSpec(a.shape, lambda i: (0,) * a.ndim)
    per_head = pl.BlockSpec((PEER_HEADS, N_KEYS, tm), lambda i: (0, 0, i))
    hshape = lambda dt: jax.ShapeDtypeStruct((PEER_HEADS, N_KEYS, n), dt)
    return pl.pallas_call(
        functools.partial(_route_kernel, tm=tm),
        grid=(n // tm,),
        in_specs=[pl.BlockSpec((tm, D_MODEL), lambda i: (i, 0)), full(wq_t), full(k1), full(k2)],
        out_specs=(pl.BlockSpec((D_MODEL, tm), lambda i: (0, i)), per_head, per_head, per_head, per_head),
        out_shape=(jax.ShapeDtypeStruct((D_MODEL, n), BF16), hshape(F32), hshape(F32), hshape(BF16),
                   hshape(BF16)),
        scratch_shapes=[pltpu.VMEM((PEER_HEADS * 2 * PEER_HALF, tm), BF16),
                        pltpu.VMEM((PEER_TOPK, tm), F32), pltpu.VMEM((PEER_TOPK, tm), F32)],
        compiler_params=_params("parallel"),
        name="peer_route",
    )(h2, wq_t, k1, k2)


ROWS_PER_STEP = 16
EXPERTS_PER_STEP = ROWS_PER_STEP * N_KEYS
ROWS_PER_SUB = 4
EXPERTS_PER_SUB = ROWS_PER_SUB * N_KEYS
DENSE_CHUNK = 2 * LANES


def _bf16_row_tile(row):
    tile = jnp.broadcast_to(row, (BF16_ROWS, row.shape[1])).astype(BF16)
    return pltpu.repeat(tile, N_KEYS // BF16_ROWS, axis=0)


def _dense_kernel(ht_ref, u_ref, vt_ref, cnt_ref, e1_ref, r2_ref, e2_ref, h_ref, g_ref, b_ref,
                  y_ref, acc_sc, act_sc, coef_sc, *, alpha):
    e = pl.program_id(1)

    @pl.when(e == 0)
    def _():
        acc_sc[...] = jnp.zeros_like(acc_sc)

    tm = act_sc.shape[1]
    n_sub = ROWS_PER_STEP // ROWS_PER_SUB

    def activations(sub):
        rows = slice(sub * EXPERTS_PER_SUB, (sub + 1) * EXPERTS_PER_SUB)
        act_sc[rows, :] = jnp.dot(u_ref[rows, :], ht_ref[...], preferred_element_type=F32)

    def coefficients(sub):
        for il in range(sub * ROWS_PER_SUB, (sub + 1) * ROWS_PER_SUB):
            rows = slice(il * N_KEYS, (il + 1) * N_KEYS)
            chunk = min(DENSE_CHUNK, tm)
            for c in range(tm // chunk):
                lanes = slice(c * chunk, (c + 1) * chunk)
                w = None
                for hd in range(PEER_HEADS):
                    cnt = _bf16_row_tile(cnt_ref[hd, il:il + 1, lanes])
                    e1 = _bf16_row_tile(e1_ref[hd, il:il + 1, lanes])
                    term = jnp.where(r2_ref[hd, :, lanes] < cnt, e2_ref[hd, :, lanes] * e1,
                                     jnp.zeros((), BF16))
                    w = term if w is None else w + term
                coef_sc[rows, lanes] = w * _gelu_tanh(act_sc[rows, lanes]).astype(BF16)

    def mix(sub):
        rows = slice(sub * EXPERTS_PER_SUB, (sub + 1) * EXPERTS_PER_SUB)
        acc_sc[...] += jnp.dot(vt_ref[:, rows], coef_sc[rows, :], preferred_element_type=F32)

    activations(0)
    for sub in range(n_sub):
        if sub + 1 < n_sub:
            activations(sub + 1)
        coefficients(sub)
        mix(sub)

    @pl.when(e == pl.num_programs(1) - 1)
    def _():
        peer = jnp.transpose(acc_sc[...])
        y_ref[...] = _layer_norm(alpha * h_ref[...] + peer, g_ref[...], b_ref[...])


def _dense(ht, u_b, vt_b, cnt1, e1n, rank2, e2, h2, g, bta, alpha, tm):
    n = h2.shape[0]
    n_e = N_EXPERTS // EXPERTS_PER_STEP
    full = lambda a: pl.BlockSpec(a.shape, lambda t, e: (0,) * a.ndim)
    rows_blk = pl.BlockSpec((PEER_HEADS, ROWS_PER_STEP, tm), lambda t, e: (0, e, t))
    cols_blk = pl.BlockSpec((PEER_HEADS, N_KEYS, tm), lambda t, e: (0, 0, t))
    return pl.pallas_call(
        functools.partial(_dense_kernel, alpha=alpha),
        grid=(n // tm, n_e),
        in_specs=[
            pl.BlockSpec((D_MODEL, tm), lambda t, e: (0, t)),
            pl.BlockSpec((EXPERTS_PER_STEP, D_MODEL), lambda t, e: (e, 0)),
            pl.BlockSpec((D_MODEL, EXPERTS_PER_STEP), lambda t, e: (0, e)),
            rows_blk, rows_blk, cols_blk, cols_blk,
            pl.BlockSpec((tm, D_MODEL), lambda t, e: (t, 0)),
            full(g), full(bta),
        ],
        out_specs=pl.BlockSpec((tm, D_MODEL), lambda t, e: (t, 0)),
        out_shape=jax.ShapeDtypeStruct((n, D_MODEL), F32),
        scratch_shapes=[pltpu.VMEM((D_MODEL, tm), F32), pltpu.VMEM((EXPERTS_PER_STEP, tm), F32),
                        pltpu.VMEM((EXPERTS_PER_STEP, tm), BF16)],
        compiler_params=_params("parallel", "arbitrary"),
        name="peer_dense",
    )(ht, u_b, vt_b, cnt1, e1n, rank2, e2, h2, g, bta)


def _block_diag(w):
    nb, bi, bo = w.shape
    eye = jnp.eye(nb, dtype=w.dtype)
    return (eye[:, None, :, None] * w[:, :, None, :]).reshape(nb * bi, nb * bo)


def _prep_weights(w_in, b_forget, conv_w, conv_b, w_rg_a, b_rg_a, w_rg_x, b_rg_x, lru_lambda,
                  w_attn_up, w_rnn_up, w_out, ln1_g, ln1_b, peer_w_query, peer_keys_1, peer_keys_2,
                  peer_u, peer_v, ln2_g, ln2_b):
    c_f = 3 * D_ATTN
    w_in_p = jnp.concatenate(
        [w_in[:, :c_f], jnp.pad(w_in[:, c_f:c_f + N_ATTN_HEADS], ((0, 0), (0, F_PAD - N_ATTN_HEADS))),
         w_in[:, c_f + N_ATTN_HEADS:]], axis=1).astype(BF16)
    row = lambda a: a.reshape(1, -1).astype(F32)
    return dict(
        w_in=w_in_p, b_forget=row(b_forget), conv_w=conv_w.astype(F32), conv_b=row(conv_b),
        wa=_block_diag(w_rg_a).astype(BF16), ba=row(b_rg_a), wx=_block_diag(w_rg_x).astype(BF16),
        bx=row(b_rg_x), lam=row(lru_lambda),
        wau=w_attn_up.astype(BF16), wru=w_rnn_up.astype(BF16), wo=w_out.astype(BF16),
        ln1_g=row(ln1_g), ln1_b=row(ln1_b),
        wq_t=jnp.transpose(peer_w_query).astype(BF16), k1=peer_keys_1.astype(BF16),
        k2=peer_keys_2.astype(BF16), u=peer_u.astype(BF16), vt=jnp.transpose(peer_v).astype(BF16),
        ln2_g=row(ln2_g), ln2_b=row(ln2_b),
    )


def _pick_block(n, target):
    blk = min(n, target)
    assert n % blk == 0, (n, blk)
    return blk


def _trunk_layer(x, past_k, past_v, past_logf, conv_hist, h0, p, alpha):
    bsz, t, _ = x.shape
    n = bsz * t
    x2 = x.reshape(n, D_MODEL)
    q, k, v, kb, vb, lf, xr, gg, sa, sr = _inproj(x2, p["w_in"], p["b_forget"], _pick_block(n, 256))

    tq = _pick_block(t, 256)
    tk = 512 if past_k is None else LANES
    n_past = 0 if past_k is None else past_k.shape[1]
    t_all = n_past + t
    t_pad = -(-t_all // tk) * tk
    lf3 = lf.reshape(bsz, t, N_ATTN_HEADS)
    kb3 = kb.reshape(bsz, t, D_ATTN)
    vb3 = vb.reshape(bsz, t, D_ATTN)
    if past_k is not None:
        lf3 = jnp.concatenate([past_logf.astype(F32), lf3], axis=1)
        kb3 = jnp.concatenate([past_k.reshape(bsz, n_past, D_ATTN).astype(BF16), kb3], axis=1)
        vb3 = jnp.concatenate([past_v.reshape(bsz, n_past, D_ATTN).astype(BF16), vb3], axis=1)
    pad = ((0, 0), (0, t_pad - t_all), (0, 0))
    lf3, kb3, vb3 = jnp.pad(lf3, pad), jnp.pad(kb3, pad), jnp.pad(vb3, pad)
    f_t = _cumsum_time(jnp.transpose(lf3, (0, 2, 1)))
    f_g = f_t.reshape(bsz, N_HEAD_GROUPS, HEADS_PER_GROUP, t_pad)
    fq = jnp.transpose(f_g[:, :, :, n_past:n_past + t], (0, 1, 3, 2))
    fk = jnp.transpose(f_g.reshape(bsz, N_HEAD_GROUPS, HEADS_PER_GROUP, t_pad // tk, tk), (0, 1, 3, 2, 4))
    o = _attention(q.reshape(bsz, t, D_ATTN), kb3, vb3, fq, fk, tq=tq, tk=tk, q_off=n_past)

    hist8 = jnp.pad(conv_hist.astype(F32), ((0, 0), (SUBLANES - (CONV_WIDTH - 1), 0), (0, 0)))
    xr3 = xr.reshape(bsz, t, D_RNN)
    rnn_out, h_last = _rnn(xr3, gg.reshape(bsz, t, D_RNN), hist8, h0.astype(F32).reshape(bsz, 1, D_RNN),
                           p["conv_w"], p["conv_b"], p["wa"], p["ba"], p["wx"], p["bx"], p["lam"],
                           _pick_block(t, 256))
    new_hist = jnp.concatenate([conv_hist.astype(F32), xr3], axis=1)[:, -(CONV_WIDTH - 1):]

    h = _outproj(o.reshape(n, D_ATTN), rnn_out.reshape(n, D_RNN), sa, sr, x2, p["wau"], p["wru"], p["wo"],
                 p["ln1_g"], p["ln1_b"], alpha, _pick_block(n, 256))
    ht, cnt1, e1n, rank2, e2 = _route(h, p["wq_t"], p["k1"], p["k2"], LANES)
    y = _dense(ht, p["u"], p["vt"], cnt1, e1n, rank2, e2, h, p["ln2_g"], p["ln2_b"], alpha,
               _pick_block(n, 512))
    return (y.reshape(bsz, t, D_MODEL), k.reshape(bsz, t, N_ATTN_HEADS, ATTN_HEAD_DIM),
            v.reshape(bsz, t, N_ATTN_HEADS, ATTN_HEAD_DIM), lf.reshape(bsz, t, N_ATTN_HEADS), new_hist,
            h_last.reshape(bsz, D_RNN))


def kernel(x_prompt, x_sample, cache_k, cache_v, cache_logf, state_conv, state_rnn, w_in, b_forget, conv_w, conv_b, w_rg_a, b_rg_a, w_rg_x, b_rg_x, lru_lambda, w_attn_up, w_rnn_up, w_out, ln1_g, ln1_b, peer_w_query, peer_keys_1, peer_keys_2, peer_u, peer_v, ln2_g, ln2_b):
    depth = w_in.shape[0]
    alpha = (2 * depth) ** 0.25
    layer_weights = (w_in, b_forget, conv_w, conv_b, w_rg_a, b_rg_a, w_rg_x, b_rg_x, lru_lambda, w_attn_up,
                     w_rnn_up, w_out, ln1_g, ln1_b, peer_w_query, peer_keys_1, peer_keys_2, peer_u, peer_v,
                     ln2_g, ln2_b)
    hp, hs = x_prompt, x_sample
    prompt_state, sample_state = [], []
    for l in range(depth):
        p = _prep_weights(*(w[l] for w in layer_weights))
        zero_hist = jnp.zeros((hp.shape[0], CONV_WIDTH - 1, D_RNN), F32)
        zero_h = jnp.zeros((hp.shape[0], D_RNN), F32)
        hp, *st_p = _trunk_layer(hp, None, None, None, zero_hist, zero_h, p, alpha)
        hs, *st_s = _trunk_layer(hs, cache_k[l], cache_v[l], cache_logf[l], state_conv[l], state_rnn[l], p,
                                 alpha)
        prompt_state.append(st_p)
        sample_state.append(st_s)
    stack = lambda states, i: jnp.stack([s[i] for s in states])
    return (hp, hs) + tuple(stack(prompt_state, i) for i in range(5)) + tuple(
        stack(sample_state, i) for i in range(5))
```

```python
import functools
import math

import jax
import jax.numpy as jnp
from jax import lax
from jax.experimental import pallas as pl
from jax.experimental.pallas import tpu as pltpu

F32 = jnp.float32
BF16 = jnp.bfloat16

D_MODEL = 1024
N_ATTN_HEADS = 8
ATTN_HEAD_DIM = 64
D_ATTN = N_ATTN_HEADS * ATTN_HEAD_DIM
ATTN_SCALE = ATTN_HEAD_DIM ** -0.5
D_RNN = 512
N_RNN_BLOCKS = 8
RNN_BLOCK = D_RNN // N_RNN_BLOCKS
CONV_WIDTH = 4
LRU_C = 8.0
N_KEYS = 128
N_EXPERTS = N_KEYS * N_KEYS
PEER_HEADS = 8
PEER_TOPK = 16
PEER_HALF = 128
LN_EPS = 1e-5

LANES = 128
SUBLANES = 8
BF16_ROWS = 2 * SUBLANES
VMEM_LIMIT_BYTES = 56 * 1024 * 1024

HEADS_PER_GROUP = 4
GROUP_LANES = HEADS_PER_GROUP * ATTN_HEAD_DIM
N_HEAD_GROUPS = N_ATTN_HEADS // HEADS_PER_GROUP
ATTN_Q_BLOCK = 256
ATTN_K_BLOCK = 512
ATTN_ROW_BLOCK = 32
F_PAD = LANES
NEG_BIG = -1e30
LOG2_E = math.log2(math.e)
NOT_SELECTED_RANK = 99.0

_C_Q = 0
_C_K = _C_Q + D_ATTN
_C_V = _C_K + D_ATTN
_C_F = _C_V + D_ATTN
_C_XR = _C_F + F_PAD
_C_GATE = _C_XR + D_RNN
_C_GA = _C_GATE + D_RNN
_C_GR = _C_GA + D_MODEL
_C_END = _C_GR + D_MODEL


def _params(*sem):
    return pltpu.CompilerParams(dimension_semantics=sem, vmem_limit_bytes=VMEM_LIMIT_BYTES)


def _sigmoid(x):
    return 1.0 / (1.0 + jnp.exp(-x))


def _gelu_tanh(x):
    half = 0.5 * x
    return half + half * jnp.tanh(x * (0.7978845608028654 + 0.035677408136300125 * (x * x)))


def _softplus(x):
    return jnp.maximum(x, 0.0) + jnp.log1p(jnp.exp(-jnp.abs(x)))


def _layer_norm(x, g, b):
    mu = jnp.mean(x, axis=-1, keepdims=True)
    xc = x - mu
    var = jnp.mean(xc * xc, axis=-1, keepdims=True)
    return xc * lax.rsqrt(var + LN_EPS) * g + b


def _inproj_kernel(x_ref, w_ref, bf_ref, q_ref, k_ref, v_ref, kb_ref, vb_ref, lf_ref, xr_ref,
                   gg_ref, sa_ref, sr_ref):
    xb = x_ref[...].astype(BF16)

    def mm(lo, hi):
        return jnp.dot(xb, w_ref[:, lo:hi], preferred_element_type=F32)

    q_ref[...] = (mm(_C_Q, _C_K) * (ATTN_SCALE * LOG2_E)).astype(BF16)
    k = mm(_C_K, _C_V)
    k_ref[...] = k
    kb_ref[...] = k.astype(BF16)
    v = mm(_C_V, _C_F)
    v_ref[...] = v
    vb_ref[...] = v.astype(BF16)
    f = mm(_C_F, _C_XR)[:, :N_ATTN_HEADS] + bf_ref[...]
    lf_ref[...] = -_softplus(-f)
    xr_ref[...] = mm(_C_XR, _C_GATE)
    gg_ref[...] = _gelu_tanh(mm(_C_GATE, _C_GA)).astype(BF16)
    sa_ref[...] = _sigmoid(mm(_C_GA, _C_GR)).astype(BF16)
    sr_ref[...] = _sigmoid(mm(_C_GR, _C_END)).astype(BF16)


def _inproj(x2, w_in_p, b_forget, tm):
    n = x2.shape[0]
    row = lambda w: pl.BlockSpec((tm, w), lambda i: (i, 0))
    full = lambda a: pl.BlockSpec(a.shape, lambda i: (0,) * a.ndim)
    out_shape = (
        jax.ShapeDtypeStruct((n, D_ATTN), BF16),
        jax.ShapeDtypeStruct((n, D_ATTN), F32),
        jax.ShapeDtypeStruct((n, D_ATTN), F32),
        jax.ShapeDtypeStruct((n, D_ATTN), BF16),
        jax.ShapeDtypeStruct((n, D_ATTN), BF16),
        jax.ShapeDtypeStruct((n, N_ATTN_HEADS), F32),
        jax.ShapeDtypeStruct((n, D_RNN), F32),
        jax.ShapeDtypeStruct((n, D_RNN), BF16),
        jax.ShapeDtypeStruct((n, D_MODEL), BF16),
        jax.ShapeDtypeStruct((n, D_MODEL), BF16),
    )
    out_specs = (row(D_ATTN), row(D_ATTN), row(D_ATTN), row(D_ATTN), row(D_ATTN), row(N_ATTN_HEADS),
                 row(D_RNN), row(D_RNN), row(D_MODEL), row(D_MODEL))
    return pl.pallas_call(
        _inproj_kernel,
        grid=(n // tm,),
        in_specs=[row(D_MODEL), full(w_in_p), full(b_forget)],
        out_specs=out_specs,
        out_shape=out_shape,
        compiler_params=_params("parallel"),
        name="inproj",
    )(x2, w_in_p, b_forget)


def _cumsum_kernel(x_ref, o_ref):
    x = x_ref[0]
    t = x.shape[1]
    lane = lax.broadcasted_iota(jnp.int32, x.shape, 1)
    d = 1
    while d < t:
        x = x + jnp.where(lane >= d, pltpu.roll(x, d, axis=1), 0.0)
        d *= 2
    o_ref[0] = x * LOG2_E


def _cumsum_time(lf_t):
    b, h, t = lf_t.shape
    spec = pl.BlockSpec((1, h, t), lambda i: (i, 0, 0))
    return pl.pallas_call(
        _cumsum_kernel, grid=(b,), in_specs=[spec], out_specs=spec,
        out_shape=jax.ShapeDtypeStruct(lf_t.shape, F32),
        compiler_params=_params("parallel"), name="logf_cumsum",
    )(lf_t)


def _attn_kernel(q_ref, k_ref, v_ref, fq_ref, fk_ref, o_ref, q_sc, fq_sc, s_sc, p_sc, acc_sc, m_sc, al_sc,
                 lp_sc, *, tq, tk, q_off):
    qi = pl.program_id(2)
    row0 = q_off + qi * tq
    n_full = (row0 + 1) // tk
    heads = range(HEADS_PER_GROUP)
    rb = min(ATTN_ROW_BLOCK, tq)
    lane = lax.broadcasted_iota(jnp.int32, (tq, GROUP_LANES), 1)
    in_head = [(lane >= j * ATTN_HEAD_DIM) & (lane < (j + 1) * ATTN_HEAD_DIM) for j in heads]
    q = q_ref[0]
    for j in heads:
        q_sc[j * tq:(j + 1) * tq, :] = jnp.where(in_head[j], q, jnp.zeros_like(q))
        fq_sc[j * tq:(j + 1) * tq, :] = jnp.broadcast_to(fq_ref[0, 0, :, j:j + 1], (tq, LANES))
    m_sc[...] = jnp.full_like(m_sc, NEG_BIG)
    lp_sc[...] = jnp.zeros_like(lp_sc)
    acc_sc[...] = jnp.zeros_like(acc_sc)
    halves = [slice(0, 2 * tq), slice(2 * tq, 4 * tq)]
    n_lane_tiles = tk // LANES

    def absorb(c, masked):
        start = pl.multiple_of(c * tk, tk)
        ks = k_ref[0, pl.ds(start, tk), :]
        for hv in halves:
            s_sc[hv, :] = lax.dot_general(q_sc[hv, :], ks, (((1,), (1,)), ((), ())),
                                          preferred_element_type=F32)
        fk = fk_ref[0, 0, c]
        for j in heads:
            for r in range(tq // rb):
                rows = slice(j * tq + r * rb, j * tq + (r + 1) * rb)
                s = s_sc[rows, :] - fk[j:j + 1, :]
                if masked:
                    q_pos = row0 + r * rb + lax.broadcasted_iota(jnp.int32, (rb, tk), 0)
                    k_pos = c * tk + lax.broadcasted_iota(jnp.int32, (rb, tk), 1)
                    s = jnp.where(k_pos <= q_pos, s, NEG_BIG)
                tiles = [s[:, t * LANES:(t + 1) * LANES] for t in range(n_lane_tiles)]
                row_max = jnp.max(functools.reduce(jnp.maximum, tiles), axis=1, keepdims=True)
                fq = fq_sc[rows, :]
                m_old = m_sc[rows, :]
                m_new = jnp.maximum(m_old, jnp.broadcast_to(row_max, (rb, LANES)) + fq)
                alpha = jnp.exp2(m_old - m_new)
                shift = fq - m_new
                p_tiles = [jnp.exp2(tile + shift) for tile in tiles]
                m_sc[rows, :] = m_new
                al_sc[rows, :] = alpha
                lp_sc[rows, :] = alpha * lp_sc[rows, :] + functools.reduce(jnp.add, p_tiles)
                p_sc[rows, :] = jnp.concatenate(p_tiles, axis=1).astype(BF16)
        vs = v_ref[0, pl.ds(start, tk), :]
        for hv in halves:
            pv = jnp.dot(p_sc[hv, :], vs, preferred_element_type=F32)
            alpha = al_sc[hv, :]
            acc_sc[hv, :] = jnp.concatenate([alpha] * (GROUP_LANES // LANES), axis=1) * acc_sc[hv, :] + pv

    def step(c, carry):
        absorb(c, masked=False)
        return carry

    lax.fori_loop(0, n_full, step, 0)
    absorb(n_full, masked=True)
    o_all = acc_sc[...] / jnp.sum(lp_sc[...], axis=1, keepdims=True)
    out = jnp.zeros((tq, GROUP_LANES), F32)
    for j in heads:
        out = jnp.where(in_head[j], o_all[j * tq:(j + 1) * tq], out)
    o_ref[0] = out.astype(BF16)


def _attention(q, k_all, v_all, fq, fk, *, tq, tk, q_off):
    b, t, _ = q.shape
    t_k = k_all.shape[1]
    n_kv = t_k // tk
    assert all((q_off + i * tq) % tk + tq <= tk for i in range(t // tq)), "a query block straddles key chunks"
    kern = functools.partial(_attn_kernel, tq=tq, tk=tk, q_off=q_off)
    rows = HEADS_PER_GROUP * tq
    return pl.pallas_call(
        kern,
        grid=(b, N_HEAD_GROUPS, t // tq),
        in_specs=[
            pl.BlockSpec((1, tq, GROUP_LANES), lambda bi, g, i: (bi, i, g)),
            pl.BlockSpec((1, t_k, GROUP_LANES), lambda bi, g, i: (bi, 0, g)),
            pl.BlockSpec((1, t_k, GROUP_LANES), lambda bi, g, i: (bi, 0, g)),
            pl.BlockSpec((1, 1, tq, HEADS_PER_GROUP), lambda bi, g, i: (bi, g, i, 0)),
            pl.BlockSpec((1, 1, n_kv, HEADS_PER_GROUP, tk), lambda bi, g, i: (bi, g, 0, 0, 0)),
        ],
        out_specs=pl.BlockSpec((1, tq, GROUP_LANES), lambda bi, g, i: (bi, i, g)),
        out_shape=jax.ShapeDtypeStruct((b, t, D_ATTN), BF16),
        scratch_shapes=[
            pltpu.VMEM((rows, GROUP_LANES), BF16),
            pltpu.VMEM((rows, LANES), F32),
            pltpu.VMEM((rows, tk), F32),
            pltpu.VMEM((rows, tk), BF16),
            pltpu.VMEM((rows, GROUP_LANES), F32),
            pltpu.VMEM((rows, LANES), F32),
            pltpu.VMEM((rows, LANES), F32),
            pltpu.VMEM((rows, LANES), F32),
        ],
        compiler_params=_params("parallel", "parallel", "arbitrary"),
        name="fox_attention",
    )(q, k_all, v_all, fq, fk)


def _rnn_kernel(xr_ref, gg_ref, hist_ref, h0_ref, cw_ref, cb_ref, wa_ref, ba_ref, wx_ref, bx_ref,
                lam_ref, out_ref, hl_ref, prev_sc, h_sc, *, tb):
    t = pl.program_id(1)

    @pl.when(t == 0)
    def _():
        prev_sc[...] = hist_ref[0]
        h_sc[...] = h0_ref[0]

    x = xr_ref[0]
    prev = prev_sc[...]
    row8 = lax.broadcasted_iota(jnp.int32, (SUBLANES, D_RNN), 0)
    xc = x * cw_ref[CONV_WIDTH - 1:CONV_WIDTH, :] + cb_ref[...]
    for s in range(1, CONV_WIDTH):
        xs = pltpu.roll(x, s, axis=0)
        head = jnp.where(row8 < s, pltpu.roll(prev, s, axis=0), xs[:SUBLANES])
        xs = jnp.concatenate([head, xs[SUBLANES:]], axis=0)
        xc = xc + xs * cw_ref[CONV_WIDTH - 1 - s:CONV_WIDTH - s, :]
    prev_sc[...] = x[tb - SUBLANES:tb]

    xcb = xc.astype(BF16)
    r = _sigmoid(jnp.dot(xcb, wa_ref[...], preferred_element_type=F32) + ba_ref[...])
    ig = _sigmoid(jnp.dot(xcb, wx_ref[...], preferred_element_type=F32) + bx_ref[...])
    log_a = (-LRU_C) * r * _softplus(-lam_ref[...])
    a = jnp.exp(log_a)
    bterm = jnp.sqrt(1.0 - a * a) * ig * xc

    row = lax.broadcasted_iota(jnp.int32, (tb, D_RNN), 0)
    d = 1
    while d < tb:
        valid = row >= d
        a_s = pltpu.roll(a, d, axis=0)
        b_s = pltpu.roll(bterm, d, axis=0)
        bterm = jnp.where(valid, a * b_s + bterm, bterm)
        a = jnp.where(valid, a * a_s, a)
        d *= 2
    h = bterm + a * h_sc[...]
    h_last = h[tb - 1:tb]
    h_sc[...] = h_last
    hl_ref[0] = h_last
    out_ref[0] = (h * gg_ref[0].astype(F32)).astype(BF16)


def _rnn(xr, gg, hist8, h0, conv_w, conv_b, wa, ba, wx, bx, lam, tb):
    b, t, _ = xr.shape
    blk = pl.BlockSpec((1, tb, D_RNN), lambda bi, ti: (bi, ti, 0))
    full = lambda a: pl.BlockSpec(a.shape, lambda bi, ti: (0,) * a.ndim)
    per_b = lambda r: pl.BlockSpec((1, r, D_RNN), lambda bi, ti: (bi, 0, 0))
    return pl.pallas_call(
        functools.partial(_rnn_kernel, tb=tb),
        grid=(b, t // tb),
        in_specs=[blk, blk, per_b(SUBLANES), per_b(1), full(conv_w), full(conv_b), full(wa), full(ba),
                  full(wx), full(bx), full(lam)],
        out_specs=(blk, per_b(1)),
        out_shape=(jax.ShapeDtypeStruct((b, t, D_RNN), BF16), jax.ShapeDtypeStruct((b, 1, D_RNN), F32)),
        scratch_shapes=[pltpu.VMEM((SUBLANES, D_RNN), F32), pltpu.VMEM((1, D_RNN), F32)],
        compiler_params=_params("parallel", "arbitrary"),
        name="conv_rglru",
    )(xr, gg, hist8, h0, conv_w, conv_b, wa, ba, wx, bx, lam)


def _outproj_kernel(o_ref, r_ref, sa_ref, sr_ref, x_ref, wau_ref, wru_ref, wo_ref, g_ref, b_ref,
                    h_ref, *, alpha):
    up_a = jnp.dot(o_ref[...], wau_ref[...], preferred_element_type=F32)
    up_r = jnp.dot(r_ref[...], wru_ref[...], preferred_element_type=F32)
    merged = sa_ref[...].astype(F32) * up_a + sr_ref[...].astype(F32) * up_r
    mix = jnp.dot(merged.astype(BF16), wo_ref[...], preferred_element_type=F32)
    h_ref[...] = _layer_norm(alpha * x_ref[...] + mix, g_ref[...], b_ref[...])


def _outproj(o2, r2, sa, sr, x2, wau, wru, wo, g, bta, alpha, tm):
    n = x2.shape[0]
    row = lambda w: pl.BlockSpec((tm, w), lambda i: (i, 0))
    full = lambda a: pl.BlockSpec(a.shape, lambda i: (0,) * a.ndim)
    return pl.pallas_call(
        functools.partial(_outproj_kernel, alpha=alpha),
        grid=(n // tm,),
        in_specs=[row(D_ATTN), row(D_RNN), row(D_MODEL), row(D_MODEL), row(D_MODEL), full(wau), full(wru),
                  full(wo), full(g), full(bta)],
        out_specs=row(D_MODEL),
        out_shape=jax.ShapeDtypeStruct((n, D_MODEL), F32),
        compiler_params=_params("parallel"),
        name="outproj_ln1",
    )(o2, r2, sa, sr, x2, wau, wru, wo, g, bta)


def _route_kernel(h_ref, wq_ref, k1_ref, k2_ref, ht_ref, cnt_ref, e1_ref, r2_ref, e2_ref,
                  qt_sc, t1_sc, t2_sc, *, tm):
    ht = jnp.transpose(h_ref[...]).astype(BF16)
    ht_ref[...] = ht
    qt_sc[...] = jnp.dot(wq_ref[...], ht, preferred_element_type=F32).astype(BF16)
    key_iota = lax.broadcasted_iota(jnp.int32, (N_KEYS, tm), 0).astype(F32)
    top_iota = lax.broadcasted_iota(jnp.int32, (PEER_TOPK, tm), 0).astype(F32)
    front_rows = SUBLANES

    def route_head(hd, exact_ties):
        def pick_one(v, iota, n):
            m = jnp.max(v, axis=0, keepdims=True)
            sel = v == m
            if exact_ties:
                sel = iota == jnp.min(jnp.where(sel, iota, float(n)), axis=0, keepdims=True)
            return m, sel

        base = hd * 2 * PEER_HALF
        s1 = jnp.dot(k1_ref[...], qt_sc[base:base + PEER_HALF], preferred_element_type=F32)
        s2 = jnp.dot(k2_ref[...], qt_sc[base + PEER_HALF:base + 2 * PEER_HALF], preferred_element_type=F32)

        def extract(a, carry):
            v1, r1, v2, r2 = carry
            m1, sel1 = pick_one(v1, key_iota, N_KEYS)
            m2, sel2 = pick_one(v2, key_iota, N_KEYS)
            t1_sc[pl.ds(a, 1), :] = m1
            t2_sc[pl.ds(a, 1), :] = m2
            af = a.astype(F32)
            return (jnp.where(sel1, -jnp.inf, v1), jnp.where(sel1, af, r1),
                    jnp.where(sel2, -jnp.inf, v2), jnp.where(sel2, af, r2))

        no_rank = jnp.full((N_KEYS, tm), NOT_SELECTED_RANK, F32)
        _, rank1, _, rank2 = lax.fori_loop(0, PEER_TOPK, extract, (s1, no_rank, s2, no_rank))
        t1 = t1_sc[...]
        t2 = t2_sc[...]
        top0 = t1[0:1] + t2[0:1]

        def pick(_, carry):
            ptr, front, z = carry
            m, sel = pick_one(front, top_iota, PEER_TOPK)
            ptr = ptr + jnp.where(sel, 1.0, 0.0)
            lo = ptr[:front_rows]
            nxt = jnp.full(lo.shape, -jnp.inf, F32)
            for b in range(1, PEER_TOPK):
                nxt = jnp.where(lo == float(b), t2_sc[b:b + 1, :], nxt)
            nxt = jnp.where(lo == 0.0, t2[0:1], nxt)
            front = jnp.concatenate(
                [t1[:front_rows] + nxt, jnp.where(sel[front_rows:], -jnp.inf, front[front_rows:])], axis=0)
            return ptr, front, z + jnp.exp(m - top0)

        cnt, _, z = lax.fori_loop(
            0, PEER_TOPK, pick,
            (jnp.zeros((PEER_TOPK, tm), F32), t1 + t2[0:1], jnp.zeros((1, tm), F32)))

        cnt1 = jnp.zeros((N_KEYS, tm), F32)
        for a in range(PEER_TOPK):
            cnt1 = jnp.where(rank1 == float(a), cnt[a:a + 1], cnt1)
        cnt_ref[hd] = cnt1
        e1_ref[hd] = jnp.exp(s1 - t1[0:1]) / z
        r2_ref[hd] = rank2.astype(BF16)
        e2_ref[hd] = jnp.exp(s2 - t2[0:1]).astype(BF16)
        n1 = jnp.sum(jnp.where(rank1 < float(PEER_TOPK), 1.0, 0.0), axis=0, keepdims=True)
        n2 = jnp.sum(jnp.where(rank2 < float(PEER_TOPK), 1.0, 0.0), axis=0, keepdims=True)
        n12 = jnp.sum(cnt, axis=0, keepdims=True)
        return jnp.maximum(jnp.maximum(n1, n2), n12)

    picked = [route_head(hd, exact_ties=False) for hd in range(PEER_HEADS)]

    @pl.when(jnp.max(functools.reduce(jnp.maximum, picked)) > float(PEER_TOPK))
    def _():
        for hd in range(PEER_HEADS):
            @pl.when(jnp.max(picked[hd]) > float(PEER_TOPK))
            def _():
                route_head(hd, exact_ties=True)


def _route(h2, wq_t, k1, k2, tm):
    n = h2.shape[0]
    full = lambda a: pl.BlockSpec(a.shape, lambda i: (0,) * a.ndim)
    per_head = pl.BlockSpec((PEER_HEADS, N_KEYS, tm), lambda i: (0, 0, i))
    hshape = lambda dt: jax.ShapeDtypeStruct((PEER_HEADS, N_KEYS, n), dt)
    return pl.pallas_call(
        functools.partial(_route_kernel, tm=tm),
        grid=(n // tm,),
        in_specs=[pl.BlockSpec((tm, D_MODEL), lambda i: (i, 0)), full(wq_t), full(k1), full(k2)],
        out_specs=(pl.BlockSpec((D_MODEL, tm), lambda i: (0, i)), per_head, per_head, per_head, per_head),
        out_shape=(jax.ShapeDtypeStruct((D_MODEL, n), BF16), hshape(F32), hshape(F32), hshape(BF16),
                   hshape(BF16)),
        scratch_shapes=[pltpu.VMEM((PEER_HEADS * 2 * PEER_HALF, tm), BF16),
                        pltpu.VMEM((PEER_TOPK, tm), F32), pltpu.VMEM((PEER_TOPK, tm), F32)],
        compiler_params=_params("parallel"),
        name="peer_route",
    )(h2, wq_t, k1, k2)


ROWS_PER_STEP = 16
EXPERTS_PER_STEP = ROWS_PER_STEP * N_KEYS
ROWS_PER_SUB = 4
EXPERTS_PER_SUB = ROWS_PER_SUB * N_KEYS
DENSE_CHUNK = 2 * LANES


def _bf16_row_tile(row):
    tile = jnp.broadcast_to(row, (BF16_ROWS, row.shape[1])).astype(BF16)
    return pltpu.repeat(tile, N_KEYS // BF16_ROWS, axis=0)


def _dense_kernel(ht_ref, u_ref, vt_ref, cnt_ref, e1_ref, r2_ref, e2_ref, h_ref, g_ref, b_ref,
                  y_ref, acc_sc, act_sc, coef_sc, *, alpha):
    e = pl.program_id(1)

    @pl.when(e == 0)
    def _():
        acc_sc[...] = jnp.zeros_like(acc_sc)

    tm = act_sc.shape[1]
    n_sub = ROWS_PER_STEP // ROWS_PER_SUB

    def activations(sub):
        rows = slice(sub * EXPERTS_PER_SUB, (sub + 1) * EXPERTS_PER_SUB)
        act_sc[rows, :] = jnp.dot(u_ref[rows, :], ht_ref[...], preferred_element_type=F32)

    def coefficients(sub):
        for il in range(sub * ROWS_PER_SUB, (sub + 1) * ROWS_PER_SUB):
            rows = slice(il * N_KEYS, (il + 1) * N_KEYS)
            chunk = min(DENSE_CHUNK, tm)
            for c in range(tm // chunk):
                lanes = slice(c * chunk, (c + 1) * chunk)
                w = None
                for hd in range(PEER_HEADS):
                    cnt = _bf16_row_tile(cnt_ref[hd, il:il + 1, lanes])
                    e1 = _bf16_row_tile(e1_ref[hd, il:il + 1, lanes])
                    term = jnp.where(r2_ref[hd, :, lanes] < cnt, e2_ref[hd, :, lanes] * e1,
                                     jnp.zeros((), BF16))
                    w = term if w is None else w + term
                coef_sc[rows, lanes] = w * _gelu_tanh(act_sc[rows, lanes]).astype(BF16)

    def mix(sub):
        rows = slice(sub * EXPERTS_PER_SUB, (sub + 1) * EXPERTS_PER_SUB)
        acc_sc[...] += jnp.dot(vt_ref[:, rows], coef_sc[rows, :], preferred_element_type=F32)

    activations(0)
    for sub in range(n_sub):
        if sub + 1 < n_sub:
            activations(sub + 1)
        coefficients(sub)
        mix(sub)

    @pl.when(e == pl.num_programs(1) - 1)
    def _():
        peer = jnp.transpose(acc_sc[...])
        y_ref[...] = _layer_norm(alpha * h_ref[...] + peer, g_ref[...], b_ref[...])


def _dense(ht, u_b, vt_b, cnt1, e1n, rank2, e2, h2, g, bta, alpha, tm):
    n = h2.shape[0]
    n_e = N_EXPERTS // EXPERTS_PER_STEP
    full = lambda a: pl.BlockSpec(a.shape, lambda t, e: (0,) * a.ndim)
    rows_blk = pl.BlockSpec((PEER_HEADS, ROWS_PER_STEP, tm), lambda t, e: (0, e, t))
    cols_blk = pl.BlockSpec((PEER_HEADS, N_KEYS, tm), lambda t, e: (0, 0, t))
    return pl.pallas_call(
        functools.partial(_dense_kernel, alpha=alpha),
        grid=(n // tm, n_e),
        in_specs=[
            pl.BlockSpec((D_MODEL, tm), lambda t, e: (0, t)),
            pl.BlockSpec((EXPERTS_PER_STEP, D_MODEL), lambda t, e: (e, 0)),
            pl.BlockSpec((D_MODEL, EXPERTS_PER_STEP), lambda t, e: (0, e)),
            rows_blk, rows_blk, cols_blk, cols_blk,
            pl.BlockSpec((tm, D_MODEL), lambda t, e: (t, 0)),
            full(g), full(bta),
        ],
        out_specs=pl.BlockSpec((tm, D_MODEL), lambda t, e: (t, 0)),
        out_shape=jax.ShapeDtypeStruct((n, D_MODEL), F32),
        scratch_shapes=[pltpu.VMEM((D_MODEL, tm), F32), pltpu.VMEM((EXPERTS_PER_STEP, tm), F32),
                        pltpu.VMEM((EXPERTS_PER_STEP, tm), BF16)],
        compiler_params=_params("parallel", "arbitrary"),
        name="peer_dense",
    )(ht, u_b, vt_b, cnt1, e1n, rank2, e2, h2, g, bta)


def _block_diag(w):
    nb, bi, bo = w.shape
    eye = jnp.eye(nb, dtype=w.dtype)
    return (eye[:, None, :, None] * w[:, :, None, :]).reshape(nb * bi, nb * bo)


def _prep_weights(w_in, b_forget, conv_w, conv_b, w_rg_a, b_rg_a, w_rg_x, b_rg_x, lru_lambda,
                  w_attn_up, w_rnn_up, w_out, ln1_g, ln1_b, peer_w_query, peer_keys_1, peer_keys_2,
                  peer_u, peer_v, ln2_g, ln2_b):
    c_f = 3 * D_ATTN
    w_in_p = jnp.concatenate(
        [w_in[:, :c_f], jnp.pad(w_in[:, c_f:c_f + N_ATTN_HEADS], ((0, 0), (0, F_PAD - N_ATTN_HEADS))),
         w_in[:, c_f + N_ATTN_HEADS:]], axis=1).astype(BF16)
    row = lambda a: a.reshape(1, -1).astype(F32)
    return dict(
        w_in=w_in_p, b_forget=row(b_forget), conv_w=conv_w.astype(F32), conv_b=row(conv_b),
        wa=_block_diag(w_rg_a).astype(BF16), ba=row(b_rg_a), wx=_block_diag(w_rg_x).astype(BF16),
        bx=row(b_rg_x), lam=row(lru_lambda),
        wau=w_attn_up.astype(BF16), wru=w_rnn_up.astype(BF16), wo=w_out.astype(BF16),
        ln1_g=row(ln1_g), ln1_b=row(ln1_b),
        wq_t=jnp.transpose(peer_w_query).astype(BF16), k1=peer_keys_1.astype(BF16),
        k2=peer_keys_2.astype(BF16), u=peer_u.astype(BF16), vt=jnp.transpose(peer_v).astype(BF16),
        ln2_g=row(ln2_g), ln2_b=row(ln2_b),
    )


def _pick_block(n, target):
    blk = min(n, target)
    assert n % blk == 0, (n, blk)
    return blk


def _trunk_layer(x, past_k, past_v, past_logf, conv_hist, h0, p, alpha):
    bsz, t, _ = x.shape
    n = bsz * t
    x2 = x.reshape(n, D_MODEL)
    q, k, v, kb, vb, lf, xr, gg, sa, sr = _inproj(x2, p["w_in"], p["b_forget"], _pick_block(n, 256))

    tq = _pick_block(t, ATTN_Q_BLOCK)
    n_past = 0 if past_k is None else past_k.shape[1]
    t_all = n_past + t
    t_lanes = -(-t_all // LANES) * LANES
    tk = ATTN_K_BLOCK if t_lanes % ATTN_K_BLOCK == 0 else t_lanes
    t_pad = -(-t_all // tk) * tk
    lf3 = lf.reshape(bsz, t, N_ATTN_HEADS)
    kb3 = kb.reshape(bsz, t, D_ATTN)
    vb3 = vb.reshape(bsz, t, D_ATTN)
    if past_k is not None:
        lf3 = jnp.concatenate([past_logf.astype(F32), lf3], axis=1)
        kb3 = jnp.concatenate([past_k.reshape(bsz, n_past, D_ATTN).astype(BF16), kb3], axis=1)
        vb3 = jnp.concatenate([past_v.reshape(bsz, n_past, D_ATTN).astype(BF16), vb3], axis=1)
    pad = ((0, 0), (0, t_pad - t_all), (0, 0))
    lf3, kb3, vb3 = jnp.pad(lf3, pad), jnp.pad(kb3, pad), jnp.pad(vb3, pad)
    f_t = _cumsum_time(jnp.transpose(lf3, (0, 2, 1)))
    f_g = f_t.reshape(bsz, N_HEAD_GROUPS, HEADS_PER_GROUP, t_pad)
    fq = jnp.transpose(f_g[:, :, :, n_past:n_past + t], (0, 1, 3, 2))
    fk = jnp.transpose(f_g.reshape(bsz, N_HEAD_GROUPS, HEADS_PER_GROUP, t_pad // tk, tk), (0, 1, 3, 2, 4))
    o = _attention(q.reshape(bsz, t, D_ATTN), kb3, vb3, fq, fk, tq=tq, tk=tk, q_off=n_past)

    hist8 = jnp.pad(conv_hist.astype(F32), ((0, 0), (SUBLANES - (CONV_WIDTH - 1), 0), (0, 0)))
    xr3 = xr.reshape(bsz, t, D_RNN)
    rnn_out, h_last = _rnn(xr3, gg.reshape(bsz, t, D_RNN), hist8, h0.astype(F32).reshape(bsz, 1, D_RNN),
                           p["conv_w"], p["conv_b"], p["wa"], p["ba"], p["wx"], p["bx"], p["lam"],
                           _pick_block(t, 256))
    new_hist = jnp.concatenate([conv_hist.astype(F32), xr3], axis=1)[:, -(CONV_WIDTH - 1):]

    h = _outproj(o.reshape(n, D_ATTN), rnn_out.reshape(n, D_RNN), sa, sr, x2, p["wau"], p["wru"], p["wo"],
                 p["ln1_g"], p["ln1_b"], alpha, _pick_block(n, 256))
    ht, cnt1, e1n, rank2, e2 = _route(h, p["wq_t"], p["k1"], p["k2"], LANES)
    y = _dense(ht, p["u"], p["vt"], cnt1, e1n, rank2, e2, h, p["ln2_g"], p["ln2_b"], alpha,
               _pick_block(n, 512))
    return (y.reshape(bsz, t, D_MODEL), k.reshape(bsz, t, N_ATTN_HEADS, ATTN_HEAD_DIM),
            v.reshape(bsz, t, N_ATTN_HEADS, ATTN_HEAD_DIM), lf.reshape(bsz, t, N_ATTN_HEADS), new_hist,
            h_last.reshape(bsz, D_RNN))


def kernel(x_prompt, x_sample, cache_k, cache_v, cache_logf, state_conv, state_rnn, w_in, b_forget, conv_w, conv_b, w_rg_a, b_rg_a, w_rg_x, b_rg_x, lru_lambda, w_attn_up, w_rnn_up, w_out, ln1_g, ln1_b, peer_w_query, peer_keys_1, peer_keys_2, peer_u, peer_v, ln2_g, ln2_b):
    depth = w_in.shape[0]
    alpha = (2 * depth) ** 0.25
    layer_weights = (w_in, b_forget, conv_w, conv_b, w_rg_a, b_rg_a, w_rg_x, b_rg_x, lru_lambda, w_attn_up,
                     w_rnn_up, w_out, ln1_g, ln1_b, peer_w_query, peer_keys_1, peer_keys_2, peer_u, peer_v,
                     ln2_g, ln2_b)
    hp, hs = x_prompt, x_sample
    prompt_state, sample_state = [], []
    for l in range(depth):
        p = _prep_weights(*(w[l] for w in layer_weights))
        zero_hist = jnp.zeros((hp.shape[0], CONV_WIDTH - 1, D_RNN), F32)
        zero_h = jnp.zeros((hp.shape[0], D_RNN), F32)
        hp, *st_p = _trunk_layer(hp, None, None, None, zero_hist, zero_h, p, alpha)
        hs, *st_s = _trunk_layer(hs, cache_k[l], cache_v[l], cache_logf[l], state_conv[l], state_rnn[l], p,
                                 alpha)
        prompt_state.append(st_p)
        sample_state.append(st_s)
    stack = lambda states, i: jnp.stack([s[i] for s in states])
    return (hp, hs) + tuple(stack(prompt_state, i) for i in range(5)) + tuple(
        stack(sample_state, i) for i in range(5))
```

```python
import functools
import math

import jax
import jax.numpy as jnp
from jax import lax
from jax.experimental import pallas as pl
from jax.experimental.pallas import tpu as pltpu

F32 = jnp.float32
BF16 = jnp.bfloat16

D_MODEL = 1024
N_ATTN_HEADS = 8
ATTN_HEAD_DIM = 64
D_ATTN = N_ATTN_HEADS * ATTN_HEAD_DIM
ATTN_SCALE = ATTN_HEAD_DIM ** -0.5
D_RNN = 512
N_RNN_BLOCKS = 8
RNN_BLOCK = D_RNN // N_RNN_BLOCKS
CONV_WIDTH = 4
LRU_C = 8.0
N_KEYS = 128
N_EXPERTS = N_KEYS * N_KEYS
PEER_HEADS = 8
PEER_TOPK = 16
PEER_HALF = 128
LN_EPS = 1e-5

LANES = 128
SUBLANES = 8
BF16_ROWS = 2 * SUBLANES
VMEM_LIMIT_BYTES = 56 * 1024 * 1024

HEADS_PER_GROUP = 4
GROUP_LANES = HEADS_PER_GROUP * ATTN_HEAD_DIM
N_HEAD_GROUPS = N_ATTN_HEADS // HEADS_PER_GROUP
ATTN_Q_BLOCK = 256
ATTN_K_BLOCK = 512
ATTN_ROW_BLOCK = 32
F_PAD = LANES
NEG_BIG = -1e30
LOG2_E = math.log2(math.e)
NOT_SELECTED_RANK = 99.0

_C_Q = 0
_C_K = _C_Q + D_ATTN
_C_V = _C_K + D_ATTN
_C_F = _C_V + D_ATTN
_C_XR = _C_F + F_PAD
_C_GATE = _C_XR + D_RNN
_C_GA = _C_GATE + D_RNN
_C_GR = _C_GA + D_MODEL
_C_END = _C_GR + D_MODEL


def _params(*sem):
    return pltpu.CompilerParams(dimension_semantics=sem, vmem_limit_bytes=VMEM_LIMIT_BYTES)


def _sigmoid(x):
    return 1.0 / (1.0 + jnp.exp(-x))


def _gelu_tanh(x):
    half = 0.5 * x
    return half + half * jnp.tanh(x * (0.7978845608028654 + 0.035677408136300125 * (x * x)))


def _softplus(x):
    return jnp.maximum(x, 0.0) + jnp.log1p(jnp.exp(-jnp.abs(x)))


def _layer_norm(x, g, b):
    mu = jnp.mean(x, axis=-1, keepdims=True)
    xc = x - mu
    var = jnp.mean(xc * xc, axis=-1, keepdims=True)
    return xc * lax.rsqrt(var + LN_EPS) * g + b


def _inproj_kernel(x_ref, w_ref, bf_ref, q_ref, k_ref, v_ref, kb_ref, vb_ref, lf_ref, xr_ref,
                   gg_ref, sa_ref, sr_ref):
    xb = x_ref[...].astype(BF16)

    def mm(lo, hi):
        return jnp.dot(xb, w_ref[:, lo:hi], preferred_element_type=F32)

    q_ref[...] = (mm(_C_Q, _C_K) * (ATTN_SCALE * LOG2_E)).astype(BF16)
    k = mm(_C_K, _C_V)
    k_ref[...] = k
    kb_ref[...] = k.astype(BF16)
    v = mm(_C_V, _C_F)
    v_ref[...] = v
    vb_ref[...] = v.astype(BF16)
    f = mm(_C_F, _C_XR)[:, :N_ATTN_HEADS] + bf_ref[...]
    lf_ref[...] = -_softplus(-f)
    xr_ref[...] = mm(_C_XR, _C_GATE)
    gg_ref[...] = _gelu_tanh(mm(_C_GATE, _C_GA)).astype(BF16)
    sa_ref[...] = _sigmoid(mm(_C_GA, _C_GR)).astype(BF16)
    sr_ref[...] = _sigmoid(mm(_C_GR, _C_END)).astype(BF16)


def _inproj(x2, w_in_p, b_forget, tm):
    n = x2.shape[0]
    row = lambda w: pl.BlockSpec((tm, w), lambda i: (i, 0))
    full = lambda a: pl.BlockSpec(a.shape, lambda i: (0,) * a.ndim)
    out_shape = (
        jax.ShapeDtypeStruct((n, D_ATTN), BF16),
        jax.ShapeDtypeStruct((n, D_ATTN), F32),
        jax.ShapeDtypeStruct((n, D_ATTN), F32),
        jax.ShapeDtypeStruct((n, D_ATTN), BF16),
        jax.ShapeDtypeStruct((n, D_ATTN), BF16),
        jax.ShapeDtypeStruct((n, N_ATTN_HEADS), F32),
        jax.ShapeDtypeStruct((n, D_RNN), F32),
        jax.ShapeDtypeStruct((n, D_RNN), BF16),
        jax.ShapeDtypeStruct((n, D_MODEL), BF16),
        jax.ShapeDtypeStruct((n, D_MODEL), BF16),
    )
    out_specs = (row(D_ATTN), row(D_ATTN), row(D_ATTN), row(D_ATTN), row(D_ATTN), row(N_ATTN_HEADS),
                 row(D_RNN), row(D_RNN), row(D_MODEL), row(D_MODEL))
    return pl.pallas_call(
        _inproj_kernel,
        grid=(n // tm,),
        in_specs=[row(D_MODEL), full(w_in_p), full(b_forget)],
        out_specs=out_specs,
        out_shape=out_shape,
        compiler_params=_params("parallel"),
        name="inproj",
    )(x2, w_in_p, b_forget)


def _cumsum_kernel(x_ref, o_ref):
    x = x_ref[0]
    t = x.shape[1]
    lane = lax.broadcasted_iota(jnp.int32, x.shape, 1)
    d = 1
    while d < t:
        x = x + jnp.where(lane >= d, pltpu.roll(x, d, axis=1), 0.0)
        d *= 2
    o_ref[0] = x * LOG2_E


def _cumsum_time(lf_t):
    b, h, t = lf_t.shape
    spec = pl.BlockSpec((1, h, t), lambda i: (i, 0, 0))
    return pl.pallas_call(
        _cumsum_kernel, grid=(b,), in_specs=[spec], out_specs=spec,
        out_shape=jax.ShapeDtypeStruct(lf_t.shape, F32),
        compiler_params=_params("parallel"), name="logf_cumsum",
    )(lf_t)


def _attn_kernel(q_ref, k_ref, v_ref, fq_ref, fk_ref, o_ref, q_sc, fq_sc, s_sc, p_sc, acc_sc, m_sc, al_sc,
                 lp_sc, *, tq, tk, q_off):
    qi = pl.program_id(2)
    row0 = q_off + qi * tq
    n_full = (row0 + 1) // tk
    heads = range(HEADS_PER_GROUP)
    rb = min(ATTN_ROW_BLOCK, tq)
    lane = lax.broadcasted_iota(jnp.int32, (tq, GROUP_LANES), 1)
    in_head = [(lane >= j * ATTN_HEAD_DIM) & (lane < (j + 1) * ATTN_HEAD_DIM) for j in heads]
    q = q_ref[0]
    for j in heads:
        q_sc[j * tq:(j + 1) * tq, :] = jnp.where(in_head[j], q, jnp.zeros_like(q))
        fq_sc[j * tq:(j + 1) * tq, :] = jnp.broadcast_to(fq_ref[0, 0, :, j:j + 1], (tq, LANES))
    m_sc[...] = jnp.full_like(m_sc, NEG_BIG)
    lp_sc[...] = jnp.zeros_like(lp_sc)
    acc_sc[...] = jnp.zeros_like(acc_sc)
    halves = [slice(0, 2 * tq), slice(2 * tq, 4 * tq)]
    n_lane_tiles = tk // LANES

    def absorb(c, masked):
        start = pl.multiple_of(c * tk, tk)
        ks = k_ref[0, pl.ds(start, tk), :]
        for hv in halves:
            s_sc[hv, :] = lax.dot_general(q_sc[hv, :], ks, (((1,), (1,)), ((), ())),
                                          preferred_element_type=F32)
        fk = fk_ref[0, 0, c]
        for j in heads:
            for r in range(tq // rb):
                rows = slice(j * tq + r * rb, j * tq + (r + 1) * rb)
                s = s_sc[rows, :] - fk[j:j + 1, :]
                if masked:
                    q_pos = row0 + r * rb + lax.broadcasted_iota(jnp.int32, (rb, tk), 0)
                    k_pos = c * tk + lax.broadcasted_iota(jnp.int32, (rb, tk), 1)
                    s = jnp.where(k_pos <= q_pos, s, NEG_BIG)
                tiles = [s[:, t * LANES:(t + 1) * LANES] for t in range(n_lane_tiles)]
                row_max = jnp.max(functools.reduce(jnp.maximum, tiles), axis=1, keepdims=True)
                fq = fq_sc[rows, :]
                m_old = m_sc[rows, :]
                m_new = jnp.maximum(m_old, jnp.broadcast_to(row_max, (rb, LANES)) + fq)
                alpha = jnp.exp2(m_old - m_new)
                shift = fq - m_new
                p_tiles = [jnp.exp2(tile + shift) for tile in tiles]
                m_sc[rows, :] = m_new
                al_sc[rows, :] = alpha
                lp_sc[rows, :] = alpha * lp_sc[rows, :] + functools.reduce(jnp.add, p_tiles)
                p_sc[rows, :] = jnp.concatenate(p_tiles, axis=1).astype(BF16)
        vs = v_ref[0, pl.ds(start, tk), :]
        for hv in halves:
            pv = jnp.dot(p_sc[hv, :], vs, preferred_element_type=F32)
            alpha = al_sc[hv, :]
            acc_sc[hv, :] = jnp.concatenate([alpha] * (GROUP_LANES // LANES), axis=1) * acc_sc[hv, :] + pv

    def step(c, carry):
        absorb(c, masked=False)
        return carry

    lax.fori_loop(0, n_full, step, 0)
    absorb(n_full, masked=True)
    o_all = acc_sc[...] / jnp.sum(lp_sc[...], axis=1, keepdims=True)
    out = jnp.zeros((tq, GROUP_LANES), F32)
    for j in heads:
        out = jnp.where(in_head[j], o_all[j * tq:(j + 1) * tq], out)
    o_ref[0] = out.astype(BF16)


def _attention(q, k_all, v_all, fq, fk, *, tq, tk, q_off):
    b, t, _ = q.shape
    t_k = k_all.shape[1]
    n_kv = t_k // tk
    assert all((q_off + i * tq) % tk + tq <= tk for i in range(t // tq)), "a query block straddles key chunks"
    kern = functools.partial(_attn_kernel, tq=tq, tk=tk, q_off=q_off)
    rows = HEADS_PER_GROUP * tq
    return pl.pallas_call(
        kern,
        grid=(b, N_HEAD_GROUPS, t // tq),
        in_specs=[
            pl.BlockSpec((1, tq, GROUP_LANES), lambda bi, g, i: (bi, i, g)),
            pl.BlockSpec((1, t_k, GROUP_LANES), lambda bi, g, i: (bi, 0, g)),
            pl.BlockSpec((1, t_k, GROUP_LANES), lambda bi, g, i: (bi, 0, g)),
            pl.BlockSpec((1, 1, tq, HEADS_PER_GROUP), lambda bi, g, i: (bi, g, i, 0)),
            pl.BlockSpec((1, 1, n_kv, HEADS_PER_GROUP, tk), lambda bi, g, i: (bi, g, 0, 0, 0)),
        ],
        out_specs=pl.BlockSpec((1, tq, GROUP_LANES), lambda bi, g, i: (bi, i, g)),
        out_shape=jax.ShapeDtypeStruct((b, t, D_ATTN), BF16),
        scratch_shapes=[
            pltpu.VMEM((rows, GROUP_LANES), BF16),
            pltpu.VMEM((rows, LANES), F32),
            pltpu.VMEM((rows, tk), F32),
            pltpu.VMEM((rows, tk), BF16),
            pltpu.VMEM((rows, GROUP_LANES), F32),
            pltpu.VMEM((rows, LANES), F32),
            pltpu.VMEM((rows, LANES), F32),
            pltpu.VMEM((rows, LANES), F32),
        ],
        compiler_params=_params("parallel", "parallel", "arbitrary"),
        name="fox_attention",
    )(q, k_all, v_all, fq, fk)


def _rnn_kernel(xr_ref, gg_ref, hist_ref, h0_ref, cw_ref, cb_ref, wa_ref, ba_ref, wx_ref, bx_ref,
                lam_ref, out_ref, hl_ref, prev_sc, h_sc, *, tb):
    t = pl.program_id(1)

    @pl.when(t == 0)
    def _():
        prev_sc[...] = hist_ref[0]
        h_sc[...] = h0_ref[0]

    x = xr_ref[0]
    prev = prev_sc[...]
    row8 = lax.broadcasted_iota(jnp.int32, (SUBLANES, D_RNN), 0)
    xc = x * cw_ref[CONV_WIDTH - 1:CONV_WIDTH, :] + cb_ref[...]
    for s in range(1, CONV_WIDTH):
        xs = pltpu.roll(x, s, axis=0)
        head = jnp.where(row8 < s, pltpu.roll(prev, s, axis=0), xs[:SUBLANES])
        xs = jnp.concatenate([head, xs[SUBLANES:]], axis=0)
        xc = xc + xs * cw_ref[CONV_WIDTH - 1 - s:CONV_WIDTH - s, :]
    prev_sc[...] = x[tb - SUBLANES:tb]

    xcb = xc.astype(BF16)
    r = _sigmoid(jnp.dot(xcb, wa_ref[...], preferred_element_type=F32) + ba_ref[...])
    ig = _sigmoid(jnp.dot(xcb, wx_ref[...], preferred_element_type=F32) + bx_ref[...])
    log_a = (-LRU_C) * r * _softplus(-lam_ref[...])
    a = jnp.exp(log_a)
    bterm = jnp.sqrt(1.0 - a * a) * ig * xc

    row = lax.broadcasted_iota(jnp.int32, (tb, D_RNN), 0)
    d = 1
    while d < tb:
        valid = row >= d
        a_s = pltpu.roll(a, d, axis=0)
        b_s = pltpu.roll(bterm, d, axis=0)
        bterm = jnp.where(valid, a * b_s + bterm, bterm)
        a = jnp.where(valid, a * a_s, a)
        d *= 2
    h = bterm + a * h_sc[...]
    h_last = h[tb - 1:tb]
    h_sc[...] = h_last
    hl_ref[0] = h_last
    out_ref[0] = (h * gg_ref[0].astype(F32)).astype(BF16)


def _rnn(xr, gg, hist8, h0, conv_w, conv_b, wa, ba, wx, bx, lam, tb):
    b, t, _ = xr.shape
    blk = pl.BlockSpec((1, tb, D_RNN), lambda bi, ti: (bi, ti, 0))
    full = lambda a: pl.BlockSpec(a.shape, lambda bi, ti: (0,) * a.ndim)
    per_b = lambda r: pl.BlockSpec((1, r, D_RNN), lambda bi, ti: (bi, 0, 0))
    return pl.pallas_call(
        functools.partial(_rnn_kernel, tb=tb),
        grid=(b, t // tb),
        in_specs=[blk, blk, per_b(SUBLANES), per_b(1), full(conv_w), full(conv_b), full(wa), full(ba),
                  full(wx), full(bx), full(lam)],
        out_specs=(blk, per_b(1)),
        out_shape=(jax.ShapeDtypeStruct((b, t, D_RNN), BF16), jax.ShapeDtypeStruct((b, 1, D_RNN), F32)),
        scratch_shapes=[pltpu.VMEM((SUBLANES, D_RNN), F32), pltpu.VMEM((1, D_RNN), F32)],
        compiler_params=_params("parallel", "arbitrary"),
        name="conv_rglru",
    )(xr, gg, hist8, h0, conv_w, conv_b, wa, ba, wx, bx, lam)


def _outproj_kernel(o_ref, r_ref, sa_ref, sr_ref, x_ref, wau_ref, wru_ref, wo_ref, g_ref, b_ref,
                    h_ref, *, alpha):
    up_a = jnp.dot(o_ref[...], wau_ref[...], preferred_element_type=F32)
    up_r = jnp.dot(r_ref[...], wru_ref[...], preferred_element_type=F32)
    merged = sa_ref[...].astype(F32) * up_a + sr_ref[...].astype(F32) * up_r
    mix = jnp.dot(merged.astype(BF16), wo_ref[...], preferred_element_type=F32)
    h_ref[...] = _layer_norm(alpha * x_ref[...] + mix, g_ref[...], b_ref[...])


def _outproj(o2, r2, sa, sr, x2, wau, wru, wo, g, bta, alpha, tm):
    n = x2.shape[0]
    row = lambda w: pl.BlockSpec((tm, w), lambda i: (i, 0))
    full = lambda a: pl.BlockSpec(a.shape, lambda i: (0,) * a.ndim)
    return pl.pallas_call(
        functools.partial(_outproj_kernel, alpha=alpha),
        grid=(n // tm,),
        in_specs=[row(D_ATTN), row(D_RNN), row(D_MODEL), row(D_MODEL), row(D_MODEL), full(wau), full(wru),
                  full(wo), full(g), full(bta)],
        out_specs=row(D_MODEL),
        out_shape=jax.ShapeDtypeStruct((n, D_MODEL), F32),
        compiler_params=_params("parallel"),
        name="outproj_ln1",
    )(o2, r2, sa, sr, x2, wau, wru, wo, g, bta)


def _odd_even_merge_sort(n):
    def merge(lo, hi, r):
        step = r * 2
        if step < hi - lo:
            yield from merge(lo, hi, step)
            yield from merge(lo + r, hi, step)
            yield from ((i, i + r) for i in range(lo + r, hi - r, step))
        else:
            yield (lo, lo + r)

    def sort(lo, hi):
        if hi - lo >= 1:
            mid = lo + (hi - lo) // 2
            yield from sort(lo, mid)
            yield from sort(mid + 1, hi)
            yield from merge(lo, hi, 1)

    return tuple(sort(0, n - 1))


_SORT_TOPK = _odd_even_merge_sort(PEER_TOPK)


def _compare_exchange(v, i, j):
    v[i], v[j] = jnp.maximum(v[i], v[j]), jnp.minimum(v[i], v[j])


def _top_k_sorted(tiles):
    v = list(tiles)
    for i, j in _SORT_TOPK:
        _compare_exchange(v, i, j)
    shift = SUBLANES // 2
    while shift >= 1:
        other = [pltpu.roll(x, shift, axis=0) for x in v]
        v = [jnp.maximum(v[i], other[PEER_TOPK - 1 - i]) for i in range(PEER_TOPK)]
        d = PEER_TOPK // 2
        while d >= 1:
            for i in range(PEER_TOPK):
                if i & d == 0:
                    _compare_exchange(v, i, i + d)
            d //= 2
        shift //= 2
    return v


def _count_greater(x, t):
    assert len(t) == 16, "the bisection below is written out for 16 entries"
    one = lambda m, w: jnp.where(m, float(w), 0.0)
    b3 = t[7] > x
    b2 = jnp.where(b3, t[11], t[3]) > x
    b1 = jnp.where(b3, jnp.where(b2, t[13], t[9]), jnp.where(b2, t[5], t[1])) > x
    hi = jnp.where(b2, jnp.where(b1, t[14], t[12]), jnp.where(b1, t[10], t[8]))
    lo = jnp.where(b2, jnp.where(b1, t[6], t[4]), jnp.where(b1, t[2], t[0]))
    b0 = jnp.where(b3, hi, lo) > x
    count = one(b3, 8) + one(b2, 4) + one(b1, 2) + one(b0, 1)
    return jnp.where(t[15] > x, float(PEER_TOPK), count)


def _route_kernel(h_ref, wq_ref, k1_ref, k2_ref, ht_ref, cnt_ref, e1_ref, r2_ref, e2_ref,
                  qt_sc, t1_sc, t2_sc, *, tm):
    ht = jnp.transpose(h_ref[...]).astype(BF16)
    ht_ref[...] = ht
    qt_sc[...] = jnp.dot(wq_ref[...], ht, preferred_element_type=F32).astype(BF16)
    key_iota = lax.broadcasted_iota(jnp.int32, (N_KEYS, tm), 0).astype(F32)
    top_iota = lax.broadcasted_iota(jnp.int32, (PEER_TOPK, tm), 0).astype(F32)
    front_rows = SUBLANES

    def scores(hd):
        base = hd * 2 * PEER_HALF
        s1 = jnp.dot(k1_ref[...], qt_sc[base:base + PEER_HALF], preferred_element_type=F32)
        s2 = jnp.dot(k2_ref[...], qt_sc[base + PEER_HALF:base + 2 * PEER_HALF], preferred_element_type=F32)
        return s1, s2

    sub = lax.broadcasted_iota(jnp.int32, (SUBLANES, tm), 0)
    tiles_of = lambda s: [s[r * SUBLANES:(r + 1) * SUBLANES] for r in range(N_KEYS // SUBLANES)]
    sublane_sum = lambda x: jnp.sum(x, axis=0, keepdims=True)

    def route_head_sorted(hd):
        s1, s2 = scores(hd)
        rows1, rows2 = tiles_of(s1), tiles_of(s2)
        t1 = _top_k_sorted(rows1)
        t2 = _top_k_sorted(rows2)
        t1_lo = t1[SUBLANES - 1]
        for a in range(SUBLANES - 2, -1, -1):
            t1_lo = jnp.where(sub == a, t1[a], t1_lo)
        cand = [t1_lo + t2[b] for b in range(PEER_TOPK)]
        pool = list(cand)
        for a in range(SUBLANES, PEER_TOPK):
            pool[a] = jnp.where(sub == 1, t1[a] + t2[0], cand[a])
        top = _top_k_sorted(pool)
        tau = top[PEER_TOPK - 1]
        cnt_lo = functools.reduce(jnp.add, [jnp.where(c >= tau, 1.0, 0.0) for c in cand])
        cnt = [jnp.broadcast_to(cnt_lo[a:a + 1], (SUBLANES, tm)) for a in range(SUBLANES)]
        cnt += [jnp.where(t1[a] + t2[0] >= tau, 1.0, 0.0) for a in range(SUBLANES, PEER_TOPK)]
        z = functools.reduce(jnp.add, [jnp.exp(t - top[0]) for t in top])

        cnt1_rows, rank2_rows = [], []
        for r in range(N_KEYS // SUBLANES):
            c = jnp.zeros((SUBLANES, tm), F32)
            for a in range(PEER_TOPK):
                c = jnp.where(rows1[r] == t1[a], cnt[a], c)
            cnt1_rows.append(c)
            g = _count_greater(rows2[r], t2)
            rank2_rows.append(jnp.where(g < float(PEER_TOPK), g, NOT_SELECTED_RANK))
        cnt1 = jnp.concatenate(cnt1_rows, axis=0)
        rank2 = jnp.concatenate(rank2_rows, axis=0)
        cnt_ref[hd] = cnt1
        e1_ref[hd] = jnp.exp(s1 - t1[0][0:1]) / z[0:1]
        r2_ref[hd] = rank2.astype(BF16)
        e2_ref[hd] = jnp.exp(s2 - t2[0][0:1]).astype(BF16)

        n_sel = sublane_sum(functools.reduce(jnp.add, cnt1_rows))
        n_rank = sublane_sum(functools.reduce(
            jnp.add, [jnp.where(x < float(PEER_TOPK), 1.0, 0.0) for x in rank2_rows]))
        gap2 = functools.reduce(jnp.minimum, [t2[b] - t2[b + 1] for b in range(PEER_TOPK - 1)])[0:1]
        return (jnp.abs(n_sel - float(PEER_TOPK)) + jnp.abs(n_rank - float(PEER_TOPK))
                + jnp.where(gap2 > 0.0, 0.0, 1.0))

    def route_head_exact(hd):
        def pick_one(v, iota, n):
            m = jnp.max(v, axis=0, keepdims=True)
            return m, iota == jnp.min(jnp.where(v == m, iota, float(n)), axis=0, keepdims=True)

        s1, s2 = scores(hd)

        def extract(a, carry):
            v1, r1, v2, r2 = carry
            m1, sel1 = pick_one(v1, key_iota, N_KEYS)
            m2, sel2 = pick_one(v2, key_iota, N_KEYS)
            t1_sc[pl.ds(a, 1), :] = m1
            t2_sc[pl.ds(a, 1), :] = m2
            af = jnp.asarray(a, dtype=F32)
            return (jnp.where(sel1, -jnp.inf, v1), jnp.where(sel1, af, r1),
                    jnp.where(sel2, -jnp.inf, v2), jnp.where(sel2, af, r2))

        no_rank = jnp.full((N_KEYS, tm), NOT_SELECTED_RANK, F32)
        _, rank1, _, rank2 = lax.fori_loop(0, PEER_TOPK, extract, (s1, no_rank, s2, no_rank))
        t1 = t1_sc[...]
        t2 = t2_sc[...]
        top0 = t1[0:1] + t2[0:1]

        def pick(_, carry):
            ptr, front, z = carry
            m, sel = pick_one(front, top_iota, PEER_TOPK)
            ptr = ptr + jnp.where(sel, 1.0, 0.0)
            lo = ptr[:front_rows]
            nxt = jnp.full(lo.shape, -jnp.inf, F32)
            for b in range(1, PEER_TOPK):
                nxt = jnp.where(lo == float(b), t2_sc[b:b + 1, :], nxt)
            nxt = jnp.where(lo == 0.0, t2[0:1], nxt)
            front = jnp.concatenate(
                [t1[:front_rows] + nxt, jnp.where(sel[front_rows:], -jnp.inf, front[front_rows:])], axis=0)
            return ptr, front, z + jnp.exp(m - top0)

        cnt, _, z = lax.fori_loop(
            0, PEER_TOPK, pick,
            (jnp.zeros((PEER_TOPK, tm), F32), t1 + t2[0:1], jnp.zeros((1, tm), F32)))

        cnt1 = jnp.zeros((N_KEYS, tm), F32)
        for a in range(PEER_TOPK):
            cnt1 = jnp.where(rank1 == float(a), cnt[a:a + 1], cnt1)
        cnt_ref[hd] = cnt1
        e1_ref[hd] = jnp.exp(s1 - t1[0:1]) / z
        r2_ref[hd] = rank2.astype(BF16)
        e2_ref[hd] = jnp.exp(s2 - t2[0:1]).astype(BF16)

    doubt = [route_head_sorted(hd) for hd in range(PEER_HEADS)]

    @pl.when(jnp.max(functools.reduce(jnp.maximum, doubt)) > 0.0)
    def _():
        for hd in range(PEER_HEADS):
            @pl.when(jnp.max(doubt[hd]) > 0.0)
            def _():
                route_head_exact(hd)


def _route(h2, wq_t, k1, k2, tm):
    n = h2.shape[0]
    full = lambda a: pl.BlockSpec(a.shape, lambda i: (0,) * a.ndim)
    per_head = pl.BlockSpec((PEER_HEADS, N_KEYS, tm), lambda i: (0, 0, i))
    hshape = lambda dt: jax.ShapeDtypeStruct((PEER_HEADS, N_KEYS, n), dt)
    return pl.pallas_call(
        functools.partial(_route_kernel, tm=tm),
        grid=(n // tm,),
        in_specs=[pl.BlockSpec((tm, D_MODEL), lambda i: (i, 0)), full(wq_t), full(k1), full(k2)],
        out_specs=(pl.BlockSpec((D_MODEL, tm), lambda i: (0, i)), per_head, per_head, per_head, per_head),
        out_shape=(jax.ShapeDtypeStruct((D_MODEL, n), BF16), hshape(F32), hshape(F32), hshape(BF16),
                   hshape(BF16)),
        scratch_shapes=[pltpu.VMEM((PEER_HEADS * 2 * PEER_HALF, tm), BF16),
                        pltpu.VMEM((PEER_TOPK, tm), F32), pltpu.VMEM((PEER_TOPK, tm), F32)],
        compiler_params=_params("parallel"),
        name="peer_route",
    )(h2, wq_t, k1, k2)


ROWS_PER_STEP = 16
EXPERTS_PER_STEP = ROWS_PER_STEP * N_KEYS
ROWS_PER_SUB = 4
EXPERTS_PER_SUB = ROWS_PER_SUB * N_KEYS
DENSE_CHUNK = 2 * LANES


def _bf16_row_tile(row):
    tile = jnp.broadcast_to(row, (BF16_ROWS, row.shape[1])).astype(BF16)
    return pltpu.repeat(tile, N_KEYS // BF16_ROWS, axis=0)


def _dense_kernel(ht_ref, u_ref, vt_ref, cnt_ref, e1_ref, r2_ref, e2_ref, h_ref, g_ref, b_ref,
                  y_ref, acc_sc, act_sc, coef_sc, *, alpha):
    e = pl.program_id(1)

    @pl.when(e == 0)
    def _():
        acc_sc[...] = jnp.zeros_like(acc_sc)

    tm = act_sc.shape[1]
    n_sub = ROWS_PER_STEP // ROWS_PER_SUB

    def activations(sub):
        rows = slice(sub * EXPERTS_PER_SUB, (sub + 1) * EXPERTS_PER_SUB)
        act_sc[rows, :] = jnp.dot(u_ref[rows, :], ht_ref[...], preferred_element_type=F32)

    def coefficients(sub):
        for il in range(sub * ROWS_PER_SUB, (sub + 1) * ROWS_PER_SUB):
            rows = slice(il * N_KEYS, (il + 1) * N_KEYS)
            chunk = min(DENSE_CHUNK, tm)
            for c in range(tm // chunk):
                lanes = slice(c * chunk, (c + 1) * chunk)
                w = None
                for hd in range(PEER_HEADS):
                    cnt = _bf16_row_tile(cnt_ref[hd, il:il + 1, lanes])
                    e1 = _bf16_row_tile(e1_ref[hd, il:il + 1, lanes])
                    term = jnp.where(r2_ref[hd, :, lanes] < cnt, e2_ref[hd, :, lanes] * e1,
                                     jnp.zeros((), BF16))
                    w = term if w is None else w + term
                coef_sc[rows, lanes] = w * _gelu_tanh(act_sc[rows, lanes]).astype(BF16)

    def mix(sub):
        rows = slice(sub * EXPERTS_PER_SUB, (sub + 1) * EXPERTS_PER_SUB)
        acc_sc[...] += jnp.dot(vt_ref[:, rows], coef_sc[rows, :], preferred_element_type=F32)

    activations(0)
    for sub in range(n_sub):
        if sub + 1 < n_sub:
            activations(sub + 1)
        coefficients(sub)
        mix(sub)

    @pl.when(e == pl.num_programs(1) - 1)
    def _():
        peer = jnp.transpose(acc_sc[...])
        y_ref[...] = _layer_norm(alpha * h_ref[...] + peer, g_ref[...], b_ref[...])


def _dense(ht, u_b, vt_b, cnt1, e1n, rank2, e2, h2, g, bta, alpha, tm):
    n = h2.shape[0]
    n_e = N_EXPERTS // EXPERTS_PER_STEP
    full = lambda a: pl.BlockSpec(a.shape, lambda t, e: (0,) * a.ndim)
    rows_blk = pl.BlockSpec((PEER_HEADS, ROWS_PER_STEP, tm), lambda t, e: (0, e, t))
    cols_blk = pl.BlockSpec((PEER_HEADS, N_KEYS, tm), lambda t, e: (0, 0, t))
    return pl.pallas_call(
        functools.partial(_dense_kernel, alpha=alpha),
        grid=(n // tm, n_e),
        in_specs=[
            pl.BlockSpec((D_MODEL, tm), lambda t, e: (0, t)),
            pl.BlockSpec((EXPERTS_PER_STEP, D_MODEL), lambda t, e: (e, 0)),
            pl.BlockSpec((D_MODEL, EXPERTS_PER_STEP), lambda t, e: (0, e)),
            rows_blk, rows_blk, cols_blk, cols_blk,
            pl.BlockSpec((tm, D_MODEL), lambda t, e: (t, 0)),
            full(g), full(bta),
        ],
        out_specs=pl.BlockSpec((tm, D_MODEL), lambda t, e: (t, 0)),
        out_shape=jax.ShapeDtypeStruct((n, D_MODEL), F32),
        scratch_shapes=[pltpu.VMEM((D_MODEL, tm), F32), pltpu.VMEM((EXPERTS_PER_STEP, tm), F32),
                        pltpu.VMEM((EXPERTS_PER_STEP, tm), BF16)],
        compiler_params=_params("parallel", "arbitrary"),
        name="peer_dense",
    )(ht, u_b, vt_b, cnt1, e1n, rank2, e2, h2, g, bta)


def _block_diag(w):
    nb, bi, bo = w.shape
    eye = jnp.eye(nb, dtype=w.dtype)
    return (eye[:, None, :, None] * w[:, :, None, :]).reshape(nb * bi, nb * bo)


def _prep_weights(w_in, b_forget, conv_w, conv_b, w_rg_a, b_rg_a, w_rg_x, b_rg_x, lru_lambda,
                  w_attn_up, w_rnn_up, w_out, ln1_g, ln1_b, peer_w_query, peer_keys_1, peer_keys_2,
                  peer_u, peer_v, ln2_g, ln2_b):
    c_f = 3 * D_ATTN
    w_in_p = jnp.concatenate(
        [w_in[:, :c_f], jnp.pad(w_in[:, c_f:c_f + N_ATTN_HEADS], ((0, 0), (0, F_PAD - N_ATTN_HEADS))),
         w_in[:, c_f + N_ATTN_HEADS:]], axis=1).astype(BF16)
    row = lambda a: a.reshape(1, -1).astype(F32)
    return dict(
        w_in=w_in_p, b_forget=row(b_forget), conv_w=conv_w.astype(F32), conv_b=row(conv_b),
        wa=_block_diag(w_rg_a).astype(BF16), ba=row(b_rg_a), wx=_block_diag(w_rg_x).astype(BF16),
        bx=row(b_rg_x), lam=row(lru_lambda),
        wau=w_attn_up.astype(BF16), wru=w_rnn_up.astype(BF16), wo=w_out.astype(BF16),
        ln1_g=row(ln1_g), ln1_b=row(ln1_b),
        wq_t=jnp.transpose(peer_w_query).astype(BF16), k1=peer_keys_1.astype(BF16),
        k2=peer_keys_2.astype(BF16), u=peer_u.astype(BF16), vt=jnp.transpose(peer_v).astype(BF16),
        ln2_g=row(ln2_g), ln2_b=row(ln2_b),
    )


def _pick_block(n, target):
    blk = min(n, target)
    assert n % blk == 0, (n, blk)
    return blk


def _trunk_layer(x, past_k, past_v, past_logf, conv_hist, h0, p, alpha):
    bsz, t, _ = x.shape
    n = bsz * t
    x2 = x.reshape(n, D_MODEL)
    q, k, v, kb, vb, lf, xr, gg, sa, sr = _inproj(x2, p["w_in"], p["b_forget"], _pick_block(n, 256))

    tq = _pick_block(t, ATTN_Q_BLOCK)
    n_past = 0 if past_k is None else past_k.shape[1]
    t_all = n_past + t
    t_lanes = -(-t_all // LANES) * LANES
    tk = ATTN_K_BLOCK if t_lanes % ATTN_K_BLOCK == 0 else t_lanes
    t_pad = -(-t_all // tk) * tk
    lf3 = lf.reshape(bsz, t, N_ATTN_HEADS)
    kb3 = kb.reshape(bsz, t, D_ATTN)
    vb3 = vb.reshape(bsz, t, D_ATTN)
    if past_k is not None:
        lf3 = jnp.concatenate([past_logf.astype(F32), lf3], axis=1)
        kb3 = jnp.concatenate([past_k.reshape(bsz, n_past, D_ATTN).astype(BF16), kb3], axis=1)
        vb3 = jnp.concatenate([past_v.reshape(bsz, n_past, D_ATTN).astype(BF16), vb3], axis=1)
    pad = ((0, 0), (0, t_pad - t_all), (0, 0))
    lf3, kb3, vb3 = jnp.pad(lf3, pad), jnp.pad(kb3, pad), jnp.pad(vb3, pad)
    f_t = _cumsum_time(jnp.transpose(lf3, (0, 2, 1)))
    f_g = f_t.reshape(bsz, N_HEAD_GROUPS, HEADS_PER_GROUP, t_pad)
    fq = jnp.transpose(f_g[:, :, :, n_past:n_past + t], (0, 1, 3, 2))
    fk = jnp.transpose(f_g.reshape(bsz, N_HEAD_GROUPS, HEADS_PER_GROUP, t_pad // tk, tk), (0, 1, 3, 2, 4))
    o = _attention(q.reshape(bsz, t, D_ATTN), kb3, vb3, fq, fk, tq=tq, tk=tk, q_off=n_past)

    hist8 = jnp.pad(conv_hist.astype(F32), ((0, 0), (SUBLANES - (CONV_WIDTH - 1), 0), (0, 0)))
    xr3 = xr.reshape(bsz, t, D_RNN)
    rnn_out, h_last = _rnn(xr3, gg.reshape(bsz, t, D_RNN), hist8, h0.astype(F32).reshape(bsz, 1, D_RNN),
                           p["conv_w"], p["conv_b"], p["wa"], p["ba"], p["wx"], p["bx"], p["lam"],
                           _pick_block(t, 256))
    new_hist = jnp.concatenate([conv_hist.astype(F32), xr3], axis=1)[:, -(CONV_WIDTH - 1):]

    h = _outproj(o.reshape(n, D_ATTN), rnn_out.reshape(n, D_RNN), sa, sr, x2, p["wau"], p["wru"], p["wo"],
                 p["ln1_g"], p["ln1_b"], alpha, _pick_block(n, 256))
    ht, cnt1, e1n, rank2, e2 = _route(h, p["wq_t"], p["k1"], p["k2"], LANES)
    y = _dense(ht, p["u"], p["vt"], cnt1, e1n, rank2, e2, h, p["ln2_g"], p["ln2_b"], alpha,
               _pick_block(n, 512))
    return (y.reshape(bsz, t, D_MODEL), k.reshape(bsz, t, N_ATTN_HEADS, ATTN_HEAD_DIM),
            v.reshape(bsz, t, N_ATTN_HEADS, ATTN_HEAD_DIM), lf.reshape(bsz, t, N_ATTN_HEADS), new_hist,
            h_last.reshape(bsz, D_RNN))


def kernel(x_prompt, x_sample, cache_k, cache_v, cache_logf, state_conv, state_rnn, w_in, b_forget, conv_w, conv_b, w_rg_a, b_rg_a, w_rg_x, b_rg_x, lru_lambda, w_attn_up, w_rnn_up, w_out, ln1_g, ln1_b, peer_w_query, peer_keys_1, peer_keys_2, peer_u, peer_v, ln2_g, ln2_b):
    depth = w_in.shape[0]
    alpha = (2 * depth) ** 0.25
    layer_weights = (w_in, b_forget, conv_w, conv_b, w_rg_a, b_rg_a, w_rg_x, b_rg_x, lru_lambda, w_attn_up,
                     w_rnn_up, w_out, ln1_g, ln1_b, peer_w_query, peer_keys_1, peer_keys_2, peer_u, peer_v,
                     ln2_g, ln2_b)
    hp, hs = x_prompt, x_sample
    prompt_state, sample_state = [], []
    for l in range(depth):
        p = _prep_weights(*(w[l] for w in layer_weights))
        zero_hist = jnp.zeros((hp.shape[0], CONV_WIDTH - 1, D_RNN), F32)
        zero_h = jnp.zeros((hp.shape[0], D_RNN), F32)
        hp, *st_p = _trunk_layer(hp, None, None, None, zero_hist, zero_h, p, alpha)
        hs, *st_s = _trunk_layer(hs, cache_k[l], cache_v[l], cache_logf[l], state_conv[l], state_rnn[l], p,
                                 alpha)
        prompt_state.append(st_p)
        sample_state.append(st_s)
    stack = lambda states, i: jnp.stack([s[i] for s in states])
    return (hp, hs) + tuple(stack(prompt_state, i) for i in range(5)) + tuple(
        stack(sample_state, i) for i in range(5))
```

```python
import functools
import math

import jax
import jax.numpy as jnp
from jax import lax
from jax.experimental import pallas as pl
from jax.experimental.pallas import tpu as pltpu

F32 = jnp.float32
BF16 = jnp.bfloat16

D_MODEL = 1024
N_ATTN_HEADS = 8
ATTN_HEAD_DIM = 64
D_ATTN = N_ATTN_HEADS * ATTN_HEAD_DIM
ATTN_SCALE = ATTN_HEAD_DIM ** -0.5
D_RNN = 512
N_RNN_BLOCKS = 8
RNN_BLOCK = D_RNN // N_RNN_BLOCKS
CONV_WIDTH = 4
LRU_C = 8.0
N_KEYS = 128
N_EXPERTS = N_KEYS * N_KEYS
PEER_HEADS = 8
PEER_TOPK = 16
PEER_HALF = 128
LN_EPS = 1e-5

LANES = 128
SUBLANES = 8
BF16_ROWS = 2 * SUBLANES
VMEM_LIMIT_BYTES = 56 * 1024 * 1024

HEADS_PER_GROUP = 4
GROUP_LANES = HEADS_PER_GROUP * ATTN_HEAD_DIM
N_HEAD_GROUPS = N_ATTN_HEADS // HEADS_PER_GROUP
ATTN_Q_BLOCK = 512
ATTN_K_BLOCK = 512
ATTN_ROW_BLOCK = 32
F_PAD = LANES
NEG_BIG = -1e30
LOG2_E = math.log2(math.e)
NOT_SELECTED_RANK = 99.0

_C_Q = 0
_C_K = _C_Q + D_ATTN
_C_V = _C_K + D_ATTN
_C_F = _C_V + D_ATTN
_C_XR = _C_F + F_PAD
_C_GATE = _C_XR + D_RNN
_C_GA = _C_GATE + D_RNN
_C_GR = _C_GA + D_MODEL
_C_END = _C_GR + D_MODEL


def _params(*sem):
    return pltpu.CompilerParams(dimension_semantics=sem, vmem_limit_bytes=VMEM_LIMIT_BYTES)


def _sigmoid(x):
    return 1.0 / (1.0 + jnp.exp(-x))


def _gelu_tanh(x):
    half = 0.5 * x
    return half + half * jnp.tanh(x * (0.7978845608028654 + 0.035677408136300125 * (x * x)))


def _softplus(x):
    return jnp.maximum(x, 0.0) + jnp.log1p(jnp.exp(-jnp.abs(x)))


def _layer_norm(x, g, b):
    mu = jnp.mean(x, axis=-1, keepdims=True)
    xc = x - mu
    var = jnp.mean(xc * xc, axis=-1, keepdims=True)
    return xc * lax.rsqrt(var + LN_EPS) * g + b


def _inproj_kernel(x_ref, w_ref, bf_ref, q_ref, k_ref, v_ref, kb_ref, vb_ref, lf_ref, xr_ref,
                   gg_ref, sa_ref, sr_ref):
    xb = x_ref[...].astype(BF16)

    def mm(lo, hi):
        return jnp.dot(xb, w_ref[:, lo:hi], preferred_element_type=F32)

    q_ref[...] = (mm(_C_Q, _C_K) * (ATTN_SCALE * LOG2_E)).astype(BF16)
    k = mm(_C_K, _C_V)
    k_ref[...] = k
    kb_ref[...] = k.astype(BF16)
    v = mm(_C_V, _C_F)
    v_ref[...] = v
    vb_ref[...] = v.astype(BF16)
    f = mm(_C_F, _C_XR)[:, :N_ATTN_HEADS] + bf_ref[...]
    lf_ref[...] = -_softplus(-f)
    xr_ref[...] = mm(_C_XR, _C_GATE)
    gg_ref[...] = _gelu_tanh(mm(_C_GATE, _C_GA)).astype(BF16)
    sa_ref[...] = _sigmoid(mm(_C_GA, _C_GR)).astype(BF16)
    sr_ref[...] = _sigmoid(mm(_C_GR, _C_END)).astype(BF16)


def _inproj(x2, w_in_p, b_forget, tm):
    n = x2.shape[0]
    row = lambda w: pl.BlockSpec((tm, w), lambda i: (i, 0))
    full = lambda a: pl.BlockSpec(a.shape, lambda i: (0,) * a.ndim)
    out_shape = (
        jax.ShapeDtypeStruct((n, D_ATTN), BF16),
        jax.ShapeDtypeStruct((n, D_ATTN), F32),
        jax.ShapeDtypeStruct((n, D_ATTN), F32),
        jax.ShapeDtypeStruct((n, D_ATTN), BF16),
        jax.ShapeDtypeStruct((n, D_ATTN), BF16),
        jax.ShapeDtypeStruct((n, N_ATTN_HEADS), F32),
        jax.ShapeDtypeStruct((n, D_RNN), F32),
        jax.ShapeDtypeStruct((n, D_RNN), BF16),
        jax.ShapeDtypeStruct((n, D_MODEL), BF16),
        jax.ShapeDtypeStruct((n, D_MODEL), BF16),
    )
    out_specs = (row(D_ATTN), row(D_ATTN), row(D_ATTN), row(D_ATTN), row(D_ATTN), row(N_ATTN_HEADS),
                 row(D_RNN), row(D_RNN), row(D_MODEL), row(D_MODEL))
    return pl.pallas_call(
        _inproj_kernel,
        grid=(n // tm,),
        in_specs=[row(D_MODEL), full(w_in_p), full(b_forget)],
        out_specs=out_specs,
        out_shape=out_shape,
        compiler_params=_params("parallel"),
        name="inproj",
    )(x2, w_in_p, b_forget)


def _cumsum_kernel(x_ref, o_ref):
    x = x_ref[0]
    t = x.shape[1]
    lane = lax.broadcasted_iota(jnp.int32, x.shape, 1)
    d = 1
    while d < t:
        x = x + jnp.where(lane >= d, pltpu.roll(x, d, axis=1), 0.0)
        d *= 2
    o_ref[0] = x * LOG2_E


def _cumsum_time(lf_t):
    b, h, t = lf_t.shape
    spec = pl.BlockSpec((1, h, t), lambda i: (i, 0, 0))
    return pl.pallas_call(
        _cumsum_kernel, grid=(b,), in_specs=[spec], out_specs=spec,
        out_shape=jax.ShapeDtypeStruct(lf_t.shape, F32),
        compiler_params=_params("parallel"), name="logf_cumsum",
    )(lf_t)


def _attn_kernel(q_ref, k_ref, v_ref, fq_ref, fk_ref, o_ref, q_sc, fq_sc, s_sc, p_sc, acc_sc, m_sc, al_sc,
                 lp_sc, *, tq, tk, q_off):
    qi = pl.program_id(2)
    row0 = q_off + qi * tq
    n_full = (row0 + 1) // tk
    heads = range(HEADS_PER_GROUP)
    rb = min(ATTN_ROW_BLOCK, tq)
    lane = lax.broadcasted_iota(jnp.int32, (tq, GROUP_LANES), 1)
    in_head = [(lane >= j * ATTN_HEAD_DIM) & (lane < (j + 1) * ATTN_HEAD_DIM) for j in heads]
    q = q_ref[0]
    for j in heads:
        q_sc[j * tq:(j + 1) * tq, :] = jnp.where(in_head[j], q, jnp.zeros_like(q))
        fq_sc[j * tq:(j + 1) * tq, :] = jnp.broadcast_to(fq_ref[0, 0, :, j:j + 1], (tq, LANES))
    m_sc[...] = jnp.full_like(m_sc, NEG_BIG)
    lp_sc[...] = jnp.zeros_like(lp_sc)
    acc_sc[...] = jnp.zeros_like(acc_sc)
    halves = [slice(0, 2 * tq), slice(2 * tq, 4 * tq)]
    n_lane_tiles = tk // LANES

    def absorb(c, masked):
        start = pl.multiple_of(c * tk, tk)
        ks = k_ref[0, pl.ds(start, tk), :]
        for hv in halves:
            s_sc[hv, :] = lax.dot_general(q_sc[hv, :], ks, (((1,), (1,)), ((), ())),
                                          preferred_element_type=F32)
        fk = fk_ref[0, 0, c]
        for j in heads:
            for r in range(tq // rb):
                rows = slice(j * tq + r * rb, j * tq + (r + 1) * rb)
                s = s_sc[rows, :] - fk[j:j + 1, :]
                if masked:
                    q_pos = row0 + r * rb + lax.broadcasted_iota(jnp.int32, (rb, tk), 0)
                    k_pos = c * tk + lax.broadcasted_iota(jnp.int32, (rb, tk), 1)
                    s = jnp.where(k_pos <= q_pos, s, NEG_BIG)
                tiles = [s[:, t * LANES:(t + 1) * LANES] for t in range(n_lane_tiles)]
                row_max = jnp.max(functools.reduce(jnp.maximum, tiles), axis=1, keepdims=True)
                fq = fq_sc[rows, :]
                m_old = m_sc[rows, :]
                m_new = jnp.maximum(m_old, jnp.broadcast_to(row_max, (rb, LANES)) + fq)
                alpha = jnp.exp2(m_old - m_new)
                shift = fq - m_new
                p_tiles = [jnp.exp2(tile + shift) for tile in tiles]
                m_sc[rows, :] = m_new
                al_sc[rows, :] = alpha
                lp_sc[rows, :] = alpha * lp_sc[rows, :] + functools.reduce(jnp.add, p_tiles)
                p_sc[rows, :] = jnp.concatenate(p_tiles, axis=1).astype(BF16)
        vs = v_ref[0, pl.ds(start, tk), :]
        for hv in halves:
            pv = jnp.dot(p_sc[hv, :], vs, preferred_element_type=F32)
            alpha = al_sc[hv, :]
            acc_sc[hv, :] = jnp.concatenate([alpha] * (GROUP_LANES // LANES), axis=1) * acc_sc[hv, :] + pv

    def step(c, carry):
        absorb(c, masked=False)
        return carry

    lax.fori_loop(0, n_full, step, 0)
    absorb(n_full, masked=True)
    o_all = acc_sc[...] / jnp.sum(lp_sc[...], axis=1, keepdims=True)
    out = jnp.zeros((tq, GROUP_LANES), F32)
    for j in heads:
        out = jnp.where(in_head[j], o_all[j * tq:(j + 1) * tq], out)
    o_ref[0] = out.astype(BF16)


def _attention(q, k_all, v_all, fq, fk, *, tq, tk, q_off):
    b, t, _ = q.shape
    t_k = k_all.shape[1]
    n_kv = t_k // tk
    assert all((q_off + i * tq) % tk + tq <= tk for i in range(t // tq)), "a query block straddles key chunks"
    kern = functools.partial(_attn_kernel, tq=tq, tk=tk, q_off=q_off)
    rows = HEADS_PER_GROUP * tq
    return pl.pallas_call(
        kern,
        grid=(b, N_HEAD_GROUPS, t // tq),
        in_specs=[
            pl.BlockSpec((1, tq, GROUP_LANES), lambda bi, g, i: (bi, i, g)),
            pl.BlockSpec((1, t_k, GROUP_LANES), lambda bi, g, i: (bi, 0, g)),
            pl.BlockSpec((1, t_k, GROUP_LANES), lambda bi, g, i: (bi, 0, g)),
            pl.BlockSpec((1, 1, tq, HEADS_PER_GROUP), lambda bi, g, i: (bi, g, i, 0)),
            pl.BlockSpec((1, 1, n_kv, HEADS_PER_GROUP, tk), lambda bi, g, i: (bi, g, 0, 0, 0)),
        ],
        out_specs=pl.BlockSpec((1, tq, GROUP_LANES), lambda bi, g, i: (bi, i, g)),
        out_shape=jax.ShapeDtypeStruct((b, t, D_ATTN), BF16),
        scratch_shapes=[
            pltpu.VMEM((rows, GROUP_LANES), BF16),
            pltpu.VMEM((rows, LANES), F32),
            pltpu.VMEM((rows, tk), F32),
            pltpu.VMEM((rows, tk), BF16),
            pltpu.VMEM((rows, GROUP_LANES), F32),
            pltpu.VMEM((rows, LANES), F32),
            pltpu.VMEM((rows, LANES), F32),
            pltpu.VMEM((rows, LANES), F32),
        ],
        compiler_params=_params("parallel", "parallel", "arbitrary"),
        name="fox_attention",
    )(q, k_all, v_all, fq, fk)


def _rnn_kernel(xr_ref, gg_ref, hist_ref, h0_ref, cw_ref, cb_ref, wa_ref, ba_ref, wx_ref, bx_ref,
                lam_ref, out_ref, hl_ref, prev_sc, h_sc, *, tb):
    t = pl.program_id(1)

    @pl.when(t == 0)
    def _():
        prev_sc[...] = hist_ref[0]
        h_sc[...] = h0_ref[0]

    x = xr_ref[0]
    prev = prev_sc[...]
    row8 = lax.broadcasted_iota(jnp.int32, (SUBLANES, D_RNN), 0)
    xc = x * cw_ref[CONV_WIDTH - 1:CONV_WIDTH, :] + cb_ref[...]
    for s in range(1, CONV_WIDTH):
        xs = pltpu.roll(x, s, axis=0)
        head = jnp.where(row8 < s, pltpu.roll(prev, s, axis=0), xs[:SUBLANES])
        xs = jnp.concatenate([head, xs[SUBLANES:]], axis=0)
        xc = xc + xs * cw_ref[CONV_WIDTH - 1 - s:CONV_WIDTH - s, :]
    prev_sc[...] = x[tb - SUBLANES:tb]

    xcb = xc.astype(BF16)
    r = _sigmoid(jnp.dot(xcb, wa_ref[...], preferred_element_type=F32) + ba_ref[...])
    ig = _sigmoid(jnp.dot(xcb, wx_ref[...], preferred_element_type=F32) + bx_ref[...])
    log_a = (-LRU_C) * r * _softplus(-lam_ref[...])
    a = jnp.exp(log_a)
    bterm = jnp.sqrt(1.0 - a * a) * ig * xc

    row = lax.broadcasted_iota(jnp.int32, (tb, D_RNN), 0)
    d = 1
    while d < tb:
        valid = row >= d
        a_s = pltpu.roll(a, d, axis=0)
        b_s = pltpu.roll(bterm, d, axis=0)
        bterm = jnp.where(valid, a * b_s + bterm, bterm)
        a = jnp.where(valid, a * a_s, a)
        d *= 2
    h = bterm + a * h_sc[...]
    h_last = h[tb - 1:tb]
    h_sc[...] = h_last
    hl_ref[0] = h_last
    out_ref[0] = (h * gg_ref[0].astype(F32)).astype(BF16)


def _rnn(xr, gg, hist8, h0, conv_w, conv_b, wa, ba, wx, bx, lam, tb):
    b, t, _ = xr.shape
    blk = pl.BlockSpec((1, tb, D_RNN), lambda bi, ti: (bi, ti, 0))
    full = lambda a: pl.BlockSpec(a.shape, lambda bi, ti: (0,) * a.ndim)
    per_b = lambda r: pl.BlockSpec((1, r, D_RNN), lambda bi, ti: (bi, 0, 0))
    return pl.pallas_call(
        functools.partial(_rnn_kernel, tb=tb),
        grid=(b, t // tb),
        in_specs=[blk, blk, per_b(SUBLANES), per_b(1), full(conv_w), full(conv_b), full(wa), full(ba),
                  full(wx), full(bx), full(lam)],
        out_specs=(blk, per_b(1)),
        out_shape=(jax.ShapeDtypeStruct((b, t, D_RNN), BF16), jax.ShapeDtypeStruct((b, 1, D_RNN), F32)),
        scratch_shapes=[pltpu.VMEM((SUBLANES, D_RNN), F32), pltpu.VMEM((1, D_RNN), F32)],
        compiler_params=_params("parallel", "arbitrary"),
        name="conv_rglru",
    )(xr, gg, hist8, h0, conv_w, conv_b, wa, ba, wx, bx, lam)


def _outproj_kernel(o_ref, r_ref, sa_ref, sr_ref, x_ref, wau_ref, wru_ref, wo_ref, g_ref, b_ref,
                    h_ref, *, alpha):
    up_a = jnp.dot(o_ref[...], wau_ref[...], preferred_element_type=F32)
    up_r = jnp.dot(r_ref[...], wru_ref[...], preferred_element_type=F32)
    merged = sa_ref[...].astype(F32) * up_a + sr_ref[...].astype(F32) * up_r
    mix = jnp.dot(merged.astype(BF16), wo_ref[...], preferred_element_type=F32)
    h_ref[...] = _layer_norm(alpha * x_ref[...] + mix, g_ref[...], b_ref[...])


def _outproj(o2, r2, sa, sr, x2, wau, wru, wo, g, bta, alpha, tm):
    n = x2.shape[0]
    row = lambda w: pl.BlockSpec((tm, w), lambda i: (i, 0))
    full = lambda a: pl.BlockSpec(a.shape, lambda i: (0,) * a.ndim)
    return pl.pallas_call(
        functools.partial(_outproj_kernel, alpha=alpha),
        grid=(n // tm,),
        in_specs=[row(D_ATTN), row(D_RNN), row(D_MODEL), row(D_MODEL), row(D_MODEL), full(wau), full(wru),
                  full(wo), full(g), full(bta)],
        out_specs=row(D_MODEL),
        out_shape=jax.ShapeDtypeStruct((n, D_MODEL), F32),
        compiler_params=_params("parallel"),
        name="outproj_ln1",
    )(o2, r2, sa, sr, x2, wau, wru, wo, g, bta)


def _odd_even_merge_sort(n):
    def merge(lo, hi, r):
        step = r * 2
        if step < hi - lo:
            yield from merge(lo, hi, step)
            yield from merge(lo + r, hi, step)
            yield from ((i, i + r) for i in range(lo + r, hi - r, step))
        else:
            yield (lo, lo + r)

    def sort(lo, hi):
        if hi - lo >= 1:
            mid = lo + (hi - lo) // 2
            yield from sort(lo, mid)
            yield from sort(mid + 1, hi)
            yield from merge(lo, hi, 1)

    return tuple(sort(0, n - 1))


_SORT_TOPK = _odd_even_merge_sort(PEER_TOPK)


def _compare_exchange(v, i, j):
    v[i], v[j] = jnp.maximum(v[i], v[j]), jnp.minimum(v[i], v[j])


def _top_k_sorted(tiles):
    v = list(tiles)
    for i, j in _SORT_TOPK:
        _compare_exchange(v, i, j)
    shift = SUBLANES // 2
    while shift >= 1:
        other = [pltpu.roll(x, shift, axis=0) for x in v]
        v = [jnp.maximum(v[i], other[PEER_TOPK - 1 - i]) for i in range(PEER_TOPK)]
        d = PEER_TOPK // 2
        while d >= 1:
            for i in range(PEER_TOPK):
                if i & d == 0:
                    _compare_exchange(v, i, i + d)
            d //= 2
        shift //= 2
    return v


def _count_greater(x, t):
    assert len(t) == 16, "the bisection below is written out for 16 entries"
    one = lambda m, w: jnp.where(m, float(w), 0.0)
    b3 = t[7] > x
    b2 = jnp.where(b3, t[11], t[3]) > x
    b1 = jnp.where(b3, jnp.where(b2, t[13], t[9]), jnp.where(b2, t[5], t[1])) > x
    hi = jnp.where(b2, jnp.where(b1, t[14], t[12]), jnp.where(b1, t[10], t[8]))
    lo = jnp.where(b2, jnp.where(b1, t[6], t[4]), jnp.where(b1, t[2], t[0]))
    b0 = jnp.where(b3, hi, lo) > x
    count = one(b3, 8) + one(b2, 4) + one(b1, 2) + one(b0, 1)
    return jnp.where(t[15] > x, float(PEER_TOPK), count)


def _route_kernel(h_ref, wq_ref, k1_ref, k2_ref, ht_ref, cnt_ref, e1_ref, r2_ref, e2_ref,
                  qt_sc, t1_sc, t2_sc, *, tm):
    ht = jnp.transpose(h_ref[...]).astype(BF16)
    ht_ref[...] = ht
    qt_sc[...] = jnp.dot(wq_ref[...], ht, preferred_element_type=F32).astype(BF16)
    key_iota = lax.broadcasted_iota(jnp.int32, (N_KEYS, tm), 0).astype(F32)
    top_iota = lax.broadcasted_iota(jnp.int32, (PEER_TOPK, tm), 0).astype(F32)
    front_rows = SUBLANES

    def scores(hd):
        base = hd * 2 * PEER_HALF
        s1 = jnp.dot(k1_ref[...], qt_sc[base:base + PEER_HALF], preferred_element_type=F32)
        s2 = jnp.dot(k2_ref[...], qt_sc[base + PEER_HALF:base + 2 * PEER_HALF], preferred_element_type=F32)
        return s1, s2

    sub = lax.broadcasted_iota(jnp.int32, (SUBLANES, tm), 0)
    tiles_of = lambda s: [s[r * SUBLANES:(r + 1) * SUBLANES] for r in range(N_KEYS // SUBLANES)]
    sublane_sum = lambda x: jnp.sum(x, axis=0, keepdims=True)

    def route_head_sorted(hd):
        s1, s2 = scores(hd)
        rows1, rows2 = tiles_of(s1), tiles_of(s2)
        t1 = _top_k_sorted(rows1)
        t2 = _top_k_sorted(rows2)
        t1_lo = t1[SUBLANES - 1]
        for a in range(SUBLANES - 2, -1, -1):
            t1_lo = jnp.where(sub == a, t1[a], t1_lo)
        cand = [t1_lo + t2[b] for b in range(PEER_TOPK)]
        pool = list(cand)
        for a in range(SUBLANES, PEER_TOPK):
            pool[a] = jnp.where(sub == 1, t1[a] + t2[0], cand[a])
        top = _top_k_sorted(pool)
        tau = top[PEER_TOPK - 1]
        cnt_lo = functools.reduce(jnp.add, [jnp.where(c >= tau, 1.0, 0.0) for c in cand])
        cnt = [jnp.broadcast_to(cnt_lo[a:a + 1], (SUBLANES, tm)) for a in range(SUBLANES)]
        cnt += [jnp.where(t1[a] + t2[0] >= tau, 1.0, 0.0) for a in range(SUBLANES, PEER_TOPK)]
        z = functools.reduce(jnp.add, [jnp.exp(t - top[0]) for t in top])

        cnt1_rows, rank2_rows = [], []
        for r in range(N_KEYS // SUBLANES):
            c = jnp.zeros((SUBLANES, tm), F32)
            for a in range(PEER_TOPK):
                c = jnp.where(rows1[r] == t1[a], cnt[a], c)
            cnt1_rows.append(c)
            g = _count_greater(rows2[r], t2)
            rank2_rows.append(jnp.where(g < float(PEER_TOPK), g, NOT_SELECTED_RANK))
        cnt1 = jnp.concatenate(cnt1_rows, axis=0)
        rank2 = jnp.concatenate(rank2_rows, axis=0)
        cnt_ref[hd] = cnt1
        e1_ref[hd] = jnp.exp(s1 - t1[0][0:1]) / z[0:1]
        r2_ref[hd] = rank2.astype(BF16)
        e2_ref[hd] = jnp.exp(s2 - t2[0][0:1]).astype(BF16)

        gap = lambda t: functools.reduce(jnp.minimum, [t[b] - t[b + 1] for b in range(PEER_TOPK - 1)])[0:1]
        n_cnt = sublane_sum(cnt_lo) + functools.reduce(jnp.add, cnt[SUBLANES:])[0:1]
        n_sel = sublane_sum(functools.reduce(jnp.add, cnt1_rows))
        n_rank = sublane_sum(functools.reduce(
            jnp.add, [jnp.where(x < float(PEER_TOPK), 1.0, 0.0) for x in rank2_rows]))
        off = lambda n: jnp.abs(n - float(PEER_TOPK))
        return (off(n_cnt) + off(n_sel) + off(n_rank)
                + jnp.where(jnp.minimum(gap(t1), gap(t2)) > 0.0, 0.0, 1.0))

    def route_head_exact(hd):
        def pick_one(v, iota, n):
            m = jnp.max(v, axis=0, keepdims=True)
            return m, iota == jnp.min(jnp.where(v == m, iota, float(n)), axis=0, keepdims=True)

        s1, s2 = scores(hd)

        def extract(a, carry):
            v1, r1, v2, r2 = carry
            m1, sel1 = pick_one(v1, key_iota, N_KEYS)
            m2, sel2 = pick_one(v2, key_iota, N_KEYS)
            t1_sc[pl.ds(a, 1), :] = m1
            t2_sc[pl.ds(a, 1), :] = m2
            af = jnp.asarray(a, dtype=F32)
            return (jnp.where(sel1, -jnp.inf, v1), jnp.where(sel1, af, r1),
                    jnp.where(sel2, -jnp.inf, v2), jnp.where(sel2, af, r2))

        no_rank = jnp.full((N_KEYS, tm), NOT_SELECTED_RANK, F32)
        _, rank1, _, rank2 = lax.fori_loop(0, PEER_TOPK, extract, (s1, no_rank, s2, no_rank))
        t1 = t1_sc[...]
        t2 = t2_sc[...]
        top0 = t1[0:1] + t2[0:1]

        def pick(_, carry):
            ptr, front, z = carry
            m, sel = pick_one(front, top_iota, PEER_TOPK)
            ptr = ptr + jnp.where(sel, 1.0, 0.0)
            lo = ptr[:front_rows]
            nxt = jnp.full(lo.shape, -jnp.inf, F32)
            for b in range(1, PEER_TOPK):
                nxt = jnp.where(lo == float(b), t2_sc[b:b + 1, :], nxt)
            nxt = jnp.where(lo == 0.0, t2[0:1], nxt)
            front = jnp.concatenate(
                [t1[:front_rows] + nxt, jnp.where(sel[front_rows:], -jnp.inf, front[front_rows:])], axis=0)
            return ptr, front, z + jnp.exp(m - top0)

        cnt, _, z = lax.fori_loop(
            0, PEER_TOPK, pick,
            (jnp.zeros((PEER_TOPK, tm), F32), t1 + t2[0:1], jnp.zeros((1, tm), F32)))

        cnt1 = jnp.zeros((N_KEYS, tm), F32)
        for a in range(PEER_TOPK):
            cnt1 = jnp.where(rank1 == float(a), cnt[a:a + 1], cnt1)
        cnt_ref[hd] = cnt1
        e1_ref[hd] = jnp.exp(s1 - t1[0:1]) / z
        r2_ref[hd] = rank2.astype(BF16)
        e2_ref[hd] = jnp.exp(s2 - t2[0:1]).astype(BF16)

    doubt = [route_head_sorted(hd) for hd in range(PEER_HEADS)]

    @pl.when(jnp.max(functools.reduce(jnp.maximum, doubt)) > 0.0)
    def _():
        for hd in range(PEER_HEADS):
            @pl.when(jnp.max(doubt[hd]) > 0.0)
            def _():
                route_head_exact(hd)


def _route(h2, wq_t, k1, k2, tm):
    n = h2.shape[0]
    full = lambda a: pl.BlockSpec(a.shape, lambda i: (0,) * a.ndim)
    per_head = pl.BlockSpec((PEER_HEADS, N_KEYS, tm), lambda i: (0, 0, i))
    hshape = lambda dt: jax.ShapeDtypeStruct((PEER_HEADS, N_KEYS, n), dt)
    return pl.pallas_call(
        functools.partial(_route_kernel, tm=tm),
        grid=(n // tm,),
        in_specs=[pl.BlockSpec((tm, D_MODEL), lambda i: (i, 0)), full(wq_t), full(k1), full(k2)],
        out_specs=(pl.BlockSpec((D_MODEL, tm), lambda i: (0, i)), per_head, per_head, per_head, per_head),
        out_shape=(jax.ShapeDtypeStruct((D_MODEL, n), BF16), hshape(F32), hshape(F32), hshape(BF16),
                   hshape(BF16)),
        scratch_shapes=[pltpu.VMEM((PEER_HEADS * 2 * PEER_HALF, tm), BF16),
                        pltpu.VMEM((PEER_TOPK, tm), F32), pltpu.VMEM((PEER_TOPK, tm), F32)],
        compiler_params=_params("parallel"),
        name="peer_route",
    )(h2, wq_t, k1, k2)


ROWS_PER_STEP = 16
EXPERTS_PER_STEP = ROWS_PER_STEP * N_KEYS
ROWS_PER_SUB = 4
EXPERTS_PER_SUB = ROWS_PER_SUB * N_KEYS
SUBS_PER_MIX = 2
DENSE_CHUNK = 2 * LANES


def _bf16_row_tile(row):
    tile = jnp.broadcast_to(row, (BF16_ROWS, row.shape[1])).astype(BF16)
    return pltpu.repeat(tile, N_KEYS // BF16_ROWS, axis=0)


def _dense_kernel(ht_ref, u_ref, vt_ref, cnt_ref, e1_ref, r2_ref, e2_ref, h_ref, g_ref, b_ref,
                  y_ref, acc_sc, act_sc, coef_sc, *, alpha):
    e = pl.program_id(1)

    @pl.when(e == 0)
    def _():
        acc_sc[...] = jnp.zeros_like(acc_sc)

    tm = act_sc.shape[1]
    n_sub = ROWS_PER_STEP // ROWS_PER_SUB

    def activations(sub):
        rows = slice(sub * EXPERTS_PER_SUB, (sub + 1) * EXPERTS_PER_SUB)
        act_sc[rows, :] = jnp.dot(u_ref[rows, :], ht_ref[...], preferred_element_type=F32)

    def coefficients(sub):
        for il in range(sub * ROWS_PER_SUB, (sub + 1) * ROWS_PER_SUB):
            rows = slice(il * N_KEYS, (il + 1) * N_KEYS)
            chunk = min(DENSE_CHUNK, tm)
            for c in range(tm // chunk):
                lanes = slice(c * chunk, (c + 1) * chunk)
                w = None
                for hd in range(PEER_HEADS):
                    cnt = _bf16_row_tile(cnt_ref[hd, il:il + 1, lanes])
                    e1 = _bf16_row_tile(e1_ref[hd, il:il + 1, lanes])
                    term = jnp.where(r2_ref[hd, :, lanes] < cnt, e2_ref[hd, :, lanes] * e1,
                                     jnp.zeros((), BF16))
                    w = term if w is None else w + term
                coef_sc[rows, lanes] = w * _gelu_tanh(act_sc[rows, lanes].astype(BF16))

    def mix(first, last):
        rows = slice(first * EXPERTS_PER_SUB, (last + 1) * EXPERTS_PER_SUB)
        acc_sc[...] += jnp.dot(vt_ref[:, rows], coef_sc[rows, :], preferred_element_type=F32)

    activations(0)
    for sub in range(n_sub):
        if sub + 1 < n_sub:
            activations(sub + 1)
        coefficients(sub)
        if sub % SUBS_PER_MIX == SUBS_PER_MIX - 1:
            mix(sub - SUBS_PER_MIX + 1, sub)

    @pl.when(e == pl.num_programs(1) - 1)
    def _():
        peer = jnp.transpose(acc_sc[...])
        y_ref[...] = _layer_norm(alpha * h_ref[...] + peer, g_ref[...], b_ref[...])


def _dense(ht, u_b, vt_b, cnt1, e1n, rank2, e2, h2, g, bta, alpha, tm):
    n = h2.shape[0]
    n_e = N_EXPERTS // EXPERTS_PER_STEP
    full = lambda a: pl.BlockSpec(a.shape, lambda t, e: (0,) * a.ndim)
    rows_blk = pl.BlockSpec((PEER_HEADS, ROWS_PER_STEP, tm), lambda t, e: (0, e, t))
    cols_blk = pl.BlockSpec((PEER_HEADS, N_KEYS, tm), lambda t, e: (0, 0, t))
    return pl.pallas_call(
        functools.partial(_dense_kernel, alpha=alpha),
        grid=(n // tm, n_e),
        in_specs=[
            pl.BlockSpec((D_MODEL, tm), lambda t, e: (0, t)),
            pl.BlockSpec((EXPERTS_PER_STEP, D_MODEL), lambda t, e: (e, 0)),
            pl.BlockSpec((D_MODEL, EXPERTS_PER_STEP), lambda t, e: (0, e)),
            rows_blk, rows_blk, cols_blk, cols_blk,
            pl.BlockSpec((tm, D_MODEL), lambda t, e: (t, 0)),
            full(g), full(bta),
        ],
        out_specs=pl.BlockSpec((tm, D_MODEL), lambda t, e: (t, 0)),
        out_shape=jax.ShapeDtypeStruct((n, D_MODEL), F32),
        scratch_shapes=[pltpu.VMEM((D_MODEL, tm), F32), pltpu.VMEM((EXPERTS_PER_STEP, tm), F32),
                        pltpu.VMEM((EXPERTS_PER_STEP, tm), BF16)],
        compiler_params=_params("parallel", "arbitrary"),
        name="peer_dense",
    )(ht, u_b, vt_b, cnt1, e1n, rank2, e2, h2, g, bta)


def _block_diag(w):
    nb, bi, bo = w.shape
    eye = jnp.eye(nb, dtype=w.dtype)
    return (eye[:, None, :, None] * w[:, :, None, :]).reshape(nb * bi, nb * bo)


def _prep_weights(w_in, b_forget, conv_w, conv_b, w_rg_a, b_rg_a, w_rg_x, b_rg_x, lru_lambda,
                  w_attn_up, w_rnn_up, w_out, ln1_g, ln1_b, peer_w_query, peer_keys_1, peer_keys_2,
                  peer_u, peer_v, ln2_g, ln2_b):
    c_f = 3 * D_ATTN
    w_in_p = jnp.concatenate(
        [w_in[:, :c_f], jnp.pad(w_in[:, c_f:c_f + N_ATTN_HEADS], ((0, 0), (0, F_PAD - N_ATTN_HEADS))),
         w_in[:, c_f + N_ATTN_HEADS:]], axis=1).astype(BF16)
    row = lambda a: a.reshape(1, -1).astype(F32)
    return dict(
        w_in=w_in_p, b_forget=row(b_forget), conv_w=conv_w.astype(F32), conv_b=row(conv_b),
        wa=_block_diag(w_rg_a).astype(BF16), ba=row(b_rg_a), wx=_block_diag(w_rg_x).astype(BF16),
        bx=row(b_rg_x), lam=row(lru_lambda),
        wau=w_attn_up.astype(BF16), wru=w_rnn_up.astype(BF16), wo=w_out.astype(BF16),
        ln1_g=row(ln1_g), ln1_b=row(ln1_b),
        wq_t=jnp.transpose(peer_w_query).astype(BF16), k1=peer_keys_1.astype(BF16),
        k2=peer_keys_2.astype(BF16), u=peer_u.astype(BF16), vt=jnp.transpose(peer_v).astype(BF16),
        ln2_g=row(ln2_g), ln2_b=row(ln2_b),
    )


def _pick_block(n, target):
    blk = min(n, target)
    assert n % blk == 0, (n, blk)
    return blk


def _trunk_layer(x, past_k, past_v, past_logf, conv_hist, h0, p, alpha):
    bsz, t, _ = x.shape
    n = bsz * t
    x2 = x.reshape(n, D_MODEL)
    q, k, v, kb, vb, lf, xr, gg, sa, sr = _inproj(x2, p["w_in"], p["b_forget"], _pick_block(n, 256))

    tq = _pick_block(t, ATTN_Q_BLOCK)
    n_past = 0 if past_k is None else past_k.shape[1]
    t_all = n_past + t
    t_lanes = -(-t_all // LANES) * LANES
    tk = ATTN_K_BLOCK if t_lanes % ATTN_K_BLOCK == 0 else t_lanes
    t_pad = -(-t_all // tk) * tk
    lf3 = lf.reshape(bsz, t, N_ATTN_HEADS)
    kb3 = kb.reshape(bsz, t, D_ATTN)
    vb3 = vb.reshape(bsz, t, D_ATTN)
    if past_k is not None:
        lf3 = jnp.concatenate([past_logf.astype(F32), lf3], axis=1)
        kb3 = jnp.concatenate([past_k.reshape(bsz, n_past, D_ATTN).astype(BF16), kb3], axis=1)
        vb3 = jnp.concatenate([past_v.reshape(bsz, n_past, D_ATTN).astype(BF16), vb3], axis=1)
    pad = ((0, 0), (0, t_pad - t_all), (0, 0))
    lf3, kb3, vb3 = jnp.pad(lf3, pad), jnp.pad(kb3, pad), jnp.pad(vb3, pad)
    f_t = _cumsum_time(jnp.transpose(lf3, (0, 2, 1)))
    f_g = f_t.reshape(bsz, N_HEAD_GROUPS, HEADS_PER_GROUP, t_pad)
    fq = jnp.transpose(f_g[:, :, :, n_past:n_past + t], (0, 1, 3, 2))
    fk = jnp.transpose(f_g.reshape(bsz, N_HEAD_GROUPS, HEADS_PER_GROUP, t_pad // tk, tk), (0, 1, 3, 2, 4))
    o = _attention(q.reshape(bsz, t, D_ATTN), kb3, vb3, fq, fk, tq=tq, tk=tk, q_off=n_past)

    hist8 = jnp.pad(conv_hist.astype(F32), ((0, 0), (SUBLANES - (CONV_WIDTH - 1), 0), (0, 0)))
    xr3 = xr.reshape(bsz, t, D_RNN)
    rnn_out, h_last = _rnn(xr3, gg.reshape(bsz, t, D_RNN), hist8, h0.astype(F32).reshape(bsz, 1, D_RNN),
                           p["conv_w"], p["conv_b"], p["wa"], p["ba"], p["wx"], p["bx"], p["lam"],
                           _pick_block(t, 256))
    new_hist = jnp.concatenate([conv_hist.astype(F32), xr3], axis=1)[:, -(CONV_WIDTH - 1):]

    h = _outproj(o.reshape(n, D_ATTN), rnn_out.reshape(n, D_RNN), sa, sr, x2, p["wau"], p["wru"], p["wo"],
                 p["ln1_g"], p["ln1_b"], alpha, _pick_block(n, 256))
    ht, cnt1, e1n, rank2, e2 = _route(h, p["wq_t"], p["k1"], p["k2"], LANES)
    y = _dense(ht, p["u"], p["vt"], cnt1, e1n, rank2, e2, h, p["ln2_g"], p["ln2_b"], alpha,
               _pick_block(n, 512))
    return (y.reshape(bsz, t, D_MODEL), k.reshape(bsz, t, N_ATTN_HEADS, ATTN_HEAD_DIM),
            v.reshape(bsz, t, N_ATTN_HEADS, ATTN_HEAD_DIM), lf.reshape(bsz, t, N_ATTN_HEADS), new_hist,
            h_last.reshape(bsz, D_RNN))


def kernel(x_prompt, x_sample, cache_k, cache_v, cache_logf, state_conv, state_rnn, w_in, b_forget, conv_w, conv_b, w_rg_a, b_rg_a, w_rg_x, b_rg_x, lru_lambda, w_attn_up, w_rnn_up, w_out, ln1_g, ln1_b, peer_w_query, peer_keys_1, peer_keys_2, peer_u, peer_v, ln2_g, ln2_b):
    depth = w_in.shape[0]
    alpha = (2 * depth) ** 0.25
    layer_weights = (w_in, b_forget, conv_w, conv_b, w_rg_a, b_rg_a, w_rg_x, b_rg_x, lru_lambda, w_attn_up,
                     w_rnn_up, w_out, ln1_g, ln1_b, peer_w_query, peer_keys_1, peer_keys_2, peer_u, peer_v,
                     ln2_g, ln2_b)
    hp, hs = x_prompt, x_sample
    prompt_state, sample_state = [], []
    for l in range(depth):
        p = _prep_weights(*(w[l] for w in layer_weights))
        zero_hist = jnp.zeros((hp.shape[0], CONV_WIDTH - 1, D_RNN), F32)
        zero_h = jnp.zeros((hp.shape[0], D_RNN), F32)
        hp, *st_p = _trunk_layer(hp, None, None, None, zero_hist, zero_h, p, alpha)
        hs, *st_s = _trunk_layer(hs, cache_k[l], cache_v[l], cache_logf[l], state_conv[l], state_rnn[l], p,
                                 alpha)
        prompt_state.append(st_p)
        sample_state.append(st_s)
    stack = lambda states, i: jnp.stack([s[i] for s in states])
    return (hp, hs) + tuple(stack(prompt_state, i) for i in range(5)) + tuple(
        stack(sample_state, i) for i in range(5))
```

```python
import functools
import math

import jax
import jax.numpy as jnp
from jax import lax
from jax.experimental import pallas as pl
from jax.experimental.pallas import tpu as pltpu

F32 = jnp.float32
BF16 = jnp.bfloat16

D_MODEL = 1024
N_ATTN_HEADS = 8
ATTN_HEAD_DIM = 64
D_ATTN = N_ATTN_HEADS * ATTN_HEAD_DIM
ATTN_SCALE = ATTN_HEAD_DIM ** -0.5
D_RNN = 512
N_RNN_BLOCKS = 8
RNN_BLOCK = D_RNN // N_RNN_BLOCKS
CONV_WIDTH = 4
LRU_C = 8.0
N_KEYS = 128
N_EXPERTS = N_KEYS * N_KEYS
PEER_HEADS = 8
PEER_TOPK = 16
PEER_HALF = 128
LN_EPS = 1e-5

LANES = 128
SUBLANES = 8
BF16_ROWS = 2 * SUBLANES
VMEM_LIMIT_BYTES = 56 * 1024 * 1024

HEADS_PER_GROUP = 4
GROUP_LANES = HEADS_PER_GROUP * ATTN_HEAD_DIM
N_HEAD_GROUPS = N_ATTN_HEADS // HEADS_PER_GROUP
ATTN_Q_BLOCK = 512
ATTN_K_BLOCK = 512
ATTN_ROW_BLOCK = 32
F_PAD = LANES
NEG_BIG = -1e30
LOG2_E = math.log2(math.e)
NOT_SELECTED_RANK = 99.0

_C_Q = 0
_C_K = _C_Q + D_ATTN
_C_V = _C_K + D_ATTN
_C_F = _C_V + D_ATTN
_C_XR = _C_F + F_PAD
_C_GATE = _C_XR + D_RNN
_C_GA = _C_GATE + D_RNN
_C_GR = _C_GA + D_MODEL
_C_END = _C_GR + D_MODEL


def _params(*sem):
    return pltpu.CompilerParams(dimension_semantics=sem, vmem_limit_bytes=VMEM_LIMIT_BYTES)


def _sigmoid(x):
    return 1.0 / (1.0 + jnp.exp(-x))


def _gelu_tanh(x):
    half = 0.5 * x
    return half + half * jnp.tanh(x * (0.7978845608028654 + 0.035677408136300125 * (x * x)))


def _softplus(x):
    return jnp.maximum(x, 0.0) + jnp.log1p(jnp.exp(-jnp.abs(x)))


def _layer_norm(x, g, b):
    mu = jnp.mean(x, axis=-1, keepdims=True)
    xc = x - mu
    var = jnp.mean(xc * xc, axis=-1, keepdims=True)
    return xc * lax.rsqrt(var + LN_EPS) * g + b


def _inproj_kernel(x_ref, w_ref, bf_ref, q_ref, k_ref, v_ref, kb_ref, vb_ref, lf_ref, xr_ref,
                   gg_ref, sa_ref, sr_ref):
    xb = x_ref[...].astype(BF16)

    def mm(lo, hi):
        return jnp.dot(xb, w_ref[:, lo:hi], preferred_element_type=F32)

    q_ref[...] = (mm(_C_Q, _C_K) * (ATTN_SCALE * LOG2_E)).astype(BF16)
    k = mm(_C_K, _C_V)
    k_ref[...] = k
    kb_ref[...] = k.astype(BF16)
    v = mm(_C_V, _C_F)
    v_ref[...] = v
    vb_ref[...] = v.astype(BF16)
    f = mm(_C_F, _C_XR)[:, :N_ATTN_HEADS] + bf_ref[...]
    lf_ref[...] = -_softplus(-f)
    xr_ref[...] = mm(_C_XR, _C_GATE)
    gg_ref[...] = _gelu_tanh(mm(_C_GATE, _C_GA)).astype(BF16)
    sa_ref[...] = _sigmoid(mm(_C_GA, _C_GR)).astype(BF16)
    sr_ref[...] = _sigmoid(mm(_C_GR, _C_END)).astype(BF16)


def _inproj(x2, w_in_p, b_forget, tm):
    n = x2.shape[0]
    row = lambda w: pl.BlockSpec((tm, w), lambda i: (i, 0))
    full = lambda a: pl.BlockSpec(a.shape, lambda i: (0,) * a.ndim)
    out_shape = (
        jax.ShapeDtypeStruct((n, D_ATTN), BF16),
        jax.ShapeDtypeStruct((n, D_ATTN), F32),
        jax.ShapeDtypeStruct((n, D_ATTN), F32),
        jax.ShapeDtypeStruct((n, D_ATTN), BF16),
        jax.ShapeDtypeStruct((n, D_ATTN), BF16),
        jax.ShapeDtypeStruct((n, N_ATTN_HEADS), F32),
        jax.ShapeDtypeStruct((n, D_RNN), F32),
        jax.ShapeDtypeStruct((n, D_RNN), BF16),
        jax.ShapeDtypeStruct((n, D_MODEL), BF16),
        jax.ShapeDtypeStruct((n, D_MODEL), BF16),
    )
    out_specs = (row(D_ATTN), row(D_ATTN), row(D_ATTN), row(D_ATTN), row(D_ATTN), row(N_ATTN_HEADS),
                 row(D_RNN), row(D_RNN), row(D_MODEL), row(D_MODEL))
    return pl.pallas_call(
        _inproj_kernel,
        grid=(n // tm,),
        in_specs=[row(D_MODEL), full(w_in_p), full(b_forget)],
        out_specs=out_specs,
        out_shape=out_shape,
        compiler_params=_params("parallel"),
        name="inproj",
    )(x2, w_in_p, b_forget)


def _cumsum_kernel(x_ref, o_ref):
    x = x_ref[0]
    t = x.shape[1]
    lane = lax.broadcasted_iota(jnp.int32, x.shape, 1)
    d = 1
    while d < t:
        x = x + jnp.where(lane >= d, pltpu.roll(x, d, axis=1), 0.0)
        d *= 2
    o_ref[0] = x * LOG2_E


def _cumsum_time(lf_t):
    b, h, t = lf_t.shape
    spec = pl.BlockSpec((1, h, t), lambda i: (i, 0, 0))
    return pl.pallas_call(
        _cumsum_kernel, grid=(b,), in_specs=[spec], out_specs=spec,
        out_shape=jax.ShapeDtypeStruct(lf_t.shape, F32),
        compiler_params=_params("parallel"), name="logf_cumsum",
    )(lf_t)


def _attn_kernel(q_ref, k_ref, v_ref, fq_ref, fk_ref, o_ref, q_sc, fq_sc, s_sc, p_sc, acc_sc, m_sc, al_sc,
                 lp_sc, *, tq, tk, q_off):
    qi = pl.program_id(2)
    row0 = q_off + qi * tq
    n_full = (row0 + 1) // tk
    heads = range(HEADS_PER_GROUP)
    rb = min(ATTN_ROW_BLOCK, tq)
    lane = lax.broadcasted_iota(jnp.int32, (tq, GROUP_LANES), 1)
    in_head = [(lane >= j * ATTN_HEAD_DIM) & (lane < (j + 1) * ATTN_HEAD_DIM) for j in heads]
    q = q_ref[0]
    for j in heads:
        q_sc[j * tq:(j + 1) * tq, :] = jnp.where(in_head[j], q, jnp.zeros_like(q))
        fq_sc[j * tq:(j + 1) * tq, :] = jnp.broadcast_to(fq_ref[0, 0, :, j:j + 1], (tq, LANES))
    m_sc[...] = jnp.full_like(m_sc, NEG_BIG)
    lp_sc[...] = jnp.zeros_like(lp_sc)
    acc_sc[...] = jnp.zeros_like(acc_sc)
    halves = [slice(0, 2 * tq), slice(2 * tq, 4 * tq)]
    n_lane_tiles = tk // LANES

    def absorb(c, masked):
        start = pl.multiple_of(c * tk, tk)
        ks = k_ref[0, pl.ds(start, tk), :]
        for hv in halves:
            s_sc[hv, :] = lax.dot_general(q_sc[hv, :], ks, (((1,), (1,)), ((), ())),
                                          preferred_element_type=F32)
        fk = fk_ref[0, 0, c]
        for j in heads:
            for r in range(tq // rb):
                rows = slice(j * tq + r * rb, j * tq + (r + 1) * rb)
                s = s_sc[rows, :] - fk[j:j + 1, :]
                if masked:
                    q_pos = row0 + r * rb + lax.broadcasted_iota(jnp.int32, (rb, tk), 0)
                    k_pos = c * tk + lax.broadcasted_iota(jnp.int32, (rb, tk), 1)
                    s = jnp.where(k_pos <= q_pos, s, NEG_BIG)
                tiles = [s[:, t * LANES:(t + 1) * LANES] for t in range(n_lane_tiles)]
                row_max = jnp.max(functools.reduce(jnp.maximum, tiles), axis=1, keepdims=True)
                fq = fq_sc[rows, :]
                m_old = m_sc[rows, :]
                m_new = jnp.maximum(m_old, jnp.broadcast_to(row_max, (rb, LANES)) + fq)
                alpha = jnp.exp2(m_old - m_new)
                shift = fq - m_new
                p_tiles = [jnp.exp2(tile + shift) for tile in tiles]
                m_sc[rows, :] = m_new
                al_sc[rows, :] = alpha
                lp_sc[rows, :] = alpha * lp_sc[rows, :] + functools.reduce(jnp.add, p_tiles)
                p_sc[rows, :] = jnp.concatenate(p_tiles, axis=1).astype(BF16)
        vs = v_ref[0, pl.ds(start, tk), :]
        for hv in halves:
            pv = jnp.dot(p_sc[hv, :], vs, preferred_element_type=F32)
            alpha = al_sc[hv, :]
            acc_sc[hv, :] = jnp.concatenate([alpha] * (GROUP_LANES // LANES), axis=1) * acc_sc[hv, :] + pv

    def step(c, carry):
        absorb(c, masked=False)
        return carry

    lax.fori_loop(0, n_full, step, 0)
    absorb(n_full, masked=True)
    o_all = acc_sc[...] / jnp.sum(lp_sc[...], axis=1, keepdims=True)
    out = jnp.zeros((tq, GROUP_LANES), F32)
    for j in heads:
        out = jnp.where(in_head[j], o_all[j * tq:(j + 1) * tq], out)
    o_ref[0] = out.astype(BF16)


def _attention(q, k_all, v_all, fq, fk, *, tq, tk, q_off):
    b, t, _ = q.shape
    t_k = k_all.shape[1]
    n_kv = t_k // tk
    assert all((q_off + i * tq) % tk + tq <= tk for i in range(t // tq)), "a query block straddles key chunks"
    kern = functools.partial(_attn_kernel, tq=tq, tk=tk, q_off=q_off)
    rows = HEADS_PER_GROUP * tq
    return pl.pallas_call(
        kern,
        grid=(b, N_HEAD_GROUPS, t // tq),
        in_specs=[
            pl.BlockSpec((1, tq, GROUP_LANES), lambda bi, g, i: (bi, i, g)),
            pl.BlockSpec((1, t_k, GROUP_LANES), lambda bi, g, i: (bi, 0, g)),
            pl.BlockSpec((1, t_k, GROUP_LANES), lambda bi, g, i: (bi, 0, g)),
            pl.BlockSpec((1, 1, tq, HEADS_PER_GROUP), lambda bi, g, i: (bi, g, i, 0)),
            pl.BlockSpec((1, 1, n_kv, HEADS_PER_GROUP, tk), lambda bi, g, i: (bi, g, 0, 0, 0)),
        ],
        out_specs=pl.BlockSpec((1, tq, GROUP_LANES), lambda bi, g, i: (bi, i, g)),
        out_shape=jax.ShapeDtypeStruct((b, t, D_ATTN), BF16),
        scratch_shapes=[
            pltpu.VMEM((rows, GROUP_LANES), BF16),
            pltpu.VMEM((rows, LANES), F32),
            pltpu.VMEM((rows, tk), F32),
            pltpu.VMEM((rows, tk), BF16),
            pltpu.VMEM((rows, GROUP_LANES), F32),
            pltpu.VMEM((rows, LANES), F32),
            pltpu.VMEM((rows, LANES), F32),
            pltpu.VMEM((rows, LANES), F32),
        ],
        compiler_params=_params("parallel", "parallel", "arbitrary"),
        name="fox_attention",
    )(q, k_all, v_all, fq, fk)


def _rnn_kernel(xr_ref, gg_ref, hist_ref, h0_ref, cw_ref, cb_ref, wa_ref, ba_ref, wx_ref, bx_ref,
                lam_ref, out_ref, hl_ref, prev_sc, h_sc, *, tb):
    t = pl.program_id(1)

    @pl.when(t == 0)
    def _():
        prev_sc[...] = hist_ref[0]
        h_sc[...] = h0_ref[0]

    x = xr_ref[0]
    prev = prev_sc[...]
    row8 = lax.broadcasted_iota(jnp.int32, (SUBLANES, D_RNN), 0)
    xc = x * cw_ref[CONV_WIDTH - 1:CONV_WIDTH, :] + cb_ref[...]
    for s in range(1, CONV_WIDTH):
        xs = pltpu.roll(x, s, axis=0)
        head = jnp.where(row8 < s, pltpu.roll(prev, s, axis=0), xs[:SUBLANES])
        xs = jnp.concatenate([head, xs[SUBLANES:]], axis=0)
        xc = xc + xs * cw_ref[CONV_WIDTH - 1 - s:CONV_WIDTH - s, :]
    prev_sc[...] = x[tb - SUBLANES:tb]

    xcb = xc.astype(BF16)
    r = _sigmoid(jnp.dot(xcb, wa_ref[...], preferred_element_type=F32) + ba_ref[...])
    ig = _sigmoid(jnp.dot(xcb, wx_ref[...], preferred_element_type=F32) + bx_ref[...])
    log_a = (-LRU_C) * r * _softplus(-lam_ref[...])
    a = jnp.exp(log_a)
    bterm = jnp.sqrt(1.0 - a * a) * ig * xc

    row = lax.broadcasted_iota(jnp.int32, (tb, D_RNN), 0)
    d = 1
    while d < tb:
        valid = row >= d
        a_s = pltpu.roll(a, d, axis=0)
        b_s = pltpu.roll(bterm, d, axis=0)
        bterm = jnp.where(valid, a * b_s + bterm, bterm)
        a = jnp.where(valid, a * a_s, a)
        d *= 2
    h = bterm + a * h_sc[...]
    h_last = h[tb - 1:tb]
    h_sc[...] = h_last
    hl_ref[0] = h_last
    out_ref[0] = (h * gg_ref[0].astype(F32)).astype(BF16)


def _rnn(xr, gg, hist8, h0, conv_w, conv_b, wa, ba, wx, bx, lam, tb):
    b, t, _ = xr.shape
    blk = pl.BlockSpec((1, tb, D_RNN), lambda bi, ti: (bi, ti, 0))
    full = lambda a: pl.BlockSpec(a.shape, lambda bi, ti: (0,) * a.ndim)
    per_b = lambda r: pl.BlockSpec((1, r, D_RNN), lambda bi, ti: (bi, 0, 0))
    return pl.pallas_call(
        functools.partial(_rnn_kernel, tb=tb),
        grid=(b, t // tb),
        in_specs=[blk, blk, per_b(SUBLANES), per_b(1), full(conv_w), full(conv_b), full(wa), full(ba),
                  full(wx), full(bx), full(lam)],
        out_specs=(blk, per_b(1)),
        out_shape=(jax.ShapeDtypeStruct((b, t, D_RNN), BF16), jax.ShapeDtypeStruct((b, 1, D_RNN), F32)),
        scratch_shapes=[pltpu.VMEM((SUBLANES, D_RNN), F32), pltpu.VMEM((1, D_RNN), F32)],
        compiler_params=_params("parallel", "arbitrary"),
        name="conv_rglru",
    )(xr, gg, hist8, h0, conv_w, conv_b, wa, ba, wx, bx, lam)


def _outproj_kernel(o_ref, r_ref, sa_ref, sr_ref, x_ref, wau_ref, wru_ref, wo_ref, g_ref, b_ref,
                    h_ref, *, alpha):
    up_a = jnp.dot(o_ref[...], wau_ref[...], preferred_element_type=F32)
    up_r = jnp.dot(r_ref[...], wru_ref[...], preferred_element_type=F32)
    merged = sa_ref[...].astype(F32) * up_a + sr_ref[...].astype(F32) * up_r
    mix = jnp.dot(merged.astype(BF16), wo_ref[...], preferred_element_type=F32)
    h_ref[...] = _layer_norm(alpha * x_ref[...] + mix, g_ref[...], b_ref[...])


def _outproj(o2, r2, sa, sr, x2, wau, wru, wo, g, bta, alpha, tm):
    n = x2.shape[0]
    row = lambda w: pl.BlockSpec((tm, w), lambda i: (i, 0))
    full = lambda a: pl.BlockSpec(a.shape, lambda i: (0,) * a.ndim)
    return pl.pallas_call(
        functools.partial(_outproj_kernel, alpha=alpha),
        grid=(n // tm,),
        in_specs=[row(D_ATTN), row(D_RNN), row(D_MODEL), row(D_MODEL), row(D_MODEL), full(wau), full(wru),
                  full(wo), full(g), full(bta)],
        out_specs=row(D_MODEL),
        out_shape=jax.ShapeDtypeStruct((n, D_MODEL), F32),
        compiler_params=_params("parallel"),
        name="outproj_ln1",
    )(o2, r2, sa, sr, x2, wau, wru, wo, g, bta)


ROUTE_TOKENS = 2 * LANES


def _odd_even_merge_sort(n):
    def merge(lo, hi, r):
        step = r * 2
        if step < hi - lo:
            yield from merge(lo, hi, step)
            yield from merge(lo + r, hi, step)
            yield from ((i, i + r) for i in range(lo + r, hi - r, step))
        else:
            yield (lo, lo + r)

    def sort(lo, hi):
        if hi - lo >= 1:
            mid = lo + (hi - lo) // 2
            yield from sort(lo, mid)
            yield from sort(mid + 1, hi)
            yield from merge(lo, hi, 1)

    return tuple(sort(0, n - 1))


_SORT_TOPK = _odd_even_merge_sort(PEER_TOPK)


def _compare_exchange(v, i, j):
    v[i], v[j] = jnp.maximum(v[i], v[j]), jnp.minimum(v[i], v[j])


def _top_k_sorted(tiles):
    v = list(tiles)
    for i, j in _SORT_TOPK:
        _compare_exchange(v, i, j)
    shift = SUBLANES // 2
    while shift >= 1:
        other = [pltpu.roll(x, shift, axis=0) for x in v]
        v = [jnp.maximum(v[i], other[PEER_TOPK - 1 - i]) for i in range(PEER_TOPK)]
        d = PEER_TOPK // 2
        while d >= 1:
            for i in range(PEER_TOPK):
                if i & d == 0:
                    _compare_exchange(v, i, i + d)
            d //= 2
        shift //= 2
    return v


def _count_greater(x, t):
    assert len(t) == 16, "the bisection below is written out for 16 entries"
    one = lambda m, w: jnp.where(m, float(w), 0.0)
    b3 = t[7] > x
    b2 = jnp.where(b3, t[11], t[3]) > x
    b1 = jnp.where(b3, jnp.where(b2, t[13], t[9]), jnp.where(b2, t[5], t[1])) > x
    hi = jnp.where(b2, jnp.where(b1, t[14], t[12]), jnp.where(b1, t[10], t[8]))
    lo = jnp.where(b2, jnp.where(b1, t[6], t[4]), jnp.where(b1, t[2], t[0]))
    b0 = jnp.where(b3, hi, lo) > x
    count = one(b3, 8) + one(b2, 4) + one(b1, 2) + one(b0, 1)
    return jnp.where(t[15] > x, float(PEER_TOPK), count)


def _route_kernel(h_ref, wq_ref, k1_ref, k2_ref, ht_ref, cnt_ref, e1_ref, r2_ref, e2_ref,
                  qt_sc, t1_sc, t2_sc, *, tm):
    ht = jnp.transpose(h_ref[...]).astype(BF16)
    ht_ref[...] = ht
    qt_sc[...] = jnp.dot(wq_ref[...], ht, preferred_element_type=F32).astype(BF16)
    key_iota = lax.broadcasted_iota(jnp.int32, (N_KEYS, tm), 0).astype(F32)
    top_iota = lax.broadcasted_iota(jnp.int32, (PEER_TOPK, tm), 0).astype(F32)
    front_rows = SUBLANES

    def scores(hd):
        base = hd * 2 * PEER_HALF
        s1 = jnp.dot(k1_ref[...], qt_sc[base:base + PEER_HALF], preferred_element_type=F32)
        s2 = jnp.dot(k2_ref[...], qt_sc[base + PEER_HALF:base + 2 * PEER_HALF], preferred_element_type=F32)
        return s1, s2

    sub = lax.broadcasted_iota(jnp.int32, (SUBLANES, tm), 0)
    tiles_of = lambda s: [s[r * SUBLANES:(r + 1) * SUBLANES] for r in range(N_KEYS // SUBLANES)]
    sublane_sum = lambda x: jnp.sum(x, axis=0, keepdims=True)

    def route_head_sorted(hd):
        s1, s2 = scores(hd)
        rows1, rows2 = tiles_of(s1), tiles_of(s2)
        t1 = _top_k_sorted(rows1)
        t2 = _top_k_sorted(rows2)
        t1_lo = t1[SUBLANES - 1]
        for a in range(SUBLANES - 2, -1, -1):
            t1_lo = jnp.where(sub == a, t1[a], t1_lo)
        cand = [t1_lo + t2[b] for b in range(PEER_TOPK)]
        pool = list(cand)
        for a in range(SUBLANES, PEER_TOPK):
            pool[a] = jnp.where(sub == 1, t1[a] + t2[0], cand[a])
        top = _top_k_sorted(pool)
        tau = top[PEER_TOPK - 1]
        cnt_lo = functools.reduce(jnp.add, [jnp.where(c >= tau, 1.0, 0.0) for c in cand])
        cnt = [jnp.broadcast_to(cnt_lo[a:a + 1], (SUBLANES, tm)) for a in range(SUBLANES)]
        cnt += [jnp.where(t1[a] + t2[0] >= tau, 1.0, 0.0) for a in range(SUBLANES, PEER_TOPK)]
        z = functools.reduce(jnp.add, [jnp.exp(t - top[0]) for t in top])

        cnt1_rows, rank2_rows = [], []
        for r in range(N_KEYS // SUBLANES):
            c = jnp.zeros((SUBLANES, tm), F32)
            for a in range(PEER_TOPK):
                c = jnp.where(rows1[r] == t1[a], cnt[a], c)
            cnt1_rows.append(c)
            g = _count_greater(rows2[r], t2)
            rank2_rows.append(jnp.where(g < float(PEER_TOPK), g, NOT_SELECTED_RANK))
        cnt1 = jnp.concatenate(cnt1_rows, axis=0)
        rank2 = jnp.concatenate(rank2_rows, axis=0)
        cnt_ref[hd] = cnt1
        e1_ref[hd] = jnp.exp(s1 - t1[0][0:1]) / z[0:1]
        r2_ref[hd] = rank2.astype(BF16)
        e2_ref[hd] = jnp.exp(s2 - t2[0][0:1]).astype(BF16)

        gap = lambda t: functools.reduce(jnp.minimum, [t[b] - t[b + 1] for b in range(PEER_TOPK - 1)])[0:1]
        n_cnt = sublane_sum(cnt_lo) + functools.reduce(jnp.add, cnt[SUBLANES:])[0:1]
        n_sel = sublane_sum(functools.reduce(jnp.add, cnt1_rows))
        n_rank = sublane_sum(functools.reduce(
            jnp.add, [jnp.where(x < float(PEER_TOPK), 1.0, 0.0) for x in rank2_rows]))
        off = lambda n: jnp.abs(n - float(PEER_TOPK))
        return (off(n_cnt) + off(n_sel) + off(n_rank)
                + jnp.where(jnp.minimum(gap(t1), gap(t2)) > 0.0, 0.0, 1.0))

    def route_head_exact(hd):
        def pick_one(v, iota, n):
            m = jnp.max(v, axis=0, keepdims=True)
            return m, iota == jnp.min(jnp.where(v == m, iota, float(n)), axis=0, keepdims=True)

        s1, s2 = scores(hd)

        def extract(a, carry):
            v1, r1, v2, r2 = carry
            m1, sel1 = pick_one(v1, key_iota, N_KEYS)
            m2, sel2 = pick_one(v2, key_iota, N_KEYS)
            t1_sc[pl.ds(a, 1), :] = m1
            t2_sc[pl.ds(a, 1), :] = m2
            af = jnp.asarray(a, dtype=F32)
            return (jnp.where(sel1, -jnp.inf, v1), jnp.where(sel1, af, r1),
                    jnp.where(sel2, -jnp.inf, v2), jnp.where(sel2, af, r2))

        no_rank = jnp.full((N_KEYS, tm), NOT_SELECTED_RANK, F32)
        _, rank1, _, rank2 = lax.fori_loop(0, PEER_TOPK, extract, (s1, no_rank, s2, no_rank))
        t1 = t1_sc[...]
        t2 = t2_sc[...]
        top0 = t1[0:1] + t2[0:1]

        def pick(_, carry):
            ptr, front, z = carry
            m, sel = pick_one(front, top_iota, PEER_TOPK)
            ptr = ptr + jnp.where(sel, 1.0, 0.0)
            lo = ptr[:front_rows]
            nxt = jnp.full(lo.shape, -jnp.inf, F32)
            for b in range(1, PEER_TOPK):
                nxt = jnp.where(lo == float(b), t2_sc[b:b + 1, :], nxt)
            nxt = jnp.where(lo == 0.0, t2[0:1], nxt)
            front = jnp.concatenate(
                [t1[:front_rows] + nxt, jnp.where(sel[front_rows:], -jnp.inf, front[front_rows:])], axis=0)
            return ptr, front, z + jnp.exp(m - top0)

        cnt, _, z = lax.fori_loop(
            0, PEER_TOPK, pick,
            (jnp.zeros((PEER_TOPK, tm), F32), t1 + t2[0:1], jnp.zeros((1, tm), F32)))

        cnt1 = jnp.zeros((N_KEYS, tm), F32)
        for a in range(PEER_TOPK):
            cnt1 = jnp.where(rank1 == float(a), cnt[a:a + 1], cnt1)
        cnt_ref[hd] = cnt1
        e1_ref[hd] = jnp.exp(s1 - t1[0:1]) / z
        r2_ref[hd] = rank2.astype(BF16)
        e2_ref[hd] = jnp.exp(s2 - t2[0:1]).astype(BF16)

    doubt = [route_head_sorted(hd) for hd in range(PEER_HEADS)]

    @pl.when(jnp.max(functools.reduce(jnp.maximum, doubt)) > 0.0)
    def _():
        for hd in range(PEER_HEADS):
            @pl.when(jnp.max(doubt[hd]) > 0.0)
            def _():
                route_head_exact(hd)


def _route(h2, wq_t, k1, k2, tm):
    n = h2.shape[0]
    full = lambda a: pl.BlockSpec(a.shape, lambda i: (0,) * a.ndim)
    per_head = pl.BlockSpec((PEER_HEADS, N_KEYS, tm), lambda i: (0, 0, i))
    hshape = lambda dt: jax.ShapeDtypeStruct((PEER_HEADS, N_KEYS, n), dt)
    return pl.pallas_call(
        functools.partial(_route_kernel, tm=tm),
        grid=(n // tm,),
        in_specs=[pl.BlockSpec((tm, D_MODEL), lambda i: (i, 0)), full(wq_t), full(k1), full(k2)],
        out_specs=(pl.BlockSpec((D_MODEL, tm), lambda i: (0, i)), per_head, per_head, per_head, per_head),
        out_shape=(jax.ShapeDtypeStruct((D_MODEL, n), BF16), hshape(F32), hshape(F32), hshape(BF16),
                   hshape(BF16)),
        scratch_shapes=[pltpu.VMEM((PEER_HEADS * 2 * PEER_HALF, tm), BF16),
                        pltpu.VMEM((PEER_TOPK, tm), F32), pltpu.VMEM((PEER_TOPK, tm), F32)],
        compiler_params=_params("parallel"),
        name="peer_route",
    )(h2, wq_t, k1, k2)


ROWS_PER_STEP = 16
EXPERTS_PER_STEP = ROWS_PER_STEP * N_KEYS
ROWS_PER_SUB = 4
EXPERTS_PER_SUB = ROWS_PER_SUB * N_KEYS
SUBS_PER_MIX = 2
DENSE_CHUNK = 2 * LANES


def _bf16_row_tile(row):
    tile = jnp.broadcast_to(row, (BF16_ROWS, row.shape[1])).astype(BF16)
    return jnp.concatenate([tile] * (N_KEYS // BF16_ROWS), axis=0)


def _dense_kernel(ht_ref, u_ref, vt_ref, cnt_ref, e1_ref, r2_ref, e2_ref, h_ref, g_ref, b_ref,
                  y_ref, acc_sc, act_sc, coef_sc, *, alpha):
    e = pl.program_id(1)
    tm = act_sc.shape[1]
    n_sub = ROWS_PER_STEP // ROWS_PER_SUB

    @pl.when(e == 0)
    def _():
        acc_sc[...] = jnp.zeros_like(acc_sc)

    def activations(sub):
        rows = slice(sub * EXPERTS_PER_SUB, (sub + 1) * EXPERTS_PER_SUB)
        act_sc[rows, :] = jnp.dot(u_ref[rows, :], ht_ref[...], preferred_element_type=F32)

    def tiles(sub):
        chunk = min(DENSE_CHUNK, tm)
        for il in range(sub * ROWS_PER_SUB, (sub + 1) * ROWS_PER_SUB):
            for c in range(tm // chunk):
                yield il, slice(il * N_KEYS, (il + 1) * N_KEYS), slice(c * chunk, (c + 1) * chunk)

    def routing_weights(sub):
        for il, rows, lanes in tiles(sub):
            w = None
            for hd in range(PEER_HEADS):
                cnt = _bf16_row_tile(cnt_ref[hd, il:il + 1, lanes])
                e1 = _bf16_row_tile(e1_ref[hd, il:il + 1, lanes])
                term = jnp.where(r2_ref[hd, :, lanes] < cnt, e2_ref[hd, :, lanes] * e1, jnp.zeros((), BF16))
                w = term if w is None else w + term
            coef_sc[rows, lanes] = w

    def coefficients(sub):
        for _, rows, lanes in tiles(sub):
            coef_sc[rows, lanes] = coef_sc[rows, lanes] * _gelu_tanh(act_sc[rows, lanes].astype(BF16))

    def mix(first, last):
        rows = slice(first * EXPERTS_PER_SUB, (last + 1) * EXPERTS_PER_SUB)
        acc_sc[...] += jnp.dot(vt_ref[:, rows], coef_sc[rows, :], preferred_element_type=F32)

    routing_weights(0)
    activations(0)
    for sub in range(n_sub):
        if sub + 1 < n_sub:
            routing_weights(sub + 1)
            activations(sub + 1)
        coefficients(sub)
        if sub % SUBS_PER_MIX == SUBS_PER_MIX - 1:
            mix(sub - SUBS_PER_MIX + 1, sub)

    @pl.when(e == pl.num_programs(1) - 1)
    def _():
        peer = jnp.transpose(acc_sc[...])
        y_ref[...] = _layer_norm(alpha * h_ref[...] + peer, g_ref[...], b_ref[...])


def _dense(ht, u_b, vt_b, cnt1, e1n, rank2, e2, h2, g, bta, alpha, tm):
    n = h2.shape[0]
    n_e = N_EXPERTS // EXPERTS_PER_STEP
    full = lambda a: pl.BlockSpec(a.shape, lambda t, e: (0,) * a.ndim)
    rows_blk = pl.BlockSpec((PEER_HEADS, ROWS_PER_STEP, tm), lambda t, e: (0, e, t))
    cols_blk = pl.BlockSpec((PEER_HEADS, N_KEYS, tm), lambda t, e: (0, 0, t))
    return pl.pallas_call(
        functools.partial(_dense_kernel, alpha=alpha),
        grid=(n // tm, n_e),
        in_specs=[
            pl.BlockSpec((D_MODEL, tm), lambda t, e: (0, t)),
            pl.BlockSpec((EXPERTS_PER_STEP, D_MODEL), lambda t, e: (e, 0)),
            pl.BlockSpec((D_MODEL, EXPERTS_PER_STEP), lambda t, e: (0, e)),
            rows_blk, rows_blk, cols_blk, cols_blk,
            pl.BlockSpec((tm, D_MODEL), lambda t, e: (t, 0)),
            full(g), full(bta),
        ],
        out_specs=pl.BlockSpec((tm, D_MODEL), lambda t, e: (t, 0)),
        out_shape=jax.ShapeDtypeStruct((n, D_MODEL), F32),
        scratch_shapes=[pltpu.VMEM((D_MODEL, tm), F32), pltpu.VMEM((EXPERTS_PER_STEP, tm), F32),
                        pltpu.VMEM((EXPERTS_PER_STEP, tm), BF16)],
        compiler_params=_params("parallel", "arbitrary"),
        name="peer_dense",
    )(ht, u_b, vt_b, cnt1, e1n, rank2, e2, h2, g, bta)


def _block_diag(w):
    nb, bi, bo = w.shape
    eye = jnp.eye(nb, dtype=w.dtype)
    return (eye[:, None, :, None] * w[:, :, None, :]).reshape(nb * bi, nb * bo)


def _prep_weights(w_in, b_forget, conv_w, conv_b, w_rg_a, b_rg_a, w_rg_x, b_rg_x, lru_lambda,
                  w_attn_up, w_rnn_up, w_out, ln1_g, ln1_b, peer_w_query, peer_keys_1, peer_keys_2,
                  peer_u, peer_v, ln2_g, ln2_b):
    c_f = 3 * D_ATTN
    w_in_p = jnp.concatenate(
        [w_in[:, :c_f], jnp.pad(w_in[:, c_f:c_f + N_ATTN_HEADS], ((0, 0), (0, F_PAD - N_ATTN_HEADS))),
         w_in[:, c_f + N_ATTN_HEADS:]], axis=1).astype(BF16)
    row = lambda a: a.reshape(1, -1).astype(F32)
    return dict(
        w_in=w_in_p, b_forget=row(b_forget), conv_w=conv_w.astype(F32), conv_b=row(conv_b),
        wa=_block_diag(w_rg_a).astype(BF16), ba=row(b_rg_a), wx=_block_diag(w_rg_x).astype(BF16),
        bx=row(b_rg_x), lam=row(lru_lambda),
        wau=w_attn_up.astype(BF16), wru=w_rnn_up.astype(BF16), wo=w_out.astype(BF16),
        ln1_g=row(ln1_g), ln1_b=row(ln1_b),
        wq_t=jnp.transpose(peer_w_query).astype(BF16), k1=peer_keys_1.astype(BF16),
        k2=peer_keys_2.astype(BF16), u=peer_u.astype(BF16), vt=jnp.transpose(peer_v).astype(BF16),
        ln2_g=row(ln2_g), ln2_b=row(ln2_b),
    )


def _pick_block(n, target):
    blk = min(n, target)
    assert n % blk == 0, (n, blk)
    return blk


def _trunk_layer(x, past_k, past_v, past_logf, conv_hist, h0, p, alpha):
    bsz, t, _ = x.shape
    n = bsz * t
    x2 = x.reshape(n, D_MODEL)
    q, k, v, kb, vb, lf, xr, gg, sa, sr = _inproj(x2, p["w_in"], p["b_forget"], _pick_block(n, 256))

    tq = _pick_block(t, ATTN_Q_BLOCK)
    n_past = 0 if past_k is None else past_k.shape[1]
    t_all = n_past + t
    t_lanes = -(-t_all // LANES) * LANES
    tk = ATTN_K_BLOCK if t_lanes % ATTN_K_BLOCK == 0 else t_lanes
    t_pad = -(-t_all // tk) * tk
    lf3 = lf.reshape(bsz, t, N_ATTN_HEADS)
    kb3 = kb.reshape(bsz, t, D_ATTN)
    vb3 = vb.reshape(bsz, t, D_ATTN)
    if past_k is not None:
        lf3 = jnp.concatenate([past_logf.astype(F32), lf3], axis=1)
        kb3 = jnp.concatenate([past_k.reshape(bsz, n_past, D_ATTN).astype(BF16), kb3], axis=1)
        vb3 = jnp.concatenate([past_v.reshape(bsz, n_past, D_ATTN).astype(BF16), vb3], axis=1)
    pad = ((0, 0), (0, t_pad - t_all), (0, 0))
    lf3, kb3, vb3 = jnp.pad(lf3, pad), jnp.pad(kb3, pad), jnp.pad(vb3, pad)
    f_t = _cumsum_time(jnp.transpose(lf3, (0, 2, 1)))
    f_g = f_t.reshape(bsz, N_HEAD_GROUPS, HEADS_PER_GROUP, t_pad)
    fq = jnp.transpose(f_g[:, :, :, n_past:n_past + t], (0, 1, 3, 2))
    fk = jnp.transpose(f_g.reshape(bsz, N_HEAD_GROUPS, HEADS_PER_GROUP, t_pad // tk, tk), (0, 1, 3, 2, 4))
    o = _attention(q.reshape(bsz, t, D_ATTN), kb3, vb3, fq, fk, tq=tq, tk=tk, q_off=n_past)

    hist8 = jnp.pad(conv_hist.astype(F32), ((0, 0), (SUBLANES - (CONV_WIDTH - 1), 0), (0, 0)))
    xr3 = xr.reshape(bsz, t, D_RNN)
    rnn_out, h_last = _rnn(xr3, gg.reshape(bsz, t, D_RNN), hist8, h0.astype(F32).reshape(bsz, 1, D_RNN),
                           p["conv_w"], p["conv_b"], p["wa"], p["ba"], p["wx"], p["bx"], p["lam"],
                           _pick_block(t, 256))
    new_hist = jnp.concatenate([conv_hist.astype(F32), xr3], axis=1)[:, -(CONV_WIDTH - 1):]

    h = _outproj(o.reshape(n, D_ATTN), rnn_out.reshape(n, D_RNN), sa, sr, x2, p["wau"], p["wru"], p["wo"],
                 p["ln1_g"], p["ln1_b"], alpha, _pick_block(n, 256))
    ht, cnt1, e1n, rank2, e2 = _route(h, p["wq_t"], p["k1"], p["k2"], _pick_block(n, ROUTE_TOKENS))
    y = _dense(ht, p["u"], p["vt"], cnt1, e1n, rank2, e2, h, p["ln2_g"], p["ln2_b"], alpha,
               _pick_block(n, 512))
    return (y.reshape(bsz, t, D_MODEL), k.reshape(bsz, t, N_ATTN_HEADS, ATTN_HEAD_DIM),
            v.reshape(bsz, t, N_ATTN_HEADS, ATTN_HEAD_DIM), lf.reshape(bsz, t, N_ATTN_HEADS), new_hist,
            h_last.reshape(bsz, D_RNN))


def kernel(x_prompt, x_sample, cache_k, cache_v, cache_logf, state_conv, state_rnn, w_in, b_forget, conv_w, conv_b, w_rg_a, b_rg_a, w_rg_x, b_rg_x, lru_lambda, w_attn_up, w_rnn_up, w_out, ln1_g, ln1_b, peer_w_query, peer_keys_1, peer_keys_2, peer_u, peer_v, ln2_g, ln2_b):
    depth = w_in.shape[0]
    alpha = (2 * depth) ** 0.25
    layer_weights = (w_in, b_forget, conv_w, conv_b, w_rg_a, b_rg_a, w_rg_x, b_rg_x, lru_lambda, w_attn_up,
                     w_rnn_up, w_out, ln1_g, ln1_b, peer_w_query, peer_keys_1, peer_keys_2, peer_u, peer_v,
                     ln2_g, ln2_b)
    hp, hs = x_prompt, x_sample
    prompt_state, sample_state = [], []
    for l in range(depth):
        p = _prep_weights(*(w[l] for w in layer_weights))
        zero_hist = jnp.zeros((hp.shape[0], CONV_WIDTH - 1, D_RNN), F32)
        zero_h = jnp.zeros((hp.shape[0], D_RNN), F32)
        hp, *st_p = _trunk_layer(hp, None, None, None, zero_hist, zero_h, p, alpha)
        hs, *st_s = _trunk_layer(hs, cache_k[l], cache_v[l], cache_logf[l], state_conv[l], state_rnn[l], p,
                                 alpha)
        prompt_state.append(st_p)
        sample_state.append(st_s)
    stack = lambda states, i: jnp.stack([s[i] for s in states])
    return (hp, hs) + tuple(stack(prompt_state, i) for i in range(5)) + tuple(
        stack(sample_state, i) for i in range(5))
```

```python
import functools
import math

import jax
import jax.numpy as jnp
from jax import lax
from jax.experimental import pallas as pl
from jax.experimental.pallas import tpu as pltpu

F32 = jnp.float32
BF16 = jnp.bfloat16

D_MODEL = 1024
N_ATTN_HEADS = 8
ATTN_HEAD_DIM = 64
D_ATTN = N_ATTN_HEADS * ATTN_HEAD_DIM
ATTN_SCALE = ATTN_HEAD_DIM ** -0.5
D_RNN = 512
N_RNN_BLOCKS = 8
RNN_BLOCK = D_RNN // N_RNN_BLOCKS
CONV_WIDTH = 4
LRU_C = 8.0
N_KEYS = 128
N_EXPERTS = N_KEYS * N_KEYS
PEER_HEADS = 8
PEER_TOPK = 16
PEER_HALF = 128
LN_EPS = 1e-5

LANES = 128
SUBLANES = 8
BF16_ROWS = 2 * SUBLANES
VMEM_LIMIT_BYTES = 56 * 1024 * 1024

HEADS_PER_GROUP = 4
GROUP_LANES = HEADS_PER_GROUP * ATTN_HEAD_DIM
N_HEAD_GROUPS = N_ATTN_HEADS // HEADS_PER_GROUP
ATTN_Q_BLOCK = 512
ATTN_K_BLOCK = 512
ATTN_ROW_BLOCK = 32
F_PAD = LANES
NEG_BIG = -1e30
LOG2_E = math.log2(math.e)
NOT_SELECTED_RANK = 99.0

_C_Q = 0
_C_K = _C_Q + D_ATTN
_C_V = _C_K + D_ATTN
_C_F = _C_V + D_ATTN
_C_XR = _C_F + F_PAD
_C_GATE = _C_XR + D_RNN
_C_GA = _C_GATE + D_RNN
_C_GR = _C_GA + D_MODEL
_C_END = _C_GR + D_MODEL


def _params(*sem):
    return pltpu.CompilerParams(dimension_semantics=sem, vmem_limit_bytes=VMEM_LIMIT_BYTES)


def _sigmoid(x):
    return 1.0 / (1.0 + jnp.exp(-x))


def _gelu_tanh(x):
    half = 0.5 * x
    return half + half * jnp.tanh(x * (0.7978845608028654 + 0.035677408136300125 * (x * x)))


def _softplus(x):
    return jnp.maximum(x, 0.0) + jnp.log1p(jnp.exp(-jnp.abs(x)))


def _layer_norm(x, g, b):
    mu = jnp.mean(x, axis=-1, keepdims=True)
    xc = x - mu
    var = jnp.mean(xc * xc, axis=-1, keepdims=True)
    return xc * lax.rsqrt(var + LN_EPS) * g + b


def _inproj_kernel(x_ref, w_ref, bf_ref, q_ref, k_ref, v_ref, kb_ref, vb_ref, lf_ref, xr_ref,
                   gg_ref, sa_ref, sr_ref):
    xb = x_ref[...].astype(BF16)

    def mm(lo, hi):
        return jnp.dot(xb, w_ref[:, lo:hi], preferred_element_type=F32)

    q_ref[...] = (mm(_C_Q, _C_K) * (ATTN_SCALE * LOG2_E)).astype(BF16)
    k = mm(_C_K, _C_V)
    k_ref[...] = k
    kb_ref[...] = k.astype(BF16)
    v = mm(_C_V, _C_F)
    v_ref[...] = v
    vb_ref[...] = v.astype(BF16)
    f = mm(_C_F, _C_XR)[:, :N_ATTN_HEADS] + bf_ref[...]
    lf_ref[...] = -_softplus(-f)
    xr_ref[...] = mm(_C_XR, _C_GATE)
    gg_ref[...] = _gelu_tanh(mm(_C_GATE, _C_GA)).astype(BF16)
    sa_ref[...] = _sigmoid(mm(_C_GA, _C_GR)).astype(BF16)
    sr_ref[...] = _sigmoid(mm(_C_GR, _C_END)).astype(BF16)


def _inproj(x2, w_in_p, b_forget, tm):
    n = x2.shape[0]
    row = lambda w: pl.BlockSpec((tm, w), lambda i: (i, 0))
    full = lambda a: pl.BlockSpec(a.shape, lambda i: (0,) * a.ndim)
    out_shape = (
        jax.ShapeDtypeStruct((n, D_ATTN), BF16),
        jax.ShapeDtypeStruct((n, D_ATTN), F32),
        jax.ShapeDtypeStruct((n, D_ATTN), F32),
        jax.ShapeDtypeStruct((n, D_ATTN), BF16),
        jax.ShapeDtypeStruct((n, D_ATTN), BF16),
        jax.ShapeDtypeStruct((n, N_ATTN_HEADS), F32),
        jax.ShapeDtypeStruct((n, D_RNN), F32),
        jax.ShapeDtypeStruct((n, D_RNN), BF16),
        jax.ShapeDtypeStruct((n, D_MODEL), BF16),
        jax.ShapeDtypeStruct((n, D_MODEL), BF16),
    )
    out_specs = (row(D_ATTN), row(D_ATTN), row(D_ATTN), row(D_ATTN), row(D_ATTN), row(N_ATTN_HEADS),
                 row(D_RNN), row(D_RNN), row(D_MODEL), row(D_MODEL))
    return pl.pallas_call(
        _inproj_kernel,
        grid=(n // tm,),
        in_specs=[row(D_MODEL), full(w_in_p), full(b_forget)],
        out_specs=out_specs,
        out_shape=out_shape,
        compiler_params=_params("parallel"),
        name="inproj",
    )(x2, w_in_p, b_forget)


def _cumsum_kernel(x_ref, o_ref):
    x = x_ref[0]
    t = x.shape[1]
    lane = lax.broadcasted_iota(jnp.int32, x.shape, 1)
    d = 1
    while d < t:
        x = x + jnp.where(lane >= d, pltpu.roll(x, d, axis=1), 0.0)
        d *= 2
    o_ref[0] = x * LOG2_E


def _cumsum_time(lf_t):
    b, h, t = lf_t.shape
    spec = pl.BlockSpec((1, h, t), lambda i: (i, 0, 0))
    return pl.pallas_call(
        _cumsum_kernel, grid=(b,), in_specs=[spec], out_specs=spec,
        out_shape=jax.ShapeDtypeStruct(lf_t.shape, F32),
        compiler_params=_params("parallel"), name="logf_cumsum",
    )(lf_t)


def _attn_kernel(q_ref, k_ref, v_ref, fq_ref, fk_ref, o_ref, q_sc, fq_sc, s_sc, p_sc, acc_sc, m_sc, al_sc,
                 lp_sc, *, tq, tk, q_off):
    qi = pl.program_id(2)
    row0 = q_off + qi * tq
    n_full = (row0 + 1) // tk
    heads = range(HEADS_PER_GROUP)
    rb = min(ATTN_ROW_BLOCK, tq)
    lane = lax.broadcasted_iota(jnp.int32, (tq, GROUP_LANES), 1)
    in_head = [(lane >= j * ATTN_HEAD_DIM) & (lane < (j + 1) * ATTN_HEAD_DIM) for j in heads]
    q = q_ref[0]
    for j in heads:
        q_sc[j * tq:(j + 1) * tq, :] = jnp.where(in_head[j], q, jnp.zeros_like(q))
        fq_sc[j * tq:(j + 1) * tq, :] = jnp.broadcast_to(fq_ref[0, 0, :, j:j + 1], (tq, LANES))
    m_sc[...] = jnp.full_like(m_sc, NEG_BIG)
    lp_sc[...] = jnp.zeros_like(lp_sc)
    acc_sc[...] = jnp.zeros_like(acc_sc)
    halves = [slice(0, 2 * tq), slice(2 * tq, 4 * tq)]
    n_lane_tiles = tk // LANES

    def scores(c, s_sc):
        ks = k_ref[0, pl.ds(pl.multiple_of(c * tk, tk), tk), :]
        for hv in halves:
            s_sc[hv, :] = lax.dot_general(q_sc[hv, :], ks, (((1,), (1,)), ((), ())),
                                          preferred_element_type=F32)

    def absorb(c, s_sc, masked):
        start = pl.multiple_of(c * tk, tk)
        fk = fk_ref[0, 0, c]
        for j in heads:
            for r in range(tq // rb):
                rows = slice(j * tq + r * rb, j * tq + (r + 1) * rb)
                s = s_sc[rows, :] - fk[j:j + 1, :]
                if masked:
                    q_pos = row0 + r * rb + lax.broadcasted_iota(jnp.int32, (rb, tk), 0)
                    k_pos = c * tk + lax.broadcasted_iota(jnp.int32, (rb, tk), 1)
                    s = jnp.where(k_pos <= q_pos, s, NEG_BIG)
                tiles = [s[:, t * LANES:(t + 1) * LANES] for t in range(n_lane_tiles)]
                row_max = jnp.max(functools.reduce(jnp.maximum, tiles), axis=1, keepdims=True)
                fq = fq_sc[rows, :]
                m_old = m_sc[rows, :]
                m_new = jnp.maximum(m_old, jnp.broadcast_to(row_max, (rb, LANES)) + fq)
                alpha = jnp.exp2(m_old - m_new)
                shift = fq - m_new
                p_tiles = [jnp.exp2(tile + shift) for tile in tiles]
                m_sc[rows, :] = m_new
                al_sc[rows, :] = alpha
                lp_sc[rows, :] = alpha * lp_sc[rows, :] + functools.reduce(jnp.add, p_tiles)
                p_sc[rows, :] = jnp.concatenate(p_tiles, axis=1).astype(BF16)
        vs = v_ref[0, pl.ds(start, tk), :]
        for hv in halves:
            pv = jnp.dot(p_sc[hv, :], vs, preferred_element_type=F32)
            alpha = al_sc[hv, :]
            acc_sc[hv, :] = jnp.concatenate([alpha] * (GROUP_LANES // LANES), axis=1) * acc_sc[hv, :] + pv

    def step(c, carry):
        scores(c, s_sc)
        absorb(c, s_sc, masked=False)
        return carry

    lax.fori_loop(0, n_full, step, 0)
    scores(n_full, s_sc)
    absorb(n_full, s_sc, masked=True)
    o_all = acc_sc[...] / jnp.sum(lp_sc[...], axis=1, keepdims=True)
    out = jnp.zeros((tq, GROUP_LANES), F32)
    for j in heads:
        out = jnp.where(in_head[j], o_all[j * tq:(j + 1) * tq], out)
    o_ref[0] = out.astype(BF16)


def _attention(q, k_all, v_all, fq, fk, *, tq, tk, q_off):
    b, t, _ = q.shape
    t_k = k_all.shape[1]
    n_kv = t_k // tk
    assert all((q_off + i * tq) % tk + tq <= tk for i in range(t // tq)), "a query block straddles key chunks"
    kern = functools.partial(_attn_kernel, tq=tq, tk=tk, q_off=q_off)
    rows = HEADS_PER_GROUP * tq
    return pl.pallas_call(
        kern,
        grid=(b, N_HEAD_GROUPS, t // tq),
        in_specs=[
            pl.BlockSpec((1, tq, GROUP_LANES), lambda bi, g, i: (bi, i, g)),
            pl.BlockSpec((1, t_k, GROUP_LANES), lambda bi, g, i: (bi, 0, g)),
            pl.BlockSpec((1, t_k, GROUP_LANES), lambda bi, g, i: (bi, 0, g)),
            pl.BlockSpec((1, 1, tq, HEADS_PER_GROUP), lambda bi, g, i: (bi, g, i, 0)),
            pl.BlockSpec((1, 1, n_kv, HEADS_PER_GROUP, tk), lambda bi, g, i: (bi, g, 0, 0, 0)),
        ],
        out_specs=pl.BlockSpec((1, tq, GROUP_LANES), lambda bi, g, i: (bi, i, g)),
        out_shape=jax.ShapeDtypeStruct((b, t, D_ATTN), BF16),
        scratch_shapes=[
            pltpu.VMEM((rows, GROUP_LANES), BF16),
            pltpu.VMEM((rows, LANES), F32),
            pltpu.VMEM((rows, tk), F32),
            pltpu.VMEM((rows, tk), BF16),
            pltpu.VMEM((rows, GROUP_LANES), F32),
            pltpu.VMEM((rows, LANES), F32),
            pltpu.VMEM((rows, LANES), F32),
            pltpu.VMEM((rows, LANES), F32),
        ],
        compiler_params=_params("parallel", "parallel", "arbitrary"),
        name="fox_attention",
    )(q, k_all, v_all, fq, fk)


def _rnn_kernel(xr_ref, gg_ref, hist_ref, h0_ref, cw_ref, cb_ref, wa_ref, ba_ref, wx_ref, bx_ref,
                lam_ref, out_ref, hl_ref, prev_sc, h_sc, *, tb):
    t = pl.program_id(1)

    @pl.when(t == 0)
    def _():
        prev_sc[...] = hist_ref[0]
        h_sc[...] = h0_ref[0]

    x = xr_ref[0]
    prev = prev_sc[...]
    row8 = lax.broadcasted_iota(jnp.int32, (SUBLANES, D_RNN), 0)
    xc = x * cw_ref[CONV_WIDTH - 1:CONV_WIDTH, :] + cb_ref[...]
    for s in range(1, CONV_WIDTH):
        xs = pltpu.roll(x, s, axis=0)
        head = jnp.where(row8 < s, pltpu.roll(prev, s, axis=0), xs[:SUBLANES])
        xs = jnp.concatenate([head, xs[SUBLANES:]], axis=0)
        xc = xc + xs * cw_ref[CONV_WIDTH - 1 - s:CONV_WIDTH - s, :]
    prev_sc[...] = x[tb - SUBLANES:tb]

    xcb = xc.astype(BF16)
    r = _sigmoid(jnp.dot(xcb, wa_ref[...], preferred_element_type=F32) + ba_ref[...])
    ig = _sigmoid(jnp.dot(xcb, wx_ref[...], preferred_element_type=F32) + bx_ref[...])
    log_a = (-LRU_C) * r * _softplus(-lam_ref[...])
    a = jnp.exp(log_a)
    bterm = jnp.sqrt(1.0 - a * a) * ig * xc

    row = lax.broadcasted_iota(jnp.int32, (tb, D_RNN), 0)
    d = 1
    while d < tb:
        valid = row >= d
        a_s = pltpu.roll(a, d, axis=0)
        b_s = pltpu.roll(bterm, d, axis=0)
        bterm = jnp.where(valid, a * b_s + bterm, bterm)
        a = jnp.where(valid, a * a_s, a)
        d *= 2
    h = bterm + a * h_sc[...]
    h_last = h[tb - 1:tb]
    h_sc[...] = h_last
    hl_ref[0] = h_last
    out_ref[0] = (h * gg_ref[0].astype(F32)).astype(BF16)


def _rnn(xr, gg, hist8, h0, conv_w, conv_b, wa, ba, wx, bx, lam, tb):
    b, t, _ = xr.shape
    blk = pl.BlockSpec((1, tb, D_RNN), lambda bi, ti: (bi, ti, 0))
    full = lambda a: pl.BlockSpec(a.shape, lambda bi, ti: (0,) * a.ndim)
    per_b = lambda r: pl.BlockSpec((1, r, D_RNN), lambda bi, ti: (bi, 0, 0))
    return pl.pallas_call(
        functools.partial(_rnn_kernel, tb=tb),
        grid=(b, t // tb),
        in_specs=[blk, blk, per_b(SUBLANES), per_b(1), full(conv_w), full(conv_b), full(wa), full(ba),
                  full(wx), full(bx), full(lam)],
        out_specs=(blk, per_b(1)),
        out_shape=(jax.ShapeDtypeStruct((b, t, D_RNN), BF16), jax.ShapeDtypeStruct((b, 1, D_RNN), F32)),
        scratch_shapes=[pltpu.VMEM((SUBLANES, D_RNN), F32), pltpu.VMEM((1, D_RNN), F32)],
        compiler_params=_params("parallel", "arbitrary"),
        name="conv_rglru",
    )(xr, gg, hist8, h0, conv_w, conv_b, wa, ba, wx, bx, lam)


def _outproj_kernel(o_ref, r_ref, sa_ref, sr_ref, x_ref, wau_ref, wru_ref, wo_ref, g_ref, b_ref,
                    h_ref, *, alpha):
    up_a = jnp.dot(o_ref[...], wau_ref[...], preferred_element_type=F32)
    up_r = jnp.dot(r_ref[...], wru_ref[...], preferred_element_type=F32)
    merged = sa_ref[...].astype(F32) * up_a + sr_ref[...].astype(F32) * up_r
    mix = jnp.dot(merged.astype(BF16), wo_ref[...], preferred_element_type=F32)
    h_ref[...] = _layer_norm(alpha * x_ref[...] + mix, g_ref[...], b_ref[...])


def _outproj(o2, r2, sa, sr, x2, wau, wru, wo, g, bta, alpha, tm):
    n = x2.shape[0]
    row = lambda w: pl.BlockSpec((tm, w), lambda i: (i, 0))
    full = lambda a: pl.BlockSpec(a.shape, lambda i: (0,) * a.ndim)
    return pl.pallas_call(
        functools.partial(_outproj_kernel, alpha=alpha),
        grid=(n // tm,),
        in_specs=[row(D_ATTN), row(D_RNN), row(D_MODEL), row(D_MODEL), row(D_MODEL), full(wau), full(wru),
                  full(wo), full(g), full(bta)],
        out_specs=row(D_MODEL),
        out_shape=jax.ShapeDtypeStruct((n, D_MODEL), F32),
        compiler_params=_params("parallel"),
        name="outproj_ln1",
    )(o2, r2, sa, sr, x2, wau, wru, wo, g, bta)


ROUTE_TOKENS = 2 * LANES


def _odd_even_merge_sort(n):
    def merge(lo, hi, r):
        step = r * 2
        if step < hi - lo:
            yield from merge(lo, hi, step)
            yield from merge(lo + r, hi, step)
            yield from ((i, i + r) for i in range(lo + r, hi - r, step))
        else:
            yield (lo, lo + r)

    def sort(lo, hi):
        if hi - lo >= 1:
            mid = lo + (hi - lo) // 2
            yield from sort(lo, mid)
            yield from sort(mid + 1, hi)
            yield from merge(lo, hi, 1)

    return tuple(sort(0, n - 1))


_SORT_TOPK = _odd_even_merge_sort(PEER_TOPK)


def _compare_exchange(v, i, j):
    v[i], v[j] = jnp.maximum(v[i], v[j]), jnp.minimum(v[i], v[j])


def _top_k_sorted(tiles):
    v = list(tiles)
    for i, j in _SORT_TOPK:
        _compare_exchange(v, i, j)
    shift = SUBLANES // 2
    while shift >= 1:
        other = [pltpu.roll(x, shift, axis=0) for x in v]
        v = [jnp.maximum(v[i], other[PEER_TOPK - 1 - i]) for i in range(PEER_TOPK)]
        d = PEER_TOPK // 2
        while d >= 1:
            for i in range(PEER_TOPK):
                if i & d == 0:
                    _compare_exchange(v, i, i + d)
            d //= 2
        shift //= 2
    return v


def _count_greater(x, t):
    assert len(t) == 16, "the bisection below is written out for 16 entries"
    one = lambda m, w: jnp.where(m, float(w), 0.0)
    b3 = t[7] > x
    b2 = jnp.where(b3, t[11], t[3]) > x
    b1 = jnp.where(b3, jnp.where(b2, t[13], t[9]), jnp.where(b2, t[5], t[1])) > x
    hi = jnp.where(b2, jnp.where(b1, t[14], t[12]), jnp.where(b1, t[10], t[8]))
    lo = jnp.where(b2, jnp.where(b1, t[6], t[4]), jnp.where(b1, t[2], t[0]))
    b0 = jnp.where(b3, hi, lo) > x
    count = one(b3, 8) + one(b2, 4) + one(b1, 2) + one(b0, 1)
    return jnp.where(t[15] > x, float(PEER_TOPK), count)


def _route_kernel(h_ref, wq_ref, k1_ref, k2_ref, ht_ref, cnt_ref, e1_ref, r2_ref, e2_ref,
                  qt_sc, t1_sc, t2_sc, *, tm):
    ht = jnp.transpose(h_ref[...]).astype(BF16)
    ht_ref[...] = ht
    qt_sc[...] = jnp.dot(wq_ref[...], ht, preferred_element_type=F32).astype(BF16)
    key_iota = lax.broadcasted_iota(jnp.int32, (N_KEYS, tm), 0).astype(F32)
    top_iota = lax.broadcasted_iota(jnp.int32, (PEER_TOPK, tm), 0).astype(F32)
    front_rows = SUBLANES

    def scores(hd):
        base = hd * 2 * PEER_HALF
        s1 = jnp.dot(k1_ref[...], qt_sc[base:base + PEER_HALF], preferred_element_type=F32)
        s2 = jnp.dot(k2_ref[...], qt_sc[base + PEER_HALF:base + 2 * PEER_HALF], preferred_element_type=F32)
        return s1, s2

    sub = lax.broadcasted_iota(jnp.int32, (SUBLANES, tm), 0)
    tiles_of = lambda s: [s[r * SUBLANES:(r + 1) * SUBLANES] for r in range(N_KEYS // SUBLANES)]
    sublane_sum = lambda x: jnp.sum(x, axis=0, keepdims=True)

    def route_head_sorted(hd):
        s1, s2 = scores(hd)
        rows1, rows2 = tiles_of(s1), tiles_of(s2)
        t1 = _top_k_sorted(rows1)
        t2 = _top_k_sorted(rows2)
        t1_lo = t1[SUBLANES - 1]
        for a in range(SUBLANES - 2, -1, -1):
            t1_lo = jnp.where(sub == a, t1[a], t1_lo)
        cand = [t1_lo + t2[b] for b in range(PEER_TOPK)]
        pool = list(cand)
        for a in range(SUBLANES, PEER_TOPK):
            pool[a] = jnp.where(sub == 1, t1[a] + t2[0], cand[a])
        top = _top_k_sorted(pool)
        tau = top[PEER_TOPK - 1]
        cnt_lo = functools.reduce(jnp.add, [jnp.where(c >= tau, 1.0, 0.0) for c in cand])
        cnt = [jnp.broadcast_to(cnt_lo[a:a + 1], (SUBLANES, tm)) for a in range(SUBLANES)]
        cnt += [jnp.where(t1[a] + t2[0] >= tau, 1.0, 0.0) for a in range(SUBLANES, PEER_TOPK)]
        z = functools.reduce(jnp.add, [jnp.exp(t - top[0]) for t in top])

        cnt1_rows, rank2_rows = [], []
        for r in range(N_KEYS // SUBLANES):
            c = jnp.zeros((SUBLANES, tm), F32)
            for a in range(PEER_TOPK):
                c = jnp.where(rows1[r] == t1[a], cnt[a], c)
            cnt1_rows.append(c)
            g = _count_greater(rows2[r], t2)
            rank2_rows.append(jnp.where(g < float(PEER_TOPK), g, NOT_SELECTED_RANK))
        cnt1 = jnp.concatenate(cnt1_rows, axis=0)
        rank2 = jnp.concatenate(rank2_rows, axis=0)
        cnt_ref[hd] = cnt1
        e1_ref[hd] = jnp.exp(s1 - t1[0][0:1]) / z[0:1]
        r2_ref[hd] = rank2.astype(BF16)
        e2_ref[hd] = jnp.exp(s2 - t2[0][0:1]).astype(BF16)

        gap = lambda t: functools.reduce(jnp.minimum, [t[b] - t[b + 1] for b in range(PEER_TOPK - 1)])[0:1]
        n_cnt = sublane_sum(cnt_lo) + functools.reduce(jnp.add, cnt[SUBLANES:])[0:1]
        n_sel = sublane_sum(functools.reduce(jnp.add, cnt1_rows))
        n_rank = sublane_sum(functools.reduce(
            jnp.add, [jnp.where(x < float(PEER_TOPK), 1.0, 0.0) for x in rank2_rows]))
        off = lambda n: jnp.abs(n - float(PEER_TOPK))
        return (off(n_cnt) + off(n_sel) + off(n_rank)
                + jnp.where(jnp.minimum(gap(t1), gap(t2)) > 0.0, 0.0, 1.0))

    def route_head_exact(hd):
        def pick_one(v, iota, n):
            m = jnp.max(v, axis=0, keepdims=True)
            return m, iota == jnp.min(jnp.where(v == m, iota, float(n)), axis=0, keepdims=True)

        s1, s2 = scores(hd)

        def extract(a, carry):
            v1, r1, v2, r2 = carry
            m1, sel1 = pick_one(v1, key_iota, N_KEYS)
            m2, sel2 = pick_one(v2, key_iota, N_KEYS)
            t1_sc[pl.ds(a, 1), :] = m1
            t2_sc[pl.ds(a, 1), :] = m2
            af = jnp.asarray(a, dtype=F32)
            return (jnp.where(sel1, -jnp.inf, v1), jnp.where(sel1, af, r1),
                    jnp.where(sel2, -jnp.inf, v2), jnp.where(sel2, af, r2))

        no_rank = jnp.full((N_KEYS, tm), NOT_SELECTED_RANK, F32)
        _, rank1, _, rank2 = lax.fori_loop(0, PEER_TOPK, extract, (s1, no_rank, s2, no_rank))
        t1 = t1_sc[...]
        t2 = t2_sc[...]
        top0 = t1[0:1] + t2[0:1]

        def pick(_, carry):
            ptr, front, z = carry
            m, sel = pick_one(front, top_iota, PEER_TOPK)
            ptr = ptr + jnp.where(sel, 1.0, 0.0)
            lo = ptr[:front_rows]
            nxt = jnp.full(lo.shape, -jnp.inf, F32)
            for b in range(1, PEER_TOPK):
                nxt = jnp.where(lo == float(b), t2_sc[b:b + 1, :], nxt)
            nxt = jnp.where(lo == 0.0, t2[0:1], nxt)
            front = jnp.concatenate(
                [t1[:front_rows] + nxt, jnp.where(sel[front_rows:], -jnp.inf, front[front_rows:])], axis=0)
            return ptr, front, z + jnp.exp(m - top0)

        cnt, _, z = lax.fori_loop(
            0, PEER_TOPK, pick,
            (jnp.zeros((PEER_TOPK, tm), F32), t1 + t2[0:1], jnp.zeros((1, tm), F32)))

        cnt1 = jnp.zeros((N_KEYS, tm), F32)
        for a in range(PEER_TOPK):
            cnt1 = jnp.where(rank1 == float(a), cnt[a:a + 1], cnt1)
        cnt_ref[hd] = cnt1
        e1_ref[hd] = jnp.exp(s1 - t1[0:1]) / z
        r2_ref[hd] = rank2.astype(BF16)
        e2_ref[hd] = jnp.exp(s2 - t2[0:1]).astype(BF16)

    doubt = [route_head_sorted(hd) for hd in range(PEER_HEADS)]

    @pl.when(jnp.max(functools.reduce(jnp.maximum, doubt)) > 0.0)
    def _():
        for hd in range(PEER_HEADS):
            @pl.when(jnp.max(doubt[hd]) > 0.0)
            def _():
                route_head_exact(hd)


def _route(h2, wq_t, k1, k2, tm):
    n = h2.shape[0]
    full = lambda a: pl.BlockSpec(a.shape, lambda i: (0,) * a.ndim)
    per_head = pl.BlockSpec((PEER_HEADS, N_KEYS, tm), lambda i: (0, 0, i))
    hshape = lambda dt: jax.ShapeDtypeStruct((PEER_HEADS, N_KEYS, n), dt)
    return pl.pallas_call(
        functools.partial(_route_kernel, tm=tm),
        grid=(n // tm,),
        in_specs=[pl.BlockSpec((tm, D_MODEL), lambda i: (i, 0)), full(wq_t), full(k1), full(k2)],
        out_specs=(pl.BlockSpec((D_MODEL, tm), lambda i: (0, i)), per_head, per_head, per_head, per_head),
        out_shape=(jax.ShapeDtypeStruct((D_MODEL, n), BF16), hshape(F32), hshape(F32), hshape(BF16),
                   hshape(BF16)),
        scratch_shapes=[pltpu.VMEM((PEER_HEADS * 2 * PEER_HALF, tm), BF16),
                        pltpu.VMEM((PEER_TOPK, tm), F32), pltpu.VMEM((PEER_TOPK, tm), F32)],
        compiler_params=_params("parallel"),
        name="peer_route",
    )(h2, wq_t, k1, k2)


ROWS_PER_STEP = 16
EXPERTS_PER_STEP = ROWS_PER_STEP * N_KEYS
ROWS_PER_SUB = 4
EXPERTS_PER_SUB = ROWS_PER_SUB * N_KEYS
SUBS_PER_MIX = 2
DENSE_CHUNK = 2 * LANES


def _bf16_row_tile(row):
    tile = jnp.broadcast_to(row, (BF16_ROWS, row.shape[1])).astype(BF16)
    return jnp.concatenate([tile] * (N_KEYS // BF16_ROWS), axis=0)


def _dense_kernel(ht_ref, u_ref, vt_ref, cnt_ref, e1_ref, r2_ref, e2_ref, h_ref, g_ref, b_ref,
                  y_ref, acc_sc, act_sc, coef_sc, *, alpha):
    e = pl.program_id(1)
    tm = act_sc.shape[1]
    n_sub = ROWS_PER_STEP // ROWS_PER_SUB

    @pl.when(e == 0)
    def _():
        acc_sc[...] = jnp.zeros_like(acc_sc)

    chunk = min(DENSE_CHUNK, tm)
    chunks = [slice(c * chunk, (c + 1) * chunk) for c in range(tm // chunk)]
    sub_rows = lambda first, last: slice(first * EXPERTS_PER_SUB, (last + 1) * EXPERTS_PER_SUB)
    key_rows = lambda sub: range(sub * ROWS_PER_SUB, (sub + 1) * ROWS_PER_SUB)

    def activations(sub, lanes):
        rows = sub_rows(sub, sub)
        act_sc[rows, lanes] = jnp.dot(u_ref[rows, :], ht_ref[:, lanes], preferred_element_type=F32)

    def routing_weights(sub, lanes):
        for il in key_rows(sub):
            w = None
            for hd in range(PEER_HEADS):
                cnt = _bf16_row_tile(cnt_ref[hd, il:il + 1, lanes])
                e1 = _bf16_row_tile(e1_ref[hd, il:il + 1, lanes])
                term = jnp.where(r2_ref[hd, :, lanes] < cnt, e2_ref[hd, :, lanes] * e1, jnp.zeros((), BF16))
                w = term if w is None else w + term
            coef_sc[il * N_KEYS:(il + 1) * N_KEYS, lanes] = w

    def coefficients(sub, lanes):
        for il in key_rows(sub):
            rows = slice(il * N_KEYS, (il + 1) * N_KEYS)
            coef_sc[rows, lanes] = coef_sc[rows, lanes] * _gelu_tanh(act_sc[rows, lanes].astype(BF16))

    def mix(first, last, lanes):
        rows = sub_rows(first, last)
        acc_sc[:, lanes] += jnp.dot(vt_ref[:, rows], coef_sc[rows, lanes], preferred_element_type=F32)

    for lanes in chunks:
        routing_weights(0, lanes)
        activations(0, lanes)
    for sub in range(n_sub):
        for lanes in chunks:
            if sub + 1 < n_sub:
                routing_weights(sub + 1, lanes)
                activations(sub + 1, lanes)
            coefficients(sub, lanes)
            if sub % SUBS_PER_MIX == SUBS_PER_MIX - 1:
                mix(sub - SUBS_PER_MIX + 1, sub, lanes)

    @pl.when(e == pl.num_programs(1) - 1)
    def _():
        peer = jnp.transpose(acc_sc[...])
        y_ref[...] = _layer_norm(alpha * h_ref[...] + peer, g_ref[...], b_ref[...])


def _dense(ht, u_b, vt_b, cnt1, e1n, rank2, e2, h2, g, bta, alpha, tm):
    n = h2.shape[0]
    n_e = N_EXPERTS // EXPERTS_PER_STEP
    full = lambda a: pl.BlockSpec(a.shape, lambda t, e: (0,) * a.ndim)
    rows_blk = pl.BlockSpec((PEER_HEADS, ROWS_PER_STEP, tm), lambda t, e: (0, e, t))
    cols_blk = pl.BlockSpec((PEER_HEADS, N_KEYS, tm), lambda t, e: (0, 0, t))
    return pl.pallas_call(
        functools.partial(_dense_kernel, alpha=alpha),
        grid=(n // tm, n_e),
        in_specs=[
            pl.BlockSpec((D_MODEL, tm), lambda t, e: (0, t)),
            pl.BlockSpec((EXPERTS_PER_STEP, D_MODEL), lambda t, e: (e, 0)),
            pl.BlockSpec((D_MODEL, EXPERTS_PER_STEP), lambda t, e: (0, e)),
            rows_blk, rows_blk, cols_blk, cols_blk,
            pl.BlockSpec((tm, D_MODEL), lambda t, e: (t, 0)),
            full(g), full(bta),
        ],
        out_specs=pl.BlockSpec((tm, D_MODEL), lambda t, e: (t, 0)),
        out_shape=jax.ShapeDtypeStruct((n, D_MODEL), F32),
        scratch_shapes=[pltpu.VMEM((D_MODEL, tm), F32), pltpu.VMEM((EXPERTS_PER_STEP, tm), F32),
                        pltpu.VMEM((EXPERTS_PER_STEP, tm), BF16)],
        compiler_params=_params("parallel", "arbitrary"),
        name="peer_dense",
    )(ht, u_b, vt_b, cnt1, e1n, rank2, e2, h2, g, bta)


def _block_diag(w):
    nb, bi, bo = w.shape
    eye = jnp.eye(nb, dtype=w.dtype)
    return (eye[:, None, :, None] * w[:, :, None, :]).reshape(nb * bi, nb * bo)


def _prep_weights(w_in, b_forget, conv_w, conv_b, w_rg_a, b_rg_a, w_rg_x, b_rg_x, lru_lambda,
                  w_attn_up, w_rnn_up, w_out, ln1_g, ln1_b, peer_w_query, peer_keys_1, peer_keys_2,
                  peer_u, peer_v, ln2_g, ln2_b):
    c_f = 3 * D_ATTN
    w_in_p = jnp.concatenate(
        [w_in[:, :c_f], jnp.pad(w_in[:, c_f:c_f + N_ATTN_HEADS], ((0, 0), (0, F_PAD - N_ATTN_HEADS))),
         w_in[:, c_f + N_ATTN_HEADS:]], axis=1).astype(BF16)
    row = lambda a: a.reshape(1, -1).astype(F32)
    return dict(
        w_in=w_in_p, b_forget=row(b_forget), conv_w=conv_w.astype(F32), conv_b=row(conv_b),
        wa=_block_diag(w_rg_a).astype(BF16), ba=row(b_rg_a), wx=_block_diag(w_rg_x).astype(BF16),
        bx=row(b_rg_x), lam=row(lru_lambda),
        wau=w_attn_up.astype(BF16), wru=w_rnn_up.astype(BF16), wo=w_out.astype(BF16),
        ln1_g=row(ln1_g), ln1_b=row(ln1_b),
        wq_t=jnp.transpose(peer_w_query).astype(BF16), k1=peer_keys_1.astype(BF16),
        k2=peer_keys_2.astype(BF16), u=peer_u.astype(BF16), vt=jnp.transpose(peer_v).astype(BF16),
        ln2_g=row(ln2_g), ln2_b=row(ln2_b),
    )


def _pick_block(n, target):
    blk = min(n, target)
    assert n % blk == 0, (n, blk)
    return blk


def _trunk_layer(x, past_k, past_v, past_logf, conv_hist, h0, p, alpha):
    bsz, t, _ = x.shape
    n = bsz * t
    x2 = x.reshape(n, D_MODEL)
    q, k, v, kb, vb, lf, xr, gg, sa, sr = _inproj(x2, p["w_in"], p["b_forget"], _pick_block(n, 256))

    tq = _pick_block(t, ATTN_Q_BLOCK)
    n_past = 0 if past_k is None else past_k.shape[1]
    t_all = n_past + t
    t_lanes = -(-t_all // LANES) * LANES
    tk = ATTN_K_BLOCK if t_lanes % ATTN_K_BLOCK == 0 else t_lanes
    t_pad = -(-t_all // tk) * tk
    lf3 = lf.reshape(bsz, t, N_ATTN_HEADS)
    kb3 = kb.reshape(bsz, t, D_ATTN)
    vb3 = vb.reshape(bsz, t, D_ATTN)
    if past_k is not None:
        lf3 = jnp.concatenate([past_logf.astype(F32), lf3], axis=1)
        kb3 = jnp.concatenate([past_k.reshape(bsz, n_past, D_ATTN).astype(BF16), kb3], axis=1)
        vb3 = jnp.concatenate([past_v.reshape(bsz, n_past, D_ATTN).astype(BF16), vb3], axis=1)
    pad = ((0, 0), (0, t_pad - t_all), (0, 0))
    lf3, kb3, vb3 = jnp.pad(lf3, pad), jnp.pad(kb3, pad), jnp.pad(vb3, pad)
    f_t = _cumsum_time(jnp.transpose(lf3, (0, 2, 1)))
    f_g = f_t.reshape(bsz, N_HEAD_GROUPS, HEADS_PER_GROUP, t_pad)
    fq = jnp.transpose(f_g[:, :, :, n_past:n_past + t], (0, 1, 3, 2))
    fk = jnp.transpose(f_g.reshape(bsz, N_HEAD_GROUPS, HEADS_PER_GROUP, t_pad // tk, tk), (0, 1, 3, 2, 4))
    o = _attention(q.reshape(bsz, t, D_ATTN), kb3, vb3, fq, fk, tq=tq, tk=tk, q_off=n_past)

    hist8 = jnp.pad(conv_hist.astype(F32), ((0, 0), (SUBLANES - (CONV_WIDTH - 1), 0), (0, 0)))
    xr3 = xr.reshape(bsz, t, D_RNN)
    rnn_out, h_last = _rnn(xr3, gg.reshape(bsz, t, D_RNN), hist8, h0.astype(F32).reshape(bsz, 1, D_RNN),
                           p["conv_w"], p["conv_b"], p["wa"], p["ba"], p["wx"], p["bx"], p["lam"],
                           _pick_block(t, 256))
    new_hist = jnp.concatenate([conv_hist.astype(F32), xr3], axis=1)[:, -(CONV_WIDTH - 1):]

    h = _outproj(o.reshape(n, D_ATTN), rnn_out.reshape(n, D_RNN), sa, sr, x2, p["wau"], p["wru"], p["wo"],
                 p["ln1_g"], p["ln1_b"], alpha, _pick_block(n, 256))
    ht, cnt1, e1n, rank2, e2 = _route(h, p["wq_t"], p["k1"], p["k2"], _pick_block(n, ROUTE_TOKENS))
    y = _dense(ht, p["u"], p["vt"], cnt1, e1n, rank2, e2, h, p["ln2_g"], p["ln2_b"], alpha,
               _pick_block(n, 512))
    return (y.reshape(bsz, t, D_MODEL), k.reshape(bsz, t, N_ATTN_HEADS, ATTN_HEAD_DIM),
            v.reshape(bsz, t, N_ATTN_HEADS, ATTN_HEAD_DIM), lf.reshape(bsz, t, N_ATTN_HEADS), new_hist,
            h_last.reshape(bsz, D_RNN))


def kernel(x_prompt, x_sample, cache_k, cache_v, cache_logf, state_conv, state_rnn, w_in, b_forget, conv_w, conv_b, w_rg_a, b_rg_a, w_rg_x, b_rg_x, lru_lambda, w_attn_up, w_rnn_up, w_out, ln1_g, ln1_b, peer_w_query, peer_keys_1, peer_keys_2, peer_u, peer_v, ln2_g, ln2_b):
    depth = w_in.shape[0]
    alpha = (2 * depth) ** 0.25
    layer_weights = (w_in, b_forget, conv_w, conv_b, w_rg_a, b_rg_a, w_rg_x, b_rg_x, lru_lambda, w_attn_up,
                     w_rnn_up, w_out, ln1_g, ln1_b, peer_w_query, peer_keys_1, peer_keys_2, peer_u, peer_v,
                     ln2_g, ln2_b)
    hp, hs = x_prompt, x_sample
    prompt_state, sample_state = [], []
    for l in range(depth):
        p = _prep_weights(*(w[l] for w in layer_weights))
        zero_hist = jnp.zeros((hp.shape[0], CONV_WIDTH - 1, D_RNN), F32)
        zero_h = jnp.zeros((hp.shape[0], D_RNN), F32)
        hp, *st_p = _trunk_layer(hp, None, None, None, zero_hist, zero_h, p, alpha)
        hs, *st_s = _trunk_layer(hs, cache_k[l], cache_v[l], cache_logf[l], state_conv[l], state_rnn[l], p,
                                 alpha)
        prompt_state.append(st_p)
        sample_state.append(st_s)
    stack = lambda states, i: jnp.stack([s[i] for s in states])
    return (hp, hs) + tuple(stack(prompt_state, i) for i in range(5)) + tuple(
        stack(sample_state, i) for i in range(5))
```

```python
import functools
import math

import jax
import jax.numpy as jnp
from jax import lax
from jax.experimental import pallas as pl
from jax.experimental.pallas import tpu as pltpu

F32 = jnp.float32
BF16 = jnp.bfloat16

D_MODEL = 1024
N_ATTN_HEADS = 8
ATTN_HEAD_DIM = 64
D_ATTN = N_ATTN_HEADS * ATTN_HEAD_DIM
ATTN_SCALE = ATTN_HEAD_DIM ** -0.5
D_RNN = 512
N_RNN_BLOCKS = 8
RNN_BLOCK = D_RNN // N_RNN_BLOCKS
CONV_WIDTH = 4
LRU_C = 8.0
N_KEYS = 128
N_EXPERTS = N_KEYS * N_KEYS
PEER_HEADS = 8
PEER_TOPK = 16
PEER_HALF = 128
LN_EPS = 1e-5

LANES = 128
SUBLANES = 8
BF16_ROWS = 2 * SUBLANES
VMEM_LIMIT_BYTES = 56 * 1024 * 1024

HEADS_PER_GROUP = 4
GROUP_LANES = HEADS_PER_GROUP * ATTN_HEAD_DIM
N_HEAD_GROUPS = N_ATTN_HEADS // HEADS_PER_GROUP
ATTN_Q_BLOCK = 512
ATTN_K_BLOCK = 512
ATTN_ROW_BLOCK = 32
F_PAD = LANES
NEG_BIG = -1e30
LOG2_E = math.log2(math.e)
NOT_SELECTED_RANK = 99.0

_C_Q = 0
_C_K = _C_Q + D_ATTN
_C_V = _C_K + D_ATTN
_C_F = _C_V + D_ATTN
_C_XR = _C_F + F_PAD
_C_GATE = _C_XR + D_RNN
_C_GA = _C_GATE + D_RNN
_C_GR = _C_GA + D_MODEL
_C_END = _C_GR + D_MODEL


def _params(*sem):
    return pltpu.CompilerParams(dimension_semantics=sem, vmem_limit_bytes=VMEM_LIMIT_BYTES)


def _sigmoid(x):
    return 1.0 / (1.0 + jnp.exp(-x))


def _gelu_tanh(x):
    half = 0.5 * x
    return half + half * jnp.tanh(x * (0.7978845608028654 + 0.035677408136300125 * (x * x)))


def _softplus(x):
    return jnp.maximum(x, 0.0) + jnp.log1p(jnp.exp(-jnp.abs(x)))


def _layer_norm(x, g, b):
    mu = jnp.mean(x, axis=-1, keepdims=True)
    xc = x - mu
    var = jnp.mean(xc * xc, axis=-1, keepdims=True)
    return xc * lax.rsqrt(var + LN_EPS) * g + b


def _inproj_kernel(x_ref, w_ref, bf_ref, q_ref, k_ref, v_ref, kb_ref, vb_ref, lf_ref, xr_ref,
                   gg_ref, sa_ref, sr_ref):
    xb = x_ref[...].astype(BF16)

    def mm(lo, hi):
        return jnp.dot(xb, w_ref[:, lo:hi], preferred_element_type=F32)

    q_ref[...] = (mm(_C_Q, _C_K) * (ATTN_SCALE * LOG2_E)).astype(BF16)
    k = mm(_C_K, _C_V)
    k_ref[...] = k
    kb_ref[...] = k.astype(BF16)
    v = mm(_C_V, _C_F)
    v_ref[...] = v
    vb_ref[...] = v.astype(BF16)
    f = mm(_C_F, _C_XR)[:, :N_ATTN_HEADS] + bf_ref[...]
    lf_ref[...] = -_softplus(-f)
    xr_ref[...] = mm(_C_XR, _C_GATE)
    gg_ref[...] = _gelu_tanh(mm(_C_GATE, _C_GA)).astype(BF16)
    sa_ref[...] = _sigmoid(mm(_C_GA, _C_GR)).astype(BF16)
    sr_ref[...] = _sigmoid(mm(_C_GR, _C_END)).astype(BF16)


def _inproj(x2, w_in_p, b_forget, tm):
    n = x2.shape[0]
    row = lambda w: pl.BlockSpec((tm, w), lambda i: (i, 0))
    full = lambda a: pl.BlockSpec(a.shape, lambda i: (0,) * a.ndim)
    out_shape = (
        jax.ShapeDtypeStruct((n, D_ATTN), BF16),
        jax.ShapeDtypeStruct((n, D_ATTN), F32),
        jax.ShapeDtypeStruct((n, D_ATTN), F32),
        jax.ShapeDtypeStruct((n, D_ATTN), BF16),
        jax.ShapeDtypeStruct((n, D_ATTN), BF16),
        jax.ShapeDtypeStruct((n, N_ATTN_HEADS), F32),
        jax.ShapeDtypeStruct((n, D_RNN), F32),
        jax.ShapeDtypeStruct((n, D_RNN), BF16),
        jax.ShapeDtypeStruct((n, D_MODEL), BF16),
        jax.ShapeDtypeStruct((n, D_MODEL), BF16),
    )
    out_specs = (row(D_ATTN), row(D_ATTN), row(D_ATTN), row(D_ATTN), row(D_ATTN), row(N_ATTN_HEADS),
                 row(D_RNN), row(D_RNN), row(D_MODEL), row(D_MODEL))
    return pl.pallas_call(
        _inproj_kernel,
        grid=(n // tm,),
        in_specs=[row(D_MODEL), full(w_in_p), full(b_forget)],
        out_specs=out_specs,
        out_shape=out_shape,
        compiler_params=_params("parallel"),
        name="inproj",
    )(x2, w_in_p, b_forget)


def _cumsum_kernel(x_ref, o_ref):
    x = x_ref[0]
    t = x.shape[1]
    lane = lax.broadcasted_iota(jnp.int32, x.shape, 1)
    d = 1
    while d < t:
        x = x + jnp.where(lane >= d, pltpu.roll(x, d, axis=1), 0.0)
        d *= 2
    o_ref[0] = x * LOG2_E


def _cumsum_time(lf_t):
    b, h, t = lf_t.shape
    spec = pl.BlockSpec((1, h, t), lambda i: (i, 0, 0))
    return pl.pallas_call(
        _cumsum_kernel, grid=(b,), in_specs=[spec], out_specs=spec,
        out_shape=jax.ShapeDtypeStruct(lf_t.shape, F32),
        compiler_params=_params("parallel"), name="logf_cumsum",
    )(lf_t)


def _attn_kernel(q_ref, k_ref, v_ref, fq_ref, fk_ref, o_ref, q_sc, fq_sc, s_sc, p_sc, acc_sc, m_sc, al_sc,
                 lp_sc, *, tq, tk, q_off):
    qi = pl.program_id(2)
    row0 = q_off + qi * tq
    n_full = (row0 + 1) // tk
    heads = range(HEADS_PER_GROUP)
    rb = min(ATTN_ROW_BLOCK, tq)
    lane = lax.broadcasted_iota(jnp.int32, (tq, GROUP_LANES), 1)
    in_head = [(lane >= j * ATTN_HEAD_DIM) & (lane < (j + 1) * ATTN_HEAD_DIM) for j in heads]
    q = q_ref[0]
    for j in heads:
        q_sc[j * tq:(j + 1) * tq, :] = jnp.where(in_head[j], q, jnp.zeros_like(q))
        fq_sc[j * tq:(j + 1) * tq, :] = jnp.broadcast_to(fq_ref[0, 0, :, j:j + 1], (tq, LANES))
    m_sc[...] = jnp.full_like(m_sc, NEG_BIG)
    lp_sc[...] = jnp.zeros_like(lp_sc)
    acc_sc[...] = jnp.zeros_like(acc_sc)
    halves = [slice(0, 2 * tq), slice(2 * tq, 4 * tq)]
    n_lane_tiles = tk // LANES

    def scores(c, s_sc):
        ks = k_ref[0, pl.ds(pl.multiple_of(c * tk, tk), tk), :]
        for hv in halves:
            s_sc[hv, :] = lax.dot_general(q_sc[hv, :], ks, (((1,), (1,)), ((), ())),
                                          preferred_element_type=F32)

    def absorb(c, s_sc, masked):
        start = pl.multiple_of(c * tk, tk)
        fk = fk_ref[0, 0, c]
        for j in heads:
            for r in range(tq // rb):
                rows = slice(j * tq + r * rb, j * tq + (r + 1) * rb)
                s = s_sc[rows, :] - fk[j:j + 1, :]
                if masked:
                    q_pos = row0 + r * rb + lax.broadcasted_iota(jnp.int32, (rb, tk), 0)
                    k_pos = c * tk + lax.broadcasted_iota(jnp.int32, (rb, tk), 1)
                    s = jnp.where(k_pos <= q_pos, s, NEG_BIG)
                tiles = [s[:, t * LANES:(t + 1) * LANES] for t in range(n_lane_tiles)]
                row_max = jnp.max(functools.reduce(jnp.maximum, tiles), axis=1, keepdims=True)
                fq = fq_sc[rows, :]
                m_old = m_sc[rows, :]
                m_new = jnp.maximum(m_old, jnp.broadcast_to(row_max, (rb, LANES)) + fq)
                alpha = jnp.exp2(m_old - m_new)
                shift = fq - m_new
                p_tiles = [jnp.exp2(tile + shift) for tile in tiles]
                m_sc[rows, :] = m_new
                al_sc[rows, :] = alpha
                lp_sc[rows, :] = alpha * lp_sc[rows, :] + functools.reduce(jnp.add, p_tiles)
                p_sc[rows, :] = jnp.concatenate(p_tiles, axis=1).astype(BF16)
        vs = v_ref[0, pl.ds(start, tk), :]
        for hv in halves:
            pv = jnp.dot(p_sc[hv, :], vs, preferred_element_type=F32)
            alpha = al_sc[hv, :]
            acc_sc[hv, :] = jnp.concatenate([alpha] * (GROUP_LANES // LANES), axis=1) * acc_sc[hv, :] + pv

    def step(c, carry):
        scores(c, s_sc)
        absorb(c, s_sc, masked=False)
        return carry

    lax.fori_loop(0, n_full, step, 0)
    scores(n_full, s_sc)
    absorb(n_full, s_sc, masked=True)
    o_all = acc_sc[...] / jnp.sum(lp_sc[...], axis=1, keepdims=True)
    out = jnp.zeros((tq, GROUP_LANES), F32)
    for j in heads:
        out = jnp.where(in_head[j], o_all[j * tq:(j + 1) * tq], out)
    o_ref[0] = out.astype(BF16)


def _attention(q, k_all, v_all, fq, fk, *, tq, tk, q_off):
    b, t, _ = q.shape
    t_k = k_all.shape[1]
    n_kv = t_k // tk
    assert all((q_off + i * tq) % tk + tq <= tk for i in range(t // tq)), "a query block straddles key chunks"
    kern = functools.partial(_attn_kernel, tq=tq, tk=tk, q_off=q_off)
    rows = HEADS_PER_GROUP * tq
    return pl.pallas_call(
        kern,
        grid=(b, N_HEAD_GROUPS, t // tq),
        in_specs=[
            pl.BlockSpec((1, tq, GROUP_LANES), lambda bi, g, i: (bi, i, g)),
            pl.BlockSpec((1, t_k, GROUP_LANES), lambda bi, g, i: (bi, 0, g)),
            pl.BlockSpec((1, t_k, GROUP_LANES), lambda bi, g, i: (bi, 0, g)),
            pl.BlockSpec((1, 1, tq, HEADS_PER_GROUP), lambda bi, g, i: (bi, g, i, 0)),
            pl.BlockSpec((1, 1, n_kv, HEADS_PER_GROUP, tk), lambda bi, g, i: (bi, g, 0, 0, 0)),
        ],
        out_specs=pl.BlockSpec((1, tq, GROUP_LANES), lambda bi, g, i: (bi, i, g)),
        out_shape=jax.ShapeDtypeStruct((b, t, D_ATTN), BF16),
        scratch_shapes=[
            pltpu.VMEM((rows, GROUP_LANES), BF16),
            pltpu.VMEM((rows, LANES), F32),
            pltpu.VMEM((rows, tk), F32),
            pltpu.VMEM((rows, tk), BF16),
            pltpu.VMEM((rows, GROUP_LANES), F32),
            pltpu.VMEM((rows, LANES), F32),
            pltpu.VMEM((rows, LANES), F32),
            pltpu.VMEM((rows, LANES), F32),
        ],
        compiler_params=_params("parallel", "parallel", "arbitrary"),
        name="fox_attention",
    )(q, k_all, v_all, fq, fk)


def _rnn_kernel(xr_ref, gg_ref, hist_ref, h0_ref, cw_ref, cb_ref, wa_ref, ba_ref, wx_ref, bx_ref,
                lam_ref, out_ref, hl_ref, prev_sc, h_sc, *, tb):
    t = pl.program_id(1)

    @pl.when(t == 0)
    def _():
        prev_sc[...] = hist_ref[0]
        h_sc[...] = h0_ref[0]

    x = xr_ref[0]
    prev = prev_sc[...]
    row8 = lax.broadcasted_iota(jnp.int32, (SUBLANES, D_RNN), 0)
    xc = x * cw_ref[CONV_WIDTH - 1:CONV_WIDTH, :] + cb_ref[...]
    for s in range(1, CONV_WIDTH):
        xs = pltpu.roll(x, s, axis=0)
        head = jnp.where(row8 < s, pltpu.roll(prev, s, axis=0), xs[:SUBLANES])
        xs = jnp.concatenate([head, xs[SUBLANES:]], axis=0)
        xc = xc + xs * cw_ref[CONV_WIDTH - 1 - s:CONV_WIDTH - s, :]
    prev_sc[...] = x[tb - SUBLANES:tb]

    xcb = xc.astype(BF16)
    r = _sigmoid(jnp.dot(xcb, wa_ref[...], preferred_element_type=F32) + ba_ref[...])
    ig = _sigmoid(jnp.dot(xcb, wx_ref[...], preferred_element_type=F32) + bx_ref[...])
    log_a = (-LRU_C) * r * _softplus(-lam_ref[...])
    a = jnp.exp(log_a)
    bterm = jnp.sqrt(1.0 - a * a) * ig * xc

    row = lax.broadcasted_iota(jnp.int32, (tb, D_RNN), 0)
    d = 1
    while d < tb:
        valid = row >= d
        a_s = pltpu.roll(a, d, axis=0)
        b_s = pltpu.roll(bterm, d, axis=0)
        bterm = jnp.where(valid, a * b_s + bterm, bterm)
        a = jnp.where(valid, a * a_s, a)
        d *= 2
    h = bterm + a * h_sc[...]
    h_last = h[tb - 1:tb]
    h_sc[...] = h_last
    hl_ref[0] = h_last
    out_ref[0] = (h * gg_ref[0].astype(F32)).astype(BF16)


def _rnn(xr, gg, hist8, h0, conv_w, conv_b, wa, ba, wx, bx, lam, tb):
    b, t, _ = xr.shape
    blk = pl.BlockSpec((1, tb, D_RNN), lambda bi, ti: (bi, ti, 0))
    full = lambda a: pl.BlockSpec(a.shape, lambda bi, ti: (0,) * a.ndim)
    per_b = lambda r: pl.BlockSpec((1, r, D_RNN), lambda bi, ti: (bi, 0, 0))
    return pl.pallas_call(
        functools.partial(_rnn_kernel, tb=tb),
        grid=(b, t // tb),
        in_specs=[blk, blk, per_b(SUBLANES), per_b(1), full(conv_w), full(conv_b), full(wa), full(ba),
                  full(wx), full(bx), full(lam)],
        out_specs=(blk, per_b(1)),
        out_shape=(jax.ShapeDtypeStruct((b, t, D_RNN), BF16), jax.ShapeDtypeStruct((b, 1, D_RNN), F32)),
        scratch_shapes=[pltpu.VMEM((SUBLANES, D_RNN), F32), pltpu.VMEM((1, D_RNN), F32)],
        compiler_params=_params("parallel", "arbitrary"),
        name="conv_rglru",
    )(xr, gg, hist8, h0, conv_w, conv_b, wa, ba, wx, bx, lam)


def _outproj_kernel(o_ref, r_ref, sa_ref, sr_ref, x_ref, wau_ref, wru_ref, wo_ref, g_ref, b_ref,
                    h_ref, *, alpha):
    up_a = jnp.dot(o_ref[...], wau_ref[...], preferred_element_type=F32)
    up_r = jnp.dot(r_ref[...], wru_ref[...], preferred_element_type=F32)
    merged = sa_ref[...].astype(F32) * up_a + sr_ref[...].astype(F32) * up_r
    mix = jnp.dot(merged.astype(BF16), wo_ref[...], preferred_element_type=F32)
    h_ref[...] = _layer_norm(alpha * x_ref[...] + mix, g_ref[...], b_ref[...])


def _outproj(o2, r2, sa, sr, x2, wau, wru, wo, g, bta, alpha, tm):
    n = x2.shape[0]
    row = lambda w: pl.BlockSpec((tm, w), lambda i: (i, 0))
    full = lambda a: pl.BlockSpec(a.shape, lambda i: (0,) * a.ndim)
    return pl.pallas_call(
        functools.partial(_outproj_kernel, alpha=alpha),
        grid=(n // tm,),
        in_specs=[row(D_ATTN), row(D_RNN), row(D_MODEL), row(D_MODEL), row(D_MODEL), full(wau), full(wru),
                  full(wo), full(g), full(bta)],
        out_specs=row(D_MODEL),
        out_shape=jax.ShapeDtypeStruct((n, D_MODEL), F32),
        compiler_params=_params("parallel"),
        name="outproj_ln1",
    )(o2, r2, sa, sr, x2, wau, wru, wo, g, bta)


ROUTE_TOKENS = 2 * LANES


def _odd_even_merge_sort(n):
    def merge(lo, hi, r):
        step = r * 2
        if step < hi - lo:
            yield from merge(lo, hi, step)
            yield from merge(lo + r, hi, step)
            yield from ((i, i + r) for i in range(lo + r, hi - r, step))
        else:
            yield (lo, lo + r)

    def sort(lo, hi):
        if hi - lo >= 1:
            mid = lo + (hi - lo) // 2
            yield from sort(lo, mid)
            yield from sort(mid + 1, hi)
            yield from merge(lo, hi, 1)

    return tuple(sort(0, n - 1))


_SORT_TOPK = _odd_even_merge_sort(PEER_TOPK)


def _compare_exchange(v, i, j):
    v[i], v[j] = jnp.maximum(v[i], v[j]), jnp.minimum(v[i], v[j])


def _top_k_sorted(tiles):
    v = list(tiles)
    for i, j in _SORT_TOPK:
        _compare_exchange(v, i, j)
    shift = SUBLANES // 2
    while shift >= 1:
        other = [pltpu.roll(x, shift, axis=0) for x in v]
        v = [jnp.maximum(v[i], other[PEER_TOPK - 1 - i]) for i in range(PEER_TOPK)]
        d = PEER_TOPK // 2
        while d >= 1:
            for i in range(PEER_TOPK):
                if i & d == 0:
                    _compare_exchange(v, i, i + d)
            d //= 2
        shift //= 2
    return v


def _count_greater(x, t):
    assert len(t) == 16, "the bisection below is written out for 16 entries"
    one = lambda m, w: jnp.where(m, float(w), 0.0)
    b3 = t[7] > x
    b2 = jnp.where(b3, t[11], t[3]) > x
    b1 = jnp.where(b3, jnp.where(b2, t[13], t[9]), jnp.where(b2, t[5], t[1])) > x
    hi = jnp.where(b2, jnp.where(b1, t[14], t[12]), jnp.where(b1, t[10], t[8]))
    lo = jnp.where(b2, jnp.where(b1, t[6], t[4]), jnp.where(b1, t[2], t[0]))
    b0 = jnp.where(b3, hi, lo) > x
    count = one(b3, 8) + one(b2, 4) + one(b1, 2) + one(b0, 1)
    return jnp.where(t[15] > x, float(PEER_TOPK), count)


def _route_kernel(h_ref, wq_ref, k1_ref, k2_ref, ht_ref, cnt_ref, e1_ref, r2_ref, e2_ref,
                  qt_sc, t1_sc, t2_sc, *, tm):
    ht = jnp.transpose(h_ref[...]).astype(BF16)
    ht_ref[...] = ht
    qt_sc[...] = jnp.dot(wq_ref[...], ht, preferred_element_type=F32).astype(BF16)
    key_iota = lax.broadcasted_iota(jnp.int32, (N_KEYS, tm), 0).astype(F32)
    top_iota = lax.broadcasted_iota(jnp.int32, (PEER_TOPK, tm), 0).astype(F32)
    front_rows = SUBLANES

    def scores(hd):
        base = hd * 2 * PEER_HALF
        s1 = jnp.dot(k1_ref[...], qt_sc[base:base + PEER_HALF], preferred_element_type=F32)
        s2 = jnp.dot(k2_ref[...], qt_sc[base + PEER_HALF:base + 2 * PEER_HALF], preferred_element_type=F32)
        return s1, s2

    sub = lax.broadcasted_iota(jnp.int32, (SUBLANES, tm), 0)
    tiles_of = lambda s: [s[r * SUBLANES:(r + 1) * SUBLANES] for r in range(N_KEYS // SUBLANES)]
    sublane_sum = lambda x: jnp.sum(x, axis=0, keepdims=True)

    def route_head_sorted(hd):
        s1, s2 = scores(hd)
        rows1, rows2 = tiles_of(s1), tiles_of(s2)
        t1 = _top_k_sorted(rows1)
        t2 = _top_k_sorted(rows2)
        t1_lo = t1[SUBLANES - 1]
        for a in range(SUBLANES - 2, -1, -1):
            t1_lo = jnp.where(sub == a, t1[a], t1_lo)
        cand = [t1_lo + t2[b] for b in range(PEER_TOPK)]
        pool = list(cand)
        for a in range(SUBLANES, PEER_TOPK):
            pool[a] = jnp.where(sub == 1, t1[a] + t2[0], cand[a])
        top = _top_k_sorted(pool)
        tau = top[PEER_TOPK - 1]
        cnt_lo = functools.reduce(jnp.add, [jnp.where(c >= tau, 1.0, 0.0) for c in cand])
        cnt = [jnp.broadcast_to(cnt_lo[a:a + 1], (SUBLANES, tm)) for a in range(SUBLANES)]
        cnt += [jnp.where(t1[a] + t2[0] >= tau, 1.0, 0.0) for a in range(SUBLANES, PEER_TOPK)]
        z = functools.reduce(jnp.add, [jnp.exp(t - top[0]) for t in top])

        cnt1_rows, rank2_rows = [], []
        for r in range(N_KEYS // SUBLANES):
            c = jnp.zeros((SUBLANES, tm), F32)
            for a in range(PEER_TOPK):
                c = jnp.where(rows1[r] == t1[a], cnt[a], c)
            cnt1_rows.append(c)
            g = _count_greater(rows2[r], t2)
            rank2_rows.append(jnp.where(g < float(PEER_TOPK), g, NOT_SELECTED_RANK))
        cnt1 = jnp.concatenate(cnt1_rows, axis=0)
        rank2 = jnp.concatenate(rank2_rows, axis=0)
        cnt_ref[hd] = cnt1
        e1_ref[hd] = jnp.exp(s1 - t1[0][0:1]) / z[0:1]
        r2_ref[hd] = rank2.astype(BF16)
        e2_ref[hd] = jnp.exp(s2 - t2[0][0:1]).astype(BF16)

        gap = lambda t: functools.reduce(jnp.minimum, [t[b] - t[b + 1] for b in range(PEER_TOPK - 1)])[0:1]
        n_cnt = sublane_sum(cnt_lo) + functools.reduce(jnp.add, cnt[SUBLANES:])[0:1]
        n_sel = sublane_sum(functools.reduce(jnp.add, cnt1_rows))
        n_rank = sublane_sum(functools.reduce(
            jnp.add, [jnp.where(x < float(PEER_TOPK), 1.0, 0.0) for x in rank2_rows]))
        off = lambda n: jnp.abs(n - float(PEER_TOPK))
        return (off(n_cnt) + off(n_sel) + off(n_rank)
                + jnp.where(jnp.minimum(gap(t1), gap(t2)) > 0.0, 0.0, 1.0))

    def route_head_exact(hd):
        def pick_one(v, iota, n):
            m = jnp.max(v, axis=0, keepdims=True)
            return m, iota == jnp.min(jnp.where(v == m, iota, float(n)), axis=0, keepdims=True)

        s1, s2 = scores(hd)

        def extract(a, carry):
            v1, r1, v2, r2 = carry
            m1, sel1 = pick_one(v1, key_iota, N_KEYS)
            m2, sel2 = pick_one(v2, key_iota, N_KEYS)
            t1_sc[pl.ds(a, 1), :] = m1
            t2_sc[pl.ds(a, 1), :] = m2
            af = jnp.asarray(a, dtype=F32)
            return (jnp.where(sel1, -jnp.inf, v1), jnp.where(sel1, af, r1),
                    jnp.where(sel2, -jnp.inf, v2), jnp.where(sel2, af, r2))

        no_rank = jnp.full((N_KEYS, tm), NOT_SELECTED_RANK, F32)
        _, rank1, _, rank2 = lax.fori_loop(0, PEER_TOPK, extract, (s1, no_rank, s2, no_rank))
        t1 = t1_sc[...]
        t2 = t2_sc[...]
        top0 = t1[0:1] + t2[0:1]

        def pick(_, carry):
            ptr, front, z = carry
            m, sel = pick_one(front, top_iota, PEER_TOPK)
            ptr = ptr + jnp.where(sel, 1.0, 0.0)
            lo = ptr[:front_rows]
            nxt = jnp.full(lo.shape, -jnp.inf, F32)
            for b in range(1, PEER_TOPK):
                nxt = jnp.where(lo == float(b), t2_sc[b:b + 1, :], nxt)
            nxt = jnp.where(lo == 0.0, t2[0:1], nxt)
            front = jnp.concatenate(
                [t1[:front_rows] + nxt, jnp.where(sel[front_rows:], -jnp.inf, front[front_rows:])], axis=0)
            return ptr, front, z + jnp.exp(m - top0)

        cnt, _, z = lax.fori_loop(
            0, PEER_TOPK, pick,
            (jnp.zeros((PEER_TOPK, tm), F32), t1 + t2[0:1], jnp.zeros((1, tm), F32)))

        cnt1 = jnp.zeros((N_KEYS, tm), F32)
        for a in range(PEER_TOPK):
            cnt1 = jnp.where(rank1 == float(a), cnt[a:a + 1], cnt1)
        cnt_ref[hd] = cnt1
        e1_ref[hd] = jnp.exp(s1 - t1[0:1]) / z
        r2_ref[hd] = rank2.astype(BF16)
        e2_ref[hd] = jnp.exp(s2 - t2[0:1]).astype(BF16)

    doubt = [route_head_sorted(hd) for hd in range(PEER_HEADS)]

    @pl.when(jnp.max(functools.reduce(jnp.maximum, doubt)) > 0.0)
    def _():
        for hd in range(PEER_HEADS):
            @pl.when(jnp.max(doubt[hd]) > 0.0)
            def _():
                route_head_exact(hd)


def _route(h2, wq_t, k1, k2, tm):
    n = h2.shape[0]
    full = lambda a: pl.BlockSpec(a.shape, lambda i: (0,) * a.ndim)
    per_head = pl.BlockSpec((PEER_HEADS, N_KEYS, tm), lambda i: (0, 0, i))
    hshape = lambda dt: jax.ShapeDtypeStruct((PEER_HEADS, N_KEYS, n), dt)
    return pl.pallas_call(
        functools.partial(_route_kernel, tm=tm),
        grid=(n // tm,),
        in_specs=[pl.BlockSpec((tm, D_MODEL), lambda i: (i, 0)), full(wq_t), full(k1), full(k2)],
        out_specs=(pl.BlockSpec((D_MODEL, tm), lambda i: (0, i)), per_head, per_head, per_head, per_head),
        out_shape=(jax.ShapeDtypeStruct((D_MODEL, n), BF16), hshape(F32), hshape(F32), hshape(BF16),
                   hshape(BF16)),
        scratch_shapes=[pltpu.VMEM((PEER_HEADS * 2 * PEER_HALF, tm), BF16),
                        pltpu.VMEM((PEER_TOPK, tm), F32), pltpu.VMEM((PEER_TOPK, tm), F32)],
        compiler_params=_params("parallel"),
        name="peer_route",
    )(h2, wq_t, k1, k2)


ROWS_PER_STEP = 16
EXPERTS_PER_STEP = ROWS_PER_STEP * N_KEYS
ROWS_PER_SUB = 4
EXPERTS_PER_SUB = ROWS_PER_SUB * N_KEYS
SUBS_PER_MIX = 2
DENSE_CHUNK = 2 * LANES


def _bf16_row_tile(row):
    tile = jnp.broadcast_to(row, (BF16_ROWS, row.shape[1])).astype(BF16)
    return jnp.concatenate([tile] * (N_KEYS // BF16_ROWS), axis=0)


def _dense_kernel(ht_ref, u_ref, vt_ref, cnt_ref, e1_ref, r2_ref, e2_ref, h_ref, g_ref, b_ref,
                  y_ref, acc_sc, act_sc, coef_sc, *, alpha):
    e = pl.program_id(1)
    tm = act_sc.shape[1]
    n_sub = ROWS_PER_STEP // ROWS_PER_SUB

    @pl.when(e == 0)
    def _():
        acc_sc[...] = jnp.zeros_like(acc_sc)

    def activations(sub):
        rows = slice(sub * EXPERTS_PER_SUB, (sub + 1) * EXPERTS_PER_SUB)
        act_sc[rows, :] = jnp.dot(u_ref[rows, :], ht_ref[...], preferred_element_type=F32)

    def tiles(sub):
        chunk = min(DENSE_CHUNK, tm)
        for il in range(sub * ROWS_PER_SUB, (sub + 1) * ROWS_PER_SUB):
            for c in range(tm // chunk):
                yield il, slice(il * N_KEYS, (il + 1) * N_KEYS), slice(c * chunk, (c + 1) * chunk)

    def routing_weights(sub):
        for il, rows, lanes in tiles(sub):
            w = None
            for hd in range(PEER_HEADS):
                cnt = _bf16_row_tile(cnt_ref[hd, il:il + 1, lanes])
                e1 = _bf16_row_tile(e1_ref[hd, il:il + 1, lanes])
                term = jnp.where(r2_ref[hd, :, lanes] < cnt, e2_ref[hd, :, lanes] * e1, jnp.zeros((), BF16))
                w = term if w is None else w + term
            coef_sc[rows, lanes] = w

    def coefficients(sub):
        for _, rows, lanes in tiles(sub):
            coef_sc[rows, lanes] = coef_sc[rows, lanes] * _gelu_tanh(act_sc[rows, lanes].astype(BF16))

    def mix(first, last):
        rows = slice(first * EXPERTS_PER_SUB, (last + 1) * EXPERTS_PER_SUB)
        acc_sc[...] += jnp.dot(vt_ref[0, :, rows], coef_sc[rows, :], preferred_element_type=F32)

    routing_weights(0)
    activations(0)
    for sub in range(n_sub):
        if sub + 1 < n_sub:
            routing_weights(sub + 1)
            activations(sub + 1)
        coefficients(sub)
        if sub % SUBS_PER_MIX == SUBS_PER_MIX - 1:
            mix(sub - SUBS_PER_MIX + 1, sub)

    @pl.when(e == pl.num_programs(1) - 1)
    def _():
        peer = jnp.transpose(acc_sc[...])
        y_ref[...] = _layer_norm(alpha * h_ref[...] + peer, g_ref[...], b_ref[...])


def _dense(ht, u_b, vt_b, cnt1, e1n, rank2, e2, h2, g, bta, alpha, tm):
    n = h2.shape[0]
    n_e = N_EXPERTS // EXPERTS_PER_STEP
    full = lambda a: pl.BlockSpec(a.shape, lambda t, e: (0,) * a.ndim)
    rows_blk = pl.BlockSpec((PEER_HEADS, ROWS_PER_STEP, tm), lambda t, e: (0, e, t))
    cols_blk = pl.BlockSpec((PEER_HEADS, N_KEYS, tm), lambda t, e: (0, 0, t))
    return pl.pallas_call(
        functools.partial(_dense_kernel, alpha=alpha),
        grid=(n // tm, n_e),
        in_specs=[
            pl.BlockSpec((D_MODEL, tm), lambda t, e: (0, t)),
            pl.BlockSpec((EXPERTS_PER_STEP, D_MODEL), lambda t, e: (e, 0)),
            pl.BlockSpec((1, D_MODEL, EXPERTS_PER_STEP), lambda t, e: (e, 0, 0)),
            rows_blk, rows_blk, cols_blk, cols_blk,
            pl.BlockSpec((tm, D_MODEL), lambda t, e: (t, 0)),
            full(g), full(bta),
        ],
        out_specs=pl.BlockSpec((tm, D_MODEL), lambda t, e: (t, 0)),
        out_shape=jax.ShapeDtypeStruct((n, D_MODEL), F32),
        scratch_shapes=[pltpu.VMEM((D_MODEL, tm), F32), pltpu.VMEM((EXPERTS_PER_STEP, tm), F32),
                        pltpu.VMEM((EXPERTS_PER_STEP, tm), BF16)],
        compiler_params=_params("parallel", "arbitrary"),
        name="peer_dense",
    )(ht, u_b, vt_b, cnt1, e1n, rank2, e2, h2, g, bta)


def _block_diag(w):
    nb, bi, bo = w.shape
    eye = jnp.eye(nb, dtype=w.dtype)
    return (eye[:, None, :, None] * w[:, :, None, :]).reshape(nb * bi, nb * bo)


def _prep_weights(w_in, b_forget, conv_w, conv_b, w_rg_a, b_rg_a, w_rg_x, b_rg_x, lru_lambda,
                  w_attn_up, w_rnn_up, w_out, ln1_g, ln1_b, peer_w_query, peer_keys_1, peer_keys_2,
                  peer_u, peer_v, ln2_g, ln2_b):
    c_f = 3 * D_ATTN
    w_in_p = jnp.concatenate(
        [w_in[:, :c_f], jnp.pad(w_in[:, c_f:c_f + N_ATTN_HEADS], ((0, 0), (0, F_PAD - N_ATTN_HEADS))),
         w_in[:, c_f + N_ATTN_HEADS:]], axis=1).astype(BF16)
    row = lambda a: a.reshape(1, -1).astype(F32)
    return dict(
        w_in=w_in_p, b_forget=row(b_forget), conv_w=conv_w.astype(F32), conv_b=row(conv_b),
        wa=_block_diag(w_rg_a).astype(BF16), ba=row(b_rg_a), wx=_block_diag(w_rg_x).astype(BF16),
        bx=row(b_rg_x), lam=row(lru_lambda),
        wau=w_attn_up.astype(BF16), wru=w_rnn_up.astype(BF16), wo=w_out.astype(BF16),
        ln1_g=row(ln1_g), ln1_b=row(ln1_b),
        wq_t=jnp.transpose(peer_w_query).astype(BF16), k1=peer_keys_1.astype(BF16),
        k2=peer_keys_2.astype(BF16), u=peer_u.astype(BF16),
        vt=jnp.transpose(peer_v.reshape(N_EXPERTS // EXPERTS_PER_STEP, EXPERTS_PER_STEP, D_MODEL),
                         (0, 2, 1)).astype(BF16),
        ln2_g=row(ln2_g), ln2_b=row(ln2_b),
    )


def _pick_block(n, target):
    blk = min(n, target)
    assert n % blk == 0, (n, blk)
    return blk


def _trunk_layer(x, past_k, past_v, past_logf, conv_hist, h0, p, alpha):
    bsz, t, _ = x.shape
    n = bsz * t
    x2 = x.reshape(n, D_MODEL)
    q, k, v, kb, vb, lf, xr, gg, sa, sr = _inproj(x2, p["w_in"], p["b_forget"], _pick_block(n, 256))

    tq = _pick_block(t, ATTN_Q_BLOCK)
    n_past = 0 if past_k is None else past_k.shape[1]
    t_all = n_past + t
    t_lanes = -(-t_all // LANES) * LANES
    tk = ATTN_K_BLOCK if t_lanes % ATTN_K_BLOCK == 0 else t_lanes
    t_pad = -(-t_all // tk) * tk
    lf3 = lf.reshape(bsz, t, N_ATTN_HEADS)
    kb3 = kb.reshape(bsz, t, D_ATTN)
    vb3 = vb.reshape(bsz, t, D_ATTN)
    if past_k is not None:
        lf3 = jnp.concatenate([past_logf.astype(F32), lf3], axis=1)
        kb3 = jnp.concatenate([past_k.reshape(bsz, n_past, D_ATTN).astype(BF16), kb3], axis=1)
        vb3 = jnp.concatenate([past_v.reshape(bsz, n_past, D_ATTN).astype(BF16), vb3], axis=1)
    pad = ((0, 0), (0, t_pad - t_all), (0, 0))
    lf3, kb3, vb3 = jnp.pad(lf3, pad), jnp.pad(kb3, pad), jnp.pad(vb3, pad)
    f_t = _cumsum_time(jnp.transpose(lf3, (0, 2, 1)))
    f_g = f_t.reshape(bsz, N_HEAD_GROUPS, HEADS_PER_GROUP, t_pad)
    fq = jnp.transpose(f_g[:, :, :, n_past:n_past + t], (0, 1, 3, 2))
    fk = jnp.transpose(f_g.reshape(bsz, N_HEAD_GROUPS, HEADS_PER_GROUP, t_pad // tk, tk), (0, 1, 3, 2, 4))
    o = _attention(q.reshape(bsz, t, D_ATTN), kb3, vb3, fq, fk, tq=tq, tk=tk, q_off=n_past)

    hist8 = jnp.pad(conv_hist.astype(F32), ((0, 0), (SUBLANES - (CONV_WIDTH - 1), 0), (0, 0)))
    xr3 = xr.reshape(bsz, t, D_RNN)
    rnn_out, h_last = _rnn(xr3, gg.reshape(bsz, t, D_RNN), hist8, h0.astype(F32).reshape(bsz, 1, D_RNN),
                           p["conv_w"], p["conv_b"], p["wa"], p["ba"], p["wx"], p["bx"], p["lam"],
                           _pick_block(t, 256))
    new_hist = jnp.concatenate([conv_hist.astype(F32), xr3], axis=1)[:, -(CONV_WIDTH - 1):]

    h = _outproj(o.reshape(n, D_ATTN), rnn_out.reshape(n, D_RNN), sa, sr, x2, p["wau"], p["wru"], p["wo"],
                 p["ln1_g"], p["ln1_b"], alpha, _pick_block(n, 256))
    ht, cnt1, e1n, rank2, e2 = _route(h, p["wq_t"], p["k1"], p["k2"], _pick_block(n, ROUTE_TOKENS))
    y = _dense(ht, p["u"], p["vt"], cnt1, e1n, rank2, e2, h, p["ln2_g"], p["ln2_b"], alpha,
               _pick_block(n, 512))
    return (y.reshape(bsz, t, D_MODEL), k.reshape(bsz, t, N_ATTN_HEADS, ATTN_HEAD_DIM),
            v.reshape(bsz, t, N_ATTN_HEADS, ATTN_HEAD_DIM), lf.reshape(bsz, t, N_ATTN_HEADS), new_hist,
            h_last.reshape(bsz, D_RNN))


def kernel(x_prompt, x_sample, cache_k, cache_v, cache_logf, state_conv, state_rnn, w_in, b_forget, conv_w, conv_b, w_rg_a, b_rg_a, w_rg_x, b_rg_x, lru_lambda, w_attn_up, w_rnn_up, w_out, ln1_g, ln1_b, peer_w_query, peer_keys_1, peer_keys_2, peer_u, peer_v, ln2_g, ln2_b):
    depth = w_in.shape[0]
    alpha = (2 * depth) ** 0.25
    layer_weights = (w_in, b_forget, conv_w, conv_b, w_rg_a, b_rg_a, w_rg_x, b_rg_x, lru_lambda, w_attn_up,
                     w_rnn_up, w_out, ln1_g, ln1_b, peer_w_query, peer_keys_1, peer_keys_2, peer_u, peer_v,
                     ln2_g, ln2_b)
    hp, hs = x_prompt, x_sample
    prompt_state, sample_state = [], []
    for l in range(depth):
        p = _prep_weights(*(w[l] for w in layer_weights))
        zero_hist = jnp.zeros((hp.shape[0], CONV_WIDTH - 1, D_RNN), F32)
        zero_h = jnp.zeros((hp.shape[0], D_RNN), F32)
        hp, *st_p = _trunk_layer(hp, None, None, None, zero_hist, zero_h, p, alpha)
        hs, *st_s = _trunk_layer(hs, cache_k[l], cache_v[l], cache_logf[l], state_conv[l], state_rnn[l], p,
                                 alpha)
        prompt_state.append(st_p)
        sample_state.append(st_s)
    stack = lambda states, i: jnp.stack([s[i] for s in states])
    return (hp, hs) + tuple(stack(prompt_state, i) for i in range(5)) + tuple(
        stack(sample_state, i) for i in range(5))
```

```python
import functools
import math

import jax
import jax.numpy as jnp
from jax import lax
from jax.experimental import pallas as pl
from jax.experimental.pallas import tpu as pltpu

F32 = jnp.float32
BF16 = jnp.bfloat16

D_MODEL = 1024
N_ATTN_HEADS = 8
ATTN_HEAD_DIM = 64
D_ATTN = N_ATTN_HEADS * ATTN_HEAD_DIM
ATTN_SCALE = ATTN_HEAD_DIM ** -0.5
D_RNN = 512
N_RNN_BLOCKS = 8
RNN_BLOCK = D_RNN // N_RNN_BLOCKS
CONV_WIDTH = 4
LRU_C = 8.0
N_KEYS = 128
N_EXPERTS = N_KEYS * N_KEYS
PEER_HEADS = 8
PEER_TOPK = 16
PEER_HALF = 128
LN_EPS = 1e-5

LANES = 128
SUBLANES = 8
BF16_ROWS = 2 * SUBLANES
VMEM_LIMIT_BYTES = 56 * 1024 * 1024

HEADS_PER_GROUP = 4
GROUP_LANES = HEADS_PER_GROUP * ATTN_HEAD_DIM
N_HEAD_GROUPS = N_ATTN_HEADS // HEADS_PER_GROUP
ATTN_Q_BLOCK = 512
ATTN_K_BLOCK = 512
ATTN_ROW_BLOCK = 32
F_PAD = LANES
NEG_BIG = -1e30
LOG2_E = math.log2(math.e)
NOT_SELECTED_RANK = 99.0

_C_Q = 0
_C_K = _C_Q + D_ATTN
_C_V = _C_K + D_ATTN
_C_F = _C_V + D_ATTN
_C_XR = _C_F + F_PAD
_C_GATE = _C_XR + D_RNN
_C_GA = _C_GATE + D_RNN
_C_GR = _C_GA + D_MODEL
_C_END = _C_GR + D_MODEL


def _params(*sem):
    return pltpu.CompilerParams(dimension_semantics=sem, vmem_limit_bytes=VMEM_LIMIT_BYTES)


def _sigmoid(x):
    return 1.0 / (1.0 + jnp.exp(-x))


def _gelu_tanh(x):
    half = 0.5 * x
    return half + half * jnp.tanh(x * (0.7978845608028654 + 0.035677408136300125 * (x * x)))


def _softplus(x):
    return jnp.maximum(x, 0.0) + jnp.log1p(jnp.exp(-jnp.abs(x)))


def _layer_norm(x, g, b):
    mu = jnp.mean(x, axis=-1, keepdims=True)
    xc = x - mu
    var = jnp.mean(xc * xc, axis=-1, keepdims=True)
    return xc * lax.rsqrt(var + LN_EPS) * g + b


def _inproj_kernel(x_ref, w_ref, bf_ref, q_ref, k_ref, v_ref, kb_ref, vb_ref, lf_ref, xr_ref,
                   gg_ref, sa_ref, sr_ref, *, time_minor_kv):
    xb = x_ref[...].astype(BF16)

    def mm(lo, hi):
        return jnp.dot(xb, w_ref[:, lo:hi], preferred_element_type=F32)

    q_ref[...] = (mm(_C_Q, _C_K) * (ATTN_SCALE * LOG2_E)).astype(BF16)
    k = mm(_C_K, _C_V)
    kb_ref[...] = k.astype(BF16)
    v = mm(_C_V, _C_F)
    vb_ref[...] = v.astype(BF16)
    if time_minor_kv:
        k_ref[0] = jnp.transpose(k)
        v_ref[0] = jnp.transpose(v)
    else:
        k_ref[...] = k
        v_ref[...] = v
    f = mm(_C_F, _C_XR)[:, :N_ATTN_HEADS] + bf_ref[...]
    lf_ref[...] = -_softplus(-f)
    xr_ref[...] = mm(_C_XR, _C_GATE)
    gg_ref[...] = _gelu_tanh(mm(_C_GATE, _C_GA)).astype(BF16)
    sa_ref[...] = _sigmoid(mm(_C_GA, _C_GR)).astype(BF16)
    sr_ref[...] = _sigmoid(mm(_C_GR, _C_END)).astype(BF16)


def _inproj(x2, w_in_p, b_forget, bsz, t, tm):
    n = x2.shape[0]
    row = lambda w: pl.BlockSpec((tm, w), lambda i: (i, 0))
    full = lambda a: pl.BlockSpec(a.shape, lambda i: (0,) * a.ndim)
    time_minor_kv = tm % LANES == 0 and t % tm == 0
    if time_minor_kv:
        kv_shape = jax.ShapeDtypeStruct((bsz, D_ATTN, t), F32)
        kv_spec = pl.BlockSpec((1, D_ATTN, tm), lambda i: (i // (t // tm), 0, i % (t // tm)))
    else:
        kv_shape = jax.ShapeDtypeStruct((n, D_ATTN), F32)
        kv_spec = row(D_ATTN)
    out_shape = (
        jax.ShapeDtypeStruct((n, D_ATTN), BF16),
        kv_shape,
        kv_shape,
        jax.ShapeDtypeStruct((n, D_ATTN), BF16),
        jax.ShapeDtypeStruct((n, D_ATTN), BF16),
        jax.ShapeDtypeStruct((n, N_ATTN_HEADS), F32),
        jax.ShapeDtypeStruct((n, D_RNN), F32),
        jax.ShapeDtypeStruct((n, D_RNN), BF16),
        jax.ShapeDtypeStruct((n, D_MODEL), BF16),
        jax.ShapeDtypeStruct((n, D_MODEL), BF16),
    )
    out_specs = (row(D_ATTN), kv_spec, kv_spec, row(D_ATTN), row(D_ATTN), row(N_ATTN_HEADS),
                 row(D_RNN), row(D_RNN), row(D_MODEL), row(D_MODEL))
    return pl.pallas_call(
        functools.partial(_inproj_kernel, time_minor_kv=time_minor_kv),
        grid=(n // tm,),
        in_specs=[row(D_MODEL), full(w_in_p), full(b_forget)],
        out_specs=out_specs,
        out_shape=out_shape,
        compiler_params=_params("parallel"),
        name="inproj",
    )(x2, w_in_p, b_forget)


def _cumsum_kernel(x_ref, o_ref):
    x = x_ref[0]
    t = x.shape[1]
    lane = lax.broadcasted_iota(jnp.int32, x.shape, 1)
    d = 1
    while d < t:
        x = x + jnp.where(lane >= d, pltpu.roll(x, d, axis=1), 0.0)
        d *= 2
    o_ref[0] = x * LOG2_E


def _cumsum_time(lf_t):
    b, h, t = lf_t.shape
    spec = pl.BlockSpec((1, h, t), lambda i: (i, 0, 0))
    return pl.pallas_call(
        _cumsum_kernel, grid=(b,), in_specs=[spec], out_specs=spec,
        out_shape=jax.ShapeDtypeStruct(lf_t.shape, F32),
        compiler_params=_params("parallel"), name="logf_cumsum",
    )(lf_t)


def _attn_kernel(q_ref, k_ref, v_ref, fq_ref, fk_ref, o_ref, q_sc, fq_sc, s_sc, p_sc, acc_sc, m_sc, al_sc,
                 lp_sc, *, tq, tk, q_off):
    qi = pl.program_id(2)
    row0 = q_off + qi * tq
    n_full = (row0 + 1) // tk
    heads = range(HEADS_PER_GROUP)
    rb = min(ATTN_ROW_BLOCK, tq)
    lane = lax.broadcasted_iota(jnp.int32, (tq, GROUP_LANES), 1)
    in_head = [(lane >= j * ATTN_HEAD_DIM) & (lane < (j + 1) * ATTN_HEAD_DIM) for j in heads]
    q = q_ref[0]
    for j in heads:
        q_sc[j * tq:(j + 1) * tq, :] = jnp.where(in_head[j], q, jnp.zeros_like(q))
        fq_sc[j * tq:(j + 1) * tq, :] = jnp.broadcast_to(fq_ref[0, 0, :, j:j + 1], (tq, LANES))
    m_sc[...] = jnp.full_like(m_sc, NEG_BIG)
    lp_sc[...] = jnp.zeros_like(lp_sc)
    acc_sc[...] = jnp.zeros_like(acc_sc)
    halves = [slice(0, 2 * tq), slice(2 * tq, 4 * tq)]
    n_lane_tiles = tk // LANES

    def scores(c, s_sc):
        ks = k_ref[0, pl.ds(pl.multiple_of(c * tk, tk), tk), :]
        for hv in halves:
            s_sc[hv, :] = lax.dot_general(q_sc[hv, :], ks, (((1,), (1,)), ((), ())),
                                          preferred_element_type=F32)

    def absorb(c, s_sc, masked):
        start = pl.multiple_of(c * tk, tk)
        fk = fk_ref[0, 0, c]
        for j in heads:
            for r in range(tq // rb):
                rows = slice(j * tq + r * rb, j * tq + (r + 1) * rb)
                s = s_sc[rows, :] - fk[j:j + 1, :]
                if masked:
                    q_pos = row0 + r * rb + lax.broadcasted_iota(jnp.int32, (rb, tk), 0)
                    k_pos = c * tk + lax.broadcasted_iota(jnp.int32, (rb, tk), 1)
                    s = jnp.where(k_pos <= q_pos, s, NEG_BIG)
                tiles = [s[:, t * LANES:(t + 1) * LANES] for t in range(n_lane_tiles)]
                row_max = jnp.max(functools.reduce(jnp.maximum, tiles), axis=1, keepdims=True)
                fq = fq_sc[rows, :]
                m_old = m_sc[rows, :]
                m_new = jnp.maximum(m_old, jnp.broadcast_to(row_max, (rb, LANES)) + fq)
                alpha = jnp.exp2(m_old - m_new)
                shift = fq - m_new
                p_tiles = [jnp.exp2(tile + shift) for tile in tiles]
                m_sc[rows, :] = m_new
                al_sc[rows, :] = alpha
                lp_sc[rows, :] = alpha * lp_sc[rows, :] + functools.reduce(jnp.add, p_tiles)
                p_sc[rows, :] = jnp.concatenate(p_tiles, axis=1).astype(BF16)
        vs = v_ref[0, pl.ds(start, tk), :]
        for hv in halves:
            pv = jnp.dot(p_sc[hv, :], vs, preferred_element_type=F32)
            alpha = al_sc[hv, :]
            acc_sc[hv, :] = jnp.concatenate([alpha] * (GROUP_LANES // LANES), axis=1) * acc_sc[hv, :] + pv

    def step(c, carry):
        scores(c, s_sc)
        absorb(c, s_sc, masked=False)
        return carry

    lax.fori_loop(0, n_full, step, 0)
    scores(n_full, s_sc)
    absorb(n_full, s_sc, masked=True)
    o_all = acc_sc[...] / jnp.sum(lp_sc[...], axis=1, keepdims=True)
    out = jnp.zeros((tq, GROUP_LANES), F32)
    for j in heads:
        out = jnp.where(in_head[j], o_all[j * tq:(j + 1) * tq], out)
    o_ref[0] = out.astype(BF16)


def _attention(q, k_all, v_all, fq, fk, *, tq, tk, q_off):
    b, t, _ = q.shape
    t_k = k_all.shape[1]
    n_kv = t_k // tk
    assert all((q_off + i * tq) % tk + tq <= tk for i in range(t // tq)), "a query block straddles key chunks"
    kern = functools.partial(_attn_kernel, tq=tq, tk=tk, q_off=q_off)
    rows = HEADS_PER_GROUP * tq
    return pl.pallas_call(
        kern,
        grid=(b, N_HEAD_GROUPS, t // tq),
        in_specs=[
            pl.BlockSpec((1, tq, GROUP_LANES), lambda bi, g, i: (bi, i, g)),
            pl.BlockSpec((1, t_k, GROUP_LANES), lambda bi, g, i: (bi, 0, g)),
            pl.BlockSpec((1, t_k, GROUP_LANES), lambda bi, g, i: (bi, 0, g)),
            pl.BlockSpec((1, 1, tq, HEADS_PER_GROUP), lambda bi, g, i: (bi, g, i, 0)),
            pl.BlockSpec((1, 1, n_kv, HEADS_PER_GROUP, tk), lambda bi, g, i: (bi, g, 0, 0, 0)),
        ],
        out_specs=pl.BlockSpec((1, tq, GROUP_LANES), lambda bi, g, i: (bi, i, g)),
        out_shape=jax.ShapeDtypeStruct((b, t, D_ATTN), BF16),
        scratch_shapes=[
            pltpu.VMEM((rows, GROUP_LANES), BF16),
            pltpu.VMEM((rows, LANES), F32),
            pltpu.VMEM((rows, tk), F32),
            pltpu.VMEM((rows, tk), BF16),
            pltpu.VMEM((rows, GROUP_LANES), F32),
            pltpu.VMEM((rows, LANES), F32),
            pltpu.VMEM((rows, LANES), F32),
            pltpu.VMEM((rows, LANES), F32),
        ],
        compiler_params=_params("parallel", "parallel", "arbitrary"),
        name="fox_attention",
    )(q, k_all, v_all, fq, fk)


def _rnn_kernel(xr_ref, gg_ref, hist_ref, h0_ref, cw_ref, cb_ref, wa_ref, ba_ref, wx_ref, bx_ref,
                lam_ref, out_ref, hl_ref, prev_sc, h_sc, *, tb):
    t = pl.program_id(1)

    @pl.when(t == 0)
    def _():
        prev_sc[...] = hist_ref[0]
        h_sc[...] = h0_ref[0]

    x = xr_ref[0]
    prev = prev_sc[...]
    row8 = lax.broadcasted_iota(jnp.int32, (SUBLANES, D_RNN), 0)
    xc = x * cw_ref[CONV_WIDTH - 1:CONV_WIDTH, :] + cb_ref[...]
    for s in range(1, CONV_WIDTH):
        xs = pltpu.roll(x, s, axis=0)
        head = jnp.where(row8 < s, pltpu.roll(prev, s, axis=0), xs[:SUBLANES])
        xs = jnp.concatenate([head, xs[SUBLANES:]], axis=0)
        xc = xc + xs * cw_ref[CONV_WIDTH - 1 - s:CONV_WIDTH - s, :]
    prev_sc[...] = x[tb - SUBLANES:tb]

    xcb = xc.astype(BF16)
    r = _sigmoid(jnp.dot(xcb, wa_ref[...], preferred_element_type=F32) + ba_ref[...])
    ig = _sigmoid(jnp.dot(xcb, wx_ref[...], preferred_element_type=F32) + bx_ref[...])
    log_a = (-LRU_C) * r * _softplus(-lam_ref[...])
    a = jnp.exp(log_a)
    bterm = jnp.sqrt(1.0 - a * a) * ig * xc

    row = lax.broadcasted_iota(jnp.int32, (tb, D_RNN), 0)
    d = 1
    while d < tb:
        valid = row >= d
        a_s = pltpu.roll(a, d, axis=0)
        b_s = pltpu.roll(bterm, d, axis=0)
        bterm = jnp.where(valid, a * b_s + bterm, bterm)
        a = jnp.where(valid, a * a_s, a)
        d *= 2
    h = bterm + a * h_sc[...]
    h_last = h[tb - 1:tb]
    h_sc[...] = h_last
    hl_ref[0] = h_last
    out_ref[0] = (h * gg_ref[0].astype(F32)).astype(BF16)


def _rnn(xr, gg, hist8, h0, conv_w, conv_b, wa, ba, wx, bx, lam, tb):
    b, t, _ = xr.shape
    blk = pl.BlockSpec((1, tb, D_RNN), lambda bi, ti: (bi, ti, 0))
    full = lambda a: pl.BlockSpec(a.shape, lambda bi, ti: (0,) * a.ndim)
    per_b = lambda r: pl.BlockSpec((1, r, D_RNN), lambda bi, ti: (bi, 0, 0))
    return pl.pallas_call(
        functools.partial(_rnn_kernel, tb=tb),
        grid=(b, t // tb),
        in_specs=[blk, blk, per_b(SUBLANES), per_b(1), full(conv_w), full(conv_b), full(wa), full(ba),
                  full(wx), full(bx), full(lam)],
        out_specs=(blk, per_b(1)),
        out_shape=(jax.ShapeDtypeStruct((b, t, D_RNN), BF16), jax.ShapeDtypeStruct((b, 1, D_RNN), F32)),
        scratch_shapes=[pltpu.VMEM((SUBLANES, D_RNN), F32), pltpu.VMEM((1, D_RNN), F32)],
        compiler_params=_params("parallel", "arbitrary"),
        name="conv_rglru",
    )(xr, gg, hist8, h0, conv_w, conv_b, wa, ba, wx, bx, lam)


def _outproj_kernel(o_ref, r_ref, sa_ref, sr_ref, x_ref, wau_ref, wru_ref, wo_ref, g_ref, b_ref,
                    h_ref, *, alpha):
    up_a = jnp.dot(o_ref[...], wau_ref[...], preferred_element_type=F32)
    up_r = jnp.dot(r_ref[...], wru_ref[...], preferred_element_type=F32)
    merged = sa_ref[...].astype(F32) * up_a + sr_ref[...].astype(F32) * up_r
    mix = jnp.dot(merged.astype(BF16), wo_ref[...], preferred_element_type=F32)
    h_ref[...] = _layer_norm(alpha * x_ref[...] + mix, g_ref[...], b_ref[...])


def _outproj(o2, r2, sa, sr, x2, wau, wru, wo, g, bta, alpha, tm):
    n = x2.shape[0]
    row = lambda w: pl.BlockSpec((tm, w), lambda i: (i, 0))
    full = lambda a: pl.BlockSpec(a.shape, lambda i: (0,) * a.ndim)
    return pl.pallas_call(
        functools.partial(_outproj_kernel, alpha=alpha),
        grid=(n // tm,),
        in_specs=[row(D_ATTN), row(D_RNN), row(D_MODEL), row(D_MODEL), row(D_MODEL), full(wau), full(wru),
                  full(wo), full(g), full(bta)],
        out_specs=row(D_MODEL),
        out_shape=jax.ShapeDtypeStruct((n, D_MODEL), F32),
        compiler_params=_params("parallel"),
        name="outproj_ln1",
    )(o2, r2, sa, sr, x2, wau, wru, wo, g, bta)


ROUTE_TOKENS = 2 * LANES


def _odd_even_merge_sort(n):
    def merge(lo, hi, r):
        step = r * 2
        if step < hi - lo:
            yield from merge(lo, hi, step)
            yield from merge(lo + r, hi, step)
            yield from ((i, i + r) for i in range(lo + r, hi - r, step))
        else:
            yield (lo, lo + r)

    def sort(lo, hi):
        if hi - lo >= 1:
            mid = lo + (hi - lo) // 2
            yield from sort(lo, mid)
            yield from sort(mid + 1, hi)
            yield from merge(lo, hi, 1)

    return tuple(sort(0, n - 1))


_SORT_TOPK = _odd_even_merge_sort(PEER_TOPK)


def _compare_exchange(v, i, j):
    v[i], v[j] = jnp.maximum(v[i], v[j]), jnp.minimum(v[i], v[j])


def _top_k_sorted(tiles):
    v = list(tiles)
    for i, j in _SORT_TOPK:
        _compare_exchange(v, i, j)
    shift = SUBLANES // 2
    while shift >= 1:
        other = [pltpu.roll(x, shift, axis=0) for x in v]
        v = [jnp.maximum(v[i], other[PEER_TOPK - 1 - i]) for i in range(PEER_TOPK)]
        d = PEER_TOPK // 2
        while d >= 1:
            for i in range(PEER_TOPK):
                if i & d == 0:
                    _compare_exchange(v, i, i + d)
            d //= 2
        shift //= 2
    return v


def _count_greater(x, t):
    assert len(t) == 16, "the bisection below is written out for 16 entries"
    one = lambda m, w: jnp.where(m, float(w), 0.0)
    b3 = t[7] > x
    b2 = jnp.where(b3, t[11], t[3]) > x
    b1 = jnp.where(b3, jnp.where(b2, t[13], t[9]), jnp.where(b2, t[5], t[1])) > x
    hi = jnp.where(b2, jnp.where(b1, t[14], t[12]), jnp.where(b1, t[10], t[8]))
    lo = jnp.where(b2, jnp.where(b1, t[6], t[4]), jnp.where(b1, t[2], t[0]))
    b0 = jnp.where(b3, hi, lo) > x
    count = one(b3, 8) + one(b2, 4) + one(b1, 2) + one(b0, 1)
    return jnp.where(t[15] > x, float(PEER_TOPK), count)


def _route_kernel(h_ref, wq_ref, k1_ref, k2_ref, ht_ref, cnt_ref, e1_ref, r2_ref, e2_ref,
                  qt_sc, t1_sc, t2_sc, *, tm):
    ht = jnp.transpose(h_ref[...]).astype(BF16)
    ht_ref[...] = ht
    qt_sc[...] = jnp.dot(wq_ref[...], ht, preferred_element_type=F32).astype(BF16)
    key_iota = lax.broadcasted_iota(jnp.int32, (N_KEYS, tm), 0).astype(F32)
    top_iota = lax.broadcasted_iota(jnp.int32, (PEER_TOPK, tm), 0).astype(F32)
    front_rows = SUBLANES

    def scores(hd):
        base = hd * 2 * PEER_HALF
        s1 = jnp.dot(k1_ref[...], qt_sc[base:base + PEER_HALF], preferred_element_type=F32)
        s2 = jnp.dot(k2_ref[...], qt_sc[base + PEER_HALF:base + 2 * PEER_HALF], preferred_element_type=F32)
        return s1, s2

    sub = lax.broadcasted_iota(jnp.int32, (SUBLANES, tm), 0)
    tiles_of = lambda s: [s[r * SUBLANES:(r + 1) * SUBLANES] for r in range(N_KEYS // SUBLANES)]
    sublane_sum = lambda x: jnp.sum(x, axis=0, keepdims=True)

    def route_head_sorted(hd):
        s1, s2 = scores(hd)
        rows1, rows2 = tiles_of(s1), tiles_of(s2)
        t1 = _top_k_sorted(rows1)
        t2 = _top_k_sorted(rows2)
        t1_lo = t1[SUBLANES - 1]
        for a in range(SUBLANES - 2, -1, -1):
            t1_lo = jnp.where(sub == a, t1[a], t1_lo)
        cand = [t1_lo + t2[b] for b in range(PEER_TOPK)]
        pool = list(cand)
        for a in range(SUBLANES, PEER_TOPK):
            pool[a] = jnp.where(sub == 1, t1[a] + t2[0], cand[a])
        top = _top_k_sorted(pool)
        tau = top[PEER_TOPK - 1]
        cnt_lo = functools.reduce(jnp.add, [jnp.where(c >= tau, 1.0, 0.0) for c in cand])
        cnt = [jnp.broadcast_to(cnt_lo[a:a + 1], (SUBLANES, tm)) for a in range(SUBLANES)]
        cnt += [jnp.where(t1[a] + t2[0] >= tau, 1.0, 0.0) for a in range(SUBLANES, PEER_TOPK)]
        z = functools.reduce(jnp.add, [jnp.exp(t - top[0]) for t in top])

        cnt1_rows, rank2_rows = [], []
        for r in range(N_KEYS // SUBLANES):
            c = jnp.zeros((SUBLANES, tm), F32)
            for a in range(PEER_TOPK):
                c = jnp.where(rows1[r] == t1[a], cnt[a], c)
            cnt1_rows.append(c)
            g = _count_greater(rows2[r], t2)
            rank2_rows.append(jnp.where(g < float(PEER_TOPK), g, NOT_SELECTED_RANK))
        cnt1 = jnp.concatenate(cnt1_rows, axis=0)
        rank2 = jnp.concatenate(rank2_rows, axis=0)
        cnt_ref[hd] = cnt1
        e1_ref[hd] = jnp.exp(s1 - t1[0][0:1]) / z[0:1]
        r2_ref[hd] = rank2.astype(BF16)
        e2_ref[hd] = jnp.exp(s2 - t2[0][0:1]).astype(BF16)

        gap = lambda t: functools.reduce(jnp.minimum, [t[b] - t[b + 1] for b in range(PEER_TOPK - 1)])[0:1]
        n_cnt = sublane_sum(cnt_lo) + functools.reduce(jnp.add, cnt[SUBLANES:])[0:1]
        n_sel = sublane_sum(functools.reduce(jnp.add, cnt1_rows))
        n_rank = sublane_sum(functools.reduce(
            jnp.add, [jnp.where(x < float(PEER_TOPK), 1.0, 0.0) for x in rank2_rows]))
        off = lambda n: jnp.abs(n - float(PEER_TOPK))
        return (off(n_cnt) + off(n_sel) + off(n_rank)
                + jnp.where(jnp.minimum(gap(t1), gap(t2)) > 0.0, 0.0, 1.0))

    def route_head_exact(hd):
        def pick_one(v, iota, n):
            m = jnp.max(v, axis=0, keepdims=True)
            return m, iota == jnp.min(jnp.where(v == m, iota, float(n)), axis=0, keepdims=True)

        s1, s2 = scores(hd)

        def extract(a, carry):
            v1, r1, v2, r2 = carry
            m1, sel1 = pick_one(v1, key_iota, N_KEYS)
            m2, sel2 = pick_one(v2, key_iota, N_KEYS)
            t1_sc[pl.ds(a, 1), :] = m1
            t2_sc[pl.ds(a, 1), :] = m2
            af = jnp.asarray(a, dtype=F32)
            return (jnp.where(sel1, -jnp.inf, v1), jnp.where(sel1, af, r1),
                    jnp.where(sel2, -jnp.inf, v2), jnp.where(sel2, af, r2))

        no_rank = jnp.full((N_KEYS, tm), NOT_SELECTED_RANK, F32)
        _, rank1, _, rank2 = lax.fori_loop(0, PEER_TOPK, extract, (s1, no_rank, s2, no_rank))
        t1 = t1_sc[...]
        t2 = t2_sc[...]
        top0 = t1[0:1] + t2[0:1]

        def pick(_, carry):
            ptr, front, z = carry
            m, sel = pick_one(front, top_iota, PEER_TOPK)
            ptr = ptr + jnp.where(sel, 1.0, 0.0)
            lo = ptr[:front_rows]
            nxt = jnp.full(lo.shape, -jnp.inf, F32)
            for b in range(1, PEER_TOPK):
                nxt = jnp.where(lo == float(b), t2_sc[b:b + 1, :], nxt)
            nxt = jnp.where(lo == 0.0, t2[0:1], nxt)
            front = jnp.concatenate(
                [t1[:front_rows] + nxt, jnp.where(sel[front_rows:], -jnp.inf, front[front_rows:])], axis=0)
            return ptr, front, z + jnp.exp(m - top0)

        cnt, _, z = lax.fori_loop(
            0, PEER_TOPK, pick,
            (jnp.zeros((PEER_TOPK, tm), F32), t1 + t2[0:1], jnp.zeros((1, tm), F32)))

        cnt1 = jnp.zeros((N_KEYS, tm), F32)
        for a in range(PEER_TOPK):
            cnt1 = jnp.where(rank1 == float(a), cnt[a:a + 1], cnt1)
        cnt_ref[hd] = cnt1
        e1_ref[hd] = jnp.exp(s1 - t1[0:1]) / z
        r2_ref[hd] = rank2.astype(BF16)
        e2_ref[hd] = jnp.exp(s2 - t2[0:1]).astype(BF16)

    doubt = [route_head_sorted(hd) for hd in range(PEER_HEADS)]

    @pl.when(jnp.max(functools.reduce(jnp.maximum, doubt)) > 0.0)
    def _():
        for hd in range(PEER_HEADS):
            @pl.when(jnp.max(doubt[hd]) > 0.0)
            def _():
                route_head_exact(hd)


def _route(h2, wq_t, k1, k2, tm):
    n = h2.shape[0]
    full = lambda a: pl.BlockSpec(a.shape, lambda i: (0,) * a.ndim)
    per_head = pl.BlockSpec((PEER_HEADS, N_KEYS, tm), lambda i: (0, 0, i))
    hshape = lambda dt: jax.ShapeDtypeStruct((PEER_HEADS, N_KEYS, n), dt)
    return pl.pallas_call(
        functools.partial(_route_kernel, tm=tm),
        grid=(n // tm,),
        in_specs=[pl.BlockSpec((tm, D_MODEL), lambda i: (i, 0)), full(wq_t), full(k1), full(k2)],
        out_specs=(pl.BlockSpec((D_MODEL, tm), lambda i: (0, i)), per_head, per_head, per_head, per_head),
        out_shape=(jax.ShapeDtypeStruct((D_MODEL, n), BF16), hshape(F32), hshape(F32), hshape(BF16),
                   hshape(BF16)),
        scratch_shapes=[pltpu.VMEM((PEER_HEADS * 2 * PEER_HALF, tm), BF16),
                        pltpu.VMEM((PEER_TOPK, tm), F32), pltpu.VMEM((PEER_TOPK, tm), F32)],
        compiler_params=_params("parallel"),
        name="peer_route",
    )(h2, wq_t, k1, k2)


ROWS_PER_STEP = 16
EXPERTS_PER_STEP = ROWS_PER_STEP * N_KEYS
ROWS_PER_SUB = 4
EXPERTS_PER_SUB = ROWS_PER_SUB * N_KEYS
SUBS_PER_MIX = 2
DENSE_CHUNK = 2 * LANES


def _bf16_row_tile(row):
    tile = jnp.broadcast_to(row, (BF16_ROWS, row.shape[1])).astype(BF16)
    return jnp.concatenate([tile] * (N_KEYS // BF16_ROWS), axis=0)


def _dense_kernel(ht_ref, u_ref, vt_ref, cnt_ref, e1_ref, r2_ref, e2_ref, h_ref, g_ref, b_ref,
                  y_ref, acc_sc, act_sc, coef_sc, *, alpha):
    e = pl.program_id(1)
    tm = act_sc.shape[1]
    n_sub = ROWS_PER_STEP // ROWS_PER_SUB

    @pl.when(e == 0)
    def _():
        acc_sc[...] = jnp.zeros_like(acc_sc)

    def activations(sub):
        rows = slice(sub * EXPERTS_PER_SUB, (sub + 1) * EXPERTS_PER_SUB)
        act_sc[rows, :] = jnp.dot(u_ref[rows, :], ht_ref[...], preferred_element_type=F32)

    def tiles(sub):
        chunk = min(DENSE_CHUNK, tm)
        for il in range(sub * ROWS_PER_SUB, (sub + 1) * ROWS_PER_SUB):
            for c in range(tm // chunk):
                yield il, slice(il * N_KEYS, (il + 1) * N_KEYS), slice(c * chunk, (c + 1) * chunk)

    def routing_weights(sub):
        for il, rows, lanes in tiles(sub):
            w = None
            for hd in range(PEER_HEADS):
                cnt = _bf16_row_tile(cnt_ref[hd, il:il + 1, lanes])
                e1 = _bf16_row_tile(e1_ref[hd, il:il + 1, lanes])
                term = jnp.where(r2_ref[hd, :, lanes] < cnt, e2_ref[hd, :, lanes] * e1, jnp.zeros((), BF16))
                w = term if w is None else w + term
            coef_sc[rows, lanes] = w

    def coefficients(sub):
        for _, rows, lanes in tiles(sub):
            coef_sc[rows, lanes] = coef_sc[rows, lanes] * _gelu_tanh(act_sc[rows, lanes].astype(BF16))

    def mix(first, last):
        rows = slice(first * EXPERTS_PER_SUB, (last + 1) * EXPERTS_PER_SUB)
        acc_sc[...] += jnp.dot(vt_ref[:, rows], coef_sc[rows, :], preferred_element_type=F32)

    routing_weights(0)
    activations(0)
    for sub in range(n_sub):
        if sub + 1 < n_sub:
            routing_weights(sub + 1)
            activations(sub + 1)
        coefficients(sub)
        if sub % SUBS_PER_MIX == SUBS_PER_MIX - 1:
            mix(sub - SUBS_PER_MIX + 1, sub)

    @pl.when(e == pl.num_programs(1) - 1)
    def _():
        peer = jnp.transpose(acc_sc[...])
        y_ref[...] = _layer_norm(alpha * h_ref[...] + peer, g_ref[...], b_ref[...])


def _dense(ht, u_b, vt_b, cnt1, e1n, rank2, e2, h2, g, bta, alpha, tm):
    n = h2.shape[0]
    n_e = N_EXPERTS // EXPERTS_PER_STEP
    full = lambda a: pl.BlockSpec(a.shape, lambda t, e: (0,) * a.ndim)
    rows_blk = pl.BlockSpec((PEER_HEADS, ROWS_PER_STEP, tm), lambda t, e: (0, e, t))
    cols_blk = pl.BlockSpec((PEER_HEADS, N_KEYS, tm), lambda t, e: (0, 0, t))
    return pl.pallas_call(
        functools.partial(_dense_kernel, alpha=alpha),
        grid=(n // tm, n_e),
        in_specs=[
            pl.BlockSpec((D_MODEL, tm), lambda t, e: (0, t)),
            pl.BlockSpec((EXPERTS_PER_STEP, D_MODEL), lambda t, e: (e, 0)),
            pl.BlockSpec((D_MODEL, EXPERTS_PER_STEP), lambda t, e: (0, e)),
            rows_blk, rows_blk, cols_blk, cols_blk,
            pl.BlockSpec((tm, D_MODEL), lambda t, e: (t, 0)),
            full(g), full(bta),
        ],
        out_specs=pl.BlockSpec((tm, D_MODEL), lambda t, e: (t, 0)),
        out_shape=jax.ShapeDtypeStruct((n, D_MODEL), F32),
        scratch_shapes=[pltpu.VMEM((D_MODEL, tm), F32), pltpu.VMEM((EXPERTS_PER_STEP, tm), F32),
                        pltpu.VMEM((EXPERTS_PER_STEP, tm), BF16)],
        compiler_params=_params("parallel", "arbitrary"),
        name="peer_dense",
    )(ht, u_b, vt_b, cnt1, e1n, rank2, e2, h2, g, bta)


def _block_diag(w):
    nb, bi, bo = w.shape
    eye = jnp.eye(nb, dtype=w.dtype)
    return (eye[:, None, :, None] * w[:, :, None, :]).reshape(nb * bi, nb * bo)


def _prep_weights(w_in, b_forget, conv_w, conv_b, w_rg_a, b_rg_a, w_rg_x, b_rg_x, lru_lambda,
                  w_attn_up, w_rnn_up, w_out, ln1_g, ln1_b, peer_w_query, peer_keys_1, peer_keys_2,
                  peer_u, peer_v, ln2_g, ln2_b):
    c_f = 3 * D_ATTN
    w_in_p = jnp.concatenate(
        [w_in[:, :c_f], jnp.pad(w_in[:, c_f:c_f + N_ATTN_HEADS], ((0, 0), (0, F_PAD - N_ATTN_HEADS))),
         w_in[:, c_f + N_ATTN_HEADS:]], axis=1).astype(BF16)
    row = lambda a: a.reshape(1, -1).astype(F32)
    return dict(
        w_in=w_in_p, b_forget=row(b_forget), conv_w=conv_w.astype(F32), conv_b=row(conv_b),
        wa=_block_diag(w_rg_a).astype(BF16), ba=row(b_rg_a), wx=_block_diag(w_rg_x).astype(BF16),
        bx=row(b_rg_x), lam=row(lru_lambda),
        wau=w_attn_up.astype(BF16), wru=w_rnn_up.astype(BF16), wo=w_out.astype(BF16),
        ln1_g=row(ln1_g), ln1_b=row(ln1_b),
        wq_t=jnp.transpose(peer_w_query).astype(BF16), k1=peer_keys_1.astype(BF16),
        k2=peer_keys_2.astype(BF16), u=peer_u.astype(BF16), vt=jnp.transpose(peer_v).astype(BF16),
        ln2_g=row(ln2_g), ln2_b=row(ln2_b),
    )


def _pick_block(n, target):
    blk = min(n, target)
    assert n % blk == 0, (n, blk)
    return blk


def _trunk_layer(x, past_k, past_v, past_logf, conv_hist, h0, p, alpha):
    bsz, t, _ = x.shape
    n = bsz * t
    x2 = x.reshape(n, D_MODEL)
    q, k, v, kb, vb, lf, xr, gg, sa, sr = _inproj(x2, p["w_in"], p["b_forget"], bsz, t, _pick_block(n, 256))

    def state_layout(a):
        if a.ndim == 3:
            return jnp.transpose(a.reshape(bsz, N_ATTN_HEADS, ATTN_HEAD_DIM, t), (0, 3, 1, 2))
        return a.reshape(bsz, t, N_ATTN_HEADS, ATTN_HEAD_DIM)

    tq = _pick_block(t, ATTN_Q_BLOCK)
    n_past = 0 if past_k is None else past_k.shape[1]
    t_all = n_past + t
    t_lanes = -(-t_all // LANES) * LANES
    tk = ATTN_K_BLOCK if t_lanes % ATTN_K_BLOCK == 0 else t_lanes
    t_pad = -(-t_all // tk) * tk
    lf3 = lf.reshape(bsz, t, N_ATTN_HEADS)
    kb3 = kb.reshape(bsz, t, D_ATTN)
    vb3 = vb.reshape(bsz, t, D_ATTN)
    if past_k is not None:
        lf3 = jnp.concatenate([past_logf.astype(F32), lf3], axis=1)
        kb3 = jnp.concatenate([past_k.reshape(bsz, n_past, D_ATTN).astype(BF16), kb3], axis=1)
        vb3 = jnp.concatenate([past_v.reshape(bsz, n_past, D_ATTN).astype(BF16), vb3], axis=1)
    pad = ((0, 0), (0, t_pad - t_all), (0, 0))
    lf3, kb3, vb3 = jnp.pad(lf3, pad), jnp.pad(kb3, pad), jnp.pad(vb3, pad)
    f_t = _cumsum_time(jnp.transpose(lf3, (0, 2, 1)))
    f_g = f_t.reshape(bsz, N_HEAD_GROUPS, HEADS_PER_GROUP, t_pad)
    fq = jnp.transpose(f_g[:, :, :, n_past:n_past + t], (0, 1, 3, 2))
    fk = jnp.transpose(f_g.reshape(bsz, N_HEAD_GROUPS, HEADS_PER_GROUP, t_pad // tk, tk), (0, 1, 3, 2, 4))
    o = _attention(q.reshape(bsz, t, D_ATTN), kb3, vb3, fq, fk, tq=tq, tk=tk, q_off=n_past)

    hist8 = jnp.pad(conv_hist.astype(F32), ((0, 0), (SUBLANES - (CONV_WIDTH - 1), 0), (0, 0)))
    xr3 = xr.reshape(bsz, t, D_RNN)
    rnn_out, h_last = _rnn(xr3, gg.reshape(bsz, t, D_RNN), hist8, h0.astype(F32).reshape(bsz, 1, D_RNN),
                           p["conv_w"], p["conv_b"], p["wa"], p["ba"], p["wx"], p["bx"], p["lam"],
                           _pick_block(t, 256))
    new_hist = jnp.concatenate([conv_hist.astype(F32), xr3], axis=1)[:, -(CONV_WIDTH - 1):]

    h = _outproj(o.reshape(n, D_ATTN), rnn_out.reshape(n, D_RNN), sa, sr, x2, p["wau"], p["wru"], p["wo"],
                 p["ln1_g"], p["ln1_b"], alpha, _pick_block(n, 256))
    ht, cnt1, e1n, rank2, e2 = _route(h, p["wq_t"], p["k1"], p["k2"], _pick_block(n, ROUTE_TOKENS))
    y = _dense(ht, p["u"], p["vt"], cnt1, e1n, rank2, e2, h, p["ln2_g"], p["ln2_b"], alpha,
               _pick_block(n, 512))
    return (y.reshape(bsz, t, D_MODEL), state_layout(k), state_layout(v), lf.reshape(bsz, t, N_ATTN_HEADS), new_hist,
            h_last.reshape(bsz, D_RNN))


def kernel(x_prompt, x_sample, cache_k, cache_v, cache_logf, state_conv, state_rnn, w_in, b_forget, conv_w, conv_b, w_rg_a, b_rg_a, w_rg_x, b_rg_x, lru_lambda, w_attn_up, w_rnn_up, w_out, ln1_g, ln1_b, peer_w_query, peer_keys_1, peer_keys_2, peer_u, peer_v, ln2_g, ln2_b):
    depth = w_in.shape[0]
    alpha = (2 * depth) ** 0.25
    layer_weights = (w_in, b_forget, conv_w, conv_b, w_rg_a, b_rg_a, w_rg_x, b_rg_x, lru_lambda, w_attn_up,
                     w_rnn_up, w_out, ln1_g, ln1_b, peer_w_query, peer_keys_1, peer_keys_2, peer_u, peer_v,
                     ln2_g, ln2_b)
    hp, hs = x_prompt, x_sample
    prompt_state, sample_state = [], []
    for l in range(depth):
        p = _prep_weights(*(w[l] for w in layer_weights))
        zero_hist = jnp.zeros((hp.shape[0], CONV_WIDTH - 1, D_RNN), F32)
        zero_h = jnp.zeros((hp.shape[0], D_RNN), F32)
        hp, *st_p = _trunk_layer(hp, None, None, None, zero_hist, zero_h, p, alpha)
        hs, *st_s = _trunk_layer(hs, cache_k[l], cache_v[l], cache_logf[l], state_conv[l], state_rnn[l], p,
                                 alpha)
        prompt_state.append(st_p)
        sample_state.append(st_s)
    stack = lambda states, i: jnp.stack([s[i] for s in states])
    return (hp, hs) + tuple(stack(prompt_state, i) for i in range(5)) + tuple(
        stack(sample_state, i) for i in range(5))
```

```python
import functools
import math

import jax
import jax.numpy as jnp
from jax import lax
from jax.experimental import pallas as pl
from jax.experimental.pallas import tpu as pltpu

F32 = jnp.float32
BF16 = jnp.bfloat16

D_MODEL = 1024
N_ATTN_HEADS = 8
ATTN_HEAD_DIM = 64
D_ATTN = N_ATTN_HEADS * ATTN_HEAD_DIM
ATTN_SCALE = ATTN_HEAD_DIM ** -0.5
D_RNN = 512
N_RNN_BLOCKS = 8
RNN_BLOCK = D_RNN // N_RNN_BLOCKS
CONV_WIDTH = 4
LRU_C = 8.0
N_KEYS = 128
N_EXPERTS = N_KEYS * N_KEYS
PEER_HEADS = 8
PEER_TOPK = 16
PEER_HALF = 128
LN_EPS = 1e-5

LANES = 128
SUBLANES = 8
BF16_ROWS = 2 * SUBLANES
VMEM_LIMIT_BYTES = 56 * 1024 * 1024

HEADS_PER_GROUP = 4
GROUP_LANES = HEADS_PER_GROUP * ATTN_HEAD_DIM
N_HEAD_GROUPS = N_ATTN_HEADS // HEADS_PER_GROUP
ATTN_Q_BLOCK = 512
ATTN_K_BLOCK = 512
ATTN_ROW_BLOCK = 32
F_PAD = LANES
NEG_BIG = -1e30
LOG2_E = math.log2(math.e)
NOT_SELECTED_RANK = 99.0

_C_Q = 0
_C_K = _C_Q + D_ATTN
_C_V = _C_K + D_ATTN
_C_F = _C_V + D_ATTN
_C_XR = _C_F + F_PAD
_C_GATE = _C_XR + D_RNN
_C_GA = _C_GATE + D_RNN
_C_GR = _C_GA + D_MODEL
_C_END = _C_GR + D_MODEL


def _params(*sem):
    return pltpu.CompilerParams(dimension_semantics=sem, vmem_limit_bytes=VMEM_LIMIT_BYTES)


def _sigmoid(x):
    return 1.0 / (1.0 + jnp.exp(-x))


def _gelu_tanh(x):
    half = 0.5 * x
    return half + half * jnp.tanh(x * (0.7978845608028654 + 0.035677408136300125 * (x * x)))


def _softplus(x):
    return jnp.maximum(x, 0.0) + jnp.log1p(jnp.exp(-jnp.abs(x)))


def _layer_norm(x, g, b):
    mu = jnp.mean(x, axis=-1, keepdims=True)
    xc = x - mu
    var = jnp.mean(xc * xc, axis=-1, keepdims=True)
    return xc * lax.rsqrt(var + LN_EPS) * g + b


def _inproj_kernel(x_ref, w_ref, bf_ref, q_ref, k_ref, v_ref, kb_ref, vb_ref, lf_ref, xr_ref,
                   gg_ref, sa_ref, sr_ref, *, time_minor_kv):
    xb = x_ref[...].astype(BF16)

    def mm(lo, hi):
        return jnp.dot(xb, w_ref[:, lo:hi], preferred_element_type=F32)

    q_ref[...] = (mm(_C_Q, _C_K) * (ATTN_SCALE * LOG2_E)).astype(BF16)
    k = mm(_C_K, _C_V)
    kb_ref[...] = k.astype(BF16)
    v = mm(_C_V, _C_F)
    vb_ref[...] = v.astype(BF16)
    if time_minor_kv:
        k_ref[0] = jnp.transpose(k)
        v_ref[0] = jnp.transpose(v)
    else:
        k_ref[...] = k
        v_ref[...] = v
    f = mm(_C_F, _C_XR)[:, :N_ATTN_HEADS] + bf_ref[...]
    lf_ref[...] = -_softplus(-f)
    xr_ref[...] = mm(_C_XR, _C_GATE)
    gg_ref[...] = _gelu_tanh(mm(_C_GATE, _C_GA)).astype(BF16)
    sa_ref[...] = _sigmoid(mm(_C_GA, _C_GR)).astype(BF16)
    sr_ref[...] = _sigmoid(mm(_C_GR, _C_END)).astype(BF16)


def _inproj(x2, w_in_p, b_forget, bsz, t, tm):
    n = x2.shape[0]
    row = lambda w: pl.BlockSpec((tm, w), lambda i: (i, 0))
    full = lambda a: pl.BlockSpec(a.shape, lambda i: (0,) * a.ndim)
    time_minor_kv = tm % LANES == 0 and t % tm == 0
    if time_minor_kv:
        kv_shape = jax.ShapeDtypeStruct((bsz, D_ATTN, t), F32)
        kv_spec = pl.BlockSpec((1, D_ATTN, tm), lambda i: (i // (t // tm), 0, i % (t // tm)))
    else:
        kv_shape = jax.ShapeDtypeStruct((n, D_ATTN), F32)
        kv_spec = row(D_ATTN)
    out_shape = (
        jax.ShapeDtypeStruct((n, D_ATTN), BF16),
        kv_shape,
        kv_shape,
        jax.ShapeDtypeStruct((n, D_ATTN), BF16),
        jax.ShapeDtypeStruct((n, D_ATTN), BF16),
        jax.ShapeDtypeStruct((n, N_ATTN_HEADS), F32),
        jax.ShapeDtypeStruct((n, D_RNN), F32),
        jax.ShapeDtypeStruct((n, D_RNN), BF16),
        jax.ShapeDtypeStruct((n, D_MODEL), BF16),
        jax.ShapeDtypeStruct((n, D_MODEL), BF16),
    )
    out_specs = (row(D_ATTN), kv_spec, kv_spec, row(D_ATTN), row(D_ATTN), row(N_ATTN_HEADS),
                 row(D_RNN), row(D_RNN), row(D_MODEL), row(D_MODEL))
    return pl.pallas_call(
        functools.partial(_inproj_kernel, time_minor_kv=time_minor_kv),
        grid=(n // tm,),
        in_specs=[row(D_MODEL), full(w_in_p), full(b_forget)],
        out_specs=out_specs,
        out_shape=out_shape,
        compiler_params=_params("parallel"),
        name="inproj",
    )(x2, w_in_p, b_forget)


def _cumsum_kernel(x_ref, o_ref):
    x = x_ref[0]
    t = x.shape[1]
    lane = lax.broadcasted_iota(jnp.int32, x.shape, 1)
    d = 1
    while d < t:
        x = x + jnp.where(lane >= d, pltpu.roll(x, d, axis=1), 0.0)
        d *= 2
    o_ref[0] = x * LOG2_E


def _cumsum_time(lf_t):
    b, h, t = lf_t.shape
    spec = pl.BlockSpec((1, h, t), lambda i: (i, 0, 0))
    return pl.pallas_call(
        _cumsum_kernel, grid=(b,), in_specs=[spec], out_specs=spec,
        out_shape=jax.ShapeDtypeStruct(lf_t.shape, F32),
        compiler_params=_params("parallel"), name="logf_cumsum",
    )(lf_t)


def _attn_kernel(q_ref, k_ref, v_ref, fq_ref, fk_ref, o_ref, q_sc, fq_sc, s_sc, p_sc, acc_sc, m_sc, al_sc,
                 lp_sc, *, tq, tk, q_off):
    qi = pl.program_id(2)
    row0 = q_off + qi * tq
    n_full = (row0 + 1) // tk
    heads = range(HEADS_PER_GROUP)
    rb = min(ATTN_ROW_BLOCK, tq)
    lane = lax.broadcasted_iota(jnp.int32, (tq, GROUP_LANES), 1)
    in_head = [(lane >= j * ATTN_HEAD_DIM) & (lane < (j + 1) * ATTN_HEAD_DIM) for j in heads]
    q = q_ref[0]
    for j in heads:
        q_sc[j * tq:(j + 1) * tq, :] = jnp.where(in_head[j], q, jnp.zeros_like(q))
        fq_sc[j * tq:(j + 1) * tq, :] = jnp.broadcast_to(fq_ref[0, 0, :, j:j + 1], (tq, LANES))
    m_sc[...] = jnp.full_like(m_sc, NEG_BIG)
    lp_sc[...] = jnp.zeros_like(lp_sc)
    acc_sc[...] = jnp.zeros_like(acc_sc)
    halves = [slice(0, 2 * tq), slice(2 * tq, 4 * tq)]
    n_lane_tiles = tk // LANES

    def scores(c, s_sc):
        ks = k_ref[0, pl.ds(pl.multiple_of(c * tk, tk), tk), :]
        for hv in halves:
            s_sc[hv, :] = lax.dot_general(q_sc[hv, :], ks, (((1,), (1,)), ((), ())),
                                          preferred_element_type=F32)

    def absorb(c, s_sc, masked):
        start = pl.multiple_of(c * tk, tk)
        fk = fk_ref[0, 0, c]
        vs = v_ref[0, pl.ds(start, tk), :]

        def weighted_values(hv):
            pv = jnp.dot(p_sc[hv, :], vs, preferred_element_type=F32)
            alpha = al_sc[hv, :]
            acc_sc[hv, :] = jnp.concatenate([alpha] * (GROUP_LANES // LANES), axis=1) * acc_sc[hv, :] + pv

        for j in heads:
            for r in range(tq // rb):
                rows = slice(j * tq + r * rb, j * tq + (r + 1) * rb)
                s = s_sc[rows, :] - fk[j:j + 1, :]
                if masked:
                    q_pos = row0 + r * rb + lax.broadcasted_iota(jnp.int32, (rb, tk), 0)
                    k_pos = c * tk + lax.broadcasted_iota(jnp.int32, (rb, tk), 1)
                    s = jnp.where(k_pos <= q_pos, s, NEG_BIG)
                tiles = [s[:, t * LANES:(t + 1) * LANES] for t in range(n_lane_tiles)]
                row_max = jnp.max(functools.reduce(jnp.maximum, tiles), axis=1, keepdims=True)
                fq = fq_sc[rows, :]
                m_old = m_sc[rows, :]
                m_new = jnp.maximum(m_old, jnp.broadcast_to(row_max, (rb, LANES)) + fq)
                alpha = jnp.exp2(m_old - m_new)
                shift = fq - m_new
                p_tiles = [jnp.exp2(tile + shift) for tile in tiles]
                m_sc[rows, :] = m_new
                al_sc[rows, :] = alpha
                lp_sc[rows, :] = alpha * lp_sc[rows, :] + functools.reduce(jnp.add, p_tiles)
                p_sc[rows, :] = jnp.concatenate(p_tiles, axis=1).astype(BF16)
            if j % 2 == 1:
                weighted_values(halves[j // 2])

    def step(c, carry):
        scores(c, s_sc)
        absorb(c, s_sc, masked=False)
        return carry

    lax.fori_loop(0, n_full, step, 0)
    scores(n_full, s_sc)
    absorb(n_full, s_sc, masked=True)
    o_all = acc_sc[...] / jnp.sum(lp_sc[...], axis=1, keepdims=True)
    out = jnp.zeros((tq, GROUP_LANES), F32)
    for j in heads:
        out = jnp.where(in_head[j], o_all[j * tq:(j + 1) * tq], out)
    o_ref[0] = out.astype(BF16)


def _attention(q, k_all, v_all, fq, fk, *, tq, tk, q_off):
    b, t, _ = q.shape
    t_k = k_all.shape[1]
    n_kv = t_k // tk
    assert all((q_off + i * tq) % tk + tq <= tk for i in range(t // tq)), "a query block straddles key chunks"
    kern = functools.partial(_attn_kernel, tq=tq, tk=tk, q_off=q_off)
    rows = HEADS_PER_GROUP * tq
    return pl.pallas_call(
        kern,
        grid=(b, N_HEAD_GROUPS, t // tq),
        in_specs=[
            pl.BlockSpec((1, tq, GROUP_LANES), lambda bi, g, i: (bi, i, g)),
            pl.BlockSpec((1, t_k, GROUP_LANES), lambda bi, g, i: (bi, 0, g)),
            pl.BlockSpec((1, t_k, GROUP_LANES), lambda bi, g, i: (bi, 0, g)),
            pl.BlockSpec((1, 1, tq, HEADS_PER_GROUP), lambda bi, g, i: (bi, g, i, 0)),
            pl.BlockSpec((1, 1, n_kv, HEADS_PER_GROUP, tk), lambda bi, g, i: (bi, g, 0, 0, 0)),
        ],
        out_specs=pl.BlockSpec((1, tq, GROUP_LANES), lambda bi, g, i: (bi, i, g)),
        out_shape=jax.ShapeDtypeStruct((b, t, D_ATTN), BF16),
        scratch_shapes=[
            pltpu.VMEM((rows, GROUP_LANES), BF16),
            pltpu.VMEM((rows, LANES), F32),
            pltpu.VMEM((rows, tk), F32),
            pltpu.VMEM((rows, tk), BF16),
            pltpu.VMEM((rows, GROUP_LANES), F32),
            pltpu.VMEM((rows, LANES), F32),
            pltpu.VMEM((rows, LANES), F32),
            pltpu.VMEM((rows, LANES), F32),
        ],
        compiler_params=_params("parallel", "parallel", "arbitrary"),
        name="fox_attention",
    )(q, k_all, v_all, fq, fk)


def _rnn_kernel(xr_ref, gg_ref, hist_ref, h0_ref, cw_ref, cb_ref, wa_ref, ba_ref, wx_ref, bx_ref,
                lam_ref, out_ref, hl_ref, win_sc, h_sc, *, tb):
    t = pl.program_id(1)

    @pl.when(t == 0)
    def _():
        win_sc[:SUBLANES, :] = hist_ref[0]
        h_sc[...] = h0_ref[0]

    x = xr_ref[0]
    win_sc[SUBLANES:, :] = x
    xc = x * cw_ref[CONV_WIDTH - 1:CONV_WIDTH, :] + cb_ref[...]
    for s in range(1, CONV_WIDTH):
        xc = xc + win_sc[SUBLANES - s:SUBLANES - s + tb, :] * cw_ref[CONV_WIDTH - 1 - s:CONV_WIDTH - s, :]
    win_sc[:SUBLANES, :] = x[tb - SUBLANES:tb]

    xcb = xc.astype(BF16)
    r = _sigmoid(jnp.dot(xcb, wa_ref[...], preferred_element_type=F32) + ba_ref[...])
    ig = _sigmoid(jnp.dot(xcb, wx_ref[...], preferred_element_type=F32) + bx_ref[...])
    log_a = (-LRU_C) * r * _softplus(-lam_ref[...])
    a = jnp.exp(log_a)
    one_minus_a2 = 1.0 - a * a
    scale = jnp.where(one_minus_a2 > 0.0, one_minus_a2 * lax.rsqrt(one_minus_a2), 0.0)
    bterm = scale * ig * xc

    grouped = (tb // SUBLANES, SUBLANES, D_RNN)
    a, bterm = a.reshape(grouped), bterm.reshape(grouped)
    row_in_group = lax.broadcasted_iota(jnp.int32, grouped, 1)
    d = 1
    while d < SUBLANES:
        valid = row_in_group >= d
        a_s = pltpu.roll(a, d, axis=1)
        b_s = pltpu.roll(bterm, d, axis=1)
        bterm = jnp.where(valid, a * b_s + bterm, bterm)
        a = jnp.where(valid, a * a_s, a)
        d *= 2
    h_last = h_sc[...]
    groups = []
    for g in range(tb // SUBLANES):
        h_g = bterm[g] + a[g] * h_last
        h_last = h_g[SUBLANES - 1:SUBLANES]
        groups.append(h_g)
    h = jnp.concatenate(groups, axis=0)
    h_sc[...] = h_last
    hl_ref[0] = h_last
    out_ref[0] = (h * gg_ref[0].astype(F32)).astype(BF16)


def _rnn(xr, gg, hist8, h0, conv_w, conv_b, wa, ba, wx, bx, lam, tb):
    b, t, _ = xr.shape
    blk = pl.BlockSpec((1, tb, D_RNN), lambda bi, ti: (bi, ti, 0))
    full = lambda a: pl.BlockSpec(a.shape, lambda bi, ti: (0,) * a.ndim)
    per_b = lambda r: pl.BlockSpec((1, r, D_RNN), lambda bi, ti: (bi, 0, 0))
    return pl.pallas_call(
        functools.partial(_rnn_kernel, tb=tb),
        grid=(b, t // tb),
        in_specs=[blk, blk, per_b(SUBLANES), per_b(1), full(conv_w), full(conv_b), full(wa), full(ba),
                  full(wx), full(bx), full(lam)],
        out_specs=(blk, per_b(1)),
        out_shape=(jax.ShapeDtypeStruct((b, t, D_RNN), BF16), jax.ShapeDtypeStruct((b, 1, D_RNN), F32)),
        scratch_shapes=[pltpu.VMEM((SUBLANES + tb, D_RNN), F32), pltpu.VMEM((1, D_RNN), F32)],
        compiler_params=_params("parallel", "arbitrary"),
        name="conv_rglru",
    )(xr, gg, hist8, h0, conv_w, conv_b, wa, ba, wx, bx, lam)


def _outproj_kernel(o_ref, r_ref, sa_ref, sr_ref, x_ref, wau_ref, wru_ref, wo_ref, g_ref, b_ref,
                    h_ref, *, alpha):
    up_a = jnp.dot(o_ref[...], wau_ref[...], preferred_element_type=F32)
    up_r = jnp.dot(r_ref[...], wru_ref[...], preferred_element_type=F32)
    merged = sa_ref[...].astype(F32) * up_a + sr_ref[...].astype(F32) * up_r
    mix = jnp.dot(merged.astype(BF16), wo_ref[...], preferred_element_type=F32)
    h_ref[...] = _layer_norm(alpha * x_ref[...] + mix, g_ref[...], b_ref[...])


def _outproj(o2, r2, sa, sr, x2, wau, wru, wo, g, bta, alpha, tm):
    n = x2.shape[0]
    row = lambda w: pl.BlockSpec((tm, w), lambda i: (i, 0))
    full = lambda a: pl.BlockSpec(a.shape, lambda i: (0,) * a.ndim)
    return pl.pallas_call(
        functools.partial(_outproj_kernel, alpha=alpha),
        grid=(n // tm,),
        in_specs=[row(D_ATTN), row(D_RNN), row(D_MODEL), row(D_MODEL), row(D_MODEL), full(wau), full(wru),
                  full(wo), full(g), full(bta)],
        out_specs=row(D_MODEL),
        out_shape=jax.ShapeDtypeStruct((n, D_MODEL), F32),
        compiler_params=_params("parallel"),
        name="outproj_ln1",
    )(o2, r2, sa, sr, x2, wau, wru, wo, g, bta)


ROUTE_TOKENS = 2 * LANES


def _odd_even_merge_sort(n):
    def merge(lo, hi, r):
        step = r * 2
        if step < hi - lo:
            yield from merge(lo, hi, step)
            yield from merge(lo + r, hi, step)
            yield from ((i, i + r) for i in range(lo + r, hi - r, step))
        else:
            yield (lo, lo + r)

    def sort(lo, hi):
        if hi - lo >= 1:
            mid = lo + (hi - lo) // 2
            yield from sort(lo, mid)
            yield from sort(mid + 1, hi)
            yield from merge(lo, hi, 1)

    return tuple(sort(0, n - 1))


_SORT_TOPK = _odd_even_merge_sort(PEER_TOPK)


def _compare_exchange(v, i, j):
    v[i], v[j] = jnp.maximum(v[i], v[j]), jnp.minimum(v[i], v[j])


def _top_k_sorted(tiles):
    v = list(tiles)
    for i, j in _SORT_TOPK:
        _compare_exchange(v, i, j)
    shift = SUBLANES // 2
    while shift >= 1:
        other = [pltpu.roll(x, shift, axis=0) for x in v]
        v = [jnp.maximum(v[i], other[PEER_TOPK - 1 - i]) for i in range(PEER_TOPK)]
        d = PEER_TOPK // 2
        while d >= 1:
            for i in range(PEER_TOPK):
                if i & d == 0:
                    _compare_exchange(v, i, i + d)
            d //= 2
        shift //= 2
    return v


def _count_greater(x, t):
    assert len(t) == 16, "the bisection below is written out for 16 entries"
    one = lambda m, w: jnp.where(m, float(w), 0.0)
    b3 = t[7] > x
    b2 = jnp.where(b3, t[11], t[3]) > x
    b1 = jnp.where(b3, jnp.where(b2, t[13], t[9]), jnp.where(b2, t[5], t[1])) > x
    hi = jnp.where(b2, jnp.where(b1, t[14], t[12]), jnp.where(b1, t[10], t[8]))
    lo = jnp.where(b2, jnp.where(b1, t[6], t[4]), jnp.where(b1, t[2], t[0]))
    b0 = jnp.where(b3, hi, lo) > x
    count = one(b3, 8) + one(b2, 4) + one(b1, 2) + one(b0, 1)
    return jnp.where(t[15] > x, float(PEER_TOPK), count)


def _route_kernel(h_ref, wq_ref, k1_ref, k2_ref, ht_ref, cnt_ref, e1_ref, r2_ref, e2_ref,
                  qt_sc, t1_sc, t2_sc, *, tm):
    ht = jnp.transpose(h_ref[...]).astype(BF16)
    ht_ref[...] = ht
    qt_sc[...] = jnp.dot(wq_ref[...], ht, preferred_element_type=F32).astype(BF16)
    key_iota = lax.broadcasted_iota(jnp.int32, (N_KEYS, tm), 0).astype(F32)
    top_iota = lax.broadcasted_iota(jnp.int32, (PEER_TOPK, tm), 0).astype(F32)
    front_rows = SUBLANES

    def scores(hd):
        base = hd * 2 * PEER_HALF
        s1 = jnp.dot(k1_ref[...], qt_sc[base:base + PEER_HALF], preferred_element_type=F32)
        s2 = jnp.dot(k2_ref[...], qt_sc[base + PEER_HALF:base + 2 * PEER_HALF], preferred_element_type=F32)
        return s1, s2

    sub = lax.broadcasted_iota(jnp.int32, (SUBLANES, tm), 0)
    tiles_of = lambda s: [s[r * SUBLANES:(r + 1) * SUBLANES] for r in range(N_KEYS // SUBLANES)]
    sublane_sum = lambda x: jnp.sum(x, axis=0, keepdims=True)

    def route_head_sorted(hd):
        s1, s2 = scores(hd)
        rows1, rows2 = tiles_of(s1), tiles_of(s2)
        t1 = _top_k_sorted(rows1)
        t2 = _top_k_sorted(rows2)
        t1_lo = t1[SUBLANES - 1]
        for a in range(SUBLANES - 2, -1, -1):
            t1_lo = jnp.where(sub == a, t1[a], t1_lo)
        cand = [t1_lo + t2[b] for b in range(PEER_TOPK)]
        pool = list(cand)
        for a in range(SUBLANES, PEER_TOPK):
            pool[a] = jnp.where(sub == 1, t1[a] + t2[0], cand[a])
        top = _top_k_sorted(pool)
        tau = top[PEER_TOPK - 1]
        cnt_lo = functools.reduce(jnp.add, [jnp.where(c >= tau, 1.0, 0.0) for c in cand])
        cnt = [jnp.broadcast_to(cnt_lo[a:a + 1], (SUBLANES, tm)) for a in range(SUBLANES)]
        cnt += [jnp.where(t1[a] + t2[0] >= tau, 1.0, 0.0) for a in range(SUBLANES, PEER_TOPK)]
        z = functools.reduce(jnp.add, [jnp.exp(t - top[0]) for t in top])

        cnt1_rows, rank2_rows = [], []
        for r in range(N_KEYS // SUBLANES):
            c = jnp.zeros((SUBLANES, tm), F32)
            for a in range(PEER_TOPK):
                c = jnp.where(rows1[r] == t1[a], cnt[a], c)
            cnt1_rows.append(c)
            g = _count_greater(rows2[r], t2)
            rank2_rows.append(jnp.where(g < float(PEER_TOPK), g, NOT_SELECTED_RANK))
        cnt1 = jnp.concatenate(cnt1_rows, axis=0)
        rank2 = jnp.concatenate(rank2_rows, axis=0)
        cnt_ref[hd] = cnt1
        e1_ref[hd] = jnp.exp(s1 - t1[0][0:1]) / z[0:1]
        r2_ref[hd] = rank2.astype(BF16)
        e2_ref[hd] = jnp.exp(s2 - t2[0][0:1]).astype(BF16)

        gap = lambda t: functools.reduce(jnp.minimum, [t[b] - t[b + 1] for b in range(PEER_TOPK - 1)])[0:1]
        n_cnt = sublane_sum(cnt_lo) + functools.reduce(jnp.add, cnt[SUBLANES:])[0:1]
        n_sel = sublane_sum(functools.reduce(jnp.add, cnt1_rows))
        n_rank = sublane_sum(functools.reduce(
            jnp.add, [jnp.where(x < float(PEER_TOPK), 1.0, 0.0) for x in rank2_rows]))
        off = lambda n: jnp.abs(n - float(PEER_TOPK))
        return (off(n_cnt) + off(n_sel) + off(n_rank)
                + jnp.where(jnp.minimum(gap(t1), gap(t2)) > 0.0, 0.0, 1.0))

    def route_head_exact(hd):
        def pick_one(v, iota, n):
            m = jnp.max(v, axis=0, keepdims=True)
            return m, iota == jnp.min(jnp.where(v == m, iota, float(n)), axis=0, keepdims=True)

        s1, s2 = scores(hd)

        def extract(a, carry):
            v1, r1, v2, r2 = carry
            m1, sel1 = pick_one(v1, key_iota, N_KEYS)
            m2, sel2 = pick_one(v2, key_iota, N_KEYS)
            t1_sc[pl.ds(a, 1), :] = m1
            t2_sc[pl.ds(a, 1), :] = m2
            af = jnp.asarray(a, dtype=F32)
            return (jnp.where(sel1, -jnp.inf, v1), jnp.where(sel1, af, r1),
                    jnp.where(sel2, -jnp.inf, v2), jnp.where(sel2, af, r2))

        no_rank = jnp.full((N_KEYS, tm), NOT_SELECTED_RANK, F32)
        _, rank1, _, rank2 = lax.fori_loop(0, PEER_TOPK, extract, (s1, no_rank, s2, no_rank))
        t1 = t1_sc[...]
        t2 = t2_sc[...]
        top0 = t1[0:1] + t2[0:1]

        def pick(_, carry):
            ptr, front, z = carry
            m, sel = pick_one(front, top_iota, PEER_TOPK)
            ptr = ptr + jnp.where(sel, 1.0, 0.0)
            lo = ptr[:front_rows]
            nxt = jnp.full(lo.shape, -jnp.inf, F32)
            for b in range(1, PEER_TOPK):
                nxt = jnp.where(lo == float(b), t2_sc[b:b + 1, :], nxt)
            nxt = jnp.where(lo == 0.0, t2[0:1], nxt)
            front = jnp.concatenate(
                [t1[:front_rows] + nxt, jnp.where(sel[front_rows:], -jnp.inf, front[front_rows:])], axis=0)
            return ptr, front, z + jnp.exp(m - top0)

        cnt, _, z = lax.fori_loop(
            0, PEER_TOPK, pick,
            (jnp.zeros((PEER_TOPK, tm), F32), t1 + t2[0:1], jnp.zeros((1, tm), F32)))

        cnt1 = jnp.zeros((N_KEYS, tm), F32)
        for a in range(PEER_TOPK):
            cnt1 = jnp.where(rank1 == float(a), cnt[a:a + 1], cnt1)
        cnt_ref[hd] = cnt1
        e1_ref[hd] = jnp.exp(s1 - t1[0:1]) / z
        r2_ref[hd] = rank2.astype(BF16)
        e2_ref[hd] = jnp.exp(s2 - t2[0:1]).astype(BF16)

    doubt = [route_head_sorted(hd) for hd in range(PEER_HEADS)]

    @pl.when(jnp.max(functools.reduce(jnp.maximum, doubt)) > 0.0)
    def _():
        for hd in range(PEER_HEADS):
            @pl.when(jnp.max(doubt[hd]) > 0.0)
            def _():
                route_head_exact(hd)


def _route(h2, wq_t, k1, k2, tm):
    n = h2.shape[0]
    full = lambda a: pl.BlockSpec(a.shape, lambda i: (0,) * a.ndim)
    per_head = pl.BlockSpec((PEER_HEADS, N_KEYS, tm), lambda i: (0, 0, i))
    hshape = lambda dt: jax.ShapeDtypeStruct((PEER_HEADS, N_KEYS, n), dt)
    return pl.pallas_call(
        functools.partial(_route_kernel, tm=tm),
        grid=(n // tm,),
        in_specs=[pl.BlockSpec((tm, D_MODEL), lambda i: (i, 0)), full(wq_t), full(k1), full(k2)],
        out_specs=(pl.BlockSpec((D_MODEL, tm), lambda i: (0, i)), per_head, per_head, per_head, per_head),
        out_shape=(jax.ShapeDtypeStruct((D_MODEL, n), BF16), hshape(F32), hshape(F32), hshape(BF16),
                   hshape(BF16)),
        scratch_shapes=[pltpu.VMEM((PEER_HEADS * 2 * PEER_HALF, tm), BF16),
                        pltpu.VMEM((PEER_TOPK, tm), F32), pltpu.VMEM((PEER_TOPK, tm), F32)],
        compiler_params=_params("parallel"),
        name="peer_route",
    )(h2, wq_t, k1, k2)


ROWS_PER_STEP = 16
EXPERTS_PER_STEP = ROWS_PER_STEP * N_KEYS
ROWS_PER_SUB = 4
EXPERTS_PER_SUB = ROWS_PER_SUB * N_KEYS
SUBS_PER_MIX = 2
DENSE_CHUNK = 2 * LANES


def _bf16_row_tile(row):
    tile = jnp.broadcast_to(row, (BF16_ROWS, row.shape[1])).astype(BF16)
    return jnp.concatenate([tile] * (N_KEYS // BF16_ROWS), axis=0)


def _dense_kernel(ht_ref, u_ref, vt_ref, cnt_ref, e1_ref, r2_ref, e2_ref, h_ref, g_ref, b_ref,
                  y_ref, acc_sc, act_sc, coef_sc, *, alpha):
    e = pl.program_id(1)
    tm = act_sc.shape[1]
    n_sub = ROWS_PER_STEP // ROWS_PER_SUB

    @pl.when(e == 0)
    def _():
        acc_sc[...] = jnp.zeros_like(acc_sc)

    def activations(sub):
        rows = slice(sub * EXPERTS_PER_SUB, (sub + 1) * EXPERTS_PER_SUB)
        act_sc[rows, :] = jnp.dot(u_ref[rows, :], ht_ref[...], preferred_element_type=F32)

    def tiles(sub):
        chunk = min(DENSE_CHUNK, tm)
        for il in range(sub * ROWS_PER_SUB, (sub + 1) * ROWS_PER_SUB):
            for c in range(tm // chunk):
                yield il, slice(il * N_KEYS, (il + 1) * N_KEYS), slice(c * chunk, (c + 1) * chunk)

    def routing_weights(sub):
        for il, rows, lanes in tiles(sub):
            w = None
            for hd in range(PEER_HEADS):
                cnt = _bf16_row_tile(cnt_ref[hd, il:il + 1, lanes])
                e1 = _bf16_row_tile(e1_ref[hd, il:il + 1, lanes])
                term = jnp.where(r2_ref[hd, :, lanes] < cnt, e2_ref[hd, :, lanes] * e1, jnp.zeros((), BF16))
                w = term if w is None else w + term
            coef_sc[rows, lanes] = w

    def coefficients(sub):
        for _, rows, lanes in tiles(sub):
            coef_sc[rows, lanes] = coef_sc[rows, lanes] * _gelu_tanh(act_sc[rows, lanes].astype(BF16))

    def mix(first, last):
        rows = slice(first * EXPERTS_PER_SUB, (last + 1) * EXPERTS_PER_SUB)
        acc_sc[...] += jnp.dot(vt_ref[:, rows], coef_sc[rows, :], preferred_element_type=F32)

    routing_weights(0)
    activations(0)
    for sub in range(n_sub):
        if sub + 1 < n_sub:
            routing_weights(sub + 1)
            activations(sub + 1)
        coefficients(sub)
        if sub % SUBS_PER_MIX == SUBS_PER_MIX - 1:
            mix(sub - SUBS_PER_MIX + 1, sub)

    @pl.when(e == pl.num_programs(1) - 1)
    def _():
        peer = jnp.transpose(acc_sc[...])
        y_ref[...] = _layer_norm(alpha * h_ref[...] + peer, g_ref[...], b_ref[...])


def _dense(ht, u_b, vt_b, cnt1, e1n, rank2, e2, h2, g, bta, alpha, tm):
    n = h2.shape[0]
    n_e = N_EXPERTS // EXPERTS_PER_STEP
    full = lambda a: pl.BlockSpec(a.shape, lambda t, e: (0,) * a.ndim)
    rows_blk = pl.BlockSpec((PEER_HEADS, ROWS_PER_STEP, tm), lambda t, e: (0, e, t))
    cols_blk = pl.BlockSpec((PEER_HEADS, N_KEYS, tm), lambda t, e: (0, 0, t))
    return pl.pallas_call(
        functools.partial(_dense_kernel, alpha=alpha),
        grid=(n // tm, n_e),
        in_specs=[
            pl.BlockSpec((D_MODEL, tm), lambda t, e: (0, t)),
            pl.BlockSpec((EXPERTS_PER_STEP, D_MODEL), lambda t, e: (e, 0)),
            pl.BlockSpec((D_MODEL, EXPERTS_PER_STEP), lambda t, e: (0, e)),
            rows_blk, rows_blk, cols_blk, cols_blk,
            pl.BlockSpec((tm, D_MODEL), lambda t, e: (t, 0)),
            full(g), full(bta),
        ],
        out_specs=pl.BlockSpec((tm, D_MODEL), lambda t, e: (t, 0)),
        out_shape=jax.ShapeDtypeStruct((n, D_MODEL), F32),
        scratch_shapes=[pltpu.VMEM((D_MODEL, tm), F32), pltpu.VMEM((EXPERTS_PER_STEP, tm), F32),
                        pltpu.VMEM((EXPERTS_PER_STEP, tm), BF16)],
        compiler_params=_params("parallel", "arbitrary"),
        name="peer_dense",
    )(ht, u_b, vt_b, cnt1, e1n, rank2, e2, h2, g, bta)


def _block_diag(w):
    nb, bi, bo = w.shape
    eye = jnp.eye(nb, dtype=w.dtype)
    return (eye[:, None, :, None] * w[:, :, None, :]).reshape(nb * bi, nb * bo)


def _prep_weights(w_in, b_forget, conv_w, conv_b, w_rg_a, b_rg_a, w_rg_x, b_rg_x, lru_lambda,
                  w_attn_up, w_rnn_up, w_out, ln1_g, ln1_b, peer_w_query, peer_keys_1, peer_keys_2,
                  peer_u, peer_v, ln2_g, ln2_b):
    c_f = 3 * D_ATTN
    w_in_p = jnp.concatenate(
        [w_in[:, :c_f], jnp.pad(w_in[:, c_f:c_f + N_ATTN_HEADS], ((0, 0), (0, F_PAD - N_ATTN_HEADS))),
         w_in[:, c_f + N_ATTN_HEADS:]], axis=1).astype(BF16)
    row = lambda a: a.reshape(1, -1).astype(F32)
    return dict(
        w_in=w_in_p, b_forget=row(b_forget), conv_w=conv_w.astype(F32), conv_b=row(conv_b),
        wa=_block_diag(w_rg_a).astype(BF16), ba=row(b_rg_a), wx=_block_diag(w_rg_x).astype(BF16),
        bx=row(b_rg_x), lam=row(lru_lambda),
        wau=w_attn_up.astype(BF16), wru=w_rnn_up.astype(BF16), wo=w_out.astype(BF16),
        ln1_g=row(ln1_g), ln1_b=row(ln1_b),
        wq_t=jnp.transpose(peer_w_query).astype(BF16), k1=peer_keys_1.astype(BF16),
        k2=peer_keys_2.astype(BF16), u=peer_u.astype(BF16), vt=jnp.transpose(peer_v).astype(BF16),
        ln2_g=row(ln2_g), ln2_b=row(ln2_b),
    )


def _pick_block(n, target):
    blk = min(n, target)
    assert n % blk == 0, (n, blk)
    return blk


def _trunk_layer(x, past_k, past_v, past_logf, conv_hist, h0, p, alpha):
    bsz, t, _ = x.shape
    n = bsz * t
    x2 = x.reshape(n, D_MODEL)
    q, k, v, kb, vb, lf, xr, gg, sa, sr = _inproj(x2, p["w_in"], p["b_forget"], bsz, t, _pick_block(n, 256))

    def state_layout(a):
        if a.ndim == 3:
            return jnp.transpose(a.reshape(bsz, N_ATTN_HEADS, ATTN_HEAD_DIM, t), (0, 3, 1, 2))
        return a.reshape(bsz, t, N_ATTN_HEADS, ATTN_HEAD_DIM)

    tq = _pick_block(t, ATTN_Q_BLOCK)
    n_past = 0 if past_k is None else past_k.shape[1]
    t_all = n_past + t
    t_lanes = -(-t_all // LANES) * LANES
    tk = ATTN_K_BLOCK if t_lanes % ATTN_K_BLOCK == 0 else t_lanes
    t_pad = -(-t_all // tk) * tk
    lf3 = lf.reshape(bsz, t, N_ATTN_HEADS)
    kb3 = kb.reshape(bsz, t, D_ATTN)
    vb3 = vb.reshape(bsz, t, D_ATTN)
    if past_k is not None:
        lf3 = jnp.concatenate([past_logf.astype(F32), lf3], axis=1)
        kb3 = jnp.concatenate([past_k.reshape(bsz, n_past, D_ATTN).astype(BF16), kb3], axis=1)
        vb3 = jnp.concatenate([past_v.reshape(bsz, n_past, D_ATTN).astype(BF16), vb3], axis=1)
    pad = ((0, 0), (0, t_pad - t_all), (0, 0))
    lf3, kb3, vb3 = jnp.pad(lf3, pad), jnp.pad(kb3, pad), jnp.pad(vb3, pad)
    f_t = _cumsum_time(jnp.transpose(lf3, (0, 2, 1)))
    f_g = f_t.reshape(bsz, N_HEAD_GROUPS, HEADS_PER_GROUP, t_pad)
    fq = jnp.transpose(f_g[:, :, :, n_past:n_past + t], (0, 1, 3, 2))
    fk = jnp.transpose(f_g.reshape(bsz, N_HEAD_GROUPS, HEADS_PER_GROUP, t_pad // tk, tk), (0, 1, 3, 2, 4))
    o = _attention(q.reshape(bsz, t, D_ATTN), kb3, vb3, fq, fk, tq=tq, tk=tk, q_off=n_past)

    hist8 = jnp.pad(conv_hist.astype(F32), ((0, 0), (SUBLANES - (CONV_WIDTH - 1), 0), (0, 0)))
    xr3 = xr.reshape(bsz, t, D_RNN)
    rnn_out, h_last = _rnn(xr3, gg.reshape(bsz, t, D_RNN), hist8, h0.astype(F32).reshape(bsz, 1, D_RNN),
                           p["conv_w"], p["conv_b"], p["wa"], p["ba"], p["wx"], p["bx"], p["lam"],
                           _pick_block(t, 256))
    new_hist = jnp.concatenate([conv_hist.astype(F32), xr3], axis=1)[:, -(CONV_WIDTH - 1):]

    h = _outproj(o.reshape(n, D_ATTN), rnn_out.reshape(n, D_RNN), sa, sr, x2, p["wau"], p["wru"], p["wo"],
                 p["ln1_g"], p["ln1_b"], alpha, _pick_block(n, 256))
    ht, cnt1, e1n, rank2, e2 = _route(h, p["wq_t"], p["k1"], p["k2"], _pick_block(n, ROUTE_TOKENS))
    y = _dense(ht, p["u"], p["vt"], cnt1, e1n, rank2, e2, h, p["ln2_g"], p["ln2_b"], alpha,
               _pick_block(n, 512))
    return (y.reshape(bsz, t, D_MODEL), state_layout(k), state_layout(v), lf.reshape(bsz, t, N_ATTN_HEADS), new_hist,
            h_last.reshape(bsz, D_RNN))


def kernel(x_prompt, x_sample, cache_k, cache_v, cache_logf, state_conv, state_rnn, w_in, b_forget, conv_w, conv_b, w_rg_a, b_rg_a, w_rg_x, b_rg_x, lru_lambda, w_attn_up, w_rnn_up, w_out, ln1_g, ln1_b, peer_w_query, peer_keys_1, peer_keys_2, peer_u, peer_v, ln2_g, ln2_b):
    depth = w_in.shape[0]
    alpha = (2 * depth) ** 0.25
    layer_weights = (w_in, b_forget, conv_w, conv_b, w_rg_a, b_rg_a, w_rg_x, b_rg_x, lru_lambda, w_attn_up,
                     w_rnn_up, w_out, ln1_g, ln1_b, peer_w_query, peer_keys_1, peer_keys_2, peer_u, peer_v,
                     ln2_g, ln2_b)
    hp, hs = x_prompt, x_sample
    prompt_state, sample_state = [], []
    for l in range(depth):
        p = _prep_weights(*(w[l] for w in layer_weights))
        zero_hist = jnp.zeros((hp.shape[0], CONV_WIDTH - 1, D_RNN), F32)
        zero_h = jnp.zeros((hp.shape[0], D_RNN), F32)
        hp, *st_p = _trunk_layer(hp, None, None, None, zero_hist, zero_h, p, alpha)
        hs, *st_s = _trunk_layer(hs, cache_k[l], cache_v[l], cache_logf[l], state_conv[l], state_rnn[l], p,
                                 alpha)
        prompt_state.append(st_p)
        sample_state.append(st_s)
    stack = lambda states, i: jnp.stack([s[i] for s in states])
    return (hp, hs) + tuple(stack(prompt_state, i) for i in range(5)) + tuple(
        stack(sample_state, i) for i in range(5))
```

```python
import functools
import math

import jax
import jax.numpy as jnp
from jax import lax
from jax.experimental import pallas as pl
from jax.experimental.pallas import tpu as pltpu

F32 = jnp.float32
BF16 = jnp.bfloat16

D_MODEL = 1024
N_ATTN_HEADS = 8
ATTN_HEAD_DIM = 64
D_ATTN = N_ATTN_HEADS * ATTN_HEAD_DIM
ATTN_SCALE = ATTN_HEAD_DIM ** -0.5
D_RNN = 512
CONV_WIDTH = 4
LRU_C = 8.0
N_KEYS = 128
N_EXPERTS = N_KEYS * N_KEYS
PEER_HEADS = 8
PEER_TOPK = 16
PEER_HALF = 128
LN_EPS = 1e-5

LANES = 128
SUBLANES = 8
BF16_ROWS = 2 * SUBLANES
VMEM_LIMIT_BYTES = 56 * 1024 * 1024

HEADS_PER_GROUP = 4
GROUP_LANES = HEADS_PER_GROUP * ATTN_HEAD_DIM
N_HEAD_GROUPS = N_ATTN_HEADS // HEADS_PER_GROUP
ATTN_Q_BLOCK = 512
ATTN_K_BLOCK = 512
ATTN_ROW_BLOCK = 32
F_PAD = LANES
NEG_BIG = -1e30
LOG2_E = math.log2(math.e)
NOT_SELECTED_RANK = 99.0

_C_Q = 0
_C_K = _C_Q + D_ATTN
_C_V = _C_K + D_ATTN
_C_F = _C_V + D_ATTN
_C_XR = _C_F + F_PAD
_C_GATE = _C_XR + D_RNN
_C_GA = _C_GATE + D_RNN
_C_GR = _C_GA + D_MODEL
_C_END = _C_GR + D_MODEL


def _params(*sem):
    return pltpu.CompilerParams(dimension_semantics=sem, vmem_limit_bytes=VMEM_LIMIT_BYTES)


def _sigmoid(x):
    return 1.0 / (1.0 + jnp.exp(-x))


def _gelu_tanh(x):
    half = 0.5 * x
    return half + half * jnp.tanh(x * (0.7978845608028654 + 0.035677408136300125 * (x * x)))


def _softplus(x):
    return jnp.maximum(x, 0.0) + jnp.log1p(jnp.exp(-jnp.abs(x)))


def _layer_norm(x, g, b):
    mu = jnp.mean(x, axis=-1, keepdims=True)
    xc = x - mu
    var = jnp.mean(xc * xc, axis=-1, keepdims=True)
    return xc * lax.rsqrt(var + LN_EPS) * g + b


def _inproj_kernel(x_ref, w_ref, bf_ref, q_ref, k_ref, v_ref, kb_ref, vb_ref, lf_ref, xr_ref,
                   gg_ref, sa_ref, sr_ref, *, time_minor_kv):
    xb = x_ref[...].astype(BF16)

    def mm(lo, hi):
        return jnp.dot(xb, w_ref[:, lo:hi], preferred_element_type=F32)

    q_ref[...] = (mm(_C_Q, _C_K) * (ATTN_SCALE * LOG2_E)).astype(BF16)
    k = mm(_C_K, _C_V)
    kb_ref[...] = k.astype(BF16)
    v = mm(_C_V, _C_F)
    vb_ref[...] = v.astype(BF16)
    if time_minor_kv:
        k_ref[0] = jnp.transpose(k)
        v_ref[0] = jnp.transpose(v)
    else:
        k_ref[...] = k
        v_ref[...] = v
    f = jnp.transpose(mm(_C_F, _C_XR))[:N_ATTN_HEADS] + bf_ref[...]
    lf_ref[...] = -_softplus(-f)
    xr_ref[...] = mm(_C_XR, _C_GATE)
    gg_ref[...] = _gelu_tanh(mm(_C_GATE, _C_GA)).astype(BF16)
    sa_ref[...] = _sigmoid(mm(_C_GA, _C_GR)).astype(BF16)
    sr_ref[...] = _sigmoid(mm(_C_GR, _C_END)).astype(BF16)


def _inproj(x2, w_in_p, b_forget, bsz, t, tm):
    n = x2.shape[0]
    row = lambda w: pl.BlockSpec((tm, w), lambda i: (i, 0))
    full = lambda a: pl.BlockSpec(a.shape, lambda i: (0,) * a.ndim)
    time_minor_kv = tm % LANES == 0 and t % tm == 0
    if time_minor_kv:
        kv_shape = jax.ShapeDtypeStruct((bsz, D_ATTN, t), F32)
        kv_spec = pl.BlockSpec((1, D_ATTN, tm), lambda i: (i // (t // tm), 0, i % (t // tm)))
    else:
        kv_shape = jax.ShapeDtypeStruct((n, D_ATTN), F32)
        kv_spec = row(D_ATTN)
    out_shape = (
        jax.ShapeDtypeStruct((n, D_ATTN), BF16),
        kv_shape,
        kv_shape,
        jax.ShapeDtypeStruct((n, D_ATTN), BF16),
        jax.ShapeDtypeStruct((n, D_ATTN), BF16),
        jax.ShapeDtypeStruct((N_ATTN_HEADS, n), F32),
        jax.ShapeDtypeStruct((n, D_RNN), F32),
        jax.ShapeDtypeStruct((n, D_RNN), BF16),
        jax.ShapeDtypeStruct((n, D_MODEL), BF16),
        jax.ShapeDtypeStruct((n, D_MODEL), BF16),
    )
    out_specs = (row(D_ATTN), kv_spec, kv_spec, row(D_ATTN), row(D_ATTN),
                 pl.BlockSpec((N_ATTN_HEADS, tm), lambda i: (0, i)),
                 row(D_RNN), row(D_RNN), row(D_MODEL), row(D_MODEL))
    return pl.pallas_call(
        functools.partial(_inproj_kernel, time_minor_kv=time_minor_kv),
        grid=(n // tm,),
        in_specs=[row(D_MODEL), full(w_in_p), full(b_forget)],
        out_specs=out_specs,
        out_shape=out_shape,
        compiler_params=_params("parallel"),
        name="inproj",
    )(x2, w_in_p, b_forget)


def _cumsum_kernel(x_ref, o_ref):
    x = x_ref[0]
    t = x.shape[1]
    lane = lax.broadcasted_iota(jnp.int32, x.shape, 1)
    d = 1
    while d < t:
        x = x + jnp.where(lane >= d, pltpu.roll(x, d, axis=1), 0.0)
        d *= 2
    o_ref[0] = x * LOG2_E


def _cumsum_time(lf_t):
    b, h, t = lf_t.shape
    spec = pl.BlockSpec((1, h, t), lambda i: (i, 0, 0))
    return pl.pallas_call(
        _cumsum_kernel, grid=(b,), in_specs=[spec], out_specs=spec,
        out_shape=jax.ShapeDtypeStruct(lf_t.shape, F32),
        compiler_params=_params("parallel"), name="logf_cumsum",
    )(lf_t)


def _attn_kernel(q_ref, k_ref, v_ref, fq_ref, fk_ref, o_ref, q_sc, fq_sc, s_sc, p_sc, acc_sc, m_sc, al_sc,
                 lp_sc, *, tq, tk, q_off):
    qi = pl.program_id(2)
    row0 = q_off + qi * tq
    n_full = (row0 + 1) // tk
    heads = range(HEADS_PER_GROUP)
    rb = min(ATTN_ROW_BLOCK, tq)
    lane = lax.broadcasted_iota(jnp.int32, (tq, GROUP_LANES), 1)
    in_head = [(lane >= j * ATTN_HEAD_DIM) & (lane < (j + 1) * ATTN_HEAD_DIM) for j in heads]
    q = q_ref[0]
    for j in heads:
        q_sc[j * tq:(j + 1) * tq, :] = jnp.where(in_head[j], q, jnp.zeros_like(q))
        fq_sc[j * tq:(j + 1) * tq, :] = jnp.broadcast_to(fq_ref[0, 0, :, j:j + 1], (tq, LANES))
    m_sc[...] = jnp.full_like(m_sc, NEG_BIG)
    lp_sc[...] = jnp.zeros_like(lp_sc)
    acc_sc[...] = jnp.zeros_like(acc_sc)
    halves = [slice(0, 2 * tq), slice(2 * tq, 4 * tq)]
    n_lane_tiles = tk // LANES

    def scores(c, s_sc):
        ks = k_ref[0, pl.ds(pl.multiple_of(c * tk, tk), tk), :]
        for hv in halves:
            s_sc[hv, :] = lax.dot_general(q_sc[hv, :], ks, (((1,), (1,)), ((), ())),
                                          preferred_element_type=F32)

    def absorb(c, s_sc, masked):
        start = pl.multiple_of(c * tk, tk)
        fk = fk_ref[0, 0, c]
        vs = v_ref[0, pl.ds(start, tk), :]

        def weighted_values(hv):
            pv = jnp.dot(p_sc[hv, :], vs, preferred_element_type=F32)
            alpha = al_sc[hv, :]
            acc_sc[hv, :] = jnp.concatenate([alpha] * (GROUP_LANES // LANES), axis=1) * acc_sc[hv, :] + pv

        for j in heads:
            for r in range(tq // rb):
                rows = slice(j * tq + r * rb, j * tq + (r + 1) * rb)
                s = s_sc[rows, :] - fk[j:j + 1, :]
                if masked:
                    q_pos = row0 + r * rb + lax.broadcasted_iota(jnp.int32, (rb, tk), 0)
                    k_pos = c * tk + lax.broadcasted_iota(jnp.int32, (rb, tk), 1)
                    s = jnp.where(k_pos <= q_pos, s, NEG_BIG)
                tiles = [s[:, t * LANES:(t + 1) * LANES] for t in range(n_lane_tiles)]
                row_max = jnp.max(functools.reduce(jnp.maximum, tiles), axis=1, keepdims=True)
                fq = fq_sc[rows, :]
                m_old = m_sc[rows, :]
                m_new = jnp.maximum(m_old, jnp.broadcast_to(row_max, (rb, LANES)) + fq)
                alpha = jnp.exp2(m_old - m_new)
                shift = fq - m_new
                p_tiles = [jnp.exp2(tile + shift) for tile in tiles]
                m_sc[rows, :] = m_new
                al_sc[rows, :] = alpha
                lp_sc[rows, :] = alpha * lp_sc[rows, :] + functools.reduce(jnp.add, p_tiles)
                p_sc[rows, :] = jnp.concatenate(p_tiles, axis=1).astype(BF16)
            if j % 2 == 1:
                weighted_values(halves[j // 2])

    def step(c, carry):
        scores(c, s_sc)
        absorb(c, s_sc, masked=False)
        return carry

    lax.fori_loop(0, n_full, step, 0)
    scores(n_full, s_sc)
    absorb(n_full, s_sc, masked=True)
    o_all = acc_sc[...] / jnp.sum(lp_sc[...], axis=1, keepdims=True)
    out = jnp.zeros((tq, GROUP_LANES), F32)
    for j in heads:
        out = jnp.where(in_head[j], o_all[j * tq:(j + 1) * tq], out)
    o_ref[0] = out.astype(BF16)


def _attention(q, k_all, v_all, fq, fk, *, tq, tk, q_off):
    b, t, _ = q.shape
    t_k = k_all.shape[1]
    n_kv = t_k // tk
    assert all((q_off + i * tq) % tk + tq <= tk for i in range(t // tq)), "a query block straddles key chunks"
    kern = functools.partial(_attn_kernel, tq=tq, tk=tk, q_off=q_off)
    rows = HEADS_PER_GROUP * tq
    return pl.pallas_call(
        kern,
        grid=(b, N_HEAD_GROUPS, t // tq),
        in_specs=[
            pl.BlockSpec((1, tq, GROUP_LANES), lambda bi, g, i: (bi, i, g)),
            pl.BlockSpec((1, t_k, GROUP_LANES), lambda bi, g, i: (bi, 0, g)),
            pl.BlockSpec((1, t_k, GROUP_LANES), lambda bi, g, i: (bi, 0, g)),
            pl.BlockSpec((1, 1, tq, HEADS_PER_GROUP), lambda bi, g, i: (bi, g, i, 0)),
            pl.BlockSpec((1, 1, n_kv, HEADS_PER_GROUP, tk), lambda bi, g, i: (bi, g, 0, 0, 0)),
        ],
        out_specs=pl.BlockSpec((1, tq, GROUP_LANES), lambda bi, g, i: (bi, i, g)),
        out_shape=jax.ShapeDtypeStruct((b, t, D_ATTN), BF16),
        scratch_shapes=[
            pltpu.VMEM((rows, GROUP_LANES), BF16),
            pltpu.VMEM((rows, LANES), F32),
            pltpu.VMEM((rows, tk), F32),
            pltpu.VMEM((rows, tk), BF16),
            pltpu.VMEM((rows, GROUP_LANES), F32),
            pltpu.VMEM((rows, LANES), F32),
            pltpu.VMEM((rows, LANES), F32),
            pltpu.VMEM((rows, LANES), F32),
        ],
        compiler_params=_params("parallel", "parallel", "arbitrary"),
        name="fox_attention",
    )(q, k_all, v_all, fq, fk)


def _rnn_kernel(xr_ref, gg_ref, hist_ref, h0_ref, cw_ref, cb_ref, wa_ref, ba_ref, wx_ref, bx_ref,
                lam_ref, out_ref, hl_ref, win_sc, h_sc, *, tb):
    t = pl.program_id(1)

    @pl.when(t == 0)
    def _():
        win_sc[:SUBLANES, :] = hist_ref[0]
        h_sc[...] = h0_ref[0]

    x = xr_ref[0]
    win_sc[SUBLANES:, :] = x
    xc = x * cw_ref[CONV_WIDTH - 1:CONV_WIDTH, :] + cb_ref[...]
    for s in range(1, CONV_WIDTH):
        xc = xc + win_sc[SUBLANES - s:SUBLANES - s + tb, :] * cw_ref[CONV_WIDTH - 1 - s:CONV_WIDTH - s, :]
    win_sc[:SUBLANES, :] = x[tb - SUBLANES:tb]

    xcb = xc.astype(BF16)
    r = _sigmoid(jnp.dot(xcb, wa_ref[...], preferred_element_type=F32) + ba_ref[...])
    ig = _sigmoid(jnp.dot(xcb, wx_ref[...], preferred_element_type=F32) + bx_ref[...])
    log_a = (-LRU_C) * r * _softplus(-lam_ref[...])
    a = jnp.exp(log_a)
    one_minus_a2 = 1.0 - a * a
    scale = jnp.where(one_minus_a2 > 0.0, one_minus_a2 * lax.rsqrt(one_minus_a2), 0.0)
    bterm = scale * ig * xc

    grouped = (tb // SUBLANES, SUBLANES, D_RNN)
    a, bterm = a.reshape(grouped), bterm.reshape(grouped)
    row_in_group = lax.broadcasted_iota(jnp.int32, grouped, 1)
    d = 1
    while d < SUBLANES:
        valid = row_in_group >= d
        a_s = pltpu.roll(a, d, axis=1)
        b_s = pltpu.roll(bterm, d, axis=1)
        bterm = jnp.where(valid, a * b_s + bterm, bterm)
        a = jnp.where(valid, a * a_s, a)
        d *= 2
    h_last = h_sc[...]
    groups = []
    for g in range(tb // SUBLANES):
        h_g = bterm[g] + a[g] * h_last
        h_last = h_g[SUBLANES - 1:SUBLANES]
        groups.append(h_g)
    h = jnp.concatenate(groups, axis=0)
    h_sc[...] = h_last
    hl_ref[0] = h_last
    out_ref[0] = (h * gg_ref[0].astype(F32)).astype(BF16)


def _rnn(xr, gg, hist8, h0, conv_w, conv_b, wa, ba, wx, bx, lam, tb):
    b, t, _ = xr.shape
    blk = pl.BlockSpec((1, tb, D_RNN), lambda bi, ti: (bi, ti, 0))
    full = lambda a: pl.BlockSpec(a.shape, lambda bi, ti: (0,) * a.ndim)
    per_b = lambda r: pl.BlockSpec((1, r, D_RNN), lambda bi, ti: (bi, 0, 0))
    return pl.pallas_call(
        functools.partial(_rnn_kernel, tb=tb),
        grid=(b, t // tb),
        in_specs=[blk, blk, per_b(SUBLANES), per_b(1), full(conv_w), full(conv_b), full(wa), full(ba),
                  full(wx), full(bx), full(lam)],
        out_specs=(blk, per_b(1)),
        out_shape=(jax.ShapeDtypeStruct((b, t, D_RNN), BF16), jax.ShapeDtypeStruct((b, 1, D_RNN), F32)),
        scratch_shapes=[pltpu.VMEM((SUBLANES + tb, D_RNN), F32), pltpu.VMEM((1, D_RNN), F32)],
        compiler_params=_params("parallel", "arbitrary"),
        name="conv_rglru",
    )(xr, gg, hist8, h0, conv_w, conv_b, wa, ba, wx, bx, lam)


def _outproj_kernel(o_ref, r_ref, sa_ref, sr_ref, x_ref, wau_ref, wru_ref, wo_ref, g_ref, b_ref,
                    h_ref, *, alpha):
    up_a = jnp.dot(o_ref[...], wau_ref[...], preferred_element_type=F32)
    up_r = jnp.dot(r_ref[...], wru_ref[...], preferred_element_type=F32)
    merged = sa_ref[...].astype(F32) * up_a + sr_ref[...].astype(F32) * up_r
    mix = jnp.dot(merged.astype(BF16), wo_ref[...], preferred_element_type=F32)
    h_ref[...] = _layer_norm(alpha * x_ref[...] + mix, g_ref[...], b_ref[...])


def _outproj(o2, r2, sa, sr, x2, wau, wru, wo, g, bta, alpha, tm):
    n = x2.shape[0]
    row = lambda w: pl.BlockSpec((tm, w), lambda i: (i, 0))
    full = lambda a: pl.BlockSpec(a.shape, lambda i: (0,) * a.ndim)
    return pl.pallas_call(
        functools.partial(_outproj_kernel, alpha=alpha),
        grid=(n // tm,),
        in_specs=[row(D_ATTN), row(D_RNN), row(D_MODEL), row(D_MODEL), row(D_MODEL), full(wau), full(wru),
                  full(wo), full(g), full(bta)],
        out_specs=row(D_MODEL),
        out_shape=jax.ShapeDtypeStruct((n, D_MODEL), F32),
        compiler_params=_params("parallel"),
        name="outproj_ln1",
    )(o2, r2, sa, sr, x2, wau, wru, wo, g, bta)


ROUTE_TOKENS = 2 * LANES


def _odd_even_merge_sort(n):
    def merge(lo, hi, r):
        step = r * 2
        if step < hi - lo:
            yield from merge(lo, hi, step)
            yield from merge(lo + r, hi, step)
            yield from ((i, i + r) for i in range(lo + r, hi - r, step))
        else:
            yield (lo, lo + r)

    def sort(lo, hi):
        if hi - lo >= 1:
            mid = lo + (hi - lo) // 2
            yield from sort(lo, mid)
            yield from sort(mid + 1, hi)
            yield from merge(lo, hi, 1)

    return tuple(sort(0, n - 1))


_SORT_TOPK = _odd_even_merge_sort(PEER_TOPK)


def _compare_exchange(v, i, j):
    v[i], v[j] = jnp.maximum(v[i], v[j]), jnp.minimum(v[i], v[j])


def _top_k_sorted(tiles):
    v = list(tiles)
    for i, j in _SORT_TOPK:
        _compare_exchange(v, i, j)
    shift = SUBLANES // 2
    while shift >= 1:
        other = [pltpu.roll(x, shift, axis=0) for x in v]
        v = [jnp.maximum(v[i], other[PEER_TOPK - 1 - i]) for i in range(PEER_TOPK)]
        d = PEER_TOPK // 2
        while d >= 1:
            for i in range(PEER_TOPK):
                if i & d == 0:
                    _compare_exchange(v, i, i + d)
            d //= 2
        shift //= 2
    return v


def _count_greater(x, t):
    assert len(t) == 16, "the bisection below is written out for 16 entries"
    one = lambda m, w: jnp.where(m, float(w), 0.0)
    b3 = t[7] > x
    b2 = jnp.where(b3, t[11], t[3]) > x
    b1 = jnp.where(b3, jnp.where(b2, t[13], t[9]), jnp.where(b2, t[5], t[1])) > x
    hi = jnp.where(b2, jnp.where(b1, t[14], t[12]), jnp.where(b1, t[10], t[8]))
    lo = jnp.where(b2, jnp.where(b1, t[6], t[4]), jnp.where(b1, t[2], t[0]))
    b0 = jnp.where(b3, hi, lo) > x
    count = one(b3, 8) + one(b2, 4) + one(b1, 2) + one(b0, 1)
    return jnp.where(t[15] > x, float(PEER_TOPK), count)


def _route_kernel(h_ref, wq_ref, k1_ref, k2_ref, ht_ref, cnt_ref, e1_ref, r2_ref, e2_ref,
                  qt_sc, t1_sc, t2_sc, *, tm):
    ht = jnp.transpose(h_ref[...]).astype(BF16)
    ht_ref[...] = ht
    qt_sc[...] = jnp.dot(wq_ref[...], ht, preferred_element_type=F32).astype(BF16)
    key_iota = lax.broadcasted_iota(jnp.int32, (N_KEYS, tm), 0).astype(F32)
    top_iota = lax.broadcasted_iota(jnp.int32, (PEER_TOPK, tm), 0).astype(F32)
    front_rows = SUBLANES

    def scores(hd):
        base = hd * 2 * PEER_HALF
        s1 = jnp.dot(k1_ref[...], qt_sc[base:base + PEER_HALF], preferred_element_type=F32)
        s2 = jnp.dot(k2_ref[...], qt_sc[base + PEER_HALF:base + 2 * PEER_HALF], preferred_element_type=F32)
        return s1, s2

    sub = lax.broadcasted_iota(jnp.int32, (SUBLANES, tm), 0)
    tiles_of = lambda s: [s[r * SUBLANES:(r + 1) * SUBLANES] for r in range(N_KEYS // SUBLANES)]
    sublane_sum = lambda x: jnp.sum(x, axis=0, keepdims=True)

    def route_head_sorted(hd):
        s1, s2 = scores(hd)
        rows1, rows2 = tiles_of(s1), tiles_of(s2)
        t1 = _top_k_sorted(rows1)
        t2 = _top_k_sorted(rows2)
        t1_lo = t1[SUBLANES - 1]
        for a in range(SUBLANES - 2, -1, -1):
            t1_lo = jnp.where(sub == a, t1[a], t1_lo)
        cand = [t1_lo + t2[b] for b in range(PEER_TOPK)]
        pool = list(cand)
        for a in range(SUBLANES, PEER_TOPK):
            pool[a] = jnp.where(sub == 1, t1[a] + t2[0], cand[a])
        top = _top_k_sorted(pool)
        tau = top[PEER_TOPK - 1]
        cnt_lo = functools.reduce(jnp.add, [jnp.where(c >= tau, 1.0, 0.0) for c in cand])
        cnt = [jnp.broadcast_to(cnt_lo[a:a + 1], (SUBLANES, tm)) for a in range(SUBLANES)]
        cnt += [jnp.where(t1[a] + t2[0] >= tau, 1.0, 0.0) for a in range(SUBLANES, PEER_TOPK)]
        z = functools.reduce(jnp.add, [jnp.exp(t - top[0]) for t in top])

        cnt1_rows, rank2_rows = [], []
        for r in range(N_KEYS // SUBLANES):
            c = jnp.zeros((SUBLANES, tm), F32)
            for a in range(PEER_TOPK):
                c = jnp.where(rows1[r] == t1[a], cnt[a], c)
            cnt1_rows.append(c)
            g = _count_greater(rows2[r], t2)
            rank2_rows.append(jnp.where(g < float(PEER_TOPK), g, NOT_SELECTED_RANK))
        cnt1 = jnp.concatenate(cnt1_rows, axis=0)
        rank2 = jnp.concatenate(rank2_rows, axis=0)
        cnt_ref[hd] = cnt1
        e1_ref[hd] = jnp.exp(s1 - t1[0][0:1]) / z[0:1]
        r2_ref[hd] = rank2.astype(BF16)
        e2_ref[hd] = jnp.exp(s2 - t2[0][0:1]).astype(BF16)

        gap = lambda t: functools.reduce(jnp.minimum, [t[b] - t[b + 1] for b in range(PEER_TOPK - 1)])[0:1]
        n_cnt = sublane_sum(cnt_lo) + functools.reduce(jnp.add, cnt[SUBLANES:])[0:1]
        n_sel = sublane_sum(functools.reduce(jnp.add, cnt1_rows))
        n_rank = sublane_sum(functools.reduce(
            jnp.add, [jnp.where(x < float(PEER_TOPK), 1.0, 0.0) for x in rank2_rows]))
        off = lambda n: jnp.abs(n - float(PEER_TOPK))
        return (off(n_cnt) + off(n_sel) + off(n_rank)
                + jnp.where(jnp.minimum(gap(t1), gap(t2)) > 0.0, 0.0, 1.0))

    def route_head_exact(hd):
        def pick_one(v, iota, n):
            m = jnp.max(v, axis=0, keepdims=True)
            return m, iota == jnp.min(jnp.where(v == m, iota, float(n)), axis=0, keepdims=True)

        s1, s2 = scores(hd)

        def extract(a, carry):
            v1, r1, v2, r2 = carry
            m1, sel1 = pick_one(v1, key_iota, N_KEYS)
            m2, sel2 = pick_one(v2, key_iota, N_KEYS)
            t1_sc[pl.ds(a, 1), :] = m1
            t2_sc[pl.ds(a, 1), :] = m2
            af = jnp.asarray(a, dtype=F32)
            return (jnp.where(sel1, -jnp.inf, v1), jnp.where(sel1, af, r1),
                    jnp.where(sel2, -jnp.inf, v2), jnp.where(sel2, af, r2))

        no_rank = jnp.full((N_KEYS, tm), NOT_SELECTED_RANK, F32)
        _, rank1, _, rank2 = lax.fori_loop(0, PEER_TOPK, extract, (s1, no_rank, s2, no_rank))
        t1 = t1_sc[...]
        t2 = t2_sc[...]
        top0 = t1[0:1] + t2[0:1]

        def pick(_, carry):
            ptr, front, z = carry
            m, sel = pick_one(front, top_iota, PEER_TOPK)
            ptr = ptr + jnp.where(sel, 1.0, 0.0)
            lo = ptr[:front_rows]
            nxt = jnp.full(lo.shape, -jnp.inf, F32)
            for b in range(1, PEER_TOPK):
                nxt = jnp.where(lo == float(b), t2_sc[b:b + 1, :], nxt)
            nxt = jnp.where(lo == 0.0, t2[0:1], nxt)
            front = jnp.concatenate(
                [t1[:front_rows] + nxt, jnp.where(sel[front_rows:], -jnp.inf, front[front_rows:])], axis=0)
            return ptr, front, z + jnp.exp(m - top0)

        cnt, _, z = lax.fori_loop(
            0, PEER_TOPK, pick,
            (jnp.zeros((PEER_TOPK, tm), F32), t1 + t2[0:1], jnp.zeros((1, tm), F32)))

        cnt1 = jnp.zeros((N_KEYS, tm), F32)
        for a in range(PEER_TOPK):
            cnt1 = jnp.where(rank1 == float(a), cnt[a:a + 1], cnt1)
        cnt_ref[hd] = cnt1
        e1_ref[hd] = jnp.exp(s1 - t1[0:1]) / z
        r2_ref[hd] = rank2.astype(BF16)
        e2_ref[hd] = jnp.exp(s2 - t2[0:1]).astype(BF16)

    doubt = [route_head_sorted(hd) for hd in range(PEER_HEADS)]

    @pl.when(jnp.max(functools.reduce(jnp.maximum, doubt)) > 0.0)
    def _():
        for hd in range(PEER_HEADS):
            @pl.when(jnp.max(doubt[hd]) > 0.0)
            def _():
                route_head_exact(hd)


def _route(h2, wq_t, k1, k2, tm):
    n = h2.shape[0]
    full = lambda a: pl.BlockSpec(a.shape, lambda i: (0,) * a.ndim)
    per_head = pl.BlockSpec((PEER_HEADS, N_KEYS, tm), lambda i: (0, 0, i))
    hshape = lambda dt: jax.ShapeDtypeStruct((PEER_HEADS, N_KEYS, n), dt)
    return pl.pallas_call(
        functools.partial(_route_kernel, tm=tm),
        grid=(n // tm,),
        in_specs=[pl.BlockSpec((tm, D_MODEL), lambda i: (i, 0)), full(wq_t), full(k1), full(k2)],
        out_specs=(pl.BlockSpec((D_MODEL, tm), lambda i: (0, i)), per_head, per_head, per_head, per_head),
        out_shape=(jax.ShapeDtypeStruct((D_MODEL, n), BF16), hshape(F32), hshape(F32), hshape(BF16),
                   hshape(BF16)),
        scratch_shapes=[pltpu.VMEM((PEER_HEADS * 2 * PEER_HALF, tm), BF16),
                        pltpu.VMEM((PEER_TOPK, tm), F32), pltpu.VMEM((PEER_TOPK, tm), F32)],
        compiler_params=_params("parallel"),
        name="peer_route",
    )(h2, wq_t, k1, k2)


ROWS_PER_STEP = 16
EXPERTS_PER_STEP = ROWS_PER_STEP * N_KEYS
ROWS_PER_SUB = 4
EXPERTS_PER_SUB = ROWS_PER_SUB * N_KEYS
SUBS_PER_MIX = 2
DENSE_CHUNK = 2 * LANES


def _bf16_row_tile(row):
    tile = jnp.broadcast_to(row, (BF16_ROWS, row.shape[1])).astype(BF16)
    return jnp.concatenate([tile] * (N_KEYS // BF16_ROWS), axis=0)


def _dense_kernel(ht_ref, u_ref, vt_ref, cnt_ref, e1_ref, r2_ref, e2_ref, h_ref, g_ref, b_ref,
                  y_ref, acc_sc, act_sc, coef_sc, *, alpha):
    e = pl.program_id(1)
    tm = act_sc.shape[1]
    n_sub = ROWS_PER_STEP // ROWS_PER_SUB

    @pl.when(e == 0)
    def _():
        acc_sc[...] = jnp.zeros_like(acc_sc)

    def activations(sub):
        rows = slice(sub * EXPERTS_PER_SUB, (sub + 1) * EXPERTS_PER_SUB)
        act_sc[rows, :] = jnp.dot(u_ref[rows, :], ht_ref[...], preferred_element_type=F32)

    def tiles(sub):
        chunk = min(DENSE_CHUNK, tm)
        for il in range(sub * ROWS_PER_SUB, (sub + 1) * ROWS_PER_SUB):
            for c in range(tm // chunk):
                yield il, slice(il * N_KEYS, (il + 1) * N_KEYS), slice(c * chunk, (c + 1) * chunk)

    def routing_weights(sub):
        for il, rows, lanes in tiles(sub):
            w = None
            for hd in range(PEER_HEADS):
                cnt = _bf16_row_tile(cnt_ref[hd, il:il + 1, lanes])
                e1 = _bf16_row_tile(e1_ref[hd, il:il + 1, lanes])
                term = jnp.where(r2_ref[hd, :, lanes] < cnt, e2_ref[hd, :, lanes] * e1, jnp.zeros((), BF16))
                w = term if w is None else w + term
            coef_sc[rows, lanes] = w

    def coefficients(sub):
        for _, rows, lanes in tiles(sub):
            coef_sc[rows, lanes] = coef_sc[rows, lanes] * _gelu_tanh(act_sc[rows, lanes].astype(BF16))

    def mix(first, last):
        rows = slice(first * EXPERTS_PER_SUB, (last + 1) * EXPERTS_PER_SUB)
        acc_sc[...] += jnp.dot(vt_ref[:, rows], coef_sc[rows, :], preferred_element_type=F32)

    routing_weights(0)
    activations(0)
    for sub in range(n_sub):
        if sub + 1 < n_sub:
            routing_weights(sub + 1)
            activations(sub + 1)
        coefficients(sub)
        if sub % SUBS_PER_MIX == SUBS_PER_MIX - 1:
            mix(sub - SUBS_PER_MIX + 1, sub)

    @pl.when(e == pl.num_programs(1) - 1)
    def _():
        peer = jnp.transpose(acc_sc[...])
        y_ref[...] = _layer_norm(alpha * h_ref[...] + peer, g_ref[...], b_ref[...])


def _dense(ht, u_b, vt_b, cnt1, e1n, rank2, e2, h2, g, bta, alpha, tm):
    n = h2.shape[0]
    n_e = N_EXPERTS // EXPERTS_PER_STEP
    full = lambda a: pl.BlockSpec(a.shape, lambda t, e: (0,) * a.ndim)
    rows_blk = pl.BlockSpec((PEER_HEADS, ROWS_PER_STEP, tm), lambda t, e: (0, e, t))
    cols_blk = pl.BlockSpec((PEER_HEADS, N_KEYS, tm), lambda t, e: (0, 0, t))
    return pl.pallas_call(
        functools.partial(_dense_kernel, alpha=alpha),
        grid=(n // tm, n_e),
        in_specs=[
            pl.BlockSpec((D_MODEL, tm), lambda t, e: (0, t)),
            pl.BlockSpec((EXPERTS_PER_STEP, D_MODEL), lambda t, e: (e, 0)),
            pl.BlockSpec((D_MODEL, EXPERTS_PER_STEP), lambda t, e: (0, e)),
            rows_blk, rows_blk, cols_blk, cols_blk,
            pl.BlockSpec((tm, D_MODEL), lambda t, e: (t, 0)),
            full(g), full(bta),
        ],
        out_specs=pl.BlockSpec((tm, D_MODEL), lambda t, e: (t, 0)),
        out_shape=jax.ShapeDtypeStruct((n, D_MODEL), F32),
        scratch_shapes=[pltpu.VMEM((D_MODEL, tm), F32), pltpu.VMEM((EXPERTS_PER_STEP, tm), F32),
                        pltpu.VMEM((EXPERTS_PER_STEP, tm), BF16)],
        compiler_params=_params("parallel", "arbitrary"),
        name="peer_dense",
    )(ht, u_b, vt_b, cnt1, e1n, rank2, e2, h2, g, bta)


def _block_diag(w):
    nb, bi, bo = w.shape
    eye = jnp.eye(nb, dtype=w.dtype)
    return (eye[:, None, :, None] * w[:, :, None, :]).reshape(nb * bi, nb * bo)


def _prep_weights(w_in, b_forget, conv_w, conv_b, w_rg_a, b_rg_a, w_rg_x, b_rg_x, lru_lambda,
                  w_attn_up, w_rnn_up, w_out, ln1_g, ln1_b, peer_w_query, peer_keys_1, peer_keys_2,
                  peer_u, peer_v, ln2_g, ln2_b):
    c_f = 3 * D_ATTN
    w_in_p = jnp.concatenate(
        [w_in[:, :c_f], jnp.pad(w_in[:, c_f:c_f + N_ATTN_HEADS], ((0, 0), (0, F_PAD - N_ATTN_HEADS))),
         w_in[:, c_f + N_ATTN_HEADS:]], axis=1).astype(BF16)
    row = lambda a: a.reshape(1, -1).astype(F32)
    return dict(
        w_in=w_in_p, b_forget=b_forget.reshape(-1, 1).astype(F32), conv_w=conv_w.astype(F32), conv_b=row(conv_b),
        wa=_block_diag(w_rg_a).astype(BF16), ba=row(b_rg_a), wx=_block_diag(w_rg_x).astype(BF16),
        bx=row(b_rg_x), lam=row(lru_lambda),
        wau=w_attn_up.astype(BF16), wru=w_rnn_up.astype(BF16), wo=w_out.astype(BF16),
        ln1_g=row(ln1_g), ln1_b=row(ln1_b),
        wq_t=jnp.transpose(peer_w_query).astype(BF16), k1=peer_keys_1.astype(BF16),
        k2=peer_keys_2.astype(BF16), u=peer_u.astype(BF16), vt=jnp.transpose(peer_v).astype(BF16),
        ln2_g=row(ln2_g), ln2_b=row(ln2_b),
    )


def _pick_block(n, target):
    blk = min(n, target)
    assert n % blk == 0, (n, blk)
    return blk


def _trunk_layer(x, past_k, past_v, past_logf, conv_hist, h0, p, alpha):
    bsz, t, _ = x.shape
    n = bsz * t
    x2 = x.reshape(n, D_MODEL)
    q, k, v, kb, vb, lf, xr, gg, sa, sr = _inproj(x2, p["w_in"], p["b_forget"], bsz, t, _pick_block(n, 256))

    def state_layout(a):
        if a.ndim == 3:
            return jnp.transpose(a.reshape(bsz, N_ATTN_HEADS, ATTN_HEAD_DIM, t), (0, 3, 1, 2))
        return a.reshape(bsz, t, N_ATTN_HEADS, ATTN_HEAD_DIM)

    tq = _pick_block(t, ATTN_Q_BLOCK)
    n_past = 0 if past_k is None else past_k.shape[1]
    t_all = n_past + t
    t_lanes = -(-t_all // LANES) * LANES
    tk = ATTN_K_BLOCK if t_lanes % ATTN_K_BLOCK == 0 else t_lanes
    t_pad = -(-t_all // tk) * tk
    lf_bht = jnp.transpose(lf.reshape(N_ATTN_HEADS, bsz, t), (1, 0, 2))
    lf_all = lf_bht
    kb3 = kb.reshape(bsz, t, D_ATTN)
    vb3 = vb.reshape(bsz, t, D_ATTN)
    if past_k is not None:
        lf_all = jnp.concatenate([jnp.transpose(past_logf.astype(F32), (0, 2, 1)), lf_bht], axis=2)
        kb3 = jnp.concatenate([past_k.reshape(bsz, n_past, D_ATTN).astype(BF16), kb3], axis=1)
        vb3 = jnp.concatenate([past_v.reshape(bsz, n_past, D_ATTN).astype(BF16), vb3], axis=1)
    pad = ((0, 0), (0, t_pad - t_all), (0, 0))
    kb3, vb3 = jnp.pad(kb3, pad), jnp.pad(vb3, pad)
    f_t = _cumsum_time(jnp.pad(lf_all, ((0, 0), (0, 0), (0, t_pad - t_all))))
    f_g = f_t.reshape(bsz, N_HEAD_GROUPS, HEADS_PER_GROUP, t_pad)
    fq = jnp.transpose(f_g[:, :, :, n_past:n_past + t], (0, 1, 3, 2))
    fk = jnp.transpose(f_g.reshape(bsz, N_HEAD_GROUPS, HEADS_PER_GROUP, t_pad // tk, tk), (0, 1, 3, 2, 4))
    o = _attention(q.reshape(bsz, t, D_ATTN), kb3, vb3, fq, fk, tq=tq, tk=tk, q_off=n_past)

    hist8 = jnp.pad(conv_hist.astype(F32), ((0, 0), (SUBLANES - (CONV_WIDTH - 1), 0), (0, 0)))
    xr3 = xr.reshape(bsz, t, D_RNN)
    rnn_out, h_last = _rnn(xr3, gg.reshape(bsz, t, D_RNN), hist8, h0.astype(F32).reshape(bsz, 1, D_RNN),
                           p["conv_w"], p["conv_b"], p["wa"], p["ba"], p["wx"], p["bx"], p["lam"],
                           _pick_block(t, 256))
    new_hist = jnp.concatenate([conv_hist.astype(F32), xr3], axis=1)[:, -(CONV_WIDTH - 1):]

    h = _outproj(o.reshape(n, D_ATTN), rnn_out.reshape(n, D_RNN), sa, sr, x2, p["wau"], p["wru"], p["wo"],
                 p["ln1_g"], p["ln1_b"], alpha, _pick_block(n, 256))
    ht, cnt1, e1n, rank2, e2 = _route(h, p["wq_t"], p["k1"], p["k2"], _pick_block(n, ROUTE_TOKENS))
    y = _dense(ht, p["u"], p["vt"], cnt1, e1n, rank2, e2, h, p["ln2_g"], p["ln2_b"], alpha,
               _pick_block(n, 512))
    return (y.reshape(bsz, t, D_MODEL), state_layout(k), state_layout(v), jnp.transpose(lf_bht, (0, 2, 1)), new_hist,
            h_last.reshape(bsz, D_RNN))


def kernel(x_prompt, x_sample, cache_k, cache_v, cache_logf, state_conv, state_rnn, w_in, b_forget, conv_w, conv_b, w_rg_a, b_rg_a, w_rg_x, b_rg_x, lru_lambda, w_attn_up, w_rnn_up, w_out, ln1_g, ln1_b, peer_w_query, peer_keys_1, peer_keys_2, peer_u, peer_v, ln2_g, ln2_b):
    depth = w_in.shape[0]
    alpha = (2 * depth) ** 0.25
    layer_weights = (w_in, b_forget, conv_w, conv_b, w_rg_a, b_rg_a, w_rg_x, b_rg_x, lru_lambda, w_attn_up,
                     w_rnn_up, w_out, ln1_g, ln1_b, peer_w_query, peer_keys_1, peer_keys_2, peer_u, peer_v,
                     ln2_g, ln2_b)
    hp, hs = x_prompt, x_sample
    prompt_state, sample_state = [], []
    for l in range(depth):
        p = _prep_weights(*(w[l] for w in layer_weights))
        zero_hist = jnp.zeros((hp.shape[0], CONV_WIDTH - 1, D_RNN), F32)
        zero_h = jnp.zeros((hp.shape[0], D_RNN), F32)
        hp, *st_p = _trunk_layer(hp, None, None, None, zero_hist, zero_h, p, alpha)
        hs, *st_s = _trunk_layer(hs, cache_k[l], cache_v[l], cache_logf[l], state_conv[l], state_rnn[l], p,
                                 alpha)
        prompt_state.append(st_p)
        sample_state.append(st_s)
    stack = lambda states, i: jnp.stack([s[i] for s in states])
    return (hp, hs) + tuple(stack(prompt_state, i) for i in range(5)) + tuple(
        stack(sample_state, i) for i in range(5))
```

```python
import functools
import math

import jax
import jax.numpy as jnp
from jax import lax
from jax.experimental import pallas as pl
from jax.experimental.pallas import tpu as pltpu

F32 = jnp.float32
BF16 = jnp.bfloat16

D_MODEL = 1024
N_ATTN_HEADS = 8
ATTN_HEAD_DIM = 64
D_ATTN = N_ATTN_HEADS * ATTN_HEAD_DIM
ATTN_SCALE = ATTN_HEAD_DIM ** -0.5
D_RNN = 512
CONV_WIDTH = 4
LRU_C = 8.0
N_KEYS = 128
N_EXPERTS = N_KEYS * N_KEYS
PEER_HEADS = 8
PEER_TOPK = 16
PEER_HALF = 128
LN_EPS = 1e-5

LANES = 128
SUBLANES = 8
BF16_ROWS = 2 * SUBLANES
VMEM_LIMIT_BYTES = 56 * 1024 * 1024

HEADS_PER_GROUP = 4
GROUP_LANES = HEADS_PER_GROUP * ATTN_HEAD_DIM
N_HEAD_GROUPS = N_ATTN_HEADS // HEADS_PER_GROUP
ATTN_Q_BLOCK = 512
ATTN_K_BLOCK = 512
ATTN_ROW_BLOCK = 32
F_PAD = LANES
NEG_BIG = -1e30
LOG2_E = math.log2(math.e)
NOT_SELECTED_RANK = 99.0

_C_Q = 0
_C_K = _C_Q + D_ATTN
_C_V = _C_K + D_ATTN
_C_F = _C_V + D_ATTN
_C_XR = _C_F + F_PAD
_C_GATE = _C_XR + D_RNN
_C_GA = _C_GATE + D_RNN
_C_GR = _C_GA + D_MODEL
_C_END = _C_GR + D_MODEL


def _params(*sem):
    return pltpu.CompilerParams(dimension_semantics=sem, vmem_limit_bytes=VMEM_LIMIT_BYTES)


def _sigmoid(x):
    return 1.0 / (1.0 + jnp.exp(-x))


def _gelu_tanh(x):
    half = 0.5 * x
    return half + half * jnp.tanh(x * (0.7978845608028654 + 0.035677408136300125 * (x * x)))


def _softplus(x):
    return jnp.maximum(x, 0.0) + jnp.log1p(jnp.exp(-jnp.abs(x)))


def _layer_norm(x, g, b):
    mu = jnp.mean(x, axis=-1, keepdims=True)
    xc = x - mu
    var = jnp.mean(xc * xc, axis=-1, keepdims=True)
    return xc * lax.rsqrt(var + LN_EPS) * g + b


def _inproj_kernel(x_ref, w_ref, bf_ref, q_ref, k_ref, v_ref, kb_ref, vb_ref, lf_ref, xr_ref,
                   gg_ref, sa_ref, sr_ref, *, time_minor_kv):
    xb = x_ref[...].astype(BF16)

    def mm(lo, hi):
        return jnp.dot(xb, w_ref[:, lo:hi], preferred_element_type=F32)

    q_ref[...] = (mm(_C_Q, _C_K) * (ATTN_SCALE * LOG2_E)).astype(BF16)
    k = mm(_C_K, _C_V)
    kb_ref[...] = k.astype(BF16)
    v = mm(_C_V, _C_F)
    vb_ref[...] = v.astype(BF16)
    if time_minor_kv:
        k_ref[0] = jnp.transpose(k)
        v_ref[0] = jnp.transpose(v)
    else:
        k_ref[...] = k
        v_ref[...] = v
    f = jnp.transpose(mm(_C_F, _C_XR))[:N_ATTN_HEADS] + bf_ref[...]
    lf_ref[...] = -_softplus(-f)
    xr_ref[...] = mm(_C_XR, _C_GATE)
    gg_ref[...] = _gelu_tanh(mm(_C_GATE, _C_GA)).astype(BF16)
    sa_ref[...] = _sigmoid(mm(_C_GA, _C_GR)).astype(BF16)
    sr_ref[...] = _sigmoid(mm(_C_GR, _C_END)).astype(BF16)


def _inproj(x2, w_in_p, b_forget, bsz, t, tm):
    n = x2.shape[0]
    row = lambda w: pl.BlockSpec((tm, w), lambda i: (i, 0))
    full = lambda a: pl.BlockSpec(a.shape, lambda i: (0,) * a.ndim)
    time_minor_kv = tm % LANES == 0 and t % tm == 0
    if time_minor_kv:
        kv_shape = jax.ShapeDtypeStruct((bsz, D_ATTN, t), F32)
        kv_spec = pl.BlockSpec((1, D_ATTN, tm), lambda i: (i // (t // tm), 0, i % (t // tm)))
    else:
        kv_shape = jax.ShapeDtypeStruct((n, D_ATTN), F32)
        kv_spec = row(D_ATTN)
    out_shape = (
        jax.ShapeDtypeStruct((n, D_ATTN), BF16),
        kv_shape,
        kv_shape,
        jax.ShapeDtypeStruct((n, D_ATTN), BF16),
        jax.ShapeDtypeStruct((n, D_ATTN), BF16),
        jax.ShapeDtypeStruct((N_ATTN_HEADS, n), F32),
        jax.ShapeDtypeStruct((n, D_RNN), F32),
        jax.ShapeDtypeStruct((n, D_RNN), BF16),
        jax.ShapeDtypeStruct((n, D_MODEL), BF16),
        jax.ShapeDtypeStruct((n, D_MODEL), BF16),
    )
    out_specs = (row(D_ATTN), kv_spec, kv_spec, row(D_ATTN), row(D_ATTN),
                 pl.BlockSpec((N_ATTN_HEADS, tm), lambda i: (0, i)),
                 row(D_RNN), row(D_RNN), row(D_MODEL), row(D_MODEL))
    return pl.pallas_call(
        functools.partial(_inproj_kernel, time_minor_kv=time_minor_kv),
        grid=(n // tm,),
        in_specs=[row(D_MODEL), full(w_in_p), full(b_forget)],
        out_specs=out_specs,
        out_shape=out_shape,
        compiler_params=_params("parallel"),
        name="inproj",
    )(x2, w_in_p, b_forget)


def _cumsum_kernel(x_ref, o_ref):
    x = x_ref[0]
    t = x.shape[1]
    lane = lax.broadcasted_iota(jnp.int32, x.shape, 1)
    d = 1
    while d < t:
        x = x + jnp.where(lane >= d, pltpu.roll(x, d, axis=1), 0.0)
        d *= 2
    o_ref[0] = x * LOG2_E


def _cumsum_time(lf_t):
    b, h, t = lf_t.shape
    spec = pl.BlockSpec((1, h, t), lambda i: (i, 0, 0))
    return pl.pallas_call(
        _cumsum_kernel, grid=(b,), in_specs=[spec], out_specs=spec,
        out_shape=jax.ShapeDtypeStruct(lf_t.shape, F32),
        compiler_params=_params("parallel"), name="logf_cumsum",
    )(lf_t)


def _attn_kernel(q_ref, k_ref, v_ref, fq_ref, fk_ref, o_ref, q_sc, fq_sc, s_sc, p_sc, acc_sc, m_sc, al_sc,
                 lp_sc, *, tq, tk, q_off):
    qi = pl.program_id(2)
    row0 = q_off + qi * tq
    n_full = (row0 + 1) // tk
    heads = range(HEADS_PER_GROUP)
    rb = min(ATTN_ROW_BLOCK, tq)
    lane = lax.broadcasted_iota(jnp.int32, (tq, GROUP_LANES), 1)
    in_head = [(lane >= j * ATTN_HEAD_DIM) & (lane < (j + 1) * ATTN_HEAD_DIM) for j in heads]
    q = q_ref[0]
    for j in heads:
        q_sc[j * tq:(j + 1) * tq, :] = jnp.where(in_head[j], q, jnp.zeros_like(q))
        fq_sc[j * tq:(j + 1) * tq, :] = jnp.broadcast_to(fq_ref[0, 0, :, j:j + 1], (tq, LANES))
    m_sc[...] = jnp.full_like(m_sc, NEG_BIG)
    lp_sc[...] = jnp.zeros_like(lp_sc)
    acc_sc[...] = jnp.zeros_like(acc_sc)
    halves = [slice(0, 2 * tq), slice(2 * tq, 4 * tq)]
    n_lane_tiles = tk // LANES

    def scores(c, s_sc):
        ks = k_ref[0, pl.ds(pl.multiple_of(c * tk, tk), tk), :]
        for hv in halves:
            s_sc[hv, :] = lax.dot_general(q_sc[hv, :], ks, (((1,), (1,)), ((), ())),
                                          preferred_element_type=F32)

    def absorb(c, s_sc, masked):
        start = pl.multiple_of(c * tk, tk)
        fk = fk_ref[0, 0, c]
        vs = v_ref[0, pl.ds(start, tk), :]

        def weighted_values(hv):
            pv = jnp.dot(p_sc[hv, :], vs, preferred_element_type=F32)
            alpha = al_sc[hv, :]
            acc_sc[hv, :] = jnp.concatenate([alpha] * (GROUP_LANES // LANES), axis=1) * acc_sc[hv, :] + pv

        for j in heads:
            for r in range(tq // rb):
                rows = slice(j * tq + r * rb, j * tq + (r + 1) * rb)
                s = s_sc[rows, :] - fk[j:j + 1, :]
                if masked:
                    q_pos = row0 + r * rb + lax.broadcasted_iota(jnp.int32, (rb, tk), 0)
                    k_pos = c * tk + lax.broadcasted_iota(jnp.int32, (rb, tk), 1)
                    s = jnp.where(k_pos <= q_pos, s, NEG_BIG)
                tiles = [s[:, t * LANES:(t + 1) * LANES] for t in range(n_lane_tiles)]
                row_max = jnp.max(functools.reduce(jnp.maximum, tiles), axis=1, keepdims=True)
                fq = fq_sc[rows, :]
                m_old = m_sc[rows, :]
                m_new = jnp.maximum(m_old, jnp.broadcast_to(row_max, (rb, LANES)) + fq)
                alpha = jnp.exp2(m_old - m_new)
                shift = fq - m_new
                p_tiles = [jnp.exp2(tile + shift) for tile in tiles]
                m_sc[rows, :] = m_new
                al_sc[rows, :] = alpha
                lp_sc[rows, :] = alpha * lp_sc[rows, :] + functools.reduce(jnp.add, p_tiles)
                p_sc[rows, :] = jnp.concatenate(p_tiles, axis=1).astype(BF16)
            if j % 2 == 1:
                weighted_values(halves[j // 2])

    def step(c, carry):
        scores(c, s_sc)
        absorb(c, s_sc, masked=False)
        return carry

    lax.fori_loop(0, n_full, step, 0)
    scores(n_full, s_sc)
    absorb(n_full, s_sc, masked=True)
    o_all = acc_sc[...] / jnp.sum(lp_sc[...], axis=1, keepdims=True)
    out = jnp.zeros((tq, GROUP_LANES), F32)
    for j in heads:
        out = jnp.where(in_head[j], o_all[j * tq:(j + 1) * tq], out)
    o_ref[0] = out.astype(BF16)


def _attention(q, k_all, v_all, fq, fk, *, tq, tk, q_off):
    b, t, _ = q.shape
    t_k = k_all.shape[1]
    n_kv = t_k // tk
    assert all((q_off + i * tq) % tk + tq <= tk for i in range(t // tq)), "a query block straddles key chunks"
    kern = functools.partial(_attn_kernel, tq=tq, tk=tk, q_off=q_off)
    rows = HEADS_PER_GROUP * tq
    return pl.pallas_call(
        kern,
        grid=(b, N_HEAD_GROUPS, t // tq),
        in_specs=[
            pl.BlockSpec((1, tq, GROUP_LANES), lambda bi, g, i: (bi, i, g)),
            pl.BlockSpec((1, t_k, GROUP_LANES), lambda bi, g, i: (bi, 0, g)),
            pl.BlockSpec((1, t_k, GROUP_LANES), lambda bi, g, i: (bi, 0, g)),
            pl.BlockSpec((1, 1, tq, HEADS_PER_GROUP), lambda bi, g, i: (bi, g, i, 0)),
            pl.BlockSpec((1, 1, n_kv, HEADS_PER_GROUP, tk), lambda bi, g, i: (bi, g, 0, 0, 0)),
        ],
        out_specs=pl.BlockSpec((1, tq, GROUP_LANES), lambda bi, g, i: (bi, i, g)),
        out_shape=jax.ShapeDtypeStruct((b, t, D_ATTN), BF16),
        scratch_shapes=[
            pltpu.VMEM((rows, GROUP_LANES), BF16),
            pltpu.VMEM((rows, LANES), F32),
            pltpu.VMEM((rows, tk), F32),
            pltpu.VMEM((rows, tk), BF16),
            pltpu.VMEM((rows, GROUP_LANES), F32),
            pltpu.VMEM((rows, LANES), F32),
            pltpu.VMEM((rows, LANES), F32),
            pltpu.VMEM((rows, LANES), F32),
        ],
        compiler_params=_params("parallel", "parallel", "arbitrary"),
        name="fox_attention",
    )(q, k_all, v_all, fq, fk)


def _rnn_kernel(xr_ref, gg_ref, hist_ref, h0_ref, cw_ref, cb_ref, wa_ref, ba_ref, wx_ref, bx_ref,
                lam_ref, out_ref, hl_ref, win_sc, h_sc, *, tb):
    t = pl.program_id(1)

    @pl.when(t == 0)
    def _():
        win_sc[:SUBLANES, :] = hist_ref[0]
        h_sc[...] = h0_ref[0]

    x = xr_ref[0]
    win_sc[SUBLANES:, :] = x
    xc = x * cw_ref[CONV_WIDTH - 1:CONV_WIDTH, :] + cb_ref[...]
    for s in range(1, CONV_WIDTH):
        xc = xc + win_sc[SUBLANES - s:SUBLANES - s + tb, :] * cw_ref[CONV_WIDTH - 1 - s:CONV_WIDTH - s, :]
    win_sc[:SUBLANES, :] = x[tb - SUBLANES:tb]

    xcb = xc.astype(BF16)
    r = _sigmoid(jnp.dot(xcb, wa_ref[...], preferred_element_type=F32) + ba_ref[...])
    ig = _sigmoid(jnp.dot(xcb, wx_ref[...], preferred_element_type=F32) + bx_ref[...])
    log_a = (-LRU_C) * r * _softplus(-lam_ref[...])
    a = jnp.exp(log_a)
    one_minus_a2 = 1.0 - a * a
    scale = jnp.where(one_minus_a2 > 0.0, one_minus_a2 * lax.rsqrt(one_minus_a2), 0.0)
    bterm = scale * ig * xc

    grouped = (tb // SUBLANES, SUBLANES, D_RNN)
    a, bterm = a.reshape(grouped), bterm.reshape(grouped)
    row_in_group = lax.broadcasted_iota(jnp.int32, grouped, 1)
    d = 1
    while d < SUBLANES:
        valid = row_in_group >= d
        a_s = pltpu.roll(a, d, axis=1)
        b_s = pltpu.roll(bterm, d, axis=1)
        bterm = jnp.where(valid, a * b_s + bterm, bterm)
        a = jnp.where(valid, a * a_s, a)
        d *= 2
    h_last = h_sc[...]
    groups = []
    for g in range(tb // SUBLANES):
        h_g = bterm[g] + a[g] * h_last
        h_last = h_g[SUBLANES - 1:SUBLANES]
        groups.append(h_g)
    h = jnp.concatenate(groups, axis=0)
    h_sc[...] = h_last
    hl_ref[0] = h_last
    out_ref[0] = (h * gg_ref[0].astype(F32)).astype(BF16)


def _rnn(xr, gg, hist8, h0, conv_w, conv_b, wa, ba, wx, bx, lam, tb):
    b, t, _ = xr.shape
    blk = pl.BlockSpec((1, tb, D_RNN), lambda bi, ti: (bi, ti, 0))
    full = lambda a: pl.BlockSpec(a.shape, lambda bi, ti: (0,) * a.ndim)
    per_b = lambda r: pl.BlockSpec((1, r, D_RNN), lambda bi, ti: (bi, 0, 0))
    return pl.pallas_call(
        functools.partial(_rnn_kernel, tb=tb),
        grid=(b, t // tb),
        in_specs=[blk, blk, per_b(SUBLANES), per_b(1), full(conv_w), full(conv_b), full(wa), full(ba),
                  full(wx), full(bx), full(lam)],
        out_specs=(blk, per_b(1)),
        out_shape=(jax.ShapeDtypeStruct((b, t, D_RNN), BF16), jax.ShapeDtypeStruct((b, 1, D_RNN), F32)),
        scratch_shapes=[pltpu.VMEM((SUBLANES + tb, D_RNN), F32), pltpu.VMEM((1, D_RNN), F32)],
        compiler_params=_params("parallel", "arbitrary"),
        name="conv_rglru",
    )(xr, gg, hist8, h0, conv_w, conv_b, wa, ba, wx, bx, lam)


def _outproj_kernel(o_ref, r_ref, sa_ref, sr_ref, x_ref, wau_ref, wru_ref, wo_ref, g_ref, b_ref,
                    h_ref, *, alpha):
    up_a = jnp.dot(o_ref[...], wau_ref[...], preferred_element_type=F32)
    up_r = jnp.dot(r_ref[...], wru_ref[...], preferred_element_type=F32)
    merged = sa_ref[...].astype(F32) * up_a + sr_ref[...].astype(F32) * up_r
    mix = jnp.dot(merged.astype(BF16), wo_ref[...], preferred_element_type=F32)
    h_ref[...] = _layer_norm(alpha * x_ref[...] + mix, g_ref[...], b_ref[...])


def _outproj(o2, r2, sa, sr, x2, wau, wru, wo, g, bta, alpha, tm):
    n = x2.shape[0]
    row = lambda w: pl.BlockSpec((tm, w), lambda i: (i, 0))
    full = lambda a: pl.BlockSpec(a.shape, lambda i: (0,) * a.ndim)
    return pl.pallas_call(
        functools.partial(_outproj_kernel, alpha=alpha),
        grid=(n // tm,),
        in_specs=[row(D_ATTN), row(D_RNN), row(D_MODEL), row(D_MODEL), row(D_MODEL), full(wau), full(wru),
                  full(wo), full(g), full(bta)],
        out_specs=row(D_MODEL),
        out_shape=jax.ShapeDtypeStruct((n, D_MODEL), F32),
        compiler_params=_params("parallel"),
        name="outproj_ln1",
    )(o2, r2, sa, sr, x2, wau, wru, wo, g, bta)


ROUTE_TOKENS = 2 * LANES


def _odd_even_merge_sort(n):
    def merge(lo, hi, r):
        step = r * 2
        if step < hi - lo:
            yield from merge(lo, hi, step)
            yield from merge(lo + r, hi, step)
            yield from ((i, i + r) for i in range(lo + r, hi - r, step))
        else:
            yield (lo, lo + r)

    def sort(lo, hi):
        if hi - lo >= 1:
            mid = lo + (hi - lo) // 2
            yield from sort(lo, mid)
            yield from sort(mid + 1, hi)
            yield from merge(lo, hi, 1)

    return tuple(sort(0, n - 1))


_SORT_TOPK = _odd_even_merge_sort(PEER_TOPK)


def _compare_exchange(v, i, j):
    v[i], v[j] = jnp.maximum(v[i], v[j]), jnp.minimum(v[i], v[j])


def _top_k_sorted(tiles):
    v = list(tiles)
    for i, j in _SORT_TOPK:
        _compare_exchange(v, i, j)
    shift = SUBLANES // 2
    while shift >= 1:
        other = [pltpu.roll(x, shift, axis=0) for x in v]
        v = [jnp.maximum(v[i], other[PEER_TOPK - 1 - i]) for i in range(PEER_TOPK)]
        d = PEER_TOPK // 2
        while d >= 1:
            for i in range(PEER_TOPK):
                if i & d == 0:
                    _compare_exchange(v, i, i + d)
            d //= 2
        shift //= 2
    return v


def _count_greater(x, t):
    assert len(t) == 16, "the bisection below is written out for 16 entries"
    one = lambda m, w: jnp.where(m, float(w), 0.0)
    b3 = t[7] > x
    b2 = jnp.where(b3, t[11], t[3]) > x
    b1 = jnp.where(b3, jnp.where(b2, t[13], t[9]), jnp.where(b2, t[5], t[1])) > x
    hi = jnp.where(b2, jnp.where(b1, t[14], t[12]), jnp.where(b1, t[10], t[8]))
    lo = jnp.where(b2, jnp.where(b1, t[6], t[4]), jnp.where(b1, t[2], t[0]))
    b0 = jnp.where(b3, hi, lo) > x
    count = one(b3, 8) + one(b2, 4) + one(b1, 2) + one(b0, 1)
    return jnp.where(t[15] > x, float(PEER_TOPK), count)


def _route_kernel(h_ref, wq_ref, k1_ref, k2_ref, ht_ref, cnt_ref, e1_ref, r2_ref, e2_ref,
                  qt_sc, t1_sc, t2_sc, *, tm):
    ht = jnp.transpose(h_ref[...]).astype(BF16)
    ht_ref[...] = ht
    qt_sc[...] = jnp.dot(wq_ref[...], ht, preferred_element_type=F32).astype(BF16)
    key_iota = lax.broadcasted_iota(jnp.int32, (N_KEYS, tm), 0).astype(F32)
    top_iota = lax.broadcasted_iota(jnp.int32, (PEER_TOPK, tm), 0).astype(F32)
    front_rows = SUBLANES

    def scores(hd):
        base = hd * 2 * PEER_HALF
        s1 = jnp.dot(k1_ref[...], qt_sc[base:base + PEER_HALF], preferred_element_type=F32)
        s2 = jnp.dot(k2_ref[...], qt_sc[base + PEER_HALF:base + 2 * PEER_HALF], preferred_element_type=F32)
        return s1, s2

    sub = lax.broadcasted_iota(jnp.int32, (SUBLANES, tm), 0)
    tiles_of = lambda s: [s[r * SUBLANES:(r + 1) * SUBLANES] for r in range(N_KEYS // SUBLANES)]
    sublane_sum = lambda x: jnp.sum(x, axis=0, keepdims=True)

    def route_head_sorted(hd):
        s1, s2 = scores(hd)
        rows1, rows2 = tiles_of(s1), tiles_of(s2)
        t1 = _top_k_sorted(rows1)
        t2 = _top_k_sorted(rows2)
        t1_lo = t1[SUBLANES - 1]
        for a in range(SUBLANES - 2, -1, -1):
            t1_lo = jnp.where(sub == a, t1[a], t1_lo)
        cand = [t1_lo + t2[b] for b in range(PEER_TOPK)]
        pool = list(cand)
        for a in range(SUBLANES, PEER_TOPK):
            pool[a] = jnp.where(sub == 1, t1[a] + t2[0], cand[a])
        top = _top_k_sorted(pool)
        tau = top[PEER_TOPK - 1]
        cnt_lo = functools.reduce(jnp.add, [jnp.where(c >= tau, 1.0, 0.0) for c in cand])
        cnt = [jnp.broadcast_to(cnt_lo[a:a + 1], (SUBLANES, tm)) for a in range(SUBLANES)]
        cnt += [jnp.where(t1[a] + t2[0] >= tau, 1.0, 0.0) for a in range(SUBLANES, PEER_TOPK)]
        z = functools.reduce(jnp.add, [jnp.exp(t - top[0]) for t in top])

        cnt1_rows, rank2_rows = [], []
        for r in range(N_KEYS // SUBLANES):
            c = jnp.zeros((SUBLANES, tm), F32)
            for a in range(PEER_TOPK):
                c = jnp.where(rows1[r] == t1[a], cnt[a], c)
            cnt1_rows.append(c)
            g = _count_greater(rows2[r], t2)
            rank2_rows.append(jnp.where(g < float(PEER_TOPK), g, NOT_SELECTED_RANK))
        cnt1 = jnp.concatenate(cnt1_rows, axis=0)
        rank2 = jnp.concatenate(rank2_rows, axis=0)
        cnt_ref[hd] = cnt1
        e1_ref[hd] = jnp.exp(s1 - t1[0][0:1]) / z[0:1]
        r2_ref[hd] = rank2.astype(BF16)
        e2_ref[hd] = jnp.exp(s2 - t2[0][0:1]).astype(BF16)

        gap = lambda t: functools.reduce(jnp.minimum, [t[b] - t[b + 1] for b in range(PEER_TOPK - 1)])[0:1]
        n_cnt = sublane_sum(cnt_lo) + functools.reduce(jnp.add, cnt[SUBLANES:])[0:1]
        n_sel = sublane_sum(functools.reduce(jnp.add, cnt1_rows))
        n_rank = sublane_sum(functools.reduce(
            jnp.add, [jnp.where(x < float(PEER_TOPK), 1.0, 0.0) for x in rank2_rows]))
        off = lambda n: jnp.abs(n - float(PEER_TOPK))
        return (off(n_cnt) + off(n_sel) + off(n_rank)
                + jnp.where(jnp.minimum(gap(t1), gap(t2)) > 0.0, 0.0, 1.0))

    def route_head_exact(hd):
        def pick_one(v, iota, n):
            m = jnp.max(v, axis=0, keepdims=True)
            return m, iota == jnp.min(jnp.where(v == m, iota, float(n)), axis=0, keepdims=True)

        s1, s2 = scores(hd)

        def extract(a, carry):
            v1, r1, v2, r2 = carry
            m1, sel1 = pick_one(v1, key_iota, N_KEYS)
            m2, sel2 = pick_one(v2, key_iota, N_KEYS)
            t1_sc[pl.ds(a, 1), :] = m1
            t2_sc[pl.ds(a, 1), :] = m2
            af = jnp.asarray(a, dtype=F32)
            return (jnp.where(sel1, -jnp.inf, v1), jnp.where(sel1, af, r1),
                    jnp.where(sel2, -jnp.inf, v2), jnp.where(sel2, af, r2))

        no_rank = jnp.full((N_KEYS, tm), NOT_SELECTED_RANK, F32)
        _, rank1, _, rank2 = lax.fori_loop(0, PEER_TOPK, extract, (s1, no_rank, s2, no_rank))
        t1 = t1_sc[...]
        t2 = t2_sc[...]
        top0 = t1[0:1] + t2[0:1]

        def pick(_, carry):
            ptr, front, z = carry
            m, sel = pick_one(front, top_iota, PEER_TOPK)
            ptr = ptr + jnp.where(sel, 1.0, 0.0)
            lo = ptr[:front_rows]
            nxt = jnp.full(lo.shape, -jnp.inf, F32)
            for b in range(1, PEER_TOPK):
                nxt = jnp.where(lo == float(b), t2_sc[b:b + 1, :], nxt)
            nxt = jnp.where(lo == 0.0, t2[0:1], nxt)
            front = jnp.concatenate(
                [t1[:front_rows] + nxt, jnp.where(sel[front_rows:], -jnp.inf, front[front_rows:])], axis=0)
            return ptr, front, z + jnp.exp(m - top0)

        cnt, _, z = lax.fori_loop(
            0, PEER_TOPK, pick,
            (jnp.zeros((PEER_TOPK, tm), F32), t1 + t2[0:1], jnp.zeros((1, tm), F32)))

        cnt1 = jnp.zeros((N_KEYS, tm), F32)
        for a in range(PEER_TOPK):
            cnt1 = jnp.where(rank1 == float(a), cnt[a:a + 1], cnt1)
        cnt_ref[hd] = cnt1
        e1_ref[hd] = jnp.exp(s1 - t1[0:1]) / z
        r2_ref[hd] = rank2.astype(BF16)
        e2_ref[hd] = jnp.exp(s2 - t2[0:1]).astype(BF16)

    doubt = [route_head_sorted(hd) for hd in range(PEER_HEADS)]

    @pl.when(jnp.max(functools.reduce(jnp.maximum, doubt)) > 0.0)
    def _():
        for hd in range(PEER_HEADS):
            @pl.when(jnp.max(doubt[hd]) > 0.0)
            def _():
                route_head_exact(hd)


def _route(h2, wq_t, k1, k2, tm):
    n = h2.shape[0]
    full = lambda a: pl.BlockSpec(a.shape, lambda i: (0,) * a.ndim)
    per_head = pl.BlockSpec((PEER_HEADS, N_KEYS, tm), lambda i: (0, 0, i))
    hshape = lambda dt: jax.ShapeDtypeStruct((PEER_HEADS, N_KEYS, n), dt)
    return pl.pallas_call(
        functools.partial(_route_kernel, tm=tm),
        grid=(n // tm,),
        in_specs=[pl.BlockSpec((tm, D_MODEL), lambda i: (i, 0)), full(wq_t), full(k1), full(k2)],
        out_specs=(pl.BlockSpec((D_MODEL, tm), lambda i: (0, i)), per_head, per_head, per_head, per_head),
        out_shape=(jax.ShapeDtypeStruct((D_MODEL, n), BF16), hshape(F32), hshape(F32), hshape(BF16),
                   hshape(BF16)),
        scratch_shapes=[pltpu.VMEM((PEER_HEADS * 2 * PEER_HALF, tm), BF16),
                        pltpu.VMEM((PEER_TOPK, tm), F32), pltpu.VMEM((PEER_TOPK, tm), F32)],
        compiler_params=_params("parallel"),
        name="peer_route",
    )(h2, wq_t, k1, k2)


ROWS_PER_STEP = 16
EXPERTS_PER_STEP = ROWS_PER_STEP * N_KEYS
ROWS_PER_SUB = 4
EXPERTS_PER_SUB = ROWS_PER_SUB * N_KEYS
SUBS_PER_MIX = 4
DENSE_CHUNK = 2 * LANES


def _bf16_row_tile(row):
    tile = jnp.broadcast_to(row, (BF16_ROWS, row.shape[1])).astype(BF16)
    return jnp.concatenate([tile] * (N_KEYS // BF16_ROWS), axis=0)


def _dense_kernel(ht_ref, u_ref, vt_ref, cnt_ref, e1_ref, r2_ref, e2_ref, h_ref, g_ref, b_ref,
                  y_ref, acc_sc, act_sc, coef_sc, *, alpha):
    e = pl.program_id(1)
    tm = act_sc.shape[1]
    n_sub = ROWS_PER_STEP // ROWS_PER_SUB

    @pl.when(e == 0)
    def _():
        acc_sc[...] = jnp.zeros_like(acc_sc)

    def activations(sub):
        rows = slice(sub * EXPERTS_PER_SUB, (sub + 1) * EXPERTS_PER_SUB)
        act_sc[rows, :] = jnp.dot(u_ref[rows, :], ht_ref[...], preferred_element_type=F32)

    def tiles(sub):
        chunk = min(DENSE_CHUNK, tm)
        for il in range(sub * ROWS_PER_SUB, (sub + 1) * ROWS_PER_SUB):
            for c in range(tm // chunk):
                yield il, slice(il * N_KEYS, (il + 1) * N_KEYS), slice(c * chunk, (c + 1) * chunk)

    def routing_weights(sub):
        for il, rows, lanes in tiles(sub):
            w = None
            for hd in range(PEER_HEADS):
                cnt = _bf16_row_tile(cnt_ref[hd, il:il + 1, lanes])
                e1 = _bf16_row_tile(e1_ref[hd, il:il + 1, lanes])
                term = jnp.where(r2_ref[hd, :, lanes] < cnt, e2_ref[hd, :, lanes] * e1, jnp.zeros((), BF16))
                w = term if w is None else w + term
            coef_sc[rows, lanes] = w

    def coefficients(sub):
        for _, rows, lanes in tiles(sub):
            coef_sc[rows, lanes] = coef_sc[rows, lanes] * _gelu_tanh(act_sc[rows, lanes].astype(BF16))

    def mix(first, last):
        rows = slice(first * EXPERTS_PER_SUB, (last + 1) * EXPERTS_PER_SUB)
        acc_sc[...] += jnp.dot(vt_ref[:, rows], coef_sc[rows, :], preferred_element_type=F32)

    routing_weights(0)
    activations(0)
    for sub in range(n_sub):
        if sub + 1 < n_sub:
            routing_weights(sub + 1)
            activations(sub + 1)
        coefficients(sub)
        if sub % SUBS_PER_MIX == SUBS_PER_MIX - 1:
            mix(sub - SUBS_PER_MIX + 1, sub)

    @pl.when(e == pl.num_programs(1) - 1)
    def _():
        peer = jnp.transpose(acc_sc[...])
        y_ref[...] = _layer_norm(alpha * h_ref[...] + peer, g_ref[...], b_ref[...])


def _dense(ht, u_b, vt_b, cnt1, e1n, rank2, e2, h2, g, bta, alpha, tm):
    n = h2.shape[0]
    n_e = N_EXPERTS // EXPERTS_PER_STEP
    full = lambda a: pl.BlockSpec(a.shape, lambda t, e: (0,) * a.ndim)
    rows_blk = pl.BlockSpec((PEER_HEADS, ROWS_PER_STEP, tm), lambda t, e: (0, e, t))
    cols_blk = pl.BlockSpec((PEER_HEADS, N_KEYS, tm), lambda t, e: (0, 0, t))
    return pl.pallas_call(
        functools.partial(_dense_kernel, alpha=alpha),
        grid=(n // tm, n_e),
        in_specs=[
            pl.BlockSpec((D_MODEL, tm), lambda t, e: (0, t)),
            pl.BlockSpec((EXPERTS_PER_STEP, D_MODEL), lambda t, e: (e, 0)),
            pl.BlockSpec((D_MODEL, EXPERTS_PER_STEP), lambda t, e: (0, e)),
            rows_blk, rows_blk, cols_blk, cols_blk,
            pl.BlockSpec((tm, D_MODEL), lambda t, e: (t, 0)),
            full(g), full(bta),
        ],
        out_specs=pl.BlockSpec((tm, D_MODEL), lambda t, e: (t, 0)),
        out_shape=jax.ShapeDtypeStruct((n, D_MODEL), F32),
        scratch_shapes=[pltpu.VMEM((D_MODEL, tm), F32), pltpu.VMEM((EXPERTS_PER_STEP, tm), F32),
                        pltpu.VMEM((EXPERTS_PER_STEP, tm), BF16)],
        compiler_params=_params("parallel", "arbitrary"),
        name="peer_dense",
    )(ht, u_b, vt_b, cnt1, e1n, rank2, e2, h2, g, bta)


def _block_diag(w):
    nb, bi, bo = w.shape
    eye = jnp.eye(nb, dtype=w.dtype)
    return (eye[:, None, :, None] * w[:, :, None, :]).reshape(nb * bi, nb * bo)


def _prep_weights(w_in, b_forget, conv_w, conv_b, w_rg_a, b_rg_a, w_rg_x, b_rg_x, lru_lambda,
                  w_attn_up, w_rnn_up, w_out, ln1_g, ln1_b, peer_w_query, peer_keys_1, peer_keys_2,
                  peer_u, peer_v, ln2_g, ln2_b):
    c_f = 3 * D_ATTN
    w_in_p = jnp.concatenate(
        [w_in[:, :c_f], jnp.pad(w_in[:, c_f:c_f + N_ATTN_HEADS], ((0, 0), (0, F_PAD - N_ATTN_HEADS))),
         w_in[:, c_f + N_ATTN_HEADS:]], axis=1).astype(BF16)
    row = lambda a: a.reshape(1, -1).astype(F32)
    return dict(
        w_in=w_in_p, b_forget=b_forget.reshape(-1, 1).astype(F32), conv_w=conv_w.astype(F32), conv_b=row(conv_b),
        wa=_block_diag(w_rg_a).astype(BF16), ba=row(b_rg_a), wx=_block_diag(w_rg_x).astype(BF16),
        bx=row(b_rg_x), lam=row(lru_lambda),
        wau=w_attn_up.astype(BF16), wru=w_rnn_up.astype(BF16), wo=w_out.astype(BF16),
        ln1_g=row(ln1_g), ln1_b=row(ln1_b),
        wq_t=jnp.transpose(peer_w_query).astype(BF16), k1=peer_keys_1.astype(BF16),
        k2=peer_keys_2.astype(BF16), u=peer_u.astype(BF16), vt=jnp.transpose(peer_v).astype(BF16),
        ln2_g=row(ln2_g), ln2_b=row(ln2_b),
    )


def _pick_block(n, target):
    blk = min(n, target)
    assert n % blk == 0, (n, blk)
    return blk


def _trunk_layer(x, past_k, past_v, past_logf, conv_hist, h0, p, alpha):
    bsz, t, _ = x.shape
    n = bsz * t
    x2 = x.reshape(n, D_MODEL)
    q, k, v, kb, vb, lf, xr, gg, sa, sr = _inproj(x2, p["w_in"], p["b_forget"], bsz, t, _pick_block(n, 256))

    def state_layout(a):
        if a.ndim == 3:
            return jnp.transpose(a.reshape(bsz, N_ATTN_HEADS, ATTN_HEAD_DIM, t), (0, 3, 1, 2))
        return a.reshape(bsz, t, N_ATTN_HEADS, ATTN_HEAD_DIM)

    tq = _pick_block(t, ATTN_Q_BLOCK)
    n_past = 0 if past_k is None else past_k.shape[1]
    t_all = n_past + t
    t_lanes = -(-t_all // LANES) * LANES
    tk = ATTN_K_BLOCK if t_lanes % ATTN_K_BLOCK == 0 else t_lanes
    t_pad = -(-t_all // tk) * tk
    lf_bht = jnp.transpose(lf.reshape(N_ATTN_HEADS, bsz, t), (1, 0, 2))
    lf_all = lf_bht
    kb3 = kb.reshape(bsz, t, D_ATTN)
    vb3 = vb.reshape(bsz, t, D_ATTN)
    if past_k is not None:
        lf_all = jnp.concatenate([jnp.transpose(past_logf.astype(F32), (0, 2, 1)), lf_bht], axis=2)
        kb3 = jnp.concatenate([past_k.reshape(bsz, n_past, D_ATTN).astype(BF16), kb3], axis=1)
        vb3 = jnp.concatenate([past_v.reshape(bsz, n_past, D_ATTN).astype(BF16), vb3], axis=1)
    pad = ((0, 0), (0, t_pad - t_all), (0, 0))
    kb3, vb3 = jnp.pad(kb3, pad), jnp.pad(vb3, pad)
    f_t = _cumsum_time(jnp.pad(lf_all, ((0, 0), (0, 0), (0, t_pad - t_all))))
    f_g = f_t.reshape(bsz, N_HEAD_GROUPS, HEADS_PER_GROUP, t_pad)
    fq = jnp.transpose(f_g[:, :, :, n_past:n_past + t], (0, 1, 3, 2))
    fk = jnp.transpose(f_g.reshape(bsz, N_HEAD_GROUPS, HEADS_PER_GROUP, t_pad // tk, tk), (0, 1, 3, 2, 4))
    o = _attention(q.reshape(bsz, t, D_ATTN), kb3, vb3, fq, fk, tq=tq, tk=tk, q_off=n_past)

    hist8 = jnp.pad(conv_hist.astype(F32), ((0, 0), (SUBLANES - (CONV_WIDTH - 1), 0), (0, 0)))
    xr3 = xr.reshape(bsz, t, D_RNN)
    rnn_out, h_last = _rnn(xr3, gg.reshape(bsz, t, D_RNN), hist8, h0.astype(F32).reshape(bsz, 1, D_RNN),
                           p["conv_w"], p["conv_b"], p["wa"], p["ba"], p["wx"], p["bx"], p["lam"],
                           _pick_block(t, 256))
    new_hist = jnp.concatenate([conv_hist.astype(F32), xr3], axis=1)[:, -(CONV_WIDTH - 1):]

    h = _outproj(o.reshape(n, D_ATTN), rnn_out.reshape(n, D_RNN), sa, sr, x2, p["wau"], p["wru"], p["wo"],
                 p["ln1_g"], p["ln1_b"], alpha, _pick_block(n, 256))
    ht, cnt1, e1n, rank2, e2 = _route(h, p["wq_t"], p["k1"], p["k2"], _pick_block(n, ROUTE_TOKENS))
    y = _dense(ht, p["u"], p["vt"], cnt1, e1n, rank2, e2, h, p["ln2_g"], p["ln2_b"], alpha,
               _pick_block(n, 512))
    return (y.reshape(bsz, t, D_MODEL), state_layout(k), state_layout(v), jnp.transpose(lf_bht, (0, 2, 1)), new_hist,
            h_last.reshape(bsz, D_RNN))


def kernel(x_prompt, x_sample, cache_k, cache_v, cache_logf, state_conv, state_rnn, w_in, b_forget, conv_w, conv_b, w_rg_a, b_rg_a, w_rg_x, b_rg_x, lru_lambda, w_attn_up, w_rnn_up, w_out, ln1_g, ln1_b, peer_w_query, peer_keys_1, peer_keys_2, peer_u, peer_v, ln2_g, ln2_b):
    depth = w_in.shape[0]
    alpha = (2 * depth) ** 0.25
    layer_weights = (w_in, b_forget, conv_w, conv_b, w_rg_a, b_rg_a, w_rg_x, b_rg_x, lru_lambda, w_attn_up,
                     w_rnn_up, w_out, ln1_g, ln1_b, peer_w_query, peer_keys_1, peer_keys_2, peer_u, peer_v,
                     ln2_g, ln2_b)
    hp, hs = x_prompt, x_sample
    prompt_state, sample_state = [], []
    for l in range(depth):
        p = _prep_weights(*(w[l] for w in layer_weights))
        zero_hist = jnp.zeros((hp.shape[0], CONV_WIDTH - 1, D_RNN), F32)
        zero_h = jnp.zeros((hp.shape[0], D_RNN), F32)
        hp, *st_p = _trunk_layer(hp, None, None, None, zero_hist, zero_h, p, alpha)
        hs, *st_s = _trunk_layer(hs, cache_k[l], cache_v[l], cache_logf[l], state_conv[l], state_rnn[l], p,
                                 alpha)
        prompt_state.append(st_p)
        sample_state.append(st_s)
    stack = lambda states, i: jnp.stack([s[i] for s in states])
    return (hp, hs) + tuple(stack(prompt_state, i) for i in range(5)) + tuple(
        stack(sample_state, i) for i in range(5))
```

```python
import functools
import math

import jax
import jax.numpy as jnp
from jax import lax
from jax.experimental import pallas as pl
from jax.experimental.pallas import tpu as pltpu

F32 = jnp.float32
BF16 = jnp.bfloat16

D_MODEL = 1024
N_ATTN_HEADS = 8
ATTN_HEAD_DIM = 64
D_ATTN = N_ATTN_HEADS * ATTN_HEAD_DIM
ATTN_SCALE = ATTN_HEAD_DIM ** -0.5
D_RNN = 512
CONV_WIDTH = 4
LRU_C = 8.0
N_KEYS = 128
N_EXPERTS = N_KEYS * N_KEYS
PEER_HEADS = 8
PEER_TOPK = 16
PEER_HALF = 128
LN_EPS = 1e-5

LANES = 128
SUBLANES = 8
BF16_ROWS = 2 * SUBLANES
VMEM_LIMIT_BYTES = 56 * 1024 * 1024

HEADS_PER_GROUP = 4
GROUP_LANES = HEADS_PER_GROUP * ATTN_HEAD_DIM
N_HEAD_GROUPS = N_ATTN_HEADS // HEADS_PER_GROUP
ATTN_Q_BLOCK = 512
ATTN_K_BLOCK = 512
ATTN_ROW_BLOCK = 32
F_PAD = LANES
NEG_BIG = -1e30
LOG2_E = math.log2(math.e)
NOT_SELECTED_RANK = 99.0

_C_Q = 0
_C_K = _C_Q + D_ATTN
_C_V = _C_K + D_ATTN
_C_F = _C_V + D_ATTN
_C_XR = _C_F + F_PAD
_C_GATE = _C_XR + D_RNN
_C_GA = _C_GATE + D_RNN
_C_GR = _C_GA + D_MODEL
_C_END = _C_GR + D_MODEL


def _params(*sem):
    return pltpu.CompilerParams(dimension_semantics=sem, vmem_limit_bytes=VMEM_LIMIT_BYTES)


def _sigmoid(x):
    return 1.0 / (1.0 + jnp.exp(-x))


def _gelu_tanh(x):
    half = 0.5 * x
    return half + half * jnp.tanh(x * (0.7978845608028654 + 0.035677408136300125 * (x * x)))


def _softplus(x):
    return jnp.maximum(x, 0.0) + jnp.log1p(jnp.exp(-jnp.abs(x)))


def _layer_norm(x, g, b):
    mu = jnp.mean(x, axis=-1, keepdims=True)
    xc = x - mu
    var = jnp.mean(xc * xc, axis=-1, keepdims=True)
    return xc * lax.rsqrt(var + LN_EPS) * g + b


def _inproj_kernel(x_ref, w_ref, bf_ref, q_ref, k_ref, v_ref, kb_ref, vb_ref, lf_ref, xr_ref,
                   gg_ref, sa_ref, sr_ref, *, time_minor_kv):
    xb = x_ref[...].astype(BF16)

    def mm(lo, hi):
        return jnp.dot(xb, w_ref[:, lo:hi], preferred_element_type=F32)

    q_ref[...] = (mm(_C_Q, _C_K) * (ATTN_SCALE * LOG2_E)).astype(BF16)
    k = mm(_C_K, _C_V)
    kb_ref[...] = k.astype(BF16)
    v = mm(_C_V, _C_F)
    vb_ref[...] = v.astype(BF16)
    if time_minor_kv:
        k_ref[0] = jnp.transpose(k)
        v_ref[0] = jnp.transpose(v)
    else:
        k_ref[...] = k
        v_ref[...] = v
    f = jnp.transpose(mm(_C_F, _C_XR))[:N_ATTN_HEADS] + bf_ref[...]
    lf_ref[...] = -_softplus(-f)
    xr_ref[...] = mm(_C_XR, _C_GATE)
    gg_ref[...] = _gelu_tanh(mm(_C_GATE, _C_GA)).astype(BF16)
    sa_ref[...] = _sigmoid(mm(_C_GA, _C_GR)).astype(BF16)
    sr_ref[...] = _sigmoid(mm(_C_GR, _C_END)).astype(BF16)


def _inproj(x2, w_in_p, b_forget, bsz, t, tm):
    n = x2.shape[0]
    row = lambda w: pl.BlockSpec((tm, w), lambda i: (i, 0))
    full = lambda a: pl.BlockSpec(a.shape, lambda i: (0,) * a.ndim)
    time_minor_kv = tm % LANES == 0 and t % tm == 0
    if time_minor_kv:
        kv_shape = jax.ShapeDtypeStruct((bsz, D_ATTN, t), F32)
        kv_spec = pl.BlockSpec((1, D_ATTN, tm), lambda i: (i // (t // tm), 0, i % (t // tm)))
    else:
        kv_shape = jax.ShapeDtypeStruct((n, D_ATTN), F32)
        kv_spec = row(D_ATTN)
    out_shape = (
        jax.ShapeDtypeStruct((n, D_ATTN), BF16),
        kv_shape,
        kv_shape,
        jax.ShapeDtypeStruct((n, D_ATTN), BF16),
        jax.ShapeDtypeStruct((n, D_ATTN), BF16),
        jax.ShapeDtypeStruct((N_ATTN_HEADS, n), F32),
        jax.ShapeDtypeStruct((n, D_RNN), F32),
        jax.ShapeDtypeStruct((n, D_RNN), BF16),
        jax.ShapeDtypeStruct((n, D_MODEL), BF16),
        jax.ShapeDtypeStruct((n, D_MODEL), BF16),
    )
    out_specs = (row(D_ATTN), kv_spec, kv_spec, row(D_ATTN), row(D_ATTN),
                 pl.BlockSpec((N_ATTN_HEADS, tm), lambda i: (0, i)),
                 row(D_RNN), row(D_RNN), row(D_MODEL), row(D_MODEL))
    return pl.pallas_call(
        functools.partial(_inproj_kernel, time_minor_kv=time_minor_kv),
        grid=(n // tm,),
        in_specs=[row(D_MODEL), full(w_in_p), full(b_forget)],
        out_specs=out_specs,
        out_shape=out_shape,
        compiler_params=_params("parallel"),
        name="inproj",
    )(x2, w_in_p, b_forget)


def _cumsum_kernel(x_ref, o_ref):
    x = x_ref[0]
    t = x.shape[1]
    lane = lax.broadcasted_iota(jnp.int32, x.shape, 1)
    d = 1
    while d < t:
        x = x + jnp.where(lane >= d, pltpu.roll(x, d, axis=1), 0.0)
        d *= 2
    o_ref[0] = x * LOG2_E


def _cumsum_time(lf_t):
    b, h, t = lf_t.shape
    spec = pl.BlockSpec((1, h, t), lambda i: (i, 0, 0))
    return pl.pallas_call(
        _cumsum_kernel, grid=(b,), in_specs=[spec], out_specs=spec,
        out_shape=jax.ShapeDtypeStruct(lf_t.shape, F32),
        compiler_params=_params("parallel"), name="logf_cumsum",
    )(lf_t)


def _attn_kernel(q_ref, k_ref, v_ref, fq_ref, fk_ref, o_ref, q_sc, fq_sc, s_sc, p_sc, acc_sc, m_sc, al_sc,
                 lp_sc, *, tq, tk, q_off):
    qi = pl.program_id(2)
    row0 = q_off + qi * tq
    n_full = (row0 + 1) // tk
    heads = range(HEADS_PER_GROUP)
    rb = min(ATTN_ROW_BLOCK, tq)
    lane = lax.broadcasted_iota(jnp.int32, (tq, GROUP_LANES), 1)
    in_head = [(lane >= j * ATTN_HEAD_DIM) & (lane < (j + 1) * ATTN_HEAD_DIM) for j in heads]
    q = q_ref[0]
    for j in heads:
        q_sc[j * tq:(j + 1) * tq, :] = jnp.where(in_head[j], q, jnp.zeros_like(q))
        fq_sc[j * tq:(j + 1) * tq, :] = jnp.broadcast_to(fq_ref[0, 0, :, j:j + 1], (tq, LANES))
    m_sc[...] = jnp.full_like(m_sc, NEG_BIG)
    lp_sc[...] = jnp.zeros_like(lp_sc)
    acc_sc[...] = jnp.zeros_like(acc_sc)
    halves = [slice(0, 2 * tq), slice(2 * tq, 4 * tq)]
    n_lane_tiles = tk // LANES

    def scores(c, s_sc):
        ks = k_ref[0, pl.ds(pl.multiple_of(c * tk, tk), tk), :]
        for hv in halves:
            s_sc[hv, :] = lax.dot_general(q_sc[hv, :], ks, (((1,), (1,)), ((), ())),
                                          preferred_element_type=F32)

    def absorb(c, s_sc, masked):
        start = pl.multiple_of(c * tk, tk)
        fk = fk_ref[0, 0, c]
        vs = v_ref[0, pl.ds(start, tk), :]

        def weighted_values(hv):
            pv = jnp.dot(p_sc[hv, :], vs, preferred_element_type=F32)
            alpha = al_sc[hv, :]
            acc_sc[hv, :] = jnp.concatenate([alpha] * (GROUP_LANES // LANES), axis=1) * acc_sc[hv, :] + pv

        for j in heads:
            for r in range(tq // rb):
                rows = slice(j * tq + r * rb, j * tq + (r + 1) * rb)
                s = s_sc[rows, :] - fk[j:j + 1, :]
                if masked:
                    q_pos = row0 + r * rb + lax.broadcasted_iota(jnp.int32, (rb, tk), 0)
                    k_pos = c * tk + lax.broadcasted_iota(jnp.int32, (rb, tk), 1)
                    s = jnp.where(k_pos <= q_pos, s, NEG_BIG)
                tiles = [s[:, t * LANES:(t + 1) * LANES] for t in range(n_lane_tiles)]
                row_max = jnp.max(functools.reduce(jnp.maximum, tiles), axis=1, keepdims=True)
                fq = fq_sc[rows, :]
                m_old = m_sc[rows, :]
                m_new = jnp.maximum(m_old, jnp.broadcast_to(row_max, (rb, LANES)) + fq)
                alpha = jnp.exp2(m_old - m_new)
                shift = fq - m_new
                p_tiles = [jnp.exp2(tile + shift) for tile in tiles]
                m_sc[rows, :] = m_new
                al_sc[rows, :] = alpha
                lp_sc[rows, :] = alpha * lp_sc[rows, :] + functools.reduce(jnp.add, p_tiles)
                p_sc[rows, :] = jnp.concatenate(p_tiles, axis=1).astype(BF16)
            if j % 2 == 1:
                weighted_values(halves[j // 2])

    def step(c, carry):
        scores(c, s_sc)
        absorb(c, s_sc, masked=False)
        return carry

    lax.fori_loop(0, n_full, step, 0)
    scores(n_full, s_sc)
    absorb(n_full, s_sc, masked=True)
    o_all = acc_sc[...] / jnp.sum(lp_sc[...], axis=1, keepdims=True)
    out = jnp.zeros((tq, GROUP_LANES), F32)
    for j in heads:
        out = jnp.where(in_head[j], o_all[j * tq:(j + 1) * tq], out)
    o_ref[0] = out.astype(BF16)


def _attention(q, k_all, v_all, fq, fk, *, tq, tk, q_off):
    b, t, _ = q.shape
    t_k = k_all.shape[1]
    n_kv = t_k // tk
    assert all((q_off + i * tq) % tk + tq <= tk for i in range(t // tq)), "a query block straddles key chunks"
    kern = functools.partial(_attn_kernel, tq=tq, tk=tk, q_off=q_off)
    rows = HEADS_PER_GROUP * tq
    return pl.pallas_call(
        kern,
        grid=(b, N_HEAD_GROUPS, t // tq),
        in_specs=[
            pl.BlockSpec((1, tq, GROUP_LANES), lambda bi, g, i: (bi, i, g)),
            pl.BlockSpec((1, t_k, GROUP_LANES), lambda bi, g, i: (bi, 0, g)),
            pl.BlockSpec((1, t_k, GROUP_LANES), lambda bi, g, i: (bi, 0, g)),
            pl.BlockSpec((1, 1, tq, HEADS_PER_GROUP), lambda bi, g, i: (bi, g, i, 0)),
            pl.BlockSpec((1, 1, n_kv, HEADS_PER_GROUP, tk), lambda bi, g, i: (bi, g, 0, 0, 0)),
        ],
        out_specs=pl.BlockSpec((1, tq, GROUP_LANES), lambda bi, g, i: (bi, i, g)),
        out_shape=jax.ShapeDtypeStruct((b, t, D_ATTN), BF16),
        scratch_shapes=[
            pltpu.VMEM((rows, GROUP_LANES), BF16),
            pltpu.VMEM((rows, LANES), F32),
            pltpu.VMEM((rows, tk), F32),
            pltpu.VMEM((rows, tk), BF16),
            pltpu.VMEM((rows, GROUP_LANES), F32),
            pltpu.VMEM((rows, LANES), F32),
            pltpu.VMEM((rows, LANES), F32),
            pltpu.VMEM((rows, LANES), F32),
        ],
        compiler_params=_params("parallel", "parallel", "arbitrary"),
        name="fox_attention",
    )(q, k_all, v_all, fq, fk)


def _rnn_kernel(xr_ref, gg_ref, hist_ref, h0_ref, cw_ref, cb_ref, wa_ref, ba_ref, wx_ref, bx_ref,
                lam_ref, out_ref, hl_ref, win_sc, h_sc, *, tb):
    t = pl.program_id(1)

    @pl.when(t == 0)
    def _():
        win_sc[:SUBLANES, :] = hist_ref[0]
        h_sc[...] = h0_ref[0]

    x = xr_ref[0]
    win_sc[SUBLANES:, :] = x
    xc = x * cw_ref[CONV_WIDTH - 1:CONV_WIDTH, :] + cb_ref[...]
    for s in range(1, CONV_WIDTH):
        xc = xc + win_sc[SUBLANES - s:SUBLANES - s + tb, :] * cw_ref[CONV_WIDTH - 1 - s:CONV_WIDTH - s, :]
    win_sc[:SUBLANES, :] = x[tb - SUBLANES:tb]

    xcb = xc.astype(BF16)
    r = _sigmoid(jnp.dot(xcb, wa_ref[...], preferred_element_type=F32) + ba_ref[...])
    ig = _sigmoid(jnp.dot(xcb, wx_ref[...], preferred_element_type=F32) + bx_ref[...])
    log_a = (-LRU_C) * r * _softplus(-lam_ref[...])
    a = jnp.exp(log_a)
    one_minus_a2 = 1.0 - a * a
    scale = jnp.where(one_minus_a2 > 0.0, one_minus_a2 * lax.rsqrt(one_minus_a2), 0.0)
    bterm = scale * ig * xc

    grouped = (tb // SUBLANES, SUBLANES, D_RNN)
    a, bterm = a.reshape(grouped), bterm.reshape(grouped)
    row_in_group = lax.broadcasted_iota(jnp.int32, grouped, 1)
    d = 1
    while d < SUBLANES:
        valid = row_in_group >= d
        a_s = pltpu.roll(a, d, axis=1)
        b_s = pltpu.roll(bterm, d, axis=1)
        bterm = jnp.where(valid, a * b_s + bterm, bterm)
        a = jnp.where(valid, a * a_s, a)
        d *= 2
    h_last = h_sc[...]
    groups = []
    for g in range(tb // SUBLANES):
        h_g = bterm[g] + a[g] * h_last
        h_last = h_g[SUBLANES - 1:SUBLANES]
        groups.append(h_g)
    h = jnp.concatenate(groups, axis=0)
    h_sc[...] = h_last
    hl_ref[0] = h_last
    out_ref[0] = (h * gg_ref[0].astype(F32)).astype(BF16)


def _rnn(xr, gg, hist8, h0, conv_w, conv_b, wa, ba, wx, bx, lam, tb):
    b, t, _ = xr.shape
    blk = pl.BlockSpec((1, tb, D_RNN), lambda bi, ti: (bi, ti, 0))
    full = lambda a: pl.BlockSpec(a.shape, lambda bi, ti: (0,) * a.ndim)
    per_b = lambda r: pl.BlockSpec((1, r, D_RNN), lambda bi, ti: (bi, 0, 0))
    return pl.pallas_call(
        functools.partial(_rnn_kernel, tb=tb),
        grid=(b, t // tb),
        in_specs=[blk, blk, per_b(SUBLANES), per_b(1), full(conv_w), full(conv_b), full(wa), full(ba),
                  full(wx), full(bx), full(lam)],
        out_specs=(blk, per_b(1)),
        out_shape=(jax.ShapeDtypeStruct((b, t, D_RNN), BF16), jax.ShapeDtypeStruct((b, 1, D_RNN), F32)),
        scratch_shapes=[pltpu.VMEM((SUBLANES + tb, D_RNN), F32), pltpu.VMEM((1, D_RNN), F32)],
        compiler_params=_params("parallel", "arbitrary"),
        name="conv_rglru",
    )(xr, gg, hist8, h0, conv_w, conv_b, wa, ba, wx, bx, lam)


def _outproj_kernel(o_ref, r_ref, sa_ref, sr_ref, x_ref, wau_ref, wru_ref, wo_ref, g_ref, b_ref,
                    h_ref, *, alpha):
    up_a = jnp.dot(o_ref[...], wau_ref[...], preferred_element_type=F32)
    up_r = jnp.dot(r_ref[...], wru_ref[...], preferred_element_type=F32)
    merged = sa_ref[...].astype(F32) * up_a + sr_ref[...].astype(F32) * up_r
    mix = jnp.dot(merged.astype(BF16), wo_ref[...], preferred_element_type=F32)
    h_ref[...] = _layer_norm(alpha * x_ref[...] + mix, g_ref[...], b_ref[...])


def _outproj(o2, r2, sa, sr, x2, wau, wru, wo, g, bta, alpha, tm):
    n = x2.shape[0]
    row = lambda w: pl.BlockSpec((tm, w), lambda i: (i, 0))
    full = lambda a: pl.BlockSpec(a.shape, lambda i: (0,) * a.ndim)
    return pl.pallas_call(
        functools.partial(_outproj_kernel, alpha=alpha),
        grid=(n // tm,),
        in_specs=[row(D_ATTN), row(D_RNN), row(D_MODEL), row(D_MODEL), row(D_MODEL), full(wau), full(wru),
                  full(wo), full(g), full(bta)],
        out_specs=row(D_MODEL),
        out_shape=jax.ShapeDtypeStruct((n, D_MODEL), F32),
        compiler_params=_params("parallel"),
        name="outproj_ln1",
    )(o2, r2, sa, sr, x2, wau, wru, wo, g, bta)


ROUTE_TOKENS = 2 * LANES


def _odd_even_merge_sort(n):
    def merge(lo, hi, r):
        step = r * 2
        if step < hi - lo:
            yield from merge(lo, hi, step)
            yield from merge(lo + r, hi, step)
            yield from ((i, i + r) for i in range(lo + r, hi - r, step))
        else:
            yield (lo, lo + r)

    def sort(lo, hi):
        if hi - lo >= 1:
            mid = lo + (hi - lo) // 2
            yield from sort(lo, mid)
            yield from sort(mid + 1, hi)
            yield from merge(lo, hi, 1)

    return tuple(sort(0, n - 1))


_SORT_TOPK = _odd_even_merge_sort(PEER_TOPK)


def _compare_exchange(v, i, j):
    v[i], v[j] = jnp.maximum(v[i], v[j]), jnp.minimum(v[i], v[j])


def _sort_bitonic(v):
    d = PEER_TOPK // 2
    while d >= 1:
        for i in range(PEER_TOPK):
            if i & d == 0:
                _compare_exchange(v, i, i + d)
        d //= 2
    return v


def _top_k_sorted(tiles, presorted=False):
    v = list(tiles)
    if not presorted:
        for i, j in _SORT_TOPK:
            _compare_exchange(v, i, j)
    shift = SUBLANES // 2
    while shift >= 1:
        other = [pltpu.roll(x, shift, axis=0) for x in v]
        v = _sort_bitonic([jnp.maximum(v[i], other[PEER_TOPK - 1 - i]) for i in range(PEER_TOPK)])
        shift //= 2
    return v


def _count_greater(x, t):
    assert len(t) == 16, "the bisection below is written out for 16 entries"
    one = lambda m, w: jnp.where(m, float(w), 0.0)
    b3 = t[7] > x
    b2 = jnp.where(b3, t[11], t[3]) > x
    b1 = jnp.where(b3, jnp.where(b2, t[13], t[9]), jnp.where(b2, t[5], t[1])) > x
    hi = jnp.where(b2, jnp.where(b1, t[14], t[12]), jnp.where(b1, t[10], t[8]))
    lo = jnp.where(b2, jnp.where(b1, t[6], t[4]), jnp.where(b1, t[2], t[0]))
    b0 = jnp.where(b3, hi, lo) > x
    count = one(b3, 8) + one(b2, 4) + one(b1, 2) + one(b0, 1)
    return jnp.where(t[15] > x, float(PEER_TOPK), count)


def _route_kernel(h_ref, wq_ref, k1_ref, k2_ref, ht_ref, cnt_ref, e1_ref, r2_ref, e2_ref,
                  qt_sc, t1_sc, t2_sc, *, tm):
    ht = jnp.transpose(h_ref[...]).astype(BF16)
    ht_ref[...] = ht
    qt_sc[...] = jnp.dot(wq_ref[...], ht, preferred_element_type=F32).astype(BF16)
    key_iota = lax.broadcasted_iota(jnp.int32, (N_KEYS, tm), 0).astype(F32)
    top_iota = lax.broadcasted_iota(jnp.int32, (PEER_TOPK, tm), 0).astype(F32)
    front_rows = SUBLANES

    def scores(hd):
        base = hd * 2 * PEER_HALF
        s1 = jnp.dot(k1_ref[...], qt_sc[base:base + PEER_HALF], preferred_element_type=F32)
        s2 = jnp.dot(k2_ref[...], qt_sc[base + PEER_HALF:base + 2 * PEER_HALF], preferred_element_type=F32)
        return s1, s2

    sub = lax.broadcasted_iota(jnp.int32, (SUBLANES, tm), 0)
    tiles_of = lambda s: [s[r * SUBLANES:(r + 1) * SUBLANES] for r in range(N_KEYS // SUBLANES)]
    sublane_sum = lambda x: jnp.sum(x, axis=0, keepdims=True)

    def route_head_sorted(hd):
        s1, s2 = scores(hd)
        rows1, rows2 = tiles_of(s1), tiles_of(s2)
        t1 = _top_k_sorted(rows1)
        t2 = _top_k_sorted(rows2)
        t1_lo = t1[SUBLANES - 1]
        for a in range(SUBLANES - 2, -1, -1):
            t1_lo = jnp.where(sub == a, t1[a], t1_lo)
        cand = [t1_lo + t2[b] for b in range(PEER_TOPK)]
        top_lo = _top_k_sorted(cand, presorted=True)
        hi_sums = [t1[a] + t2[0] for a in range(SUBLANES, PEER_TOPK)]
        top = _sort_bitonic(top_lo[:SUBLANES] + [jnp.maximum(top_lo[i], hi_sums[PEER_TOPK - 1 - i])
                                                 for i in range(SUBLANES, PEER_TOPK)])
        tau = top[PEER_TOPK - 1]
        cnt_lo = functools.reduce(jnp.add, [jnp.where(c >= tau, 1.0, 0.0) for c in cand])
        cnt = [jnp.broadcast_to(cnt_lo[a:a + 1], (SUBLANES, tm)) for a in range(SUBLANES)]
        cnt += [jnp.where(s >= tau, 1.0, 0.0) for s in hi_sums]
        z = functools.reduce(jnp.add, [jnp.exp(t - top[0]) for t in top])

        cnt1_rows, rank2_rows = [], []
        for r in range(N_KEYS // SUBLANES):
            c = jnp.where(rows1[r] + t2[0] >= tau, 1.0, 0.0)
            c = jnp.where(rows1[r] >= t1[PEER_TOPK - 1], c, 0.0)
            for a in range(SUBLANES - 1, -1, -1):
                c = jnp.where(rows1[r] >= t1[a], cnt[a], c)
            cnt1_rows.append(c)
            g = _count_greater(rows2[r], t2)
            rank2_rows.append(jnp.where(g < float(PEER_TOPK), g, NOT_SELECTED_RANK))
        cnt1 = jnp.concatenate(cnt1_rows, axis=0)
        rank2 = jnp.concatenate(rank2_rows, axis=0)
        cnt_ref[hd] = cnt1
        e1_ref[hd] = jnp.exp(s1 - t1[0][0:1]) / z[0:1]
        r2_ref[hd] = rank2.astype(BF16)
        e2_ref[hd] = jnp.exp(s2 - t2[0][0:1]).astype(BF16)

        gap = lambda t: functools.reduce(jnp.minimum, [t[b] - t[b + 1] for b in range(PEER_TOPK - 1)])[0:1]
        n_cnt = sublane_sum(cnt_lo) + functools.reduce(jnp.add, cnt[SUBLANES:])[0:1]
        n_sel = sublane_sum(functools.reduce(jnp.add, cnt1_rows))
        n_rank = sublane_sum(functools.reduce(
            jnp.add, [jnp.where(x < float(PEER_TOPK), 1.0, 0.0) for x in rank2_rows]))
        off = lambda n: jnp.abs(n - float(PEER_TOPK))
        return (off(n_cnt) + off(n_sel) + off(n_rank)
                + jnp.where(jnp.minimum(gap(t1), gap(t2)) > 0.0, 0.0, 1.0))

    def route_head_exact(hd):
        def pick_one(v, iota, n):
            m = jnp.max(v, axis=0, keepdims=True)
            return m, iota == jnp.min(jnp.where(v == m, iota, float(n)), axis=0, keepdims=True)

        s1, s2 = scores(hd)

        def extract(a, carry):
            v1, r1, v2, r2 = carry
            m1, sel1 = pick_one(v1, key_iota, N_KEYS)
            m2, sel2 = pick_one(v2, key_iota, N_KEYS)
            t1_sc[pl.ds(a, 1), :] = m1
            t2_sc[pl.ds(a, 1), :] = m2
            af = jnp.asarray(a, dtype=F32)
            return (jnp.where(sel1, -jnp.inf, v1), jnp.where(sel1, af, r1),
                    jnp.where(sel2, -jnp.inf, v2), jnp.where(sel2, af, r2))

        no_rank = jnp.full((N_KEYS, tm), NOT_SELECTED_RANK, F32)
        _, rank1, _, rank2 = lax.fori_loop(0, PEER_TOPK, extract, (s1, no_rank, s2, no_rank))
        t1 = t1_sc[...]
        t2 = t2_sc[...]
        top0 = t1[0:1] + t2[0:1]

        def pick(_, carry):
            ptr, front, z = carry
            m, sel = pick_one(front, top_iota, PEER_TOPK)
            ptr = ptr + jnp.where(sel, 1.0, 0.0)
            lo = ptr[:front_rows]
            nxt = jnp.full(lo.shape, -jnp.inf, F32)
            for b in range(1, PEER_TOPK):
                nxt = jnp.where(lo == float(b), t2_sc[b:b + 1, :], nxt)
            nxt = jnp.where(lo == 0.0, t2[0:1], nxt)
            front = jnp.concatenate(
                [t1[:front_rows] + nxt, jnp.where(sel[front_rows:], -jnp.inf, front[front_rows:])], axis=0)
            return ptr, front, z + jnp.exp(m - top0)

        cnt, _, z = lax.fori_loop(
            0, PEER_TOPK, pick,
            (jnp.zeros((PEER_TOPK, tm), F32), t1 + t2[0:1], jnp.zeros((1, tm), F32)))

        cnt1 = jnp.zeros((N_KEYS, tm), F32)
        for a in range(PEER_TOPK):
            cnt1 = jnp.where(rank1 == float(a), cnt[a:a + 1], cnt1)
        cnt_ref[hd] = cnt1
        e1_ref[hd] = jnp.exp(s1 - t1[0:1]) / z
        r2_ref[hd] = rank2.astype(BF16)
        e2_ref[hd] = jnp.exp(s2 - t2[0:1]).astype(BF16)

    doubt = [route_head_sorted(hd) for hd in range(PEER_HEADS)]

    @pl.when(jnp.max(functools.reduce(jnp.maximum, doubt)) > 0.0)
    def _():
        for hd in range(PEER_HEADS):
            @pl.when(jnp.max(doubt[hd]) > 0.0)
            def _():
                route_head_exact(hd)


def _route(h2, wq_t, k1, k2, tm):
    n = h2.shape[0]
    full = lambda a: pl.BlockSpec(a.shape, lambda i: (0,) * a.ndim)
    per_head = pl.BlockSpec((PEER_HEADS, N_KEYS, tm), lambda i: (0, 0, i))
    hshape = lambda dt: jax.ShapeDtypeStruct((PEER_HEADS, N_KEYS, n), dt)
    return pl.pallas_call(
        functools.partial(_route_kernel, tm=tm),
        grid=(n // tm,),
        in_specs=[pl.BlockSpec((tm, D_MODEL), lambda i: (i, 0)), full(wq_t), full(k1), full(k2)],
        out_specs=(pl.BlockSpec((D_MODEL, tm), lambda i: (0, i)), per_head, per_head, per_head, per_head),
        out_shape=(jax.ShapeDtypeStruct((D_MODEL, n), BF16), hshape(F32), hshape(F32), hshape(BF16),
                   hshape(BF16)),
        scratch_shapes=[pltpu.VMEM((PEER_HEADS * 2 * PEER_HALF, tm), BF16),
                        pltpu.VMEM((PEER_TOPK, tm), F32), pltpu.VMEM((PEER_TOPK, tm), F32)],
        compiler_params=_params("parallel"),
        name="peer_route",
    )(h2, wq_t, k1, k2)


ROWS_PER_STEP = 16
EXPERTS_PER_STEP = ROWS_PER_STEP * N_KEYS
ROWS_PER_SUB = 4
EXPERTS_PER_SUB = ROWS_PER_SUB * N_KEYS
SUBS_PER_MIX = 4
DENSE_CHUNK = 2 * LANES


def _bf16_row_tile(row):
    tile = jnp.broadcast_to(row, (BF16_ROWS, row.shape[1])).astype(BF16)
    return jnp.concatenate([tile] * (N_KEYS // BF16_ROWS), axis=0)


def _dense_kernel(ht_ref, u_ref, vt_ref, cnt_ref, e1_ref, r2_ref, e2_ref, h_ref, g_ref, b_ref,
                  y_ref, acc_sc, act_sc, coef_sc, *, alpha):
    e = pl.program_id(1)
    tm = act_sc.shape[1]
    n_sub = ROWS_PER_STEP // ROWS_PER_SUB

    @pl.when(e == 0)
    def _():
        acc_sc[...] = jnp.zeros_like(acc_sc)

    def activations(sub):
        rows = slice(sub * EXPERTS_PER_SUB, (sub + 1) * EXPERTS_PER_SUB)
        act_sc[rows, :] = jnp.dot(u_ref[rows, :], ht_ref[...], preferred_element_type=F32)

    def tiles(sub):
        chunk = min(DENSE_CHUNK, tm)
        for il in range(sub * ROWS_PER_SUB, (sub + 1) * ROWS_PER_SUB):
            for c in range(tm // chunk):
                yield il, slice(il * N_KEYS, (il + 1) * N_KEYS), slice(c * chunk, (c + 1) * chunk)

    def routing_weights(sub):
        for il, rows, lanes in tiles(sub):
            w = None
            for hd in range(PEER_HEADS):
                cnt = _bf16_row_tile(cnt_ref[hd, il:il + 1, lanes])
                e1 = _bf16_row_tile(e1_ref[hd, il:il + 1, lanes])
                term = jnp.where(r2_ref[hd, :, lanes] < cnt, e2_ref[hd, :, lanes] * e1, jnp.zeros((), BF16))
                w = term if w is None else w + term
            coef_sc[rows, lanes] = w

    def coefficients(sub):
        for _, rows, lanes in tiles(sub):
            coef_sc[rows, lanes] = coef_sc[rows, lanes] * _gelu_tanh(act_sc[rows, lanes].astype(BF16))

    def mix(first, last):
        rows = slice(first * EXPERTS_PER_SUB, (last + 1) * EXPERTS_PER_SUB)
        acc_sc[...] += jnp.dot(vt_ref[:, rows], coef_sc[rows, :], preferred_element_type=F32)

    routing_weights(0)
    activations(0)
    for sub in range(n_sub):
        if sub + 1 < n_sub:
            routing_weights(sub + 1)
            activations(sub + 1)
        coefficients(sub)
        if sub % SUBS_PER_MIX == SUBS_PER_MIX - 1:
            mix(sub - SUBS_PER_MIX + 1, sub)

    @pl.when(e == pl.num_programs(1) - 1)
    def _():
        peer = jnp.transpose(acc_sc[...])
        y_ref[...] = _layer_norm(alpha * h_ref[...] + peer, g_ref[...], b_ref[...])


def _dense(ht, u_b, vt_b, cnt1, e1n, rank2, e2, h2, g, bta, alpha, tm):
    n = h2.shape[0]
    n_e = N_EXPERTS // EXPERTS_PER_STEP
    full = lambda a: pl.BlockSpec(a.shape, lambda t, e: (0,) * a.ndim)
    rows_blk = pl.BlockSpec((PEER_HEADS, ROWS_PER_STEP, tm), lambda t, e: (0, e, t))
    cols_blk = pl.BlockSpec((PEER_HEADS, N_KEYS, tm), lambda t, e: (0, 0, t))
    return pl.pallas_call(
        functools.partial(_dense_kernel, alpha=alpha),
        grid=(n // tm, n_e),
        in_specs=[
            pl.BlockSpec((D_MODEL, tm), lambda t, e: (0, t)),
            pl.BlockSpec((EXPERTS_PER_STEP, D_MODEL), lambda t, e: (e, 0)),
            pl.BlockSpec((D_MODEL, EXPERTS_PER_STEP), lambda t, e: (0, e)),
            rows_blk, rows_blk, cols_blk, cols_blk,
            pl.BlockSpec((tm, D_MODEL), lambda t, e: (t, 0)),
            full(g), full(bta),
        ],
        out_specs=pl.BlockSpec((tm, D_MODEL), lambda t, e: (t, 0)),
        out_shape=jax.ShapeDtypeStruct((n, D_MODEL), F32),
        scratch_shapes=[pltpu.VMEM((D_MODEL, tm), F32), pltpu.VMEM((EXPERTS_PER_STEP, tm), F32),
                        pltpu.VMEM((EXPERTS_PER_STEP, tm), BF16)],
        compiler_params=_params("parallel", "arbitrary"),
        name="peer_dense",
    )(ht, u_b, vt_b, cnt1, e1n, rank2, e2, h2, g, bta)


def _block_diag(w):
    nb, bi, bo = w.shape
    eye = jnp.eye(nb, dtype=w.dtype)
    return (eye[:, None, :, None] * w[:, :, None, :]).reshape(nb * bi, nb * bo)


def _prep_weights(w_in, b_forget, conv_w, conv_b, w_rg_a, b_rg_a, w_rg_x, b_rg_x, lru_lambda,
                  w_attn_up, w_rnn_up, w_out, ln1_g, ln1_b, peer_w_query, peer_keys_1, peer_keys_2,
                  peer_u, peer_v, ln2_g, ln2_b):
    c_f = 3 * D_ATTN
    w_in_p = jnp.concatenate(
        [w_in[:, :c_f], jnp.pad(w_in[:, c_f:c_f + N_ATTN_HEADS], ((0, 0), (0, F_PAD - N_ATTN_HEADS))),
         w_in[:, c_f + N_ATTN_HEADS:]], axis=1).astype(BF16)
    row = lambda a: a.reshape(1, -1).astype(F32)
    return dict(
        w_in=w_in_p, b_forget=b_forget.reshape(-1, 1).astype(F32), conv_w=conv_w.astype(F32), conv_b=row(conv_b),
        wa=_block_diag(w_rg_a).astype(BF16), ba=row(b_rg_a), wx=_block_diag(w_rg_x).astype(BF16),
        bx=row(b_rg_x), lam=row(lru_lambda),
        wau=w_attn_up.astype(BF16), wru=w_rnn_up.astype(BF16), wo=w_out.astype(BF16),
        ln1_g=row(ln1_g), ln1_b=row(ln1_b),
        wq_t=jnp.transpose(peer_w_query).astype(BF16), k1=peer_keys_1.astype(BF16),
        k2=peer_keys_2.astype(BF16), u=peer_u.astype(BF16), vt=jnp.transpose(peer_v).astype(BF16),
        ln2_g=row(ln2_g), ln2_b=row(ln2_b),
    )


def _pick_block(n, target):
    blk = min(n, target)
    assert n % blk == 0, (n, blk)
    return blk


def _trunk_layer(x, past_k, past_v, past_logf, conv_hist, h0, p, alpha):
    bsz, t, _ = x.shape
    n = bsz * t
    x2 = x.reshape(n, D_MODEL)
    q, k, v, kb, vb, lf, xr, gg, sa, sr = _inproj(x2, p["w_in"], p["b_forget"], bsz, t, _pick_block(n, 256))

    def state_layout(a):
        if a.ndim == 3:
            return jnp.transpose(a.reshape(bsz, N_ATTN_HEADS, ATTN_HEAD_DIM, t), (0, 3, 1, 2))
        return a.reshape(bsz, t, N_ATTN_HEADS, ATTN_HEAD_DIM)

    tq = _pick_block(t, ATTN_Q_BLOCK)
    n_past = 0 if past_k is None else past_k.shape[1]
    t_all = n_past + t
    t_lanes = -(-t_all // LANES) * LANES
    tk = ATTN_K_BLOCK if t_lanes % ATTN_K_BLOCK == 0 else t_lanes
    t_pad = -(-t_all // tk) * tk
    lf_bht = jnp.transpose(lf.reshape(N_ATTN_HEADS, bsz, t), (1, 0, 2))
    lf_all = lf_bht
    kb3 = kb.reshape(bsz, t, D_ATTN)
    vb3 = vb.reshape(bsz, t, D_ATTN)
    if past_k is not None:
        lf_all = jnp.concatenate([jnp.transpose(past_logf.astype(F32), (0, 2, 1)), lf_bht], axis=2)
        kb3 = jnp.concatenate([past_k.reshape(bsz, n_past, D_ATTN).astype(BF16), kb3], axis=1)
        vb3 = jnp.concatenate([past_v.reshape(bsz, n_past, D_ATTN).astype(BF16), vb3], axis=1)
    pad = ((0, 0), (0, t_pad - t_all), (0, 0))
    kb3, vb3 = jnp.pad(kb3, pad), jnp.pad(vb3, pad)
    f_t = _cumsum_time(jnp.pad(lf_all, ((0, 0), (0, 0), (0, t_pad - t_all))))
    f_g = f_t.reshape(bsz, N_HEAD_GROUPS, HEADS_PER_GROUP, t_pad)
    fq = jnp.transpose(f_g[:, :, :, n_past:n_past + t], (0, 1, 3, 2))
    fk = jnp.transpose(f_g.reshape(bsz, N_HEAD_GROUPS, HEADS_PER_GROUP, t_pad // tk, tk), (0, 1, 3, 2, 4))
    o = _attention(q.reshape(bsz, t, D_ATTN), kb3, vb3, fq, fk, tq=tq, tk=tk, q_off=n_past)

    hist8 = jnp.pad(conv_hist.astype(F32), ((0, 0), (SUBLANES - (CONV_WIDTH - 1), 0), (0, 0)))
    xr3 = xr.reshape(bsz, t, D_RNN)
    rnn_out, h_last = _rnn(xr3, gg.reshape(bsz, t, D_RNN), hist8, h0.astype(F32).reshape(bsz, 1, D_RNN),
                           p["conv_w"], p["conv_b"], p["wa"], p["ba"], p["wx"], p["bx"], p["lam"],
                           _pick_block(t, 256))
    new_hist = jnp.concatenate([conv_hist.astype(F32), xr3], axis=1)[:, -(CONV_WIDTH - 1):]

    h = _outproj(o.reshape(n, D_ATTN), rnn_out.reshape(n, D_RNN), sa, sr, x2, p["wau"], p["wru"], p["wo"],
                 p["ln1_g"], p["ln1_b"], alpha, _pick_block(n, 256))
    ht, cnt1, e1n, rank2, e2 = _route(h, p["wq_t"], p["k1"], p["k2"], _pick_block(n, ROUTE_TOKENS))
    y = _dense(ht, p["u"], p["vt"], cnt1, e1n, rank2, e2, h, p["ln2_g"], p["ln2_b"], alpha,
               _pick_block(n, 512))
    return (y.reshape(bsz, t, D_MODEL), state_layout(k), state_layout(v), jnp.transpose(lf_bht, (0, 2, 1)), new_hist,
            h_last.reshape(bsz, D_RNN))


def kernel(x_prompt, x_sample, cache_k, cache_v, cache_logf, state_conv, state_rnn, w_in, b_forget, conv_w, conv_b, w_rg_a, b_rg_a, w_rg_x, b_rg_x, lru_lambda, w_attn_up, w_rnn_up, w_out, ln1_g, ln1_b, peer_w_query, peer_keys_1, peer_keys_2, peer_u, peer_v, ln2_g, ln2_b):
    depth = w_in.shape[0]
    alpha = (2 * depth) ** 0.25
    layer_weights = (w_in, b_forget, conv_w, conv_b, w_rg_a, b_rg_a, w_rg_x, b_rg_x, lru_lambda, w_attn_up,
                     w_rnn_up, w_out, ln1_g, ln1_b, peer_w_query, peer_keys_1, peer_keys_2, peer_u, peer_v,
                     ln2_g, ln2_b)
    hp, hs = x_prompt, x_sample
    prompt_state, sample_state = [], []
    for l in range(depth):
        p = _prep_weights(*(w[l] for w in layer_weights))
        zero_hist = jnp.zeros((hp.shape[0], CONV_WIDTH - 1, D_RNN), F32)
        zero_h = jnp.zeros((hp.shape[0], D_RNN), F32)
        hp, *st_p = _trunk_layer(hp, None, None, None, zero_hist, zero_h, p, alpha)
        hs, *st_s = _trunk_layer(hs, cache_k[l], cache_v[l], cache_logf[l], state_conv[l], state_rnn[l], p,
                                 alpha)
        prompt_state.append(st_p)
        sample_state.append(st_s)
    stack = lambda states, i: jnp.stack([s[i] for s in states])
    return (hp, hs) + tuple(stack(prompt_state, i) for i in range(5)) + tuple(
        stack(sample_state, i) for i in range(5))
```

```python
import functools
import math

import jax
import jax.numpy as jnp
from jax import lax
from jax.experimental import pallas as pl
from jax.experimental.pallas import tpu as pltpu

F32 = jnp.float32
BF16 = jnp.bfloat16

D_MODEL = 1024
N_ATTN_HEADS = 8
ATTN_HEAD_DIM = 64
D_ATTN = N_ATTN_HEADS * ATTN_HEAD_DIM
ATTN_SCALE = ATTN_HEAD_DIM ** -0.5
D_RNN = 512
CONV_WIDTH = 4
LRU_C = 8.0
N_KEYS = 128
N_EXPERTS = N_KEYS * N_KEYS
PEER_HEADS = 8
PEER_TOPK = 16
PEER_HALF = 128
LN_EPS = 1e-5

LANES = 128
SUBLANES = 8
BF16_ROWS = 2 * SUBLANES
VMEM_LIMIT_BYTES = 56 * 1024 * 1024

HEADS_PER_GROUP = 4
GROUP_LANES = HEADS_PER_GROUP * ATTN_HEAD_DIM
N_HEAD_GROUPS = N_ATTN_HEADS // HEADS_PER_GROUP
ATTN_Q_BLOCK = 512
ATTN_K_BLOCK = 512
ATTN_ROW_BLOCK = 32
F_PAD = LANES
NEG_BIG = -1e30
LOG2_E = math.log2(math.e)
NOT_SELECTED_RANK = 99.0

_C_Q = 0
_C_K = _C_Q + D_ATTN
_C_V = _C_K + D_ATTN
_C_F = _C_V + D_ATTN
_C_XR = _C_F + F_PAD
_C_GATE = _C_XR + D_RNN
_C_GA = _C_GATE + D_RNN
_C_GR = _C_GA + D_MODEL
_C_END = _C_GR + D_MODEL


def _params(*sem):
    return pltpu.CompilerParams(dimension_semantics=sem, vmem_limit_bytes=VMEM_LIMIT_BYTES)


def _sigmoid(x):
    return 1.0 / (1.0 + jnp.exp(-x))


def _gelu_tanh(x):
    half = 0.5 * x
    return half + half * jnp.tanh(x * (0.7978845608028654 + 0.035677408136300125 * (x * x)))


def _softplus(x):
    return jnp.maximum(x, 0.0) + jnp.log1p(jnp.exp(-jnp.abs(x)))


def _layer_norm(x, g, b):
    mu = jnp.mean(x, axis=-1, keepdims=True)
    xc = x - mu
    var = jnp.mean(xc * xc, axis=-1, keepdims=True)
    return xc * lax.rsqrt(var + LN_EPS) * g + b


def _inproj_kernel(x_ref, w_ref, bf_ref, q_ref, k_ref, v_ref, kb_ref, vb_ref, lf_ref, xr_ref,
                   gg_ref, sa_ref, sr_ref, *, time_minor_kv):
    xb = x_ref[...].astype(BF16)

    def mm(lo, hi):
        return jnp.dot(xb, w_ref[:, lo:hi], preferred_element_type=F32)

    q_ref[...] = (mm(_C_Q, _C_K) * (ATTN_SCALE * LOG2_E)).astype(BF16)
    k = mm(_C_K, _C_V)
    kb_ref[...] = k.astype(BF16)
    v = mm(_C_V, _C_F)
    vb_ref[...] = v.astype(BF16)
    if time_minor_kv:
        k_ref[0] = jnp.transpose(k)
        v_ref[0] = jnp.transpose(v)
    else:
        k_ref[...] = k
        v_ref[...] = v
    f = jnp.transpose(mm(_C_F, _C_XR))[:N_ATTN_HEADS] + bf_ref[...]
    lf_ref[...] = -_softplus(-f)
    xr_ref[...] = mm(_C_XR, _C_GATE)
    gg_ref[...] = _gelu_tanh(mm(_C_GATE, _C_GA)).astype(BF16)
    sa_ref[...] = _sigmoid(mm(_C_GA, _C_GR)).astype(BF16)
    sr_ref[...] = _sigmoid(mm(_C_GR, _C_END)).astype(BF16)


def _inproj(x2, w_in_p, b_forget, bsz, t, tm):
    n = x2.shape[0]
    row = lambda w: pl.BlockSpec((tm, w), lambda i: (i, 0))
    full = lambda a: pl.BlockSpec(a.shape, lambda i: (0,) * a.ndim)
    time_minor_kv = tm % LANES == 0 and t % tm == 0
    if time_minor_kv:
        kv_shape = jax.ShapeDtypeStruct((bsz, D_ATTN, t), F32)
        kv_spec = pl.BlockSpec((1, D_ATTN, tm), lambda i: (i // (t // tm), 0, i % (t // tm)))
    else:
        kv_shape = jax.ShapeDtypeStruct((n, D_ATTN), F32)
        kv_spec = row(D_ATTN)
    out_shape = (
        jax.ShapeDtypeStruct((n, D_ATTN), BF16),
        kv_shape,
        kv_shape,
        jax.ShapeDtypeStruct((n, D_ATTN), BF16),
        jax.ShapeDtypeStruct((n, D_ATTN), BF16),
        jax.ShapeDtypeStruct((N_ATTN_HEADS, n), F32),
        jax.ShapeDtypeStruct((n, D_RNN), F32),
        jax.ShapeDtypeStruct((n, D_RNN), BF16),
        jax.ShapeDtypeStruct((n, D_MODEL), BF16),
        jax.ShapeDtypeStruct((n, D_MODEL), BF16),
    )
    out_specs = (row(D_ATTN), kv_spec, kv_spec, row(D_ATTN), row(D_ATTN),
                 pl.BlockSpec((N_ATTN_HEADS, tm), lambda i: (0, i)),
                 row(D_RNN), row(D_RNN), row(D_MODEL), row(D_MODEL))
    return pl.pallas_call(
        functools.partial(_inproj_kernel, time_minor_kv=time_minor_kv),
        grid=(n // tm,),
        in_specs=[row(D_MODEL), full(w_in_p), full(b_forget)],
        out_specs=out_specs,
        out_shape=out_shape,
        compiler_params=_params("parallel"),
        name="inproj",
    )(x2, w_in_p, b_forget)


def _cumsum_kernel(x_ref, o_ref):
    x = x_ref[0]
    t = x.shape[1]
    lane = lax.broadcasted_iota(jnp.int32, x.shape, 1)
    d = 1
    while d < t:
        x = x + jnp.where(lane >= d, pltpu.roll(x, d, axis=1), 0.0)
        d *= 2
    o_ref[0] = x * LOG2_E


def _cumsum_time(lf_t):
    b, h, t = lf_t.shape
    spec = pl.BlockSpec((1, h, t), lambda i: (i, 0, 0))
    return pl.pallas_call(
        _cumsum_kernel, grid=(b,), in_specs=[spec], out_specs=spec,
        out_shape=jax.ShapeDtypeStruct(lf_t.shape, F32),
        compiler_params=_params("parallel"), name="logf_cumsum",
    )(lf_t)


def _attn_kernel(q_ref, k_ref, v_ref, fq_ref, fk_ref, o_ref, q_sc, fq_sc, s_sc, p_sc, acc_sc, m_sc, al_sc,
                 lp_sc, *, tq, tk, q_off):
    qi = pl.program_id(2)
    row0 = q_off + qi * tq
    n_full = (row0 + 1) // tk
    heads = range(HEADS_PER_GROUP)
    rb = min(ATTN_ROW_BLOCK, tq)
    lane = lax.broadcasted_iota(jnp.int32, (tq, GROUP_LANES), 1)
    in_head = [(lane >= j * ATTN_HEAD_DIM) & (lane < (j + 1) * ATTN_HEAD_DIM) for j in heads]
    q = q_ref[0]
    for j in heads:
        q_sc[j * tq:(j + 1) * tq, :] = jnp.where(in_head[j], q, jnp.zeros_like(q))
        fq_sc[j * tq:(j + 1) * tq, :] = jnp.broadcast_to(fq_ref[0, 0, :, j:j + 1], (tq, LANES))
    m_sc[...] = jnp.full_like(m_sc, NEG_BIG)
    lp_sc[...] = jnp.zeros_like(lp_sc)
    acc_sc[...] = jnp.zeros_like(acc_sc)
    halves = [slice(0, 2 * tq), slice(2 * tq, 4 * tq)]
    n_lane_tiles = tk // LANES

    def scores(c, s_sc):
        ks = k_ref[0, pl.ds(pl.multiple_of(c * tk, tk), tk), :]
        for hv in halves:
            s_sc[hv, :] = lax.dot_general(q_sc[hv, :], ks, (((1,), (1,)), ((), ())),
                                          preferred_element_type=F32)

    def absorb(c, s_sc, masked):
        start = pl.multiple_of(c * tk, tk)
        fk = fk_ref[0, 0, c]
        vs = v_ref[0, pl.ds(start, tk), :]

        def weighted_values(hv):
            pv = jnp.dot(p_sc[hv, :], vs, preferred_element_type=F32)
            alpha = al_sc[hv, :]
            acc_sc[hv, :] = jnp.concatenate([alpha] * (GROUP_LANES // LANES), axis=1) * acc_sc[hv, :] + pv

        for j in heads:
            for r in range(tq // rb):
                rows = slice(j * tq + r * rb, j * tq + (r + 1) * rb)
                s = s_sc[rows, :] - fk[j:j + 1, :]
                if masked:
                    q_pos = row0 + r * rb + lax.broadcasted_iota(jnp.int32, (rb, tk), 0)
                    k_pos = c * tk + lax.broadcasted_iota(jnp.int32, (rb, tk), 1)
                    s = jnp.where(k_pos <= q_pos, s, NEG_BIG)
                tiles = [s[:, t * LANES:(t + 1) * LANES] for t in range(n_lane_tiles)]
                row_max = jnp.max(functools.reduce(jnp.maximum, tiles), axis=1, keepdims=True)
                fq = fq_sc[rows, :]
                m_old = m_sc[rows, :]
                m_new = jnp.maximum(m_old, jnp.broadcast_to(row_max, (rb, LANES)) + fq)
                alpha = jnp.exp2(m_old - m_new)
                shift = fq - m_new
                p_tiles = [jnp.exp2(tile + shift) for tile in tiles]
                m_sc[rows, :] = m_new
                al_sc[rows, :] = alpha
                lp_sc[rows, :] = alpha * lp_sc[rows, :] + functools.reduce(jnp.add, p_tiles)
                p_sc[rows, :] = jnp.concatenate(p_tiles, axis=1).astype(BF16)
            if j % 2 == 1:
                weighted_values(halves[j // 2])

    def step(c, carry):
        scores(c, s_sc)
        absorb(c, s_sc, masked=False)
        return carry

    lax.fori_loop(0, n_full, step, 0)
    scores(n_full, s_sc)
    absorb(n_full, s_sc, masked=True)
    o_all = acc_sc[...] / jnp.sum(lp_sc[...], axis=1, keepdims=True)
    out = jnp.zeros((tq, GROUP_LANES), F32)
    for j in heads:
        out = jnp.where(in_head[j], o_all[j * tq:(j + 1) * tq], out)
    o_ref[0] = out.astype(BF16)


def _attention(q, k_all, v_all, fq, fk, *, tq, tk, q_off):
    b, t, _ = q.shape
    t_k = k_all.shape[1]
    n_kv = t_k // tk
    assert all((q_off + i * tq) % tk + tq <= tk for i in range(t // tq)), "a query block straddles key chunks"
    kern = functools.partial(_attn_kernel, tq=tq, tk=tk, q_off=q_off)
    rows = HEADS_PER_GROUP * tq
    return pl.pallas_call(
        kern,
        grid=(b, N_HEAD_GROUPS, t // tq),
        in_specs=[
            pl.BlockSpec((1, tq, GROUP_LANES), lambda bi, g, i: (bi, i, g)),
            pl.BlockSpec((1, t_k, GROUP_LANES), lambda bi, g, i: (bi, 0, g)),
            pl.BlockSpec((1, t_k, GROUP_LANES), lambda bi, g, i: (bi, 0, g)),
            pl.BlockSpec((1, 1, tq, HEADS_PER_GROUP), lambda bi, g, i: (bi, g, i, 0)),
            pl.BlockSpec((1, 1, n_kv, HEADS_PER_GROUP, tk), lambda bi, g, i: (bi, g, 0, 0, 0)),
        ],
        out_specs=pl.BlockSpec((1, tq, GROUP_LANES), lambda bi, g, i: (bi, i, g)),
        out_shape=jax.ShapeDtypeStruct((b, t, D_ATTN), BF16),
        scratch_shapes=[
            pltpu.VMEM((rows, GROUP_LANES), BF16),
            pltpu.VMEM((rows, LANES), F32),
            pltpu.VMEM((rows, tk), F32),
            pltpu.VMEM((rows, tk), BF16),
            pltpu.VMEM((rows, GROUP_LANES), F32),
            pltpu.VMEM((rows, LANES), F32),
            pltpu.VMEM((rows, LANES), F32),
            pltpu.VMEM((rows, LANES), F32),
        ],
        compiler_params=_params("parallel", "parallel", "arbitrary"),
        name="fox_attention",
    )(q, k_all, v_all, fq, fk)


def _rnn_kernel(xr_ref, gg_ref, hist_ref, h0_ref, cw_ref, cb_ref, wa_ref, ba_ref, wx_ref, bx_ref,
                lam_ref, out_ref, hl_ref, win_sc, h_sc, *, tb):
    t = pl.program_id(1)

    @pl.when(t == 0)
    def _():
        win_sc[:SUBLANES, :] = hist_ref[0]
        h_sc[...] = h0_ref[0]

    x = xr_ref[0]
    win_sc[SUBLANES:, :] = x
    xc = x * cw_ref[CONV_WIDTH - 1:CONV_WIDTH, :] + cb_ref[...]
    for s in range(1, CONV_WIDTH):
        xc = xc + win_sc[SUBLANES - s:SUBLANES - s + tb, :] * cw_ref[CONV_WIDTH - 1 - s:CONV_WIDTH - s, :]
    win_sc[:SUBLANES, :] = x[tb - SUBLANES:tb]

    xcb = xc.astype(BF16)
    r = _sigmoid(jnp.dot(xcb, wa_ref[...], preferred_element_type=F32) + ba_ref[...])
    ig = _sigmoid(jnp.dot(xcb, wx_ref[...], preferred_element_type=F32) + bx_ref[...])
    log_a = (-LRU_C) * r * _softplus(-lam_ref[...])
    a = jnp.exp(log_a)
    one_minus_a2 = 1.0 - a * a
    scale = jnp.where(one_minus_a2 > 0.0, one_minus_a2 * lax.rsqrt(one_minus_a2), 0.0)
    bterm = scale * ig * xc

    grouped = (tb // SUBLANES, SUBLANES, D_RNN)
    a, bterm = a.reshape(grouped), bterm.reshape(grouped)
    row_in_group = lax.broadcasted_iota(jnp.int32, grouped, 1)
    d = 1
    while d < SUBLANES:
        valid = row_in_group >= d
        a_s = pltpu.roll(a, d, axis=1)
        b_s = pltpu.roll(bterm, d, axis=1)
        bterm = jnp.where(valid, a * b_s + bterm, bterm)
        a = jnp.where(valid, a * a_s, a)
        d *= 2
    h_last = h_sc[...]
    groups = []
    for g in range(tb // SUBLANES):
        h_g = bterm[g] + a[g] * h_last
        h_last = h_g[SUBLANES - 1:SUBLANES]
        groups.append(h_g)
    h = jnp.concatenate(groups, axis=0)
    h_sc[...] = h_last
    hl_ref[0] = h_last
    out_ref[0] = (h * gg_ref[0].astype(F32)).astype(BF16)


def _rnn(xr, gg, hist8, h0, conv_w, conv_b, wa, ba, wx, bx, lam, tb):
    b, t, _ = xr.shape
    blk = pl.BlockSpec((1, tb, D_RNN), lambda bi, ti: (bi, ti, 0))
    full = lambda a: pl.BlockSpec(a.shape, lambda bi, ti: (0,) * a.ndim)
    per_b = lambda r: pl.BlockSpec((1, r, D_RNN), lambda bi, ti: (bi, 0, 0))
    return pl.pallas_call(
        functools.partial(_rnn_kernel, tb=tb),
        grid=(b, t // tb),
        in_specs=[blk, blk, per_b(SUBLANES), per_b(1), full(conv_w), full(conv_b), full(wa), full(ba),
                  full(wx), full(bx), full(lam)],
        out_specs=(blk, per_b(1)),
        out_shape=(jax.ShapeDtypeStruct((b, t, D_RNN), BF16), jax.ShapeDtypeStruct((b, 1, D_RNN), F32)),
        scratch_shapes=[pltpu.VMEM((SUBLANES + tb, D_RNN), F32), pltpu.VMEM((1, D_RNN), F32)],
        compiler_params=_params("parallel", "arbitrary"),
        name="conv_rglru",
    )(xr, gg, hist8, h0, conv_w, conv_b, wa, ba, wx, bx, lam)


def _outproj_kernel(o_ref, r_ref, sa_ref, sr_ref, x_ref, wau_ref, wru_ref, wo_ref, g_ref, b_ref,
                    h_ref, *, alpha):
    up_a = jnp.dot(o_ref[...], wau_ref[...], preferred_element_type=F32)
    up_r = jnp.dot(r_ref[...], wru_ref[...], preferred_element_type=F32)
    merged = sa_ref[...].astype(F32) * up_a + sr_ref[...].astype(F32) * up_r
    mix = jnp.dot(merged.astype(BF16), wo_ref[...], preferred_element_type=F32)
    h_ref[...] = _layer_norm(alpha * x_ref[...] + mix, g_ref[...], b_ref[...])


def _outproj(o2, r2, sa, sr, x2, wau, wru, wo, g, bta, alpha, tm):
    n = x2.shape[0]
    row = lambda w: pl.BlockSpec((tm, w), lambda i: (i, 0))
    full = lambda a: pl.BlockSpec(a.shape, lambda i: (0,) * a.ndim)
    return pl.pallas_call(
        functools.partial(_outproj_kernel, alpha=alpha),
        grid=(n // tm,),
        in_specs=[row(D_ATTN), row(D_RNN), row(D_MODEL), row(D_MODEL), row(D_MODEL), full(wau), full(wru),
                  full(wo), full(g), full(bta)],
        out_specs=row(D_MODEL),
        out_shape=jax.ShapeDtypeStruct((n, D_MODEL), F32),
        compiler_params=_params("parallel"),
        name="outproj_ln1",
    )(o2, r2, sa, sr, x2, wau, wru, wo, g, bta)


ROUTE_TOKENS = 2 * LANES


def _odd_even_merge_sort(n):
    def merge(lo, hi, r):
        step = r * 2
        if step < hi - lo:
            yield from merge(lo, hi, step)
            yield from merge(lo + r, hi, step)
            yield from ((i, i + r) for i in range(lo + r, hi - r, step))
        else:
            yield (lo, lo + r)

    def sort(lo, hi):
        if hi - lo >= 1:
            mid = lo + (hi - lo) // 2
            yield from sort(lo, mid)
            yield from sort(mid + 1, hi)
            yield from merge(lo, hi, 1)

    return tuple(sort(0, n - 1))


_SORT_TOPK = _odd_even_merge_sort(PEER_TOPK)


def _compare_exchange(v, i, j):
    v[i], v[j] = jnp.maximum(v[i], v[j]), jnp.minimum(v[i], v[j])


def _sort_bitonic(v):
    d = PEER_TOPK // 2
    while d >= 1:
        for i in range(PEER_TOPK):
            if i & d == 0:
                _compare_exchange(v, i, i + d)
        d //= 2
    return v


def _top_k_sorted(tiles, presorted=False):
    v = list(tiles)
    if not presorted:
        for i, j in _SORT_TOPK:
            _compare_exchange(v, i, j)
    shift = SUBLANES // 2
    while shift >= 1:
        other = [pltpu.roll(x, shift, axis=0) for x in v]
        v = _sort_bitonic([jnp.maximum(v[i], other[PEER_TOPK - 1 - i]) for i in range(PEER_TOPK)])
        shift //= 2
    return v


def _count_greater(x, t):
    assert len(t) == 16, "the bisection below is written out for 16 entries"
    one = lambda m, w: jnp.where(m, float(w), 0.0)
    b3 = t[7] > x
    b2 = jnp.where(b3, t[11], t[3]) > x
    b1 = jnp.where(b3, jnp.where(b2, t[13], t[9]), jnp.where(b2, t[5], t[1])) > x
    hi = jnp.where(b2, jnp.where(b1, t[14], t[12]), jnp.where(b1, t[10], t[8]))
    lo = jnp.where(b2, jnp.where(b1, t[6], t[4]), jnp.where(b1, t[2], t[0]))
    b0 = jnp.where(b3, hi, lo) > x
    count = one(b3, 8) + one(b2, 4) + one(b1, 2) + one(b0, 1)
    return jnp.where(t[15] > x, float(PEER_TOPK), count)


def _route_kernel(h_ref, wq_ref, k1_ref, k2_ref, ht_ref, cnt_ref, e1_ref, r2_ref, e2_ref,
                  qt_sc, t1_sc, t2_sc, *, tm):
    ht = jnp.transpose(h_ref[...]).astype(BF16)
    ht_ref[...] = ht
    qt_sc[...] = jnp.dot(wq_ref[...], ht, preferred_element_type=F32).astype(BF16)
    key_iota = lax.broadcasted_iota(jnp.int32, (N_KEYS, tm), 0).astype(F32)
    top_iota = lax.broadcasted_iota(jnp.int32, (PEER_TOPK, tm), 0).astype(F32)
    front_rows = SUBLANES

    def scores(hd):
        base = hd * 2 * PEER_HALF
        s1 = jnp.dot(k1_ref[...], qt_sc[base:base + PEER_HALF], preferred_element_type=F32)
        s2 = jnp.dot(k2_ref[...], qt_sc[base + PEER_HALF:base + 2 * PEER_HALF], preferred_element_type=F32)
        return s1, s2

    sub = lax.broadcasted_iota(jnp.int32, (SUBLANES, tm), 0)
    tiles_of = lambda s: [s[r * SUBLANES:(r + 1) * SUBLANES] for r in range(N_KEYS // SUBLANES)]
    sublane_sum = lambda x: jnp.sum(x, axis=0, keepdims=True)

    def route_head_sorted(hd):
        s1, s2 = scores(hd)
        rows1, rows2 = tiles_of(s1), tiles_of(s2)
        t1 = _top_k_sorted(rows1)
        t2 = _top_k_sorted(rows2)
        t1_lo = t1[SUBLANES - 1]
        for a in range(SUBLANES - 2, -1, -1):
            t1_lo = jnp.where(sub == a, t1[a], t1_lo)
        cand = [t1_lo + t2[b] for b in range(PEER_TOPK)]
        top_lo = _top_k_sorted(cand, presorted=True)
        hi_sums = [t1[a] + t2[0] for a in range(SUBLANES, PEER_TOPK)]
        top = _sort_bitonic(top_lo[:SUBLANES] + [jnp.maximum(top_lo[i], hi_sums[PEER_TOPK - 1 - i])
                                                 for i in range(SUBLANES, PEER_TOPK)])
        tau = top[PEER_TOPK - 1]
        cnt_lo = functools.reduce(jnp.add, [jnp.where(c >= tau, 1.0, 0.0) for c in cand])
        cnt = [jnp.broadcast_to(cnt_lo[a:a + 1], (SUBLANES, tm)) for a in range(SUBLANES)]
        cnt += [jnp.where(s >= tau, 1.0, 0.0) for s in hi_sums]
        z = functools.reduce(jnp.add, [jnp.exp(t - top[0]) for t in top])

        cnt1_rows, rank2_rows = [], []
        for r in range(N_KEYS // SUBLANES):
            c = jnp.where(rows1[r] + t2[0] >= tau, 1.0, 0.0)
            c = jnp.where(rows1[r] >= t1[PEER_TOPK - 1], c, 0.0)
            for a in range(SUBLANES - 1, -1, -1):
                c = jnp.where(rows1[r] >= t1[a], cnt[a], c)
            cnt1_rows.append(c)
            g = _count_greater(rows2[r], t2)
            rank2_rows.append(jnp.where(g < float(PEER_TOPK), g, NOT_SELECTED_RANK))
        cnt1 = jnp.concatenate(cnt1_rows, axis=0)
        rank2 = jnp.concatenate(rank2_rows, axis=0)
        cnt_ref[hd] = cnt1
        e1_ref[hd] = jnp.exp(s1 - t1[0][0:1]) / z[0:1]
        r2_ref[hd] = rank2.astype(BF16)
        e2_ref[hd] = jnp.exp(s2 - t2[0][0:1]).astype(BF16)

        gap = lambda t: functools.reduce(jnp.minimum, [t[b] - t[b + 1] for b in range(PEER_TOPK - 1)])[0:1]
        n_cnt = sublane_sum(cnt_lo) + functools.reduce(jnp.add, cnt[SUBLANES:])[0:1]
        n_sel = sublane_sum(functools.reduce(jnp.add, cnt1_rows))
        n_rank = sublane_sum(functools.reduce(
            jnp.add, [jnp.where(x < float(PEER_TOPK), 1.0, 0.0) for x in rank2_rows]))
        off = lambda n: jnp.abs(n - float(PEER_TOPK))
        return (off(n_cnt) + off(n_sel) + off(n_rank)
                + jnp.where(jnp.minimum(gap(t1), gap(t2)) > 0.0, 0.0, 1.0))

    def route_head_exact(hd):
        def pick_one(v, iota, n):
            m = jnp.max(v, axis=0, keepdims=True)
            return m, iota == jnp.min(jnp.where(v == m, iota, float(n)), axis=0, keepdims=True)

        s1, s2 = scores(hd)

        def extract(a, carry):
            v1, r1, v2, r2 = carry
            m1, sel1 = pick_one(v1, key_iota, N_KEYS)
            m2, sel2 = pick_one(v2, key_iota, N_KEYS)
            t1_sc[pl.ds(a, 1), :] = m1
            t2_sc[pl.ds(a, 1), :] = m2
            af = jnp.asarray(a, dtype=F32)
            return (jnp.where(sel1, -jnp.inf, v1), jnp.where(sel1, af, r1),
                    jnp.where(sel2, -jnp.inf, v2), jnp.where(sel2, af, r2))

        no_rank = jnp.full((N_KEYS, tm), NOT_SELECTED_RANK, F32)
        _, rank1, _, rank2 = lax.fori_loop(0, PEER_TOPK, extract, (s1, no_rank, s2, no_rank))
        t1 = t1_sc[...]
        t2 = t2_sc[...]
        top0 = t1[0:1] + t2[0:1]

        def pick(_, carry):
            ptr, front, z = carry
            m, sel = pick_one(front, top_iota, PEER_TOPK)
            ptr = ptr + jnp.where(sel, 1.0, 0.0)
            lo = ptr[:front_rows]
            nxt = jnp.full(lo.shape, -jnp.inf, F32)
            for b in range(1, PEER_TOPK):
                nxt = jnp.where(lo == float(b), t2_sc[b:b + 1, :], nxt)
            nxt = jnp.where(lo == 0.0, t2[0:1], nxt)
            front = jnp.concatenate(
                [t1[:front_rows] + nxt, jnp.where(sel[front_rows:], -jnp.inf, front[front_rows:])], axis=0)
            return ptr, front, z + jnp.exp(m - top0)

        cnt, _, z = lax.fori_loop(
            0, PEER_TOPK, pick,
            (jnp.zeros((PEER_TOPK, tm), F32), t1 + t2[0:1], jnp.zeros((1, tm), F32)))

        cnt1 = jnp.zeros((N_KEYS, tm), F32)
        for a in range(PEER_TOPK):
            cnt1 = jnp.where(rank1 == float(a), cnt[a:a + 1], cnt1)
        cnt_ref[hd] = cnt1
        e1_ref[hd] = jnp.exp(s1 - t1[0:1]) / z
        r2_ref[hd] = rank2.astype(BF16)
        e2_ref[hd] = jnp.exp(s2 - t2[0:1]).astype(BF16)

    doubt = [route_head_sorted(hd) for hd in range(PEER_HEADS)]

    @pl.when(jnp.max(functools.reduce(jnp.maximum, doubt)) > 0.0)
    def _():
        for hd in range(PEER_HEADS):
            @pl.when(jnp.max(doubt[hd]) > 0.0)
            def _():
                route_head_exact(hd)


def _route(h2, wq_t, k1, k2, tm):
    n = h2.shape[0]
    full = lambda a: pl.BlockSpec(a.shape, lambda i: (0,) * a.ndim)
    per_head = pl.BlockSpec((PEER_HEADS, N_KEYS, tm), lambda i: (0, 0, i))
    hshape = lambda dt: jax.ShapeDtypeStruct((PEER_HEADS, N_KEYS, n), dt)
    return pl.pallas_call(
        functools.partial(_route_kernel, tm=tm),
        grid=(n // tm,),
        in_specs=[pl.BlockSpec((tm, D_MODEL), lambda i: (i, 0)), full(wq_t), full(k1), full(k2)],
        out_specs=(pl.BlockSpec((D_MODEL, tm), lambda i: (0, i)), per_head, per_head, per_head, per_head),
        out_shape=(jax.ShapeDtypeStruct((D_MODEL, n), BF16), hshape(F32), hshape(F32), hshape(BF16),
                   hshape(BF16)),
        scratch_shapes=[pltpu.VMEM((PEER_HEADS * 2 * PEER_HALF, tm), BF16),
                        pltpu.VMEM((PEER_TOPK, tm), F32), pltpu.VMEM((PEER_TOPK, tm), F32)],
        compiler_params=_params("parallel"),
        name="peer_route",
    )(h2, wq_t, k1, k2)


ROWS_PER_STEP = 16
EXPERTS_PER_STEP = ROWS_PER_STEP * N_KEYS
ROWS_PER_SUB = 8
EXPERTS_PER_SUB = ROWS_PER_SUB * N_KEYS
DENSE_CHUNK = 2 * LANES


def _bf16_row_tile(row):
    tile = jnp.broadcast_to(row, (BF16_ROWS, row.shape[1])).astype(BF16)
    return jnp.concatenate([tile] * (N_KEYS // BF16_ROWS), axis=0)


def _dense_kernel(ht_ref, u_ref, vt_ref, cnt_ref, e1_ref, r2_ref, e2_ref, h_ref, g_ref, b_ref,
                  y_ref, acc_sc, act_sc, coef_sc, *, alpha):
    e = pl.program_id(1)
    tm = act_sc.shape[1]
    n_sub = ROWS_PER_STEP // ROWS_PER_SUB

    @pl.when(e == 0)
    def _():
        acc_sc[...] = jnp.zeros_like(acc_sc)

    def activations(sub):
        rows = slice(sub * EXPERTS_PER_SUB, (sub + 1) * EXPERTS_PER_SUB)
        act_sc[rows, :] = jnp.dot(u_ref[rows, :], ht_ref[...], preferred_element_type=F32)

    def tiles(sub):
        chunk = min(DENSE_CHUNK, tm)
        for il in range(sub * ROWS_PER_SUB, (sub + 1) * ROWS_PER_SUB):
            for c in range(tm // chunk):
                yield il, slice(il * N_KEYS, (il + 1) * N_KEYS), slice(c * chunk, (c + 1) * chunk)

    def routing_weights(sub):
        for il, rows, lanes in tiles(sub):
            w = None
            for hd in range(PEER_HEADS):
                cnt = _bf16_row_tile(cnt_ref[hd, il:il + 1, lanes])
                e1 = _bf16_row_tile(e1_ref[hd, il:il + 1, lanes])
                term = jnp.where(r2_ref[hd, :, lanes] < cnt, e2_ref[hd, :, lanes] * e1, jnp.zeros((), BF16))
                w = term if w is None else w + term
            coef_sc[rows, lanes] = w

    def coefficients(sub):
        for _, rows, lanes in tiles(sub):
            coef_sc[rows, lanes] = coef_sc[rows, lanes] * _gelu_tanh(act_sc[rows, lanes].astype(BF16))

    routing_weights(0)
    activations(0)
    for sub in range(n_sub):
        if sub + 1 < n_sub:
            routing_weights(sub + 1)
            activations(sub + 1)
        coefficients(sub)
    acc_sc[...] += jnp.dot(vt_ref[...], coef_sc[...], preferred_element_type=F32)

    @pl.when(e == pl.num_programs(1) - 1)
    def _():
        peer = jnp.transpose(acc_sc[...])
        y_ref[...] = _layer_norm(alpha * h_ref[...] + peer, g_ref[...], b_ref[...])


def _dense(ht, u_b, vt_b, cnt1, e1n, rank2, e2, h2, g, bta, alpha, tm):
    n = h2.shape[0]
    n_e = N_EXPERTS // EXPERTS_PER_STEP
    full = lambda a: pl.BlockSpec(a.shape, lambda t, e: (0,) * a.ndim)
    rows_blk = pl.BlockSpec((PEER_HEADS, ROWS_PER_STEP, tm), lambda t, e: (0, e, t))
    cols_blk = pl.BlockSpec((PEER_HEADS, N_KEYS, tm), lambda t, e: (0, 0, t))
    return pl.pallas_call(
        functools.partial(_dense_kernel, alpha=alpha),
        grid=(n // tm, n_e),
        in_specs=[
            pl.BlockSpec((D_MODEL, tm), lambda t, e: (0, t)),
            pl.BlockSpec((EXPERTS_PER_STEP, D_MODEL), lambda t, e: (e, 0)),
            pl.BlockSpec((D_MODEL, EXPERTS_PER_STEP), lambda t, e: (0, e)),
            rows_blk, rows_blk, cols_blk, cols_blk,
            pl.BlockSpec((tm, D_MODEL), lambda t, e: (t, 0)),
            full(g), full(bta),
        ],
        out_specs=pl.BlockSpec((tm, D_MODEL), lambda t, e: (t, 0)),
        out_shape=jax.ShapeDtypeStruct((n, D_MODEL), F32),
        scratch_shapes=[pltpu.VMEM((D_MODEL, tm), F32), pltpu.VMEM((EXPERTS_PER_STEP, tm), F32),
                        pltpu.VMEM((EXPERTS_PER_STEP, tm), BF16)],
        compiler_params=_params("parallel", "arbitrary"),
        name="peer_dense",
    )(ht, u_b, vt_b, cnt1, e1n, rank2, e2, h2, g, bta)


def _block_diag(w):
    nb, bi, bo = w.shape
    eye = jnp.eye(nb, dtype=w.dtype)
    return (eye[:, None, :, None] * w[:, :, None, :]).reshape(nb * bi, nb * bo)


def _prep_weights(w_in, b_forget, conv_w, conv_b, w_rg_a, b_rg_a, w_rg_x, b_rg_x, lru_lambda,
                  w_attn_up, w_rnn_up, w_out, ln1_g, ln1_b, peer_w_query, peer_keys_1, peer_keys_2,
                  peer_u, peer_v, ln2_g, ln2_b):
    c_f = 3 * D_ATTN
    w_in_p = jnp.concatenate(
        [w_in[:, :c_f], jnp.pad(w_in[:, c_f:c_f + N_ATTN_HEADS], ((0, 0), (0, F_PAD - N_ATTN_HEADS))),
         w_in[:, c_f + N_ATTN_HEADS:]], axis=1).astype(BF16)
    row = lambda a: a.reshape(1, -1).astype(F32)
    return dict(
        w_in=w_in_p, b_forget=b_forget.reshape(-1, 1).astype(F32), conv_w=conv_w.astype(F32), conv_b=row(conv_b),
        wa=_block_diag(w_rg_a).astype(BF16), ba=row(b_rg_a), wx=_block_diag(w_rg_x).astype(BF16),
        bx=row(b_rg_x), lam=row(lru_lambda),
        wau=w_attn_up.astype(BF16), wru=w_rnn_up.astype(BF16), wo=w_out.astype(BF16),
        ln1_g=row(ln1_g), ln1_b=row(ln1_b),
        wq_t=jnp.transpose(peer_w_query).astype(BF16), k1=peer_keys_1.astype(BF16),
        k2=peer_keys_2.astype(BF16), u=peer_u.astype(BF16), vt=jnp.transpose(peer_v).astype(BF16),
        ln2_g=row(ln2_g), ln2_b=row(ln2_b),
    )


def _pick_block(n, target):
    blk = min(n, target)
    assert n % blk == 0, (n, blk)
    return blk


def _trunk_layer(x, past_k, past_v, past_logf, conv_hist, h0, p, alpha):
    bsz, t, _ = x.shape
    n = bsz * t
    x2 = x.reshape(n, D_MODEL)
    q, k, v, kb, vb, lf, xr, gg, sa, sr = _inproj(x2, p["w_in"], p["b_forget"], bsz, t, _pick_block(n, 256))

    def state_layout(a):
        if a.ndim == 3:
            return jnp.transpose(a.reshape(bsz, N_ATTN_HEADS, ATTN_HEAD_DIM, t), (0, 3, 1, 2))
        return a.reshape(bsz, t, N_ATTN_HEADS, ATTN_HEAD_DIM)

    tq = _pick_block(t, ATTN_Q_BLOCK)
    n_past = 0 if past_k is None else past_k.shape[1]
    t_all = n_past + t
    t_lanes = -(-t_all // LANES) * LANES
    tk = ATTN_K_BLOCK if t_lanes % ATTN_K_BLOCK == 0 else t_lanes
    t_pad = -(-t_all // tk) * tk
    lf_bht = jnp.transpose(lf.reshape(N_ATTN_HEADS, bsz, t), (1, 0, 2))
    lf_all = lf_bht
    kb3 = kb.reshape(bsz, t, D_ATTN)
    vb3 = vb.reshape(bsz, t, D_ATTN)
    if past_k is not None:
        lf_all = jnp.concatenate([jnp.transpose(past_logf.astype(F32), (0, 2, 1)), lf_bht], axis=2)
        kb3 = jnp.concatenate([past_k.reshape(bsz, n_past, D_ATTN).astype(BF16), kb3], axis=1)
        vb3 = jnp.concatenate([past_v.reshape(bsz, n_past, D_ATTN).astype(BF16), vb3], axis=1)
    pad = ((0, 0), (0, t_pad - t_all), (0, 0))
    kb3, vb3 = jnp.pad(kb3, pad), jnp.pad(vb3, pad)
    f_t = _cumsum_time(jnp.pad(lf_all, ((0, 0), (0, 0), (0, t_pad - t_all))))
    f_g = f_t.reshape(bsz, N_HEAD_GROUPS, HEADS_PER_GROUP, t_pad)
    fq = jnp.transpose(f_g[:, :, :, n_past:n_past + t], (0, 1, 3, 2))
    fk = jnp.transpose(f_g.reshape(bsz, N_HEAD_GROUPS, HEADS_PER_GROUP, t_pad // tk, tk), (0, 1, 3, 2, 4))
    o = _attention(q.reshape(bsz, t, D_ATTN), kb3, vb3, fq, fk, tq=tq, tk=tk, q_off=n_past)

    hist8 = jnp.pad(conv_hist.astype(F32), ((0, 0), (SUBLANES - (CONV_WIDTH - 1), 0), (0, 0)))
    xr3 = xr.reshape(bsz, t, D_RNN)
    rnn_out, h_last = _rnn(xr3, gg.reshape(bsz, t, D_RNN), hist8, h0.astype(F32).reshape(bsz, 1, D_RNN),
                           p["conv_w"], p["conv_b"], p["wa"], p["ba"], p["wx"], p["bx"], p["lam"],
                           _pick_block(t, 256))
    new_hist = jnp.concatenate([conv_hist.astype(F32), xr3], axis=1)[:, -(CONV_WIDTH - 1):]

    h = _outproj(o.reshape(n, D_ATTN), rnn_out.reshape(n, D_RNN), sa, sr, x2, p["wau"], p["wru"], p["wo"],
                 p["ln1_g"], p["ln1_b"], alpha, _pick_block(n, 512))
    ht, cnt1, e1n, rank2, e2 = _route(h, p["wq_t"], p["k1"], p["k2"], _pick_block(n, ROUTE_TOKENS))
    y = _dense(ht, p["u"], p["vt"], cnt1, e1n, rank2, e2, h, p["ln2_g"], p["ln2_b"], alpha,
               _pick_block(n, 512))
    return (y.reshape(bsz, t, D_MODEL), state_layout(k), state_layout(v), jnp.transpose(lf_bht, (0, 2, 1)), new_hist,
            h_last.reshape(bsz, D_RNN))


def kernel(x_prompt, x_sample, cache_k, cache_v, cache_logf, state_conv, state_rnn, w_in, b_forget, conv_w, conv_b, w_rg_a, b_rg_a, w_rg_x, b_rg_x, lru_lambda, w_attn_up, w_rnn_up, w_out, ln1_g, ln1_b, peer_w_query, peer_keys_1, peer_keys_2, peer_u, peer_v, ln2_g, ln2_b):
    depth = w_in.shape[0]
    alpha = (2 * depth) ** 0.25
    layer_weights = (w_in, b_forget, conv_w, conv_b, w_rg_a, b_rg_a, w_rg_x, b_rg_x, lru_lambda, w_attn_up,
                     w_rnn_up, w_out, ln1_g, ln1_b, peer_w_query, peer_keys_1, peer_keys_2, peer_u, peer_v,
                     ln2_g, ln2_b)
    hp, hs = x_prompt, x_sample
    prompt_state, sample_state = [], []
    for l in range(depth):
        p = _prep_weights(*(w[l] for w in layer_weights))
        zero_hist = jnp.zeros((hp.shape[0], CONV_WIDTH - 1, D_RNN), F32)
        zero_h = jnp.zeros((hp.shape[0], D_RNN), F32)
        hp, *st_p = _trunk_layer(hp, None, None, None, zero_hist, zero_h, p, alpha)
        hs, *st_s = _trunk_layer(hs, cache_k[l], cache_v[l], cache_logf[l], state_conv[l], state_rnn[l], p,
                                 alpha)
        prompt_state.append(st_p)
        sample_state.append(st_s)
    stack = lambda states, i: jnp.stack([s[i] for s in states])
    return (hp, hs) + tuple(stack(prompt_state, i) for i in range(5)) + tuple(
        stack(sample_state, i) for i in range(5))
```

```python
import functools
import math

import jax
import jax.numpy as jnp
from jax import lax
from jax.experimental import pallas as pl
from jax.experimental.pallas import tpu as pltpu

F32 = jnp.float32
BF16 = jnp.bfloat16

D_MODEL = 1024
N_ATTN_HEADS = 8
ATTN_HEAD_DIM = 64
D_ATTN = N_ATTN_HEADS * ATTN_HEAD_DIM
ATTN_SCALE = ATTN_HEAD_DIM ** -0.5
D_RNN = 512
CONV_WIDTH = 4
LRU_C = 8.0
N_KEYS = 128
N_EXPERTS = N_KEYS * N_KEYS
PEER_HEADS = 8
PEER_TOPK = 16
PEER_HALF = 128
LN_EPS = 1e-5

LANES = 128
SUBLANES = 8
BF16_ROWS = 2 * SUBLANES
VMEM_LIMIT_BYTES = 56 * 1024 * 1024

HEADS_PER_GROUP = 4
GROUP_LANES = HEADS_PER_GROUP * ATTN_HEAD_DIM
N_HEAD_GROUPS = N_ATTN_HEADS // HEADS_PER_GROUP
ATTN_Q_BLOCK = 512
ATTN_K_BLOCK = 512
ATTN_ROW_BLOCK = 32
F_PAD = LANES
NEG_BIG = -1e30
LOG2_E = math.log2(math.e)
NOT_SELECTED_RANK = 99.0

_C_Q = 0
_C_K = _C_Q + D_ATTN
_C_V = _C_K + D_ATTN
_C_F = _C_V + D_ATTN
_C_XR = _C_F + F_PAD
_C_GATE = _C_XR + D_RNN
_C_GA = _C_GATE + D_RNN
_C_GR = _C_GA + D_MODEL
_C_END = _C_GR + D_MODEL


def _params(*sem):
    return pltpu.CompilerParams(dimension_semantics=sem, vmem_limit_bytes=VMEM_LIMIT_BYTES)


def _sigmoid(x):
    return 1.0 / (1.0 + jnp.exp(-x))


def _gelu_tanh(x):
    half = 0.5 * x
    return half + half * jnp.tanh(x * (0.7978845608028654 + 0.035677408136300125 * (x * x)))


def _softplus(x):
    return jnp.maximum(x, 0.0) + jnp.log1p(jnp.exp(-jnp.abs(x)))


def _layer_norm(x, g, b):
    mu = jnp.mean(x, axis=-1, keepdims=True)
    xc = x - mu
    var = jnp.mean(xc * xc, axis=-1, keepdims=True)
    return xc * lax.rsqrt(var + LN_EPS) * g + b


def _inproj_kernel(x_ref, w_ref, bf_ref, q_ref, k_ref, v_ref, kb_ref, vb_ref, lf_ref, xr_ref,
                   gg_ref, sa_ref, sr_ref, *, time_minor_kv):
    xb = x_ref[...].astype(BF16)

    def mm(lo, hi):
        return jnp.dot(xb, w_ref[:, lo:hi], preferred_element_type=F32)

    q_ref[...] = (mm(_C_Q, _C_K) * (ATTN_SCALE * LOG2_E)).astype(BF16)
    k = mm(_C_K, _C_V)
    kb_ref[...] = k.astype(BF16)
    v = mm(_C_V, _C_F)
    vb_ref[...] = v.astype(BF16)
    if time_minor_kv:
        k_ref[0] = jnp.transpose(k)
        v_ref[0] = jnp.transpose(v)
    else:
        k_ref[...] = k
        v_ref[...] = v
    f = jnp.transpose(mm(_C_F, _C_XR))[:N_ATTN_HEADS] + bf_ref[...]
    lf_ref[...] = -_softplus(-f)
    xr_ref[...] = mm(_C_XR, _C_GATE)
    gg_ref[...] = _gelu_tanh(mm(_C_GATE, _C_GA)).astype(BF16)
    sa_ref[...] = _sigmoid(mm(_C_GA, _C_GR)).astype(BF16)
    sr_ref[...] = _sigmoid(mm(_C_GR, _C_END)).astype(BF16)


def _inproj(x2, w_in_p, b_forget, bsz, t, tm):
    n = x2.shape[0]
    row = lambda w: pl.BlockSpec((tm, w), lambda i: (i, 0))
    full = lambda a: pl.BlockSpec(a.shape, lambda i: (0,) * a.ndim)
    time_minor_kv = tm % LANES == 0 and t % tm == 0
    if time_minor_kv:
        kv_shape = jax.ShapeDtypeStruct((bsz, D_ATTN, t), F32)
        kv_spec = pl.BlockSpec((1, D_ATTN, tm), lambda i: (i // (t // tm), 0, i % (t // tm)))
    else:
        kv_shape = jax.ShapeDtypeStruct((n, D_ATTN), F32)
        kv_spec = row(D_ATTN)
    out_shape = (
        jax.ShapeDtypeStruct((n, D_ATTN), BF16),
        kv_shape,
        kv_shape,
        jax.ShapeDtypeStruct((n, D_ATTN), BF16),
        jax.ShapeDtypeStruct((n, D_ATTN), BF16),
        jax.ShapeDtypeStruct((N_ATTN_HEADS, n), F32),
        jax.ShapeDtypeStruct((n, D_RNN), F32),
        jax.ShapeDtypeStruct((n, D_RNN), BF16),
        jax.ShapeDtypeStruct((n, D_MODEL), BF16),
        jax.ShapeDtypeStruct((n, D_MODEL), BF16),
    )
    out_specs = (row(D_ATTN), kv_spec, kv_spec, row(D_ATTN), row(D_ATTN),
                 pl.BlockSpec((N_ATTN_HEADS, tm), lambda i: (0, i)),
                 row(D_RNN), row(D_RNN), row(D_MODEL), row(D_MODEL))
    return pl.pallas_call(
        functools.partial(_inproj_kernel, time_minor_kv=time_minor_kv),
        grid=(n // tm,),
        in_specs=[row(D_MODEL), full(w_in_p), full(b_forget)],
        out_specs=out_specs,
        out_shape=out_shape,
        compiler_params=_params("parallel"),
        name="inproj",
    )(x2, w_in_p, b_forget)


def _cumsum_kernel(x_ref, o_ref):
    x = x_ref[0]
    t = x.shape[1]
    lane = lax.broadcasted_iota(jnp.int32, x.shape, 1)
    d = 1
    while d < t:
        x = x + jnp.where(lane >= d, pltpu.roll(x, d, axis=1), 0.0)
        d *= 2
    o_ref[0] = x * LOG2_E


def _cumsum_time(lf_t):
    b, h, t = lf_t.shape
    spec = pl.BlockSpec((1, h, t), lambda i: (i, 0, 0))
    return pl.pallas_call(
        _cumsum_kernel, grid=(b,), in_specs=[spec], out_specs=spec,
        out_shape=jax.ShapeDtypeStruct(lf_t.shape, F32),
        compiler_params=_params("parallel"), name="logf_cumsum",
    )(lf_t)


def _attn_kernel(q_ref, k_ref, v_ref, fq_ref, fk_ref, o_ref, q_sc, fq_sc, s_sc, p_sc, acc_sc, m_sc, al_sc,
                 lp_sc, *, tq, tk, q_off):
    qi = pl.program_id(2)
    row0 = q_off + qi * tq
    n_full = (row0 + 1) // tk
    heads = range(HEADS_PER_GROUP)
    rb = min(ATTN_ROW_BLOCK, tq)
    lane = lax.broadcasted_iota(jnp.int32, (tq, GROUP_LANES), 1)
    in_head = [(lane >= j * ATTN_HEAD_DIM) & (lane < (j + 1) * ATTN_HEAD_DIM) for j in heads]
    q = q_ref[0]
    for j in heads:
        q_sc[j * tq:(j + 1) * tq, :] = jnp.where(in_head[j], q, jnp.zeros_like(q))
        fq_sc[j * tq:(j + 1) * tq, :] = jnp.broadcast_to(fq_ref[0, 0, :, j:j + 1], (tq, LANES))
    m_sc[...] = jnp.full_like(m_sc, NEG_BIG)
    lp_sc[...] = jnp.zeros_like(lp_sc)
    acc_sc[...] = jnp.zeros_like(acc_sc)
    halves = [slice(0, 2 * tq), slice(2 * tq, 4 * tq)]
    n_lane_tiles = tk // LANES

    def scores(c, s_sc):
        ks = k_ref[0, pl.ds(pl.multiple_of(c * tk, tk), tk), :]
        for hv in halves:
            s_sc[hv, :] = lax.dot_general(q_sc[hv, :], ks, (((1,), (1,)), ((), ())),
                                          preferred_element_type=F32)

    def absorb(c, s_sc, masked):
        start = pl.multiple_of(c * tk, tk)
        fk = fk_ref[0, 0, c]
        vs = v_ref[0, pl.ds(start, tk), :]

        def weighted_values(hv):
            pv = jnp.dot(p_sc[hv, :], vs, preferred_element_type=F32)
            alpha = al_sc[hv, :]
            acc_sc[hv, :] = jnp.concatenate([alpha] * (GROUP_LANES // LANES), axis=1) * acc_sc[hv, :] + pv

        for j in heads:
            for r in range(tq // rb):
                rows = slice(j * tq + r * rb, j * tq + (r + 1) * rb)
                s = s_sc[rows, :] - fk[j:j + 1, :]
                if masked:
                    q_pos = row0 + r * rb + lax.broadcasted_iota(jnp.int32, (rb, tk), 0)
                    k_pos = c * tk + lax.broadcasted_iota(jnp.int32, (rb, tk), 1)
                    s = jnp.where(k_pos <= q_pos, s, NEG_BIG)
                tiles = [s[:, t * LANES:(t + 1) * LANES] for t in range(n_lane_tiles)]
                row_max = jnp.max(functools.reduce(jnp.maximum, tiles), axis=1, keepdims=True)
                fq = fq_sc[rows, :]
                m_old = m_sc[rows, :]
                m_new = jnp.maximum(m_old, jnp.broadcast_to(row_max, (rb, LANES)) + fq)
                alpha = jnp.exp2(m_old - m_new)
                shift = fq - m_new
                p_tiles = [jnp.exp2(tile + shift) for tile in tiles]
                m_sc[rows, :] = m_new
                al_sc[rows, :] = alpha
                lp_sc[rows, :] = alpha * lp_sc[rows, :] + functools.reduce(jnp.add, p_tiles)
                p_sc[rows, :] = jnp.concatenate(p_tiles, axis=1).astype(BF16)
            if j % 2 == 1:
                weighted_values(halves[j // 2])

    def step(c, carry):
        scores(c, s_sc)
        absorb(c, s_sc, masked=False)
        return carry

    lax.fori_loop(0, n_full, step, 0)
    scores(n_full, s_sc)
    absorb(n_full, s_sc, masked=True)
    o_all = acc_sc[...] / jnp.sum(lp_sc[...], axis=1, keepdims=True)
    out = jnp.zeros((tq, GROUP_LANES), F32)
    for j in heads:
        out = jnp.where(in_head[j], o_all[j * tq:(j + 1) * tq], out)
    o_ref[0] = out.astype(BF16)


def _attention(q, k_all, v_all, fq, fk, *, tq, tk, q_off):
    b, t, _ = q.shape
    t_k = k_all.shape[1]
    n_kv = t_k // tk
    assert all((q_off + i * tq) % tk + tq <= tk for i in range(t // tq)), "a query block straddles key chunks"
    kern = functools.partial(_attn_kernel, tq=tq, tk=tk, q_off=q_off)
    rows = HEADS_PER_GROUP * tq
    return pl.pallas_call(
        kern,
        grid=(b, N_HEAD_GROUPS, t // tq),
        in_specs=[
            pl.BlockSpec((1, tq, GROUP_LANES), lambda bi, g, i: (bi, i, g)),
            pl.BlockSpec((1, t_k, GROUP_LANES), lambda bi, g, i: (bi, 0, g)),
            pl.BlockSpec((1, t_k, GROUP_LANES), lambda bi, g, i: (bi, 0, g)),
            pl.BlockSpec((1, 1, tq, HEADS_PER_GROUP), lambda bi, g, i: (bi, g, i, 0)),
            pl.BlockSpec((1, 1, n_kv, HEADS_PER_GROUP, tk), lambda bi, g, i: (bi, g, 0, 0, 0)),
        ],
        out_specs=pl.BlockSpec((1, tq, GROUP_LANES), lambda bi, g, i: (bi, i, g)),
        out_shape=jax.ShapeDtypeStruct((b, t, D_ATTN), BF16),
        scratch_shapes=[
            pltpu.VMEM((rows, GROUP_LANES), BF16),
            pltpu.VMEM((rows, LANES), F32),
            pltpu.VMEM((rows, tk), F32),
            pltpu.VMEM((rows, tk), BF16),
            pltpu.VMEM((rows, GROUP_LANES), F32),
            pltpu.VMEM((rows, LANES), F32),
            pltpu.VMEM((rows, LANES), F32),
            pltpu.VMEM((rows, LANES), F32),
        ],
        compiler_params=_params("parallel", "parallel", "arbitrary"),
        name="fox_attention",
    )(q, k_all, v_all, fq, fk)


def _rnn_kernel(xr_ref, gg_ref, hist_ref, h0_ref, cw_ref, cb_ref, wa_ref, ba_ref, wx_ref, bx_ref,
                lam_ref, out_ref, hl_ref, win_sc, h_sc, *, tb):
    t = pl.program_id(1)

    @pl.when(t == 0)
    def _():
        win_sc[:SUBLANES, :] = hist_ref[0]
        h_sc[...] = h0_ref[0]

    x = xr_ref[0]
    win_sc[SUBLANES:, :] = x
    xc = x * cw_ref[CONV_WIDTH - 1:CONV_WIDTH, :] + cb_ref[...]
    for s in range(1, CONV_WIDTH):
        xc = xc + win_sc[SUBLANES - s:SUBLANES - s + tb, :] * cw_ref[CONV_WIDTH - 1 - s:CONV_WIDTH - s, :]
    win_sc[:SUBLANES, :] = x[tb - SUBLANES:tb]

    xcb = xc.astype(BF16)
    r = _sigmoid(jnp.dot(xcb, wa_ref[...], preferred_element_type=F32) + ba_ref[...])
    ig = _sigmoid(jnp.dot(xcb, wx_ref[...], preferred_element_type=F32) + bx_ref[...])
    log_a = (-LRU_C) * r * _softplus(-lam_ref[...])
    a = jnp.exp(log_a)
    one_minus_a2 = 1.0 - a * a
    scale = jnp.where(one_minus_a2 > 0.0, one_minus_a2 * lax.rsqrt(one_minus_a2), 0.0)
    bterm = scale * ig * xc

    grouped = (tb // SUBLANES, SUBLANES, D_RNN)
    a, bterm = a.reshape(grouped), bterm.reshape(grouped)
    row_in_group = lax.broadcasted_iota(jnp.int32, grouped, 1)
    d = 1
    while d < SUBLANES:
        valid = row_in_group >= d
        a_s = pltpu.roll(a, d, axis=1)
        b_s = pltpu.roll(bterm, d, axis=1)
        bterm = jnp.where(valid, a * b_s + bterm, bterm)
        a = jnp.where(valid, a * a_s, a)
        d *= 2
    h_last = h_sc[...]
    groups = []
    for g in range(tb // SUBLANES):
        h_g = bterm[g] + a[g] * h_last
        h_last = h_g[SUBLANES - 1:SUBLANES]
        groups.append(h_g)
    h = jnp.concatenate(groups, axis=0)
    h_sc[...] = h_last
    hl_ref[0] = h_last
    out_ref[0] = (h * gg_ref[0].astype(F32)).astype(BF16)


def _rnn(xr, gg, hist8, h0, conv_w, conv_b, wa, ba, wx, bx, lam, tb):
    b, t, _ = xr.shape
    blk = pl.BlockSpec((1, tb, D_RNN), lambda bi, ti: (bi, ti, 0))
    full = lambda a: pl.BlockSpec(a.shape, lambda bi, ti: (0,) * a.ndim)
    per_b = lambda r: pl.BlockSpec((1, r, D_RNN), lambda bi, ti: (bi, 0, 0))
    return pl.pallas_call(
        functools.partial(_rnn_kernel, tb=tb),
        grid=(b, t // tb),
        in_specs=[blk, blk, per_b(SUBLANES), per_b(1), full(conv_w), full(conv_b), full(wa), full(ba),
                  full(wx), full(bx), full(lam)],
        out_specs=(blk, per_b(1)),
        out_shape=(jax.ShapeDtypeStruct((b, t, D_RNN), BF16), jax.ShapeDtypeStruct((b, 1, D_RNN), F32)),
        scratch_shapes=[pltpu.VMEM((SUBLANES + tb, D_RNN), F32), pltpu.VMEM((1, D_RNN), F32)],
        compiler_params=_params("parallel", "arbitrary"),
        name="conv_rglru",
    )(xr, gg, hist8, h0, conv_w, conv_b, wa, ba, wx, bx, lam)


def _outproj_kernel(o_ref, r_ref, sa_ref, sr_ref, x_ref, wau_ref, wru_ref, wo_ref, g_ref, b_ref,
                    h_ref, *, alpha):
    up_a = jnp.dot(o_ref[...], wau_ref[...], preferred_element_type=F32)
    up_r = jnp.dot(r_ref[...], wru_ref[...], preferred_element_type=F32)
    merged = sa_ref[...].astype(F32) * up_a + sr_ref[...].astype(F32) * up_r
    mix = jnp.dot(merged.astype(BF16), wo_ref[...], preferred_element_type=F32)
    h_ref[...] = _layer_norm(alpha * x_ref[...] + mix, g_ref[...], b_ref[...])


def _outproj(o2, r2, sa, sr, x2, wau, wru, wo, g, bta, alpha, tm):
    n = x2.shape[0]
    row = lambda w: pl.BlockSpec((tm, w), lambda i: (i, 0))
    full = lambda a: pl.BlockSpec(a.shape, lambda i: (0,) * a.ndim)
    return pl.pallas_call(
        functools.partial(_outproj_kernel, alpha=alpha),
        grid=(n // tm,),
        in_specs=[row(D_ATTN), row(D_RNN), row(D_MODEL), row(D_MODEL), row(D_MODEL), full(wau), full(wru),
                  full(wo), full(g), full(bta)],
        out_specs=row(D_MODEL),
        out_shape=jax.ShapeDtypeStruct((n, D_MODEL), F32),
        compiler_params=_params("parallel"),
        name="outproj_ln1",
    )(o2, r2, sa, sr, x2, wau, wru, wo, g, bta)


ROUTE_TOKENS = 2 * LANES


def _odd_even_merge_sort(n):
    def merge(lo, hi, r):
        step = r * 2
        if step < hi - lo:
            yield from merge(lo, hi, step)
            yield from merge(lo + r, hi, step)
            yield from ((i, i + r) for i in range(lo + r, hi - r, step))
        else:
            yield (lo, lo + r)

    def sort(lo, hi):
        if hi - lo >= 1:
            mid = lo + (hi - lo) // 2
            yield from sort(lo, mid)
            yield from sort(mid + 1, hi)
            yield from merge(lo, hi, 1)

    return tuple(sort(0, n - 1))


_SORT_TOPK = _odd_even_merge_sort(PEER_TOPK)


def _compare_exchange(v, i, j):
    v[i], v[j] = jnp.maximum(v[i], v[j]), jnp.minimum(v[i], v[j])


def _sort_bitonic(v):
    d = PEER_TOPK // 2
    while d >= 1:
        for i in range(PEER_TOPK):
            if i & d == 0:
                _compare_exchange(v, i, i + d)
        d //= 2
    return v


def _top_k_sorted(tiles, presorted=False):
    v = list(tiles)
    if not presorted:
        for i, j in _SORT_TOPK:
            _compare_exchange(v, i, j)
    shift = SUBLANES // 2
    while shift >= 1:
        other = [pltpu.roll(x, shift, axis=0) for x in v]
        v = _sort_bitonic([jnp.maximum(v[i], other[PEER_TOPK - 1 - i]) for i in range(PEER_TOPK)])
        shift //= 2
    return v


def _count_greater(x, t):
    assert len(t) == 16, "the bisection below is written out for 16 entries"
    one = lambda m, w: jnp.where(m, float(w), 0.0)
    b3 = t[7] > x
    b2 = jnp.where(b3, t[11], t[3]) > x
    b1 = jnp.where(b3, jnp.where(b2, t[13], t[9]), jnp.where(b2, t[5], t[1])) > x
    hi = jnp.where(b2, jnp.where(b1, t[14], t[12]), jnp.where(b1, t[10], t[8]))
    lo = jnp.where(b2, jnp.where(b1, t[6], t[4]), jnp.where(b1, t[2], t[0]))
    b0 = jnp.where(b3, hi, lo) > x
    count = one(b3, 8) + one(b2, 4) + one(b1, 2) + one(b0, 1)
    return jnp.where(t[15] > x, float(PEER_TOPK), count)


def _route_kernel(h_ref, wq_ref, k1_ref, k2_ref, ht_ref, cnt_ref, e1_ref, r2_ref, e2_ref,
                  qt_sc, t1_sc, t2_sc, *, tm):
    ht = jnp.transpose(h_ref[...]).astype(BF16)
    ht_ref[...] = ht
    qt_sc[...] = jnp.dot(wq_ref[...], ht, preferred_element_type=F32).astype(BF16)
    key_iota = lax.broadcasted_iota(jnp.int32, (N_KEYS, tm), 0).astype(F32)
    top_iota = lax.broadcasted_iota(jnp.int32, (PEER_TOPK, tm), 0).astype(F32)
    front_rows = SUBLANES

    def scores(hd):
        base = hd * 2 * PEER_HALF
        s1 = jnp.dot(k1_ref[...], qt_sc[base:base + PEER_HALF], preferred_element_type=F32)
        s2 = jnp.dot(k2_ref[...], qt_sc[base + PEER_HALF:base + 2 * PEER_HALF], preferred_element_type=F32)
        return s1, s2

    sub = lax.broadcasted_iota(jnp.int32, (SUBLANES, tm), 0)
    tiles_of = lambda s: [s[r * SUBLANES:(r + 1) * SUBLANES] for r in range(N_KEYS // SUBLANES)]
    sublane_sum = lambda x: jnp.sum(x, axis=0, keepdims=True)

    def route_head_sorted(hd):
        s1, s2 = scores(hd)
        rows1, rows2 = tiles_of(s1), tiles_of(s2)
        t1 = _top_k_sorted(rows1)
        t2 = _top_k_sorted(rows2)
        t1_lo = t1[SUBLANES - 1]
        for a in range(SUBLANES - 2, -1, -1):
            t1_lo = jnp.where(sub == a, t1[a], t1_lo)
        cand = [t1_lo + t2[b] for b in range(PEER_TOPK)]
        top_lo = _top_k_sorted(cand, presorted=True)
        hi_sums = [t1[a] + t2[0] for a in range(SUBLANES, PEER_TOPK)]
        top = _sort_bitonic(top_lo[:SUBLANES] + [jnp.maximum(top_lo[i], hi_sums[PEER_TOPK - 1 - i])
                                                 for i in range(SUBLANES, PEER_TOPK)])
        tau = top[PEER_TOPK - 1]
        cnt_lo = functools.reduce(jnp.add, [jnp.where(c >= tau, 1.0, 0.0) for c in cand])
        cnt = [jnp.broadcast_to(cnt_lo[a:a + 1], (SUBLANES, tm)) for a in range(SUBLANES)]
        cnt += [jnp.where(s >= tau, 1.0, 0.0) for s in hi_sums]
        z = functools.reduce(jnp.add, [jnp.exp(t - top[0]) for t in top])

        cnt1_rows, rank2_rows = [], []
        for r in range(N_KEYS // SUBLANES):
            c = jnp.where(rows1[r] + t2[0] >= tau, 1.0, 0.0)
            c = jnp.where(rows1[r] >= t1[PEER_TOPK - 1], c, 0.0)
            for a in range(SUBLANES - 1, -1, -1):
                c = jnp.where(rows1[r] >= t1[a], cnt[a], c)
            cnt1_rows.append(c)
            g = _count_greater(rows2[r], t2)
            rank2_rows.append(jnp.where(g < float(PEER_TOPK), g, NOT_SELECTED_RANK))
        cnt1 = jnp.concatenate(cnt1_rows, axis=0)
        rank2 = jnp.concatenate(rank2_rows, axis=0)
        cnt_ref[hd] = cnt1
        e1_ref[hd] = jnp.exp(s1 - t1[0][0:1]) / z[0:1]
        r2_ref[hd] = rank2.astype(BF16)
        e2_ref[hd] = jnp.exp(s2 - t2[0][0:1]).astype(BF16)

        gap = lambda t: functools.reduce(jnp.minimum, [t[b] - t[b + 1] for b in range(PEER_TOPK - 1)])[0:1]
        n_cnt = sublane_sum(cnt_lo) + functools.reduce(jnp.add, cnt[SUBLANES:])[0:1]
        n_sel = sublane_sum(functools.reduce(jnp.add, cnt1_rows))
        n_rank = sublane_sum(functools.reduce(
            jnp.add, [jnp.where(x < float(PEER_TOPK), 1.0, 0.0) for x in rank2_rows]))
        off = lambda n: jnp.abs(n - float(PEER_TOPK))
        return (off(n_cnt) + off(n_sel) + off(n_rank)
                + jnp.where(jnp.minimum(gap(t1), gap(t2)) > 0.0, 0.0, 1.0))

    def route_head_exact(hd):
        def pick_one(v, iota, n):
            m = jnp.max(v, axis=0, keepdims=True)
            return m, iota == jnp.min(jnp.where(v == m, iota, float(n)), axis=0, keepdims=True)

        s1, s2 = scores(hd)

        def extract(a, carry):
            v1, r1, v2, r2 = carry
            m1, sel1 = pick_one(v1, key_iota, N_KEYS)
            m2, sel2 = pick_one(v2, key_iota, N_KEYS)
            t1_sc[pl.ds(a, 1), :] = m1
            t2_sc[pl.ds(a, 1), :] = m2
            af = jnp.asarray(a, dtype=F32)
            return (jnp.where(sel1, -jnp.inf, v1), jnp.where(sel1, af, r1),
                    jnp.where(sel2, -jnp.inf, v2), jnp.where(sel2, af, r2))

        no_rank = jnp.full((N_KEYS, tm), NOT_SELECTED_RANK, F32)
        _, rank1, _, rank2 = lax.fori_loop(0, PEER_TOPK, extract, (s1, no_rank, s2, no_rank))
        t1 = t1_sc[...]
        t2 = t2_sc[...]
        top0 = t1[0:1] + t2[0:1]

        def pick(_, carry):
            ptr, front, z = carry
            m, sel = pick_one(front, top_iota, PEER_TOPK)
            ptr = ptr + jnp.where(sel, 1.0, 0.0)
            lo = ptr[:front_rows]
            nxt = jnp.full(lo.shape, -jnp.inf, F32)
            for b in range(1, PEER_TOPK):
                nxt = jnp.where(lo == float(b), t2_sc[b:b + 1, :], nxt)
            nxt = jnp.where(lo == 0.0, t2[0:1], nxt)
            front = jnp.concatenate(
                [t1[:front_rows] + nxt, jnp.where(sel[front_rows:], -jnp.inf, front[front_rows:])], axis=0)
            return ptr, front, z + jnp.exp(m - top0)

        cnt, _, z = lax.fori_loop(
            0, PEER_TOPK, pick,
            (jnp.zeros((PEER_TOPK, tm), F32), t1 + t2[0:1], jnp.zeros((1, tm), F32)))

        cnt1 = jnp.zeros((N_KEYS, tm), F32)
        for a in range(PEER_TOPK):
            cnt1 = jnp.where(rank1 == float(a), cnt[a:a + 1], cnt1)
        cnt_ref[hd] = cnt1
        e1_ref[hd] = jnp.exp(s1 - t1[0:1]) / z
        r2_ref[hd] = rank2.astype(BF16)
        e2_ref[hd] = jnp.exp(s2 - t2[0:1]).astype(BF16)

    doubt = [route_head_sorted(hd) for hd in range(PEER_HEADS)]

    @pl.when(jnp.max(functools.reduce(jnp.maximum, doubt)) > 0.0)
    def _():
        for hd in range(PEER_HEADS):
            @pl.when(jnp.max(doubt[hd]) > 0.0)
            def _():
                route_head_exact(hd)


def _route(h2, wq_t, k1, k2, tm):
    n = h2.shape[0]
    full = lambda a: pl.BlockSpec(a.shape, lambda i: (0,) * a.ndim)
    per_head = pl.BlockSpec((PEER_HEADS, N_KEYS, tm), lambda i: (0, 0, i))
    hshape = lambda dt: jax.ShapeDtypeStruct((PEER_HEADS, N_KEYS, n), dt)
    return pl.pallas_call(
        functools.partial(_route_kernel, tm=tm),
        grid=(n // tm,),
        in_specs=[pl.BlockSpec((tm, D_MODEL), lambda i: (i, 0)), full(wq_t), full(k1), full(k2)],
        out_specs=(pl.BlockSpec((D_MODEL, tm), lambda i: (0, i)), per_head, per_head, per_head, per_head),
        out_shape=(jax.ShapeDtypeStruct((D_MODEL, n), BF16), hshape(F32), hshape(F32), hshape(BF16),
                   hshape(BF16)),
        scratch_shapes=[pltpu.VMEM((PEER_HEADS * 2 * PEER_HALF, tm), BF16),
                        pltpu.VMEM((PEER_TOPK, tm), F32), pltpu.VMEM((PEER_TOPK, tm), F32)],
        compiler_params=_params("parallel"),
        name="peer_route",
    )(h2, wq_t, k1, k2)


ROWS_PER_STEP = 16
EXPERTS_PER_STEP = ROWS_PER_STEP * N_KEYS
ROWS_PER_SUB = 4
EXPERTS_PER_SUB = ROWS_PER_SUB * N_KEYS
DENSE_CHUNK = 2 * LANES


def _bf16_row_tile(row):
    tile = jnp.broadcast_to(row, (BF16_ROWS, row.shape[1])).astype(BF16)
    return jnp.concatenate([tile] * (N_KEYS // BF16_ROWS), axis=0)


def _dense_kernel(ht_ref, u_ref, vt_ref, cnt_ref, e1_ref, r2_ref, e2_ref, h_ref, g_ref, b_ref,
                  y_ref, acc_sc, act_sc, coef_sc, *, alpha):
    e = pl.program_id(1)
    tm = act_sc.shape[1]
    n_sub = ROWS_PER_STEP // ROWS_PER_SUB

    @pl.when(e == 0)
    def _():
        acc_sc[...] = jnp.zeros_like(acc_sc)

    def activations(sub):
        rows = slice(sub * EXPERTS_PER_SUB, (sub + 1) * EXPERTS_PER_SUB)
        act_sc[rows, :] = jnp.dot(u_ref[rows, :], ht_ref[...], preferred_element_type=F32)

    def tiles(sub):
        chunk = min(DENSE_CHUNK, tm)
        for il in range(sub * ROWS_PER_SUB, (sub + 1) * ROWS_PER_SUB):
            for c in range(tm // chunk):
                yield il, slice(il * N_KEYS, (il + 1) * N_KEYS), slice(c * chunk, (c + 1) * chunk)

    def routing_weights(sub):
        for il, rows, lanes in tiles(sub):
            w = None
            for hd in range(PEER_HEADS):
                cnt = _bf16_row_tile(cnt_ref[hd, il:il + 1, lanes])
                e1 = _bf16_row_tile(e1_ref[hd, il:il + 1, lanes])
                term = jnp.where(r2_ref[hd, :, lanes] < cnt, e2_ref[hd, :, lanes] * e1, jnp.zeros((), BF16))
                w = term if w is None else w + term
            coef_sc[rows, lanes] = w

    def coefficients(sub):
        for _, rows, lanes in tiles(sub):
            coef_sc[rows, lanes] = coef_sc[rows, lanes] * _gelu_tanh(act_sc[rows, lanes].astype(BF16))

    routing_weights(0)
    activations(0)
    for sub in range(n_sub):
        if sub + 1 < n_sub:
            routing_weights(sub + 1)
            activations(sub + 1)
        coefficients(sub)
    acc_sc[...] += jnp.dot(vt_ref[...], coef_sc[...], preferred_element_type=F32)

    @pl.when(e == pl.num_programs(1) - 1)
    def _():
        peer = jnp.transpose(acc_sc[...])
        y_ref[...] = _layer_norm(alpha * h_ref[...] + peer, g_ref[...], b_ref[...])


def _dense(ht, u_b, vt_b, cnt1, e1n, rank2, e2, h2, g, bta, alpha, tm):
    n = h2.shape[0]
    n_e = N_EXPERTS // EXPERTS_PER_STEP
    full = lambda a: pl.BlockSpec(a.shape, lambda t, e: (0,) * a.ndim)
    rows_blk = pl.BlockSpec((PEER_HEADS, ROWS_PER_STEP, tm), lambda t, e: (0, e, t))
    cols_blk = pl.BlockSpec((PEER_HEADS, N_KEYS, tm), lambda t, e: (0, 0, t))
    return pl.pallas_call(
        functools.partial(_dense_kernel, alpha=alpha),
        grid=(n // tm, n_e),
        in_specs=[
            pl.BlockSpec((D_MODEL, tm), lambda t, e: (0, t)),
            pl.BlockSpec((EXPERTS_PER_STEP, D_MODEL), lambda t, e: (e, 0)),
            pl.BlockSpec((D_MODEL, EXPERTS_PER_STEP), lambda t, e: (0, e)),
            rows_blk, rows_blk, cols_blk, cols_blk,
            pl.BlockSpec((tm, D_MODEL), lambda t, e: (t, 0)),
            full(g), full(bta),
        ],
        out_specs=pl.BlockSpec((tm, D_MODEL), lambda t, e: (t, 0)),
        out_shape=jax.ShapeDtypeStruct((n, D_MODEL), F32),
        scratch_shapes=[pltpu.VMEM((D_MODEL, tm), F32), pltpu.VMEM((EXPERTS_PER_STEP, tm), F32),
                        pltpu.VMEM((EXPERTS_PER_STEP, tm), BF16)],
        compiler_params=_params("parallel", "arbitrary"),
        name="peer_dense",
    )(ht, u_b, vt_b, cnt1, e1n, rank2, e2, h2, g, bta)


def _block_diag(w):
    nb, bi, bo = w.shape
    eye = jnp.eye(nb, dtype=w.dtype)
    return (eye[:, None, :, None] * w[:, :, None, :]).reshape(nb * bi, nb * bo)


def _prep_weights(w_in, b_forget, conv_w, conv_b, w_rg_a, b_rg_a, w_rg_x, b_rg_x, lru_lambda,
                  w_attn_up, w_rnn_up, w_out, ln1_g, ln1_b, peer_w_query, peer_keys_1, peer_keys_2,
                  peer_u, peer_v, ln2_g, ln2_b):
    c_f = 3 * D_ATTN
    w_in_p = jnp.concatenate(
        [w_in[:, :c_f], jnp.pad(w_in[:, c_f:c_f + N_ATTN_HEADS], ((0, 0), (0, F_PAD - N_ATTN_HEADS))),
         w_in[:, c_f + N_ATTN_HEADS:]], axis=1).astype(BF16)
    row = lambda a: a.reshape(1, -1).astype(F32)
    return dict(
        w_in=w_in_p, b_forget=b_forget.reshape(-1, 1).astype(F32), conv_w=conv_w.astype(F32), conv_b=row(conv_b),
        wa=_block_diag(w_rg_a).astype(BF16), ba=row(b_rg_a), wx=_block_diag(w_rg_x).astype(BF16),
        bx=row(b_rg_x), lam=row(lru_lambda),
        wau=w_attn_up.astype(BF16), wru=w_rnn_up.astype(BF16), wo=w_out.astype(BF16),
        ln1_g=row(ln1_g), ln1_b=row(ln1_b),
        wq_t=jnp.transpose(peer_w_query).astype(BF16), k1=peer_keys_1.astype(BF16),
        k2=peer_keys_2.astype(BF16), u=peer_u.astype(BF16), vt=jnp.transpose(peer_v).astype(BF16),
        ln2_g=row(ln2_g), ln2_b=row(ln2_b),
    )


def _pick_block(n, target):
    blk = min(n, target)
    assert n % blk == 0, (n, blk)
    return blk


def _trunk_layer(x, past_k, past_v, past_logf, conv_hist, h0, p, alpha):
    bsz, t, _ = x.shape
    n = bsz * t
    x2 = x.reshape(n, D_MODEL)
    q, k, v, kb, vb, lf, xr, gg, sa, sr = _inproj(x2, p["w_in"], p["b_forget"], bsz, t, _pick_block(n, 512))

    def state_layout(a):
        if a.ndim == 3:
            return jnp.transpose(a.reshape(bsz, N_ATTN_HEADS, ATTN_HEAD_DIM, t), (0, 3, 1, 2))
        return a.reshape(bsz, t, N_ATTN_HEADS, ATTN_HEAD_DIM)

    tq = _pick_block(t, ATTN_Q_BLOCK)
    n_past = 0 if past_k is None else past_k.shape[1]
    t_all = n_past + t
    t_lanes = -(-t_all // LANES) * LANES
    tk = ATTN_K_BLOCK if t_lanes % ATTN_K_BLOCK == 0 else t_lanes
    t_pad = -(-t_all // tk) * tk
    lf_bht = jnp.transpose(lf.reshape(N_ATTN_HEADS, bsz, t), (1, 0, 2))
    lf_all = lf_bht
    kb3 = kb.reshape(bsz, t, D_ATTN)
    vb3 = vb.reshape(bsz, t, D_ATTN)
    if past_k is not None:
        lf_all = jnp.concatenate([jnp.transpose(past_logf.astype(F32), (0, 2, 1)), lf_bht], axis=2)
        kb3 = jnp.concatenate([past_k.reshape(bsz, n_past, D_ATTN).astype(BF16), kb3], axis=1)
        vb3 = jnp.concatenate([past_v.reshape(bsz, n_past, D_ATTN).astype(BF16), vb3], axis=1)
    pad = ((0, 0), (0, t_pad - t_all), (0, 0))
    kb3, vb3 = jnp.pad(kb3, pad), jnp.pad(vb3, pad)
    f_t = _cumsum_time(jnp.pad(lf_all, ((0, 0), (0, 0), (0, t_pad - t_all))))
    f_g = f_t.reshape(bsz, N_HEAD_GROUPS, HEADS_PER_GROUP, t_pad)
    fq = jnp.transpose(f_g[:, :, :, n_past:n_past + t], (0, 1, 3, 2))
    fk = jnp.transpose(f_g.reshape(bsz, N_HEAD_GROUPS, HEADS_PER_GROUP, t_pad // tk, tk), (0, 1, 3, 2, 4))
    o = _attention(q.reshape(bsz, t, D_ATTN), kb3, vb3, fq, fk, tq=tq, tk=tk, q_off=n_past)

    hist8 = jnp.pad(conv_hist.astype(F32), ((0, 0), (SUBLANES - (CONV_WIDTH - 1), 0), (0, 0)))
    xr3 = xr.reshape(bsz, t, D_RNN)
    rnn_out, h_last = _rnn(xr3, gg.reshape(bsz, t, D_RNN), hist8, h0.astype(F32).reshape(bsz, 1, D_RNN),
                           p["conv_w"], p["conv_b"], p["wa"], p["ba"], p["wx"], p["bx"], p["lam"],
                           _pick_block(t, 512))
    new_hist = jnp.concatenate([conv_hist.astype(F32), xr3], axis=1)[:, -(CONV_WIDTH - 1):]

    h = _outproj(o.reshape(n, D_ATTN), rnn_out.reshape(n, D_RNN), sa, sr, x2, p["wau"], p["wru"], p["wo"],
                 p["ln1_g"], p["ln1_b"], alpha, _pick_block(n, 1024))
    ht, cnt1, e1n, rank2, e2 = _route(h, p["wq_t"], p["k1"], p["k2"], _pick_block(n, ROUTE_TOKENS))
    y = _dense(ht, p["u"], p["vt"], cnt1, e1n, rank2, e2, h, p["ln2_g"], p["ln2_b"], alpha,
               _pick_block(n, 512))
    return (y.reshape(bsz, t, D_MODEL), state_layout(k), state_layout(v), jnp.transpose(lf_bht, (0, 2, 1)), new_hist,
            h_last.reshape(bsz, D_RNN))


def kernel(x_prompt, x_sample, cache_k, cache_v, cache_logf, state_conv, state_rnn, w_in, b_forget, conv_w, conv_b, w_rg_a, b_rg_a, w_rg_x, b_rg_x, lru_lambda, w_attn_up, w_rnn_up, w_out, ln1_g, ln1_b, peer_w_query, peer_keys_1, peer_keys_2, peer_u, peer_v, ln2_g, ln2_b):
    depth = w_in.shape[0]
    alpha = (2 * depth) ** 0.25
    layer_weights = (w_in, b_forget, conv_w, conv_b, w_rg_a, b_rg_a, w_rg_x, b_rg_x, lru_lambda, w_attn_up,
                     w_rnn_up, w_out, ln1_g, ln1_b, peer_w_query, peer_keys_1, peer_keys_2, peer_u, peer_v,
                     ln2_g, ln2_b)
    hp, hs = x_prompt, x_sample
    prompt_state, sample_state = [], []
    for l in range(depth):
        p = _prep_weights(*(w[l] for w in layer_weights))
        zero_hist = jnp.zeros((hp.shape[0], CONV_WIDTH - 1, D_RNN), F32)
        zero_h = jnp.zeros((hp.shape[0], D_RNN), F32)
        hp, *st_p = _trunk_layer(hp, None, None, None, zero_hist, zero_h, p, alpha)
        hs, *st_s = _trunk_layer(hs, cache_k[l], cache_v[l], cache_logf[l], state_conv[l], state_rnn[l], p,
                                 alpha)
        prompt_state.append(st_p)
        sample_state.append(st_s)
    stack = lambda states, i: jnp.stack([s[i] for s in states])
    return (hp, hs) + tuple(stack(prompt_state, i) for i in range(5)) + tuple(
        stack(sample_state, i) for i in range(5))
```

```python
import functools
import math

import jax
import jax.numpy as jnp
from jax import lax
from jax.experimental import pallas as pl
from jax.experimental.pallas import tpu as pltpu

F32 = jnp.float32
BF16 = jnp.bfloat16

D_MODEL = 1024
N_ATTN_HEADS = 8
ATTN_HEAD_DIM = 64
D_ATTN = N_ATTN_HEADS * ATTN_HEAD_DIM
ATTN_SCALE = ATTN_HEAD_DIM ** -0.5
D_RNN = 512
CONV_WIDTH = 4
LRU_C = 8.0
N_KEYS = 128
N_EXPERTS = N_KEYS * N_KEYS
PEER_HEADS = 8
PEER_TOPK = 16
PEER_HALF = 128
LN_EPS = 1e-5

LANES = 128
SUBLANES = 8
BF16_ROWS = 2 * SUBLANES
VMEM_LIMIT_BYTES = 56 * 1024 * 1024

HEADS_PER_GROUP = 4
GROUP_LANES = HEADS_PER_GROUP * ATTN_HEAD_DIM
N_HEAD_GROUPS = N_ATTN_HEADS // HEADS_PER_GROUP
ATTN_Q_BLOCK = 512
ATTN_K_BLOCK = 512
ATTN_ROW_BLOCK = 32
F_PAD = LANES
NEG_BIG = -1e30
LOG2_E = math.log2(math.e)
NOT_SELECTED_RANK = 99.0

_C_Q = 0
_C_K = _C_Q + D_ATTN
_C_V = _C_K + D_ATTN
_C_F = _C_V + D_ATTN
_C_XR = _C_F + F_PAD
_C_GATE = _C_XR + D_RNN
_C_GA = _C_GATE + D_RNN
_C_GR = _C_GA + D_MODEL
_C_END = _C_GR + D_MODEL


def _params(*sem):
    return pltpu.CompilerParams(dimension_semantics=sem, vmem_limit_bytes=VMEM_LIMIT_BYTES)


def _sigmoid(x):
    return 1.0 / (1.0 + jnp.exp(-x))


def _gelu_tanh(x):
    half = 0.5 * x
    return half + half * jnp.tanh(x * (0.7978845608028654 + 0.035677408136300125 * (x * x)))


def _softplus(x):
    return jnp.maximum(x, 0.0) + jnp.log1p(jnp.exp(-jnp.abs(x)))


def _layer_norm(x, g, b):
    mu = jnp.mean(x, axis=-1, keepdims=True)
    xc = x - mu
    var = jnp.mean(xc * xc, axis=-1, keepdims=True)
    return xc * lax.rsqrt(var + LN_EPS) * g + b


def _inproj_kernel(x_ref, w_ref, bf_ref, q_ref, k_ref, v_ref, kb_ref, vb_ref, lf_ref, xr_ref,
                   gg_ref, sa_ref, sr_ref, *, time_minor_kv):
    xb = x_ref[...].astype(BF16)

    def mm(lo, hi):
        return jnp.dot(xb, w_ref[:, lo:hi], preferred_element_type=F32)

    q_ref[...] = (mm(_C_Q, _C_K) * (ATTN_SCALE * LOG2_E)).astype(BF16)
    k = mm(_C_K, _C_V)
    kb_ref[...] = k.astype(BF16)
    v = mm(_C_V, _C_F)
    vb_ref[...] = v.astype(BF16)
    if time_minor_kv:
        k_ref[0] = jnp.transpose(k)
        v_ref[0] = jnp.transpose(v)
    else:
        k_ref[...] = k
        v_ref[...] = v
    f = jnp.transpose(mm(_C_F, _C_XR))[:N_ATTN_HEADS] + bf_ref[...]
    lf_ref[...] = -_softplus(-f)
    xr_ref[...] = mm(_C_XR, _C_GATE)
    gg_ref[...] = _gelu_tanh(mm(_C_GATE, _C_GA)).astype(BF16)
    sa_ref[...] = _sigmoid(mm(_C_GA, _C_GR)).astype(BF16)
    sr_ref[...] = _sigmoid(mm(_C_GR, _C_END)).astype(BF16)


def _inproj(x2, w_in_p, b_forget, bsz, t, tm):
    n = x2.shape[0]
    row = lambda w: pl.BlockSpec((tm, w), lambda i: (i, 0))
    full = lambda a: pl.BlockSpec(a.shape, lambda i: (0,) * a.ndim)
    time_minor_kv = tm % LANES == 0 and t % tm == 0
    if time_minor_kv:
        kv_shape = jax.ShapeDtypeStruct((bsz, D_ATTN, t), F32)
        kv_spec = pl.BlockSpec((1, D_ATTN, tm), lambda i: (i // (t // tm), 0, i % (t // tm)))
    else:
        kv_shape = jax.ShapeDtypeStruct((n, D_ATTN), F32)
        kv_spec = row(D_ATTN)
    out_shape = (
        jax.ShapeDtypeStruct((n, D_ATTN), BF16),
        kv_shape,
        kv_shape,
        jax.ShapeDtypeStruct((n, D_ATTN), BF16),
        jax.ShapeDtypeStruct((n, D_ATTN), BF16),
        jax.ShapeDtypeStruct((N_ATTN_HEADS, n), F32),
        jax.ShapeDtypeStruct((n, D_RNN), F32),
        jax.ShapeDtypeStruct((n, D_RNN), BF16),
        jax.ShapeDtypeStruct((n, D_MODEL), BF16),
        jax.ShapeDtypeStruct((n, D_MODEL), BF16),
    )
    out_specs = (row(D_ATTN), kv_spec, kv_spec, row(D_ATTN), row(D_ATTN),
                 pl.BlockSpec((N_ATTN_HEADS, tm), lambda i: (0, i)),
                 row(D_RNN), row(D_RNN), row(D_MODEL), row(D_MODEL))
    return pl.pallas_call(
        functools.partial(_inproj_kernel, time_minor_kv=time_minor_kv),
        grid=(n // tm,),
        in_specs=[row(D_MODEL), full(w_in_p), full(b_forget)],
        out_specs=out_specs,
        out_shape=out_shape,
        compiler_params=_params("parallel"),
        name="inproj",
    )(x2, w_in_p, b_forget)


def _cumsum_kernel(x_ref, o_ref):
    x = x_ref[0]
    t = x.shape[1]
    lane = lax.broadcasted_iota(jnp.int32, x.shape, 1)
    d = 1
    while d < t:
        x = x + jnp.where(lane >= d, pltpu.roll(x, d, axis=1), 0.0)
        d *= 2
    o_ref[0] = x * LOG2_E


def _cumsum_time(lf_t):
    b, h, t = lf_t.shape
    spec = pl.BlockSpec((1, h, t), lambda i: (i, 0, 0))
    return pl.pallas_call(
        _cumsum_kernel, grid=(b,), in_specs=[spec], out_specs=spec,
        out_shape=jax.ShapeDtypeStruct(lf_t.shape, F32),
        compiler_params=_params("parallel"), name="logf_cumsum",
    )(lf_t)


def _attn_kernel(q_ref, k_ref, v_ref, fq_ref, fk_ref, o_ref, q_sc, fq_sc, s_sc, p_sc, acc_sc, m_sc, al_sc,
                 lp_sc, *, tq, tk, q_off):
    qi = pl.program_id(2)
    row0 = q_off + qi * tq
    n_full = (row0 + 1) // tk
    heads = range(HEADS_PER_GROUP)
    rb = min(ATTN_ROW_BLOCK, tq)
    lane = lax.broadcasted_iota(jnp.int32, (tq, GROUP_LANES), 1)
    in_head = [(lane >= j * ATTN_HEAD_DIM) & (lane < (j + 1) * ATTN_HEAD_DIM) for j in heads]
    q = q_ref[0]
    for j in heads:
        q_sc[j * tq:(j + 1) * tq, :] = jnp.where(in_head[j], q, jnp.zeros_like(q))
        fq_sc[j * tq:(j + 1) * tq, :] = jnp.broadcast_to(fq_ref[0, 0, :, j:j + 1], (tq, LANES))
    m_sc[...] = jnp.full_like(m_sc, NEG_BIG)
    lp_sc[...] = jnp.zeros_like(lp_sc)
    acc_sc[...] = jnp.zeros_like(acc_sc)
    halves = [slice(0, 2 * tq), slice(2 * tq, 4 * tq)]
    n_lane_tiles = tk // LANES

    def scores(c, s_sc):
        ks = k_ref[0, pl.ds(pl.multiple_of(c * tk, tk), tk), :]
        for hv in halves:
            s_sc[hv, :] = lax.dot_general(q_sc[hv, :], ks, (((1,), (1,)), ((), ())),
                                          preferred_element_type=F32)

    def absorb(c, s_sc, masked):
        start = pl.multiple_of(c * tk, tk)
        fk = fk_ref[0, 0, c]
        vs = v_ref[0, pl.ds(start, tk), :]

        def weighted_values(hv):
            pv = jnp.dot(p_sc[hv, :], vs, preferred_element_type=F32)
            alpha = al_sc[hv, :]
            acc_sc[hv, :] = jnp.concatenate([alpha] * (GROUP_LANES // LANES), axis=1) * acc_sc[hv, :] + pv

        for j in heads:
            for r in range(tq // rb):
                rows = slice(j * tq + r * rb, j * tq + (r + 1) * rb)
                s = s_sc[rows, :] - fk[j:j + 1, :]
                if masked:
                    q_pos = row0 + r * rb + lax.broadcasted_iota(jnp.int32, (rb, tk), 0)
                    k_pos = c * tk + lax.broadcasted_iota(jnp.int32, (rb, tk), 1)
                    s = jnp.where(k_pos <= q_pos, s, NEG_BIG)
                tiles = [s[:, t * LANES:(t + 1) * LANES] for t in range(n_lane_tiles)]
                row_max = jnp.max(functools.reduce(jnp.maximum, tiles), axis=1, keepdims=True)
                fq = fq_sc[rows, :]
                m_old = m_sc[rows, :]
                m_new = jnp.maximum(m_old, jnp.broadcast_to(row_max, (rb, LANES)) + fq)
                alpha = jnp.exp2(m_old - m_new)
                shift = fq - m_new
                p_tiles = [jnp.exp2(tile + shift) for tile in tiles]
                m_sc[rows, :] = m_new
                al_sc[rows, :] = alpha
                lp_sc[rows, :] = alpha * lp_sc[rows, :] + functools.reduce(jnp.add, p_tiles)
                p_sc[rows, :] = jnp.concatenate(p_tiles, axis=1).astype(BF16)
            if j % 2 == 1:
                weighted_values(halves[j // 2])

    def step(c, carry):
        scores(c, s_sc)
        absorb(c, s_sc, masked=False)
        return carry

    lax.fori_loop(0, n_full, step, 0)
    scores(n_full, s_sc)
    absorb(n_full, s_sc, masked=True)
    o_all = acc_sc[...] / jnp.sum(lp_sc[...], axis=1, keepdims=True)
    out = jnp.zeros((tq, GROUP_LANES), F32)
    for j in heads:
        out = jnp.where(in_head[j], o_all[j * tq:(j + 1) * tq], out)
    o_ref[0] = out.astype(BF16)


def _attention(q, k_all, v_all, fq, fk, *, tq, tk, q_off):
    b, t, _ = q.shape
    t_k = k_all.shape[1]
    n_kv = t_k // tk
    assert all((q_off + i * tq) % tk + tq <= tk for i in range(t // tq)), "a query block straddles key chunks"
    kern = functools.partial(_attn_kernel, tq=tq, tk=tk, q_off=q_off)
    rows = HEADS_PER_GROUP * tq
    return pl.pallas_call(
        kern,
        grid=(b, N_HEAD_GROUPS, t // tq),
        in_specs=[
            pl.BlockSpec((1, tq, GROUP_LANES), lambda bi, g, i: (bi, i, g)),
            pl.BlockSpec((1, t_k, GROUP_LANES), lambda bi, g, i: (bi, 0, g)),
            pl.BlockSpec((1, t_k, GROUP_LANES), lambda bi, g, i: (bi, 0, g)),
            pl.BlockSpec((1, 1, tq, HEADS_PER_GROUP), lambda bi, g, i: (bi, g, i, 0)),
            pl.BlockSpec((1, 1, n_kv, HEADS_PER_GROUP, tk), lambda bi, g, i: (bi, g, 0, 0, 0)),
        ],
        out_specs=pl.BlockSpec((1, tq, GROUP_LANES), lambda bi, g, i: (bi, i, g)),
        out_shape=jax.ShapeDtypeStruct((b, t, D_ATTN), BF16),
        scratch_shapes=[
            pltpu.VMEM((rows, GROUP_LANES), BF16),
            pltpu.VMEM((rows, LANES), F32),
            pltpu.VMEM((rows, tk), F32),
            pltpu.VMEM((rows, tk), BF16),
            pltpu.VMEM((rows, GROUP_LANES), F32),
            pltpu.VMEM((rows, LANES), F32),
            pltpu.VMEM((rows, LANES), F32),
            pltpu.VMEM((rows, LANES), F32),
        ],
        compiler_params=_params("parallel", "parallel", "arbitrary"),
        name="fox_attention",
    )(q, k_all, v_all, fq, fk)


def _rnn_kernel(xr_ref, gg_ref, hist_ref, h0_ref, cw_ref, cb_ref, wa_ref, ba_ref, wx_ref, bx_ref,
                lam_ref, out_ref, hl_ref, win_sc, h_sc, *, tb):
    t = pl.program_id(1)

    @pl.when(t == 0)
    def _():
        win_sc[:SUBLANES, :] = hist_ref[0]
        h_sc[...] = h0_ref[0]

    x = xr_ref[0]
    win_sc[SUBLANES:, :] = x
    xc = x * cw_ref[CONV_WIDTH - 1:CONV_WIDTH, :] + cb_ref[...]
    for s in range(1, CONV_WIDTH):
        xc = xc + win_sc[SUBLANES - s:SUBLANES - s + tb, :] * cw_ref[CONV_WIDTH - 1 - s:CONV_WIDTH - s, :]
    win_sc[:SUBLANES, :] = x[tb - SUBLANES:tb]

    xcb = xc.astype(BF16)
    r = _sigmoid(jnp.dot(xcb, wa_ref[...], preferred_element_type=F32) + ba_ref[...])
    ig = _sigmoid(jnp.dot(xcb, wx_ref[...], preferred_element_type=F32) + bx_ref[...])
    log_a = (-LRU_C) * r * _softplus(-lam_ref[...])
    a = jnp.exp(log_a)
    one_minus_a2 = 1.0 - a * a
    scale = jnp.where(one_minus_a2 > 0.0, one_minus_a2 * lax.rsqrt(one_minus_a2), 0.0)
    bterm = scale * ig * xc

    grouped = (tb // SUBLANES, SUBLANES, D_RNN)
    a, bterm = a.reshape(grouped), bterm.reshape(grouped)
    row_in_group = lax.broadcasted_iota(jnp.int32, grouped, 1)
    d = 1
    while d < SUBLANES:
        valid = row_in_group >= d
        a_s = pltpu.roll(a, d, axis=1)
        b_s = pltpu.roll(bterm, d, axis=1)
        bterm = jnp.where(valid, a * b_s + bterm, bterm)
        a = jnp.where(valid, a * a_s, a)
        d *= 2
    h_last = h_sc[...]
    groups = []
    for g in range(tb // SUBLANES):
        h_g = bterm[g] + a[g] * h_last
        h_last = h_g[SUBLANES - 1:SUBLANES]
        groups.append(h_g)
    h = jnp.concatenate(groups, axis=0)
    h_sc[...] = h_last
    hl_ref[0] = h_last
    out_ref[0] = (h * gg_ref[0].astype(F32)).astype(BF16)


def _rnn(xr, gg, hist8, h0, conv_w, conv_b, wa, ba, wx, bx, lam, tb):
    b, t, _ = xr.shape
    blk = pl.BlockSpec((1, tb, D_RNN), lambda bi, ti: (bi, ti, 0))
    full = lambda a: pl.BlockSpec(a.shape, lambda bi, ti: (0,) * a.ndim)
    per_b = lambda r: pl.BlockSpec((1, r, D_RNN), lambda bi, ti: (bi, 0, 0))
    return pl.pallas_call(
        functools.partial(_rnn_kernel, tb=tb),
        grid=(b, t // tb),
        in_specs=[blk, blk, per_b(SUBLANES), per_b(1), full(conv_w), full(conv_b), full(wa), full(ba),
                  full(wx), full(bx), full(lam)],
        out_specs=(blk, per_b(1)),
        out_shape=(jax.ShapeDtypeStruct((b, t, D_RNN), BF16), jax.ShapeDtypeStruct((b, 1, D_RNN), F32)),
        scratch_shapes=[pltpu.VMEM((SUBLANES + tb, D_RNN), F32), pltpu.VMEM((1, D_RNN), F32)],
        compiler_params=_params("parallel", "arbitrary"),
        name="conv_rglru",
    )(xr, gg, hist8, h0, conv_w, conv_b, wa, ba, wx, bx, lam)


def _outproj_kernel(o_ref, r_ref, sa_ref, sr_ref, x_ref, wau_ref, wru_ref, wo_ref, g_ref, b_ref,
                    h_ref, *, alpha):
    up_a = jnp.dot(o_ref[...], wau_ref[...], preferred_element_type=F32)
    up_r = jnp.dot(r_ref[...], wru_ref[...], preferred_element_type=F32)
    merged = sa_ref[...].astype(F32) * up_a + sr_ref[...].astype(F32) * up_r
    mix = jnp.dot(merged.astype(BF16), wo_ref[...], preferred_element_type=F32)
    h_ref[...] = _layer_norm(alpha * x_ref[...] + mix, g_ref[...], b_ref[...])


def _outproj(o2, r2, sa, sr, x2, wau, wru, wo, g, bta, alpha, tm):
    n = x2.shape[0]
    row = lambda w: pl.BlockSpec((tm, w), lambda i: (i, 0))
    full = lambda a: pl.BlockSpec(a.shape, lambda i: (0,) * a.ndim)
    return pl.pallas_call(
        functools.partial(_outproj_kernel, alpha=alpha),
        grid=(n // tm,),
        in_specs=[row(D_ATTN), row(D_RNN), row(D_MODEL), row(D_MODEL), row(D_MODEL), full(wau), full(wru),
                  full(wo), full(g), full(bta)],
        out_specs=row(D_MODEL),
        out_shape=jax.ShapeDtypeStruct((n, D_MODEL), F32),
        compiler_params=_params("parallel"),
        name="outproj_ln1",
    )(o2, r2, sa, sr, x2, wau, wru, wo, g, bta)


ROUTE_TOKENS = 2 * LANES


def _odd_even_merge_sort(n):
    def merge(lo, hi, r):
        step = r * 2
        if step < hi - lo:
            yield from merge(lo, hi, step)
            yield from merge(lo + r, hi, step)
            yield from ((i, i + r) for i in range(lo + r, hi - r, step))
        else:
            yield (lo, lo + r)

    def sort(lo, hi):
        if hi - lo >= 1:
            mid = lo + (hi - lo) // 2
            yield from sort(lo, mid)
            yield from sort(mid + 1, hi)
            yield from merge(lo, hi, 1)

    return tuple(sort(0, n - 1))


_SORT_TOPK = _odd_even_merge_sort(PEER_TOPK)


def _compare_exchange(v, i, j):
    v[i], v[j] = jnp.maximum(v[i], v[j]), jnp.minimum(v[i], v[j])


def _sort_bitonic(v):
    d = PEER_TOPK // 2
    while d >= 1:
        for i in range(PEER_TOPK):
            if i & d == 0:
                _compare_exchange(v, i, i + d)
        d //= 2
    return v


def _top_k_sorted(tiles, presorted=False):
    v = list(tiles)
    if not presorted:
        for i, j in _SORT_TOPK:
            _compare_exchange(v, i, j)
    shift = SUBLANES // 2
    while shift >= 1:
        other = [pltpu.roll(x, shift, axis=0) for x in v]
        v = _sort_bitonic([jnp.maximum(v[i], other[PEER_TOPK - 1 - i]) for i in range(PEER_TOPK)])
        shift //= 2
    return v


def _count_greater(x, t):
    assert len(t) == 16, "the bisection below is written out for 16 entries"
    one = lambda m, w: jnp.where(m, float(w), 0.0)
    b3 = t[7] > x
    b2 = jnp.where(b3, t[11], t[3]) > x
    b1 = jnp.where(b3, jnp.where(b2, t[13], t[9]), jnp.where(b2, t[5], t[1])) > x
    hi = jnp.where(b2, jnp.where(b1, t[14], t[12]), jnp.where(b1, t[10], t[8]))
    lo = jnp.where(b2, jnp.where(b1, t[6], t[4]), jnp.where(b1, t[2], t[0]))
    b0 = jnp.where(b3, hi, lo) > x
    count = one(b3, 8) + one(b2, 4) + one(b1, 2) + one(b0, 1)
    return jnp.where(t[15] > x, float(PEER_TOPK), count)


def _route_kernel(h_ref, wq_ref, k1_ref, k2_ref, ht_ref, cnt_ref, e1_ref, r2_ref, e2_ref,
                  qt_sc, t1_sc, t2_sc, *, tm):
    ht = jnp.transpose(h_ref[...]).astype(BF16)
    ht_ref[...] = ht
    qt_sc[...] = jnp.dot(wq_ref[...], ht, preferred_element_type=F32).astype(BF16)
    key_iota = lax.broadcasted_iota(jnp.int32, (N_KEYS, tm), 0).astype(F32)
    top_iota = lax.broadcasted_iota(jnp.int32, (PEER_TOPK, tm), 0).astype(F32)
    front_rows = SUBLANES

    def scores(hd):
        base = hd * 2 * PEER_HALF
        s1 = jnp.dot(k1_ref[...], qt_sc[base:base + PEER_HALF], preferred_element_type=F32)
        s2 = jnp.dot(k2_ref[...], qt_sc[base + PEER_HALF:base + 2 * PEER_HALF], preferred_element_type=F32)
        return s1, s2

    sub = lax.broadcasted_iota(jnp.int32, (SUBLANES, tm), 0)
    tiles_of = lambda s: [s[r * SUBLANES:(r + 1) * SUBLANES] for r in range(N_KEYS // SUBLANES)]
    sublane_sum = lambda x: jnp.sum(x, axis=0, keepdims=True)

    def route_head_sorted(hd):
        s1, s2 = scores(hd)
        rows1, rows2 = tiles_of(s1), tiles_of(s2)
        t1 = _top_k_sorted(rows1)
        t2 = _top_k_sorted(rows2)
        t1_lo = t1[SUBLANES - 1]
        for a in range(SUBLANES - 2, -1, -1):
            t1_lo = jnp.where(sub == a, t1[a], t1_lo)
        cand = [t1_lo + t2[b] for b in range(PEER_TOPK)]
        top_lo = _top_k_sorted(cand, presorted=True)
        hi_sums = [t1[a] + t2[0] for a in range(SUBLANES, PEER_TOPK)]
        top = _sort_bitonic(top_lo[:SUBLANES] + [jnp.maximum(top_lo[i], hi_sums[PEER_TOPK - 1 - i])
                                                 for i in range(SUBLANES, PEER_TOPK)])
        tau = top[PEER_TOPK - 1]
        cnt_lo = functools.reduce(jnp.add, [jnp.where(c >= tau, 1.0, 0.0) for c in cand])
        cnt = [jnp.broadcast_to(cnt_lo[a:a + 1], (SUBLANES, tm)) for a in range(SUBLANES)]
        cnt += [jnp.where(s >= tau, 1.0, 0.0) for s in hi_sums]
        z = functools.reduce(jnp.add, [jnp.exp(t - top[0]) for t in top])

        cnt1_rows, rank2_rows = [], []
        for r in range(N_KEYS // SUBLANES):
            c = jnp.where(rows1[r] + t2[0] >= tau, 1.0, 0.0)
            c = jnp.where(rows1[r] >= t1[PEER_TOPK - 1], c, 0.0)
            for a in range(SUBLANES - 1, -1, -1):
                c = jnp.where(rows1[r] >= t1[a], cnt[a], c)
            cnt1_rows.append(c)
            g = _count_greater(rows2[r], t2)
            rank2_rows.append(jnp.where(g < float(PEER_TOPK), g, NOT_SELECTED_RANK))
        cnt1 = jnp.concatenate(cnt1_rows, axis=0)
        rank2 = jnp.concatenate(rank2_rows, axis=0)
        cnt_ref[hd] = cnt1
        e1_ref[hd] = jnp.exp(s1 - t1[0][0:1]) / z[0:1]
        r2_ref[hd] = rank2.astype(BF16)
        e2_ref[hd] = jnp.exp(s2 - t2[0][0:1]).astype(BF16)

        gap = lambda t: functools.reduce(jnp.minimum, [t[b] - t[b + 1] for b in range(PEER_TOPK - 1)])[0:1]
        n_cnt = sublane_sum(cnt_lo) + functools.reduce(jnp.add, cnt[SUBLANES:])[0:1]
        n_sel = sublane_sum(functools.reduce(jnp.add, cnt1_rows))
        n_rank = sublane_sum(functools.reduce(
            jnp.add, [jnp.where(x < float(PEER_TOPK), 1.0, 0.0) for x in rank2_rows]))
        off = lambda n: jnp.abs(n - float(PEER_TOPK))
        return (off(n_cnt) + off(n_sel) + off(n_rank)
                + jnp.where(jnp.minimum(gap(t1), gap(t2)) > 0.0, 0.0, 1.0))

    def route_head_exact(hd):
        def pick_one(v, iota, n):
            m = jnp.max(v, axis=0, keepdims=True)
            return m, iota == jnp.min(jnp.where(v == m, iota, float(n)), axis=0, keepdims=True)

        s1, s2 = scores(hd)

        def extract(a, carry):
            v1, r1, v2, r2 = carry
            m1, sel1 = pick_one(v1, key_iota, N_KEYS)
            m2, sel2 = pick_one(v2, key_iota, N_KEYS)
            t1_sc[pl.ds(a, 1), :] = m1
            t2_sc[pl.ds(a, 1), :] = m2
            af = jnp.asarray(a, dtype=F32)
            return (jnp.where(sel1, -jnp.inf, v1), jnp.where(sel1, af, r1),
                    jnp.where(sel2, -jnp.inf, v2), jnp.where(sel2, af, r2))

        no_rank = jnp.full((N_KEYS, tm), NOT_SELECTED_RANK, F32)
        _, rank1, _, rank2 = lax.fori_loop(0, PEER_TOPK, extract, (s1, no_rank, s2, no_rank))
        t1 = t1_sc[...]
        t2 = t2_sc[...]
        top0 = t1[0:1] + t2[0:1]

        def pick(_, carry):
            ptr, front, z = carry
            m, sel = pick_one(front, top_iota, PEER_TOPK)
            ptr = ptr + jnp.where(sel, 1.0, 0.0)
            lo = ptr[:front_rows]
            nxt = jnp.full(lo.shape, -jnp.inf, F32)
            for b in range(1, PEER_TOPK):
                nxt = jnp.where(lo == float(b), t2_sc[b:b + 1, :], nxt)
            nxt = jnp.where(lo == 0.0, t2[0:1], nxt)
            front = jnp.concatenate(
                [t1[:front_rows] + nxt, jnp.where(sel[front_rows:], -jnp.inf, front[front_rows:])], axis=0)
            return ptr, front, z + jnp.exp(m - top0)

        cnt, _, z = lax.fori_loop(
            0, PEER_TOPK, pick,
            (jnp.zeros((PEER_TOPK, tm), F32), t1 + t2[0:1], jnp.zeros((1, tm), F32)))

        cnt1 = jnp.zeros((N_KEYS, tm), F32)
        for a in range(PEER_TOPK):
            cnt1 = jnp.where(rank1 == float(a), cnt[a:a + 1], cnt1)
        cnt_ref[hd] = cnt1
        e1_ref[hd] = jnp.exp(s1 - t1[0:1]) / z
        r2_ref[hd] = rank2.astype(BF16)
        e2_ref[hd] = jnp.exp(s2 - t2[0:1]).astype(BF16)

    doubt = [route_head_sorted(hd) for hd in range(PEER_HEADS)]

    @pl.when(jnp.max(functools.reduce(jnp.maximum, doubt)) > 0.0)
    def _():
        for hd in range(PEER_HEADS):
            @pl.when(jnp.max(doubt[hd]) > 0.0)
            def _():
                route_head_exact(hd)


def _route(h2, wq_t, k1, k2, tm):
    n = h2.shape[0]
    full = lambda a: pl.BlockSpec(a.shape, lambda i: (0,) * a.ndim)
    per_head = pl.BlockSpec((PEER_HEADS, N_KEYS, tm), lambda i: (0, 0, i))
    hshape = lambda dt: jax.ShapeDtypeStruct((PEER_HEADS, N_KEYS, n), dt)
    return pl.pallas_call(
        functools.partial(_route_kernel, tm=tm),
        grid=(n // tm,),
        in_specs=[pl.BlockSpec((tm, D_MODEL), lambda i: (i, 0)), full(wq_t), full(k1), full(k2)],
        out_specs=(pl.BlockSpec((D_MODEL, tm), lambda i: (0, i)), per_head, per_head, per_head, per_head),
        out_shape=(jax.ShapeDtypeStruct((D_MODEL, n), BF16), hshape(F32), hshape(F32), hshape(BF16),
                   hshape(BF16)),
        scratch_shapes=[pltpu.VMEM((PEER_HEADS * 2 * PEER_HALF, tm), BF16),
                        pltpu.VMEM((PEER_TOPK, tm), F32), pltpu.VMEM((PEER_TOPK, tm), F32)],
        compiler_params=_params("parallel"),
        name="peer_route",
    )(h2, wq_t, k1, k2)


ROWS_PER_STEP = 16
EXPERTS_PER_STEP = ROWS_PER_STEP * N_KEYS
ROWS_PER_SUB = 4
EXPERTS_PER_SUB = ROWS_PER_SUB * N_KEYS
DENSE_CHUNK = 2 * LANES


def _bf16_row_tile(row):
    tile = jnp.broadcast_to(row, (BF16_ROWS, row.shape[1])).astype(BF16)
    return jnp.concatenate([tile] * (N_KEYS // BF16_ROWS), axis=0)


def _dense_kernel(ht_ref, u_ref, vt_ref, cnt_ref, e1_ref, r2_ref, e2_ref, h_ref, g_ref, b_ref,
                  y_ref, acc_sc, act_sc, coef_sc, *, alpha):
    e = pl.program_id(1)
    tm = act_sc.shape[1]
    n_sub = ROWS_PER_STEP // ROWS_PER_SUB

    @pl.when(e == 0)
    def _():
        acc_sc[...] = jnp.zeros_like(acc_sc)

    def activations(sub):
        rows = slice(sub * EXPERTS_PER_SUB, (sub + 1) * EXPERTS_PER_SUB)
        act_sc[rows, :] = jnp.dot(u_ref[rows, :], ht_ref[...], preferred_element_type=F32)

    def tiles(sub):
        chunk = min(DENSE_CHUNK, tm)
        for il in range(sub * ROWS_PER_SUB, (sub + 1) * ROWS_PER_SUB):
            for c in range(tm // chunk):
                yield il, slice(il * N_KEYS, (il + 1) * N_KEYS), slice(c * chunk, (c + 1) * chunk)

    def routing_weights(sub):
        for il, rows, lanes in tiles(sub):
            w = None
            for hd in range(PEER_HEADS):
                cnt = _bf16_row_tile(cnt_ref[hd, il:il + 1, lanes])
                e1 = _bf16_row_tile(e1_ref[hd, il:il + 1, lanes])
                term = jnp.where(r2_ref[hd, :, lanes] < cnt, e2_ref[hd, :, lanes] * e1, jnp.zeros((), BF16))
                w = term if w is None else w + term
            coef_sc[rows, lanes] = w

    def coefficients(sub):
        for _, rows, lanes in tiles(sub):
            coef_sc[rows, lanes] = coef_sc[rows, lanes] * _gelu_tanh(act_sc[rows, lanes].astype(BF16))

    routing_weights(0)
    activations(0)
    for sub in range(n_sub):
        if sub + 1 < n_sub:
            routing_weights(sub + 1)
            activations(sub + 1)
        coefficients(sub)
    acc_sc[...] += jnp.dot(vt_ref[...], coef_sc[...], preferred_element_type=F32)

    @pl.when(e == pl.num_programs(1) - 1)
    def _():
        peer = jnp.transpose(acc_sc[...])
        y_ref[...] = _layer_norm(alpha * h_ref[...] + peer, g_ref[...], b_ref[...])


def _dense(ht, u_b, vt_b, cnt1, e1n, rank2, e2, h2, g, bta, alpha, tm):
    n = h2.shape[0]
    n_e = N_EXPERTS // EXPERTS_PER_STEP
    full = lambda a: pl.BlockSpec(a.shape, lambda t, e: (0,) * a.ndim)
    rows_blk = pl.BlockSpec((PEER_HEADS, ROWS_PER_STEP, tm), lambda t, e: (0, e, t))
    cols_blk = pl.BlockSpec((PEER_HEADS, N_KEYS, tm), lambda t, e: (0, 0, t))
    return pl.pallas_call(
        functools.partial(_dense_kernel, alpha=alpha),
        grid=(n // tm, n_e),
        in_specs=[
            pl.BlockSpec((D_MODEL, tm), lambda t, e: (0, t)),
            pl.BlockSpec((EXPERTS_PER_STEP, D_MODEL), lambda t, e: (e, 0)),
            pl.BlockSpec((D_MODEL, EXPERTS_PER_STEP), lambda t, e: (0, e)),
            rows_blk, rows_blk, cols_blk, cols_blk,
            pl.BlockSpec((tm, D_MODEL), lambda t, e: (t, 0)),
            full(g), full(bta),
        ],
        out_specs=pl.BlockSpec((tm, D_MODEL), lambda t, e: (t, 0)),
        out_shape=jax.ShapeDtypeStruct((n, D_MODEL), F32),
        scratch_shapes=[pltpu.VMEM((D_MODEL, tm), F32), pltpu.VMEM((EXPERTS_PER_STEP, tm), F32),
                        pltpu.VMEM((EXPERTS_PER_STEP, tm), BF16)],
        compiler_params=_params("parallel", "arbitrary"),
        name="peer_dense",
    )(ht, u_b, vt_b, cnt1, e1n, rank2, e2, h2, g, bta)


def _block_diag(w):
    nb, bi, bo = w.shape
    eye = jnp.eye(nb, dtype=w.dtype)
    return (eye[:, None, :, None] * w[:, :, None, :]).reshape(nb * bi, nb * bo)


def _prep_weights(w_in, b_forget, conv_w, conv_b, w_rg_a, b_rg_a, w_rg_x, b_rg_x, lru_lambda,
                  w_attn_up, w_rnn_up, w_out, ln1_g, ln1_b, peer_w_query, peer_keys_1, peer_keys_2,
                  peer_u, peer_v, ln2_g, ln2_b):
    c_f = 3 * D_ATTN
    w_in_p = jnp.concatenate(
        [w_in[:, :c_f], jnp.pad(w_in[:, c_f:c_f + N_ATTN_HEADS], ((0, 0), (0, F_PAD - N_ATTN_HEADS))),
         w_in[:, c_f + N_ATTN_HEADS:]], axis=1).astype(BF16)
    row = lambda a: a.reshape(1, -1).astype(F32)
    return dict(
        w_in=w_in_p, b_forget=b_forget.reshape(-1, 1).astype(F32), conv_w=conv_w.astype(F32), conv_b=row(conv_b),
        wa=_block_diag(w_rg_a).astype(BF16), ba=row(b_rg_a), wx=_block_diag(w_rg_x).astype(BF16),
        bx=row(b_rg_x), lam=row(lru_lambda),
        wau=w_attn_up.astype(BF16), wru=w_rnn_up.astype(BF16), wo=w_out.astype(BF16),
        ln1_g=row(ln1_g), ln1_b=row(ln1_b),
        wq_t=jnp.transpose(peer_w_query).astype(BF16), k1=peer_keys_1.astype(BF16),
        k2=peer_keys_2.astype(BF16), u=peer_u.astype(BF16), vt=jnp.transpose(peer_v).astype(BF16),
        ln2_g=row(ln2_g), ln2_b=row(ln2_b),
    )


def _pick_block(n, target):
    blk = min(n, target)
    assert n % blk == 0, (n, blk)
    return blk


def _trunk_layer(x, past_k, past_v, past_logf, conv_hist, h0, p, alpha):
    bsz, t, _ = x.shape
    n = bsz * t
    x2 = x.reshape(n, D_MODEL)
    q, k, v, kb, vb, lf, xr, gg, sa, sr = _inproj(x2, p["w_in"], p["b_forget"], bsz, t, _pick_block(n, 256))

    def state_layout(a):
        if a.ndim == 3:
            return jnp.transpose(a.reshape(bsz, N_ATTN_HEADS, ATTN_HEAD_DIM, t), (0, 3, 1, 2))
        return a.reshape(bsz, t, N_ATTN_HEADS, ATTN_HEAD_DIM)

    tq = _pick_block(t, ATTN_Q_BLOCK)
    n_past = 0 if past_k is None else past_k.shape[1]
    t_all = n_past + t
    t_lanes = -(-t_all // LANES) * LANES
    tk = ATTN_K_BLOCK if t_lanes % ATTN_K_BLOCK == 0 else t_lanes
    t_pad = -(-t_all // tk) * tk
    lf_bht = jnp.transpose(lf.reshape(N_ATTN_HEADS, bsz, t), (1, 0, 2))
    lf_all = lf_bht
    kb3 = kb.reshape(bsz, t, D_ATTN)
    vb3 = vb.reshape(bsz, t, D_ATTN)
    if past_k is not None:
        lf_all = jnp.concatenate([jnp.transpose(past_logf.astype(F32), (0, 2, 1)), lf_bht], axis=2)
        kb3 = jnp.concatenate([past_k.reshape(bsz, n_past, D_ATTN).astype(BF16), kb3], axis=1)
        vb3 = jnp.concatenate([past_v.reshape(bsz, n_past, D_ATTN).astype(BF16), vb3], axis=1)
    pad = ((0, 0), (0, t_pad - t_all), (0, 0))
    kb3, vb3 = jnp.pad(kb3, pad), jnp.pad(vb3, pad)
    f_t = _cumsum_time(jnp.pad(lf_all, ((0, 0), (0, 0), (0, t_pad - t_all))))
    f_g = f_t.reshape(bsz, N_HEAD_GROUPS, HEADS_PER_GROUP, t_pad)
    fq = jnp.transpose(f_g[:, :, :, n_past:n_past + t], (0, 1, 3, 2))
    fk = jnp.transpose(f_g.reshape(bsz, N_HEAD_GROUPS, HEADS_PER_GROUP, t_pad // tk, tk), (0, 1, 3, 2, 4))
    o = _attention(q.reshape(bsz, t, D_ATTN), kb3, vb3, fq, fk, tq=tq, tk=tk, q_off=n_past)

    hist8 = jnp.pad(conv_hist.astype(F32), ((0, 0), (SUBLANES - (CONV_WIDTH - 1), 0), (0, 0)))
    xr3 = xr.reshape(bsz, t, D_RNN)
    rnn_out, h_last = _rnn(xr3, gg.reshape(bsz, t, D_RNN), hist8, h0.astype(F32).reshape(bsz, 1, D_RNN),
                           p["conv_w"], p["conv_b"], p["wa"], p["ba"], p["wx"], p["bx"], p["lam"],
                           _pick_block(t, 1024))
    new_hist = jnp.concatenate([conv_hist.astype(F32), xr3], axis=1)[:, -(CONV_WIDTH - 1):]

    h = _outproj(o.reshape(n, D_ATTN), rnn_out.reshape(n, D_RNN), sa, sr, x2, p["wau"], p["wru"], p["wo"],
                 p["ln1_g"], p["ln1_b"], alpha, _pick_block(n, 1024))
    ht, cnt1, e1n, rank2, e2 = _route(h, p["wq_t"], p["k1"], p["k2"], _pick_block(n, ROUTE_TOKENS))
    y = _dense(ht, p["u"], p["vt"], cnt1, e1n, rank2, e2, h, p["ln2_g"], p["ln2_b"], alpha,
               _pick_block(n, 512))
    return (y.reshape(bsz, t, D_MODEL), state_layout(k), state_layout(v), jnp.transpose(lf_bht, (0, 2, 1)), new_hist,
            h_last.reshape(bsz, D_RNN))


def kernel(x_prompt, x_sample, cache_k, cache_v, cache_logf, state_conv, state_rnn, w_in, b_forget, conv_w, conv_b, w_rg_a, b_rg_a, w_rg_x, b_rg_x, lru_lambda, w_attn_up, w_rnn_up, w_out, ln1_g, ln1_b, peer_w_query, peer_keys_1, peer_keys_2, peer_u, peer_v, ln2_g, ln2_b):
    depth = w_in.shape[0]
    alpha = (2 * depth) ** 0.25
    layer_weights = (w_in, b_forget, conv_w, conv_b, w_rg_a, b_rg_a, w_rg_x, b_rg_x, lru_lambda, w_attn_up,
                     w_rnn_up, w_out, ln1_g, ln1_b, peer_w_query, peer_keys_1, peer_keys_2, peer_u, peer_v,
                     ln2_g, ln2_b)
    hp, hs = x_prompt, x_sample
    prompt_state, sample_state = [], []
    for l in range(depth):
        p = _prep_weights(*(w[l] for w in layer_weights))
        zero_hist = jnp.zeros((hp.shape[0], CONV_WIDTH - 1, D_RNN), F32)
        zero_h = jnp.zeros((hp.shape[0], D_RNN), F32)
        hp, *st_p = _trunk_layer(hp, None, None, None, zero_hist, zero_h, p, alpha)
        hs, *st_s = _trunk_layer(hs, cache_k[l], cache_v[l], cache_logf[l], state_conv[l], state_rnn[l], p,
                                 alpha)
        prompt_state.append(st_p)
        sample_state.append(st_s)
    stack = lambda states, i: jnp.stack([s[i] for s in states])
    return (hp, hs) + tuple(stack(prompt_state, i) for i in range(5)) + tuple(
        stack(sample_state, i) for i in range(5))
```

```python
import functools
import math

import jax
import jax.numpy as jnp
from jax import lax
from jax.experimental import pallas as pl
from jax.experimental.pallas import tpu as pltpu

F32 = jnp.float32
BF16 = jnp.bfloat16

D_MODEL = 1024
N_ATTN_HEADS = 8
ATTN_HEAD_DIM = 64
D_ATTN = N_ATTN_HEADS * ATTN_HEAD_DIM
ATTN_SCALE = ATTN_HEAD_DIM ** -0.5
D_RNN = 512
CONV_WIDTH = 4
LRU_C = 8.0
N_KEYS = 128
N_EXPERTS = N_KEYS * N_KEYS
PEER_HEADS = 8
PEER_TOPK = 16
PEER_HALF = 128
LN_EPS = 1e-5

LANES = 128
SUBLANES = 8
BF16_ROWS = 2 * SUBLANES
VMEM_LIMIT_BYTES = 56 * 1024 * 1024

HEADS_PER_GROUP = 4
GROUP_LANES = HEADS_PER_GROUP * ATTN_HEAD_DIM
N_HEAD_GROUPS = N_ATTN_HEADS // HEADS_PER_GROUP
ATTN_Q_BLOCK = 512
ATTN_K_BLOCK = 512
ATTN_ROW_BLOCK = 32
F_PAD = LANES
NEG_BIG = -1e30
LOG2_E = math.log2(math.e)
NOT_SELECTED_RANK = 99.0

_C_Q = 0
_C_K = _C_Q + D_ATTN
_C_V = _C_K + D_ATTN
_C_F = _C_V + D_ATTN
_C_XR = _C_F + F_PAD
_C_GATE = _C_XR + D_RNN
_C_GA = _C_GATE + D_RNN
_C_GR = _C_GA + D_MODEL
_C_END = _C_GR + D_MODEL


def _params(*sem):
    return pltpu.CompilerParams(dimension_semantics=sem, vmem_limit_bytes=VMEM_LIMIT_BYTES)


def _sigmoid(x):
    return 1.0 / (1.0 + jnp.exp(-x))


def _gelu_tanh(x):
    half = 0.5 * x
    return half + half * jnp.tanh(x * (0.7978845608028654 + 0.035677408136300125 * (x * x)))


def _softplus(x):
    return jnp.maximum(x, 0.0) + jnp.log1p(jnp.exp(-jnp.abs(x)))


def _layer_norm(x, g, b):
    mu = jnp.mean(x, axis=-1, keepdims=True)
    xc = x - mu
    var = jnp.mean(xc * xc, axis=-1, keepdims=True)
    return xc * lax.rsqrt(var + LN_EPS) * g + b


def _inproj_kernel(x_ref, w_ref, bf_ref, q_ref, k_ref, v_ref, kb_ref, vb_ref, lf_ref, xr_ref,
                   gg_ref, sa_ref, sr_ref, *, time_minor_kv):
    xb = x_ref[...].astype(BF16)

    def mm(lo, hi):
        return jnp.dot(xb, w_ref[:, lo:hi], preferred_element_type=F32)

    q_ref[...] = (mm(_C_Q, _C_K) * (ATTN_SCALE * LOG2_E)).astype(BF16)
    k = mm(_C_K, _C_V)
    kb_ref[...] = k.astype(BF16)
    v = mm(_C_V, _C_F)
    vb_ref[...] = v.astype(BF16)
    if time_minor_kv:
        k_ref[0] = jnp.transpose(k)
        v_ref[0] = jnp.transpose(v)
    else:
        k_ref[...] = k
        v_ref[...] = v
    f = jnp.transpose(mm(_C_F, _C_XR))[:N_ATTN_HEADS] + bf_ref[...]
    lf_ref[...] = -_softplus(-f)
    xr_ref[...] = mm(_C_XR, _C_GATE)
    gg_ref[...] = _gelu_tanh(mm(_C_GATE, _C_GA)).astype(BF16)
    sa_ref[...] = _sigmoid(mm(_C_GA, _C_GR)).astype(BF16)
    sr_ref[...] = _sigmoid(mm(_C_GR, _C_END)).astype(BF16)


def _inproj(x2, w_in_p, b_forget, bsz, t, tm):
    n = x2.shape[0]
    row = lambda w: pl.BlockSpec((tm, w), lambda i: (i, 0))
    full = lambda a: pl.BlockSpec(a.shape, lambda i: (0,) * a.ndim)
    time_minor_kv = tm % LANES == 0 and t % tm == 0
    if time_minor_kv:
        kv_shape = jax.ShapeDtypeStruct((bsz, D_ATTN, t), F32)
        kv_spec = pl.BlockSpec((1, D_ATTN, tm), lambda i: (i // (t // tm), 0, i % (t // tm)))
    else:
        kv_shape = jax.ShapeDtypeStruct((n, D_ATTN), F32)
        kv_spec = row(D_ATTN)
    out_shape = (
        jax.ShapeDtypeStruct((n, D_ATTN), BF16),
        kv_shape,
        kv_shape,
        jax.ShapeDtypeStruct((n, D_ATTN), BF16),
        jax.ShapeDtypeStruct((n, D_ATTN), BF16),
        jax.ShapeDtypeStruct((N_ATTN_HEADS, n), F32),
        jax.ShapeDtypeStruct((n, D_RNN), F32),
        jax.ShapeDtypeStruct((n, D_RNN), BF16),
        jax.ShapeDtypeStruct((n, D_MODEL), BF16),
        jax.ShapeDtypeStruct((n, D_MODEL), BF16),
    )
    out_specs = (row(D_ATTN), kv_spec, kv_spec, row(D_ATTN), row(D_ATTN),
                 pl.BlockSpec((N_ATTN_HEADS, tm), lambda i: (0, i)),
                 row(D_RNN), row(D_RNN), row(D_MODEL), row(D_MODEL))
    return pl.pallas_call(
        functools.partial(_inproj_kernel, time_minor_kv=time_minor_kv),
        grid=(n // tm,),
        in_specs=[row(D_MODEL), full(w_in_p), full(b_forget)],
        out_specs=out_specs,
        out_shape=out_shape,
        compiler_params=_params("parallel"),
        name="inproj",
    )(x2, w_in_p, b_forget)


def _cumsum_kernel(x_ref, o_ref):
    x = x_ref[0]
    t = x.shape[1]
    lane = lax.broadcasted_iota(jnp.int32, x.shape, 1)
    d = 1
    while d < t:
        x = x + jnp.where(lane >= d, pltpu.roll(x, d, axis=1), 0.0)
        d *= 2
    o_ref[0] = x * LOG2_E


def _cumsum_time(lf_t):
    b, h, t = lf_t.shape
    spec = pl.BlockSpec((1, h, t), lambda i: (i, 0, 0))
    return pl.pallas_call(
        _cumsum_kernel, grid=(b,), in_specs=[spec], out_specs=spec,
        out_shape=jax.ShapeDtypeStruct(lf_t.shape, F32),
        compiler_params=_params("parallel"), name="logf_cumsum",
    )(lf_t)


def _attn_kernel(q_ref, k_ref, v_ref, fq_ref, fk_ref, o_ref, q_sc, fq_sc, s_sc, p_sc, acc_sc, m_sc, al_sc,
                 lp_sc, *, tq, tk, q_off):
    qi = pl.program_id(2)
    row0 = q_off + qi * tq
    n_full = (row0 + 1) // tk
    heads = range(HEADS_PER_GROUP)
    rb = min(ATTN_ROW_BLOCK, tq)
    lane = lax.broadcasted_iota(jnp.int32, (tq, GROUP_LANES), 1)
    in_head = [(lane >= j * ATTN_HEAD_DIM) & (lane < (j + 1) * ATTN_HEAD_DIM) for j in heads]
    q = q_ref[0]
    for j in heads:
        q_sc[j * tq:(j + 1) * tq, :] = jnp.where(in_head[j], q, jnp.zeros_like(q))
        fq_sc[j * tq:(j + 1) * tq, :] = jnp.broadcast_to(fq_ref[0, 0, :, j:j + 1], (tq, LANES))
    m_sc[...] = jnp.full_like(m_sc, NEG_BIG)
    lp_sc[...] = jnp.zeros_like(lp_sc)
    acc_sc[...] = jnp.zeros_like(acc_sc)
    per_head = 2 if tq // rb >= 2 else 1
    stream_rows = tq // per_head
    halves = [slice(i * stream_rows, (i + 1) * stream_rows) for i in range(per_head * HEADS_PER_GROUP)]
    n_lane_tiles = tk // LANES

    def scores(c, s_sc):
        ks = k_ref[0, pl.ds(pl.multiple_of(c * tk, tk), tk), :]
        for hv in halves:
            s_sc[hv, :] = lax.dot_general(q_sc[hv, :], ks, (((1,), (1,)), ((), ())),
                                          preferred_element_type=F32)

    def absorb(c, s_sc, masked):
        start = pl.multiple_of(c * tk, tk)
        fk = fk_ref[0, 0, c]
        vs = v_ref[0, pl.ds(start, tk), :]

        def weighted_values(hv):
            pv = jnp.dot(p_sc[hv, :], vs, preferred_element_type=F32)
            alpha = al_sc[hv, :]
            acc_sc[hv, :] = jnp.concatenate([alpha] * (GROUP_LANES // LANES), axis=1) * acc_sc[hv, :] + pv

        for j in heads:
            for r in range(tq // rb):
                rows = slice(j * tq + r * rb, j * tq + (r + 1) * rb)
                s = s_sc[rows, :] - fk[j:j + 1, :]
                if masked:
                    q_pos = row0 + r * rb + lax.broadcasted_iota(jnp.int32, (rb, tk), 0)
                    k_pos = c * tk + lax.broadcasted_iota(jnp.int32, (rb, tk), 1)
                    s = jnp.where(k_pos <= q_pos, s, NEG_BIG)
                tiles = [s[:, t * LANES:(t + 1) * LANES] for t in range(n_lane_tiles)]
                row_max = jnp.max(functools.reduce(jnp.maximum, tiles), axis=1, keepdims=True)
                fq = fq_sc[rows, :]
                m_old = m_sc[rows, :]
                m_new = jnp.maximum(m_old, jnp.broadcast_to(row_max, (rb, LANES)) + fq)
                alpha = jnp.exp2(m_old - m_new)
                shift = fq - m_new
                p_tiles = [jnp.exp2(tile + shift) for tile in tiles]
                m_sc[rows, :] = m_new
                al_sc[rows, :] = alpha
                lp_sc[rows, :] = alpha * lp_sc[rows, :] + functools.reduce(jnp.add, p_tiles)
                p_sc[rows, :] = jnp.concatenate(p_tiles, axis=1).astype(BF16)
                if (r + 1) * rb % stream_rows == 0:
                    weighted_values(halves[(j * tq + (r + 1) * rb) // stream_rows - 1])

    def step(c, carry):
        scores(c, s_sc)
        absorb(c, s_sc, masked=False)
        return carry

    lax.fori_loop(0, n_full, step, 0)
    scores(n_full, s_sc)
    absorb(n_full, s_sc, masked=True)
    o_all = acc_sc[...] / jnp.sum(lp_sc[...], axis=1, keepdims=True)
    out = jnp.zeros((tq, GROUP_LANES), F32)
    for j in heads:
        out = jnp.where(in_head[j], o_all[j * tq:(j + 1) * tq], out)
    o_ref[0] = out.astype(BF16)


def _attention(q, k_all, v_all, fq, fk, *, tq, tk, q_off):
    b, t, _ = q.shape
    t_k = k_all.shape[1]
    n_kv = t_k // tk
    assert all((q_off + i * tq) % tk + tq <= tk for i in range(t // tq)), "a query block straddles key chunks"
    kern = functools.partial(_attn_kernel, tq=tq, tk=tk, q_off=q_off)
    rows = HEADS_PER_GROUP * tq
    return pl.pallas_call(
        kern,
        grid=(b, N_HEAD_GROUPS, t // tq),
        in_specs=[
            pl.BlockSpec((1, tq, GROUP_LANES), lambda bi, g, i: (bi, i, g)),
            pl.BlockSpec((1, t_k, GROUP_LANES), lambda bi, g, i: (bi, 0, g)),
            pl.BlockSpec((1, t_k, GROUP_LANES), lambda bi, g, i: (bi, 0, g)),
            pl.BlockSpec((1, 1, tq, HEADS_PER_GROUP), lambda bi, g, i: (bi, g, i, 0)),
            pl.BlockSpec((1, 1, n_kv, HEADS_PER_GROUP, tk), lambda bi, g, i: (bi, g, 0, 0, 0)),
        ],
        out_specs=pl.BlockSpec((1, tq, GROUP_LANES), lambda bi, g, i: (bi, i, g)),
        out_shape=jax.ShapeDtypeStruct((b, t, D_ATTN), BF16),
        scratch_shapes=[
            pltpu.VMEM((rows, GROUP_LANES), BF16),
            pltpu.VMEM((rows, LANES), F32),
            pltpu.VMEM((rows, tk), F32),
            pltpu.VMEM((rows, tk), BF16),
            pltpu.VMEM((rows, GROUP_LANES), F32),
            pltpu.VMEM((rows, LANES), F32),
            pltpu.VMEM((rows, LANES), F32),
            pltpu.VMEM((rows, LANES), F32),
        ],
        compiler_params=_params("parallel", "parallel", "arbitrary"),
        name="fox_attention",
    )(q, k_all, v_all, fq, fk)


def _rnn_kernel(xr_ref, gg_ref, hist_ref, h0_ref, cw_ref, cb_ref, wa_ref, ba_ref, wx_ref, bx_ref,
                lam_ref, out_ref, hl_ref, win_sc, h_sc, *, tb):
    t = pl.program_id(1)

    @pl.when(t == 0)
    def _():
        win_sc[:SUBLANES, :] = hist_ref[0]
        h_sc[...] = h0_ref[0]

    x = xr_ref[0]
    win_sc[SUBLANES:, :] = x
    xc = x * cw_ref[CONV_WIDTH - 1:CONV_WIDTH, :] + cb_ref[...]
    for s in range(1, CONV_WIDTH):
        xc = xc + win_sc[SUBLANES - s:SUBLANES - s + tb, :] * cw_ref[CONV_WIDTH - 1 - s:CONV_WIDTH - s, :]
    win_sc[:SUBLANES, :] = x[tb - SUBLANES:tb]

    xcb = xc.astype(BF16)
    r = _sigmoid(jnp.dot(xcb, wa_ref[...], preferred_element_type=F32) + ba_ref[...])
    ig = _sigmoid(jnp.dot(xcb, wx_ref[...], preferred_element_type=F32) + bx_ref[...])
    log_a = (-LRU_C) * r * _softplus(-lam_ref[...])
    a = jnp.exp(log_a)
    one_minus_a2 = 1.0 - a * a
    scale = jnp.where(one_minus_a2 > 0.0, one_minus_a2 * lax.rsqrt(one_minus_a2), 0.0)
    bterm = scale * ig * xc

    grouped = (tb // SUBLANES, SUBLANES, D_RNN)
    a, bterm = a.reshape(grouped), bterm.reshape(grouped)
    row_in_group = lax.broadcasted_iota(jnp.int32, grouped, 1)
    d = 1
    while d < SUBLANES:
        valid = row_in_group >= d
        a_s = pltpu.roll(a, d, axis=1)
        b_s = pltpu.roll(bterm, d, axis=1)
        bterm = jnp.where(valid, a * b_s + bterm, bterm)
        a = jnp.where(valid, a * a_s, a)
        d *= 2
    h_last = h_sc[...]
    groups = []
    for g in range(tb // SUBLANES):
        h_g = bterm[g] + a[g] * h_last
        h_last = h_g[SUBLANES - 1:SUBLANES]
        groups.append(h_g)
    h = jnp.concatenate(groups, axis=0)
    h_sc[...] = h_last
    hl_ref[0] = h_last
    out_ref[0] = (h * gg_ref[0].astype(F32)).astype(BF16)


def _rnn(xr, gg, hist8, h0, conv_w, conv_b, wa, ba, wx, bx, lam, tb):
    b, t, _ = xr.shape
    blk = pl.BlockSpec((1, tb, D_RNN), lambda bi, ti: (bi, ti, 0))
    full = lambda a: pl.BlockSpec(a.shape, lambda bi, ti: (0,) * a.ndim)
    per_b = lambda r: pl.BlockSpec((1, r, D_RNN), lambda bi, ti: (bi, 0, 0))
    return pl.pallas_call(
        functools.partial(_rnn_kernel, tb=tb),
        grid=(b, t // tb),
        in_specs=[blk, blk, per_b(SUBLANES), per_b(1), full(conv_w), full(conv_b), full(wa), full(ba),
                  full(wx), full(bx), full(lam)],
        out_specs=(blk, per_b(1)),
        out_shape=(jax.ShapeDtypeStruct((b, t, D_RNN), BF16), jax.ShapeDtypeStruct((b, 1, D_RNN), F32)),
        scratch_shapes=[pltpu.VMEM((SUBLANES + tb, D_RNN), F32), pltpu.VMEM((1, D_RNN), F32)],
        compiler_params=_params("parallel", "arbitrary"),
        name="conv_rglru",
    )(xr, gg, hist8, h0, conv_w, conv_b, wa, ba, wx, bx, lam)


def _outproj_kernel(o_ref, r_ref, sa_ref, sr_ref, x_ref, wau_ref, wru_ref, wo_ref, g_ref, b_ref,
                    h_ref, *, alpha):
    up_a = jnp.dot(o_ref[...], wau_ref[...], preferred_element_type=F32)
    up_r = jnp.dot(r_ref[...], wru_ref[...], preferred_element_type=F32)
    merged = sa_ref[...].astype(F32) * up_a + sr_ref[...].astype(F32) * up_r
    mix = jnp.dot(merged.astype(BF16), wo_ref[...], preferred_element_type=F32)
    h_ref[...] = _layer_norm(alpha * x_ref[...] + mix, g_ref[...], b_ref[...])


def _outproj(o2, r2, sa, sr, x2, wau, wru, wo, g, bta, alpha, tm):
    n = x2.shape[0]
    row = lambda w: pl.BlockSpec((tm, w), lambda i: (i, 0))
    full = lambda a: pl.BlockSpec(a.shape, lambda i: (0,) * a.ndim)
    return pl.pallas_call(
        functools.partial(_outproj_kernel, alpha=alpha),
        grid=(n // tm,),
        in_specs=[row(D_ATTN), row(D_RNN), row(D_MODEL), row(D_MODEL), row(D_MODEL), full(wau), full(wru),
                  full(wo), full(g), full(bta)],
        out_specs=row(D_MODEL),
        out_shape=jax.ShapeDtypeStruct((n, D_MODEL), F32),
        compiler_params=_params("parallel"),
        name="outproj_ln1",
    )(o2, r2, sa, sr, x2, wau, wru, wo, g, bta)


ROUTE_TOKENS = 2 * LANES


def _odd_even_merge_sort(n):
    def merge(lo, hi, r):
        step = r * 2
        if step < hi - lo:
            yield from merge(lo, hi, step)
            yield from merge(lo + r, hi, step)
            yield from ((i, i + r) for i in range(lo + r, hi - r, step))
        else:
            yield (lo, lo + r)

    def sort(lo, hi):
        if hi - lo >= 1:
            mid = lo + (hi - lo) // 2
            yield from sort(lo, mid)
            yield from sort(mid + 1, hi)
            yield from merge(lo, hi, 1)

    return tuple(sort(0, n - 1))


_SORT_TOPK = _odd_even_merge_sort(PEER_TOPK)


def _compare_exchange(v, i, j):
    v[i], v[j] = jnp.maximum(v[i], v[j]), jnp.minimum(v[i], v[j])


def _sort_bitonic(v):
    d = PEER_TOPK // 2
    while d >= 1:
        for i in range(PEER_TOPK):
            if i & d == 0:
                _compare_exchange(v, i, i + d)
        d //= 2
    return v


def _top_k_sorted(tiles, presorted=False):
    v = list(tiles)
    if not presorted:
        for i, j in _SORT_TOPK:
            _compare_exchange(v, i, j)
    shift = SUBLANES // 2
    while shift >= 1:
        other = [pltpu.roll(x, shift, axis=0) for x in v]
        v = _sort_bitonic([jnp.maximum(v[i], other[PEER_TOPK - 1 - i]) for i in range(PEER_TOPK)])
        shift //= 2
    return v


def _count_greater(x, t):
    assert len(t) == 16, "the bisection below is written out for 16 entries"
    one = lambda m, w: jnp.where(m, float(w), 0.0)
    b3 = t[7] > x
    b2 = jnp.where(b3, t[11], t[3]) > x
    b1 = jnp.where(b3, jnp.where(b2, t[13], t[9]), jnp.where(b2, t[5], t[1])) > x
    hi = jnp.where(b2, jnp.where(b1, t[14], t[12]), jnp.where(b1, t[10], t[8]))
    lo = jnp.where(b2, jnp.where(b1, t[6], t[4]), jnp.where(b1, t[2], t[0]))
    b0 = jnp.where(b3, hi, lo) > x
    count = one(b3, 8) + one(b2, 4) + one(b1, 2) + one(b0, 1)
    return jnp.where(t[15] > x, float(PEER_TOPK), count)


def _route_kernel(h_ref, wq_ref, k1_ref, k2_ref, ht_ref, cnt_ref, e1_ref, r2_ref, e2_ref,
                  qt_sc, t1_sc, t2_sc, *, tm):
    ht = jnp.transpose(h_ref[...]).astype(BF16)
    ht_ref[...] = ht
    qt_sc[...] = jnp.dot(wq_ref[...], ht, preferred_element_type=F32).astype(BF16)
    key_iota = lax.broadcasted_iota(jnp.int32, (N_KEYS, tm), 0).astype(F32)
    top_iota = lax.broadcasted_iota(jnp.int32, (PEER_TOPK, tm), 0).astype(F32)
    front_rows = SUBLANES

    def scores(hd):
        base = hd * 2 * PEER_HALF
        s1 = jnp.dot(k1_ref[...], qt_sc[base:base + PEER_HALF], preferred_element_type=F32)
        s2 = jnp.dot(k2_ref[...], qt_sc[base + PEER_HALF:base + 2 * PEER_HALF], preferred_element_type=F32)
        return s1, s2

    sub = lax.broadcasted_iota(jnp.int32, (SUBLANES, tm), 0)
    tiles_of = lambda s: [s[r * SUBLANES:(r + 1) * SUBLANES] for r in range(N_KEYS // SUBLANES)]
    sublane_sum = lambda x: jnp.sum(x, axis=0, keepdims=True)

    def route_head_sorted(hd):
        s1, s2 = scores(hd)
        rows1, rows2 = tiles_of(s1), tiles_of(s2)
        t1 = _top_k_sorted(rows1)
        t2 = _top_k_sorted(rows2)
        t1_lo = t1[SUBLANES - 1]
        for a in range(SUBLANES - 2, -1, -1):
            t1_lo = jnp.where(sub == a, t1[a], t1_lo)
        cand = [t1_lo + t2[b] for b in range(PEER_TOPK)]
        top_lo = _top_k_sorted(cand, presorted=True)
        hi_sums = [t1[a] + t2[0] for a in range(SUBLANES, PEER_TOPK)]
        top = _sort_bitonic(top_lo[:SUBLANES] + [jnp.maximum(top_lo[i], hi_sums[PEER_TOPK - 1 - i])
                                                 for i in range(SUBLANES, PEER_TOPK)])
        tau = top[PEER_TOPK - 1]
        cnt_lo = functools.reduce(jnp.add, [jnp.where(c >= tau, 1.0, 0.0) for c in cand])
        cnt = [jnp.broadcast_to(cnt_lo[a:a + 1], (SUBLANES, tm)) for a in range(SUBLANES)]
        cnt += [jnp.where(s >= tau, 1.0, 0.0) for s in hi_sums]
        z = functools.reduce(jnp.add, [jnp.exp(t - top[0]) for t in top])

        cnt1_rows, rank2_rows = [], []
        for r in range(N_KEYS // SUBLANES):
            c = jnp.where(rows1[r] + t2[0] >= tau, 1.0, 0.0)
            c = jnp.where(rows1[r] >= t1[PEER_TOPK - 1], c, 0.0)
            for a in range(SUBLANES - 1, -1, -1):
                c = jnp.where(rows1[r] >= t1[a], cnt[a], c)
            cnt1_rows.append(c)
            g = _count_greater(rows2[r], t2)
            rank2_rows.append(jnp.where(g < float(PEER_TOPK), g, NOT_SELECTED_RANK))
        cnt1 = jnp.concatenate(cnt1_rows, axis=0)
        rank2 = jnp.concatenate(rank2_rows, axis=0)
        cnt_ref[hd] = cnt1
        e1_ref[hd] = jnp.exp(s1 - t1[0][0:1]) / z[0:1]
        r2_ref[hd] = rank2.astype(BF16)
        e2_ref[hd] = jnp.exp(s2 - t2[0][0:1]).astype(BF16)

        gap = lambda t: functools.reduce(jnp.minimum, [t[b] - t[b + 1] for b in range(PEER_TOPK - 1)])[0:1]
        n_cnt = sublane_sum(cnt_lo) + functools.reduce(jnp.add, cnt[SUBLANES:])[0:1]
        n_sel = sublane_sum(functools.reduce(jnp.add, cnt1_rows))
        n_rank = sublane_sum(functools.reduce(
            jnp.add, [jnp.where(x < float(PEER_TOPK), 1.0, 0.0) for x in rank2_rows]))
        off = lambda n: jnp.abs(n - float(PEER_TOPK))
        return (off(n_cnt) + off(n_sel) + off(n_rank)
                + jnp.where(jnp.minimum(gap(t1), gap(t2)) > 0.0, 0.0, 1.0))

    def route_head_exact(hd):
        def pick_one(v, iota, n):
            m = jnp.max(v, axis=0, keepdims=True)
            return m, iota == jnp.min(jnp.where(v == m, iota, float(n)), axis=0, keepdims=True)

        s1, s2 = scores(hd)

        def extract(a, carry):
            v1, r1, v2, r2 = carry
            m1, sel1 = pick_one(v1, key_iota, N_KEYS)
            m2, sel2 = pick_one(v2, key_iota, N_KEYS)
            t1_sc[pl.ds(a, 1), :] = m1
            t2_sc[pl.ds(a, 1), :] = m2
            af = jnp.asarray(a, dtype=F32)
            return (jnp.where(sel1, -jnp.inf, v1), jnp.where(sel1, af, r1),
                    jnp.where(sel2, -jnp.inf, v2), jnp.where(sel2, af, r2))

        no_rank = jnp.full((N_KEYS, tm), NOT_SELECTED_RANK, F32)
        _, rank1, _, rank2 = lax.fori_loop(0, PEER_TOPK, extract, (s1, no_rank, s2, no_rank))
        t1 = t1_sc[...]
        t2 = t2_sc[...]
        top0 = t1[0:1] + t2[0:1]

        def pick(_, carry):
            ptr, front, z = carry
            m, sel = pick_one(front, top_iota, PEER_TOPK)
            ptr = ptr + jnp.where(sel, 1.0, 0.0)
            lo = ptr[:front_rows]
            nxt = jnp.full(lo.shape, -jnp.inf, F32)
            for b in range(1, PEER_TOPK):
                nxt = jnp.where(lo == float(b), t2_sc[b:b + 1, :], nxt)
            nxt = jnp.where(lo == 0.0, t2[0:1], nxt)
            front = jnp.concatenate(
                [t1[:front_rows] + nxt, jnp.where(sel[front_rows:], -jnp.inf, front[front_rows:])], axis=0)
            return ptr, front, z + jnp.exp(m - top0)

        cnt, _, z = lax.fori_loop(
            0, PEER_TOPK, pick,
            (jnp.zeros((PEER_TOPK, tm), F32), t1 + t2[0:1], jnp.zeros((1, tm), F32)))

        cnt1 = jnp.zeros((N_KEYS, tm), F32)
        for a in range(PEER_TOPK):
            cnt1 = jnp.where(rank1 == float(a), cnt[a:a + 1], cnt1)
        cnt_ref[hd] = cnt1
        e1_ref[hd] = jnp.exp(s1 - t1[0:1]) / z
        r2_ref[hd] = rank2.astype(BF16)
        e2_ref[hd] = jnp.exp(s2 - t2[0:1]).astype(BF16)

    doubt = [route_head_sorted(hd) for hd in range(PEER_HEADS)]

    @pl.when(jnp.max(functools.reduce(jnp.maximum, doubt)) > 0.0)
    def _():
        for hd in range(PEER_HEADS):
            @pl.when(jnp.max(doubt[hd]) > 0.0)
            def _():
                route_head_exact(hd)


def _route(h2, wq_t, k1, k2, tm):
    n = h2.shape[0]
    full = lambda a: pl.BlockSpec(a.shape, lambda i: (0,) * a.ndim)
    per_head = pl.BlockSpec((PEER_HEADS, N_KEYS, tm), lambda i: (0, 0, i))
    hshape = lambda dt: jax.ShapeDtypeStruct((PEER_HEADS, N_KEYS, n), dt)
    return pl.pallas_call(
        functools.partial(_route_kernel, tm=tm),
        grid=(n // tm,),
        in_specs=[pl.BlockSpec((tm, D_MODEL), lambda i: (i, 0)), full(wq_t), full(k1), full(k2)],
        out_specs=(pl.BlockSpec((D_MODEL, tm), lambda i: (0, i)), per_head, per_head, per_head, per_head),
        out_shape=(jax.ShapeDtypeStruct((D_MODEL, n), BF16), hshape(F32), hshape(F32), hshape(BF16),
                   hshape(BF16)),
        scratch_shapes=[pltpu.VMEM((PEER_HEADS * 2 * PEER_HALF, tm), BF16),
                        pltpu.VMEM((PEER_TOPK, tm), F32), pltpu.VMEM((PEER_TOPK, tm), F32)],
        compiler_params=_params("parallel"),
        name="peer_route",
    )(h2, wq_t, k1, k2)


ROWS_PER_STEP = 16
EXPERTS_PER_STEP = ROWS_PER_STEP * N_KEYS
ROWS_PER_SUB = 4
EXPERTS_PER_SUB = ROWS_PER_SUB * N_KEYS
DENSE_CHUNK = 2 * LANES


def _bf16_row_tile(row):
    tile = jnp.broadcast_to(row, (BF16_ROWS, row.shape[1])).astype(BF16)
    return jnp.concatenate([tile] * (N_KEYS // BF16_ROWS), axis=0)


def _dense_kernel(ht_ref, u_ref, vt_ref, cnt_ref, e1_ref, r2_ref, e2_ref, h_ref, g_ref, b_ref,
                  y_ref, acc_sc, act_sc, coef_sc, *, alpha):
    e = pl.program_id(1)
    tm = act_sc.shape[1]
    n_sub = ROWS_PER_STEP // ROWS_PER_SUB

    @pl.when(e == 0)
    def _():
        acc_sc[...] = jnp.zeros_like(acc_sc)

    def activations(sub):
        rows = slice(sub * EXPERTS_PER_SUB, (sub + 1) * EXPERTS_PER_SUB)
        act_sc[rows, :] = jnp.dot(u_ref[rows, :], ht_ref[...], preferred_element_type=F32)

    def tiles(sub):
        chunk = min(DENSE_CHUNK, tm)
        for il in range(sub * ROWS_PER_SUB, (sub + 1) * ROWS_PER_SUB):
            for c in range(tm // chunk):
                yield il, slice(il * N_KEYS, (il + 1) * N_KEYS), slice(c * chunk, (c + 1) * chunk)

    def routing_weights(sub):
        for il, rows, lanes in tiles(sub):
            w = None
            for hd in range(PEER_HEADS):
                cnt = _bf16_row_tile(cnt_ref[hd, il:il + 1, lanes])
                e1 = _bf16_row_tile(e1_ref[hd, il:il + 1, lanes])
                term = jnp.where(r2_ref[hd, :, lanes] < cnt, e2_ref[hd, :, lanes] * e1, jnp.zeros((), BF16))
                w = term if w is None else w + term
            coef_sc[rows, lanes] = w

    def coefficients(sub):
        for _, rows, lanes in tiles(sub):
            coef_sc[rows, lanes] = coef_sc[rows, lanes] * _gelu_tanh(act_sc[rows, lanes].astype(BF16))

    routing_weights(0)
    activations(0)
    for sub in range(n_sub):
        if sub + 1 < n_sub:
            routing_weights(sub + 1)
            activations(sub + 1)
        coefficients(sub)
    acc_sc[...] += jnp.dot(vt_ref[...], coef_sc[...], preferred_element_type=F32)

    @pl.when(e == pl.num_programs(1) - 1)
    def _():
        peer = jnp.transpose(acc_sc[...])
        y_ref[...] = _layer_norm(alpha * h_ref[...] + peer, g_ref[...], b_ref[...])


def _dense(ht, u_b, vt_b, cnt1, e1n, rank2, e2, h2, g, bta, alpha, tm):
    n = h2.shape[0]
    n_e = N_EXPERTS // EXPERTS_PER_STEP
    full = lambda a: pl.BlockSpec(a.shape, lambda t, e: (0,) * a.ndim)
    rows_blk = pl.BlockSpec((PEER_HEADS, ROWS_PER_STEP, tm), lambda t, e: (0, e, t))
    cols_blk = pl.BlockSpec((PEER_HEADS, N_KEYS, tm), lambda t, e: (0, 0, t))
    return pl.pallas_call(
        functools.partial(_dense_kernel, alpha=alpha),
        grid=(n // tm, n_e),
        in_specs=[
            pl.BlockSpec((D_MODEL, tm), lambda t, e: (0, t)),
            pl.BlockSpec((EXPERTS_PER_STEP, D_MODEL), lambda t, e: (e, 0)),
            pl.BlockSpec((D_MODEL, EXPERTS_PER_STEP), lambda t, e: (0, e)),
            rows_blk, rows_blk, cols_blk, cols_blk,
            pl.BlockSpec((tm, D_MODEL), lambda t, e: (t, 0)),
            full(g), full(bta),
        ],
        out_specs=pl.BlockSpec((tm, D_MODEL), lambda t, e: (t, 0)),
        out_shape=jax.ShapeDtypeStruct((n, D_MODEL), F32),
        scratch_shapes=[pltpu.VMEM((D_MODEL, tm), F32), pltpu.VMEM((EXPERTS_PER_STEP, tm), F32),
                        pltpu.VMEM((EXPERTS_PER_STEP, tm), BF16)],
        compiler_params=_params("parallel", "arbitrary"),
        name="peer_dense",
    )(ht, u_b, vt_b, cnt1, e1n, rank2, e2, h2, g, bta)


def _block_diag(w):
    nb, bi, bo = w.shape
    eye = jnp.eye(nb, dtype=w.dtype)
    return (eye[:, None, :, None] * w[:, :, None, :]).reshape(nb * bi, nb * bo)


def _prep_weights(w_in, b_forget, conv_w, conv_b, w_rg_a, b_rg_a, w_rg_x, b_rg_x, lru_lambda,
                  w_attn_up, w_rnn_up, w_out, ln1_g, ln1_b, peer_w_query, peer_keys_1, peer_keys_2,
                  peer_u, peer_v, ln2_g, ln2_b):
    c_f = 3 * D_ATTN
    w_in_p = jnp.concatenate(
        [w_in[:, :c_f], jnp.pad(w_in[:, c_f:c_f + N_ATTN_HEADS], ((0, 0), (0, F_PAD - N_ATTN_HEADS))),
         w_in[:, c_f + N_ATTN_HEADS:]], axis=1).astype(BF16)
    row = lambda a: a.reshape(1, -1).astype(F32)
    return dict(
        w_in=w_in_p, b_forget=b_forget.reshape(-1, 1).astype(F32), conv_w=conv_w.astype(F32), conv_b=row(conv_b),
        wa=_block_diag(w_rg_a).astype(BF16), ba=row(b_rg_a), wx=_block_diag(w_rg_x).astype(BF16),
        bx=row(b_rg_x), lam=row(lru_lambda),
        wau=w_attn_up.astype(BF16), wru=w_rnn_up.astype(BF16), wo=w_out.astype(BF16),
        ln1_g=row(ln1_g), ln1_b=row(ln1_b),
        wq_t=jnp.transpose(peer_w_query).astype(BF16), k1=peer_keys_1.astype(BF16),
        k2=peer_keys_2.astype(BF16), u=peer_u.astype(BF16), vt=jnp.transpose(peer_v).astype(BF16),
        ln2_g=row(ln2_g), ln2_b=row(ln2_b),
    )


def _pick_block(n, target):
    blk = min(n, target)
    assert n % blk == 0, (n, blk)
    return blk


def _trunk_layer(x, past_k, past_v, past_logf, conv_hist, h0, p, alpha):
    bsz, t, _ = x.shape
    n = bsz * t
    x2 = x.reshape(n, D_MODEL)
    q, k, v, kb, vb, lf, xr, gg, sa, sr = _inproj(x2, p["w_in"], p["b_forget"], bsz, t, _pick_block(n, 256))

    def state_layout(a):
        if a.ndim == 3:
            return jnp.transpose(a.reshape(bsz, N_ATTN_HEADS, ATTN_HEAD_DIM, t), (0, 3, 1, 2))
        return a.reshape(bsz, t, N_ATTN_HEADS, ATTN_HEAD_DIM)

    tq = _pick_block(t, ATTN_Q_BLOCK)
    n_past = 0 if past_k is None else past_k.shape[1]
    t_all = n_past + t
    t_lanes = -(-t_all // LANES) * LANES
    tk = ATTN_K_BLOCK if t_lanes % ATTN_K_BLOCK == 0 else t_lanes
    t_pad = -(-t_all // tk) * tk
    lf_bht = jnp.transpose(lf.reshape(N_ATTN_HEADS, bsz, t), (1, 0, 2))
    lf_all = lf_bht
    kb3 = kb.reshape(bsz, t, D_ATTN)
    vb3 = vb.reshape(bsz, t, D_ATTN)
    if past_k is not None:
        lf_all = jnp.concatenate([jnp.transpose(past_logf.astype(F32), (0, 2, 1)), lf_bht], axis=2)
        kb3 = jnp.concatenate([past_k.reshape(bsz, n_past, D_ATTN).astype(BF16), kb3], axis=1)
        vb3 = jnp.concatenate([past_v.reshape(bsz, n_past, D_ATTN).astype(BF16), vb3], axis=1)
    pad = ((0, 0), (0, t_pad - t_all), (0, 0))
    kb3, vb3 = jnp.pad(kb3, pad), jnp.pad(vb3, pad)
    f_t = _cumsum_time(jnp.pad(lf_all, ((0, 0), (0, 0), (0, t_pad - t_all))))
    f_g = f_t.reshape(bsz, N_HEAD_GROUPS, HEADS_PER_GROUP, t_pad)
    fq = jnp.transpose(f_g[:, :, :, n_past:n_past + t], (0, 1, 3, 2))
    fk = jnp.transpose(f_g.reshape(bsz, N_HEAD_GROUPS, HEADS_PER_GROUP, t_pad // tk, tk), (0, 1, 3, 2, 4))
    o = _attention(q.reshape(bsz, t, D_ATTN), kb3, vb3, fq, fk, tq=tq, tk=tk, q_off=n_past)

    hist8 = jnp.pad(conv_hist.astype(F32), ((0, 0), (SUBLANES - (CONV_WIDTH - 1), 0), (0, 0)))
    xr3 = xr.reshape(bsz, t, D_RNN)
    rnn_out, h_last = _rnn(xr3, gg.reshape(bsz, t, D_RNN), hist8, h0.astype(F32).reshape(bsz, 1, D_RNN),
                           p["conv_w"], p["conv_b"], p["wa"], p["ba"], p["wx"], p["bx"], p["lam"],
                           _pick_block(t, 1024))
    new_hist = jnp.concatenate([conv_hist.astype(F32), xr3], axis=1)[:, -(CONV_WIDTH - 1):]

    h = _outproj(o.reshape(n, D_ATTN), rnn_out.reshape(n, D_RNN), sa, sr, x2, p["wau"], p["wru"], p["wo"],
                 p["ln1_g"], p["ln1_b"], alpha, _pick_block(n, 1024))
    ht, cnt1, e1n, rank2, e2 = _route(h, p["wq_t"], p["k1"], p["k2"], _pick_block(n, ROUTE_TOKENS))
    y = _dense(ht, p["u"], p["vt"], cnt1, e1n, rank2, e2, h, p["ln2_g"], p["ln2_b"], alpha,
               _pick_block(n, 512))
    return (y.reshape(bsz, t, D_MODEL), state_layout(k), state_layout(v), jnp.transpose(lf_bht, (0, 2, 1)), new_hist,
            h_last.reshape(bsz, D_RNN))


def kernel(x_prompt, x_sample, cache_k, cache_v, cache_logf, state_conv, state_rnn, w_in, b_forget, conv_w, conv_b, w_rg_a, b_rg_a, w_rg_x, b_rg_x, lru_lambda, w_attn_up, w_rnn_up, w_out, ln1_g, ln1_b, peer_w_query, peer_keys_1, peer_keys_2, peer_u, peer_v, ln2_g, ln2_b):
    depth = w_in.shape[0]
    alpha = (2 * depth) ** 0.25
    layer_weights = (w_in, b_forget, conv_w, conv_b, w_rg_a, b_rg_a, w_rg_x, b_rg_x, lru_lambda, w_attn_up,
                     w_rnn_up, w_out, ln1_g, ln1_b, peer_w_query, peer_keys_1, peer_keys_2, peer_u, peer_v,
                     ln2_g, ln2_b)
    hp, hs = x_prompt, x_sample
    prompt_state, sample_state = [], []
    for l in range(depth):
        p = _prep_weights(*(w[l] for w in layer_weights))
        zero_hist = jnp.zeros((hp.shape[0], CONV_WIDTH - 1, D_RNN), F32)
        zero_h = jnp.zeros((hp.shape[0], D_RNN), F32)
        hp, *st_p = _trunk_layer(hp, None, None, None, zero_hist, zero_h, p, alpha)
        hs, *st_s = _trunk_layer(hs, cache_k[l], cache_v[l], cache_logf[l], state_conv[l], state_rnn[l], p,
                                 alpha)
        prompt_state.append(st_p)
        sample_state.append(st_s)
    stack = lambda states, i: jnp.stack([s[i] for s in states])
    return (hp, hs) + tuple(stack(prompt_state, i) for i in range(5)) + tuple(
        stack(sample_state, i) for i in range(5))
```

```python
import functools
import math

import jax
import jax.numpy as jnp
from jax import lax
from jax.experimental import pallas as pl
from jax.experimental.pallas import tpu as pltpu

F32 = jnp.float32
BF16 = jnp.bfloat16

D_MODEL = 1024
N_ATTN_HEADS = 8
ATTN_HEAD_DIM = 64
D_ATTN = N_ATTN_HEADS * ATTN_HEAD_DIM
ATTN_SCALE = ATTN_HEAD_DIM ** -0.5
D_RNN = 512
CONV_WIDTH = 4
LRU_C = 8.0
N_KEYS = 128
N_EXPERTS = N_KEYS * N_KEYS
PEER_HEADS = 8
PEER_TOPK = 16
PEER_HALF = 128
LN_EPS = 1e-5

LANES = 128
SUBLANES = 8
BF16_ROWS = 2 * SUBLANES
VMEM_LIMIT_BYTES = 56 * 1024 * 1024

HEADS_PER_GROUP = 4
GROUP_LANES = HEADS_PER_GROUP * ATTN_HEAD_DIM
N_HEAD_GROUPS = N_ATTN_HEADS // HEADS_PER_GROUP
ATTN_Q_BLOCK = 512
ATTN_K_BLOCK = 512
ATTN_ROW_BLOCK = 32
F_PAD = LANES
NEG_BIG = -1e30
LOG2_E = math.log2(math.e)
NOT_SELECTED_RANK = 99.0

_C_Q = 0
_C_K = _C_Q + D_ATTN
_C_V = _C_K + D_ATTN
_C_F = _C_V + D_ATTN
_C_XR = _C_F + F_PAD
_C_GATE = _C_XR + D_RNN
_C_GA = _C_GATE + D_RNN
_C_GR = _C_GA + D_MODEL
_C_END = _C_GR + D_MODEL


def _params(*sem):
    return pltpu.CompilerParams(dimension_semantics=sem, vmem_limit_bytes=VMEM_LIMIT_BYTES)


def _sigmoid(x):
    return 1.0 / (1.0 + jnp.exp(-x))


def _gelu_tanh(x):
    k0 = -2.0 * 0.7978845608028654 * LOG2_E
    k1 = -2.0 * 0.035677408136300125 * LOG2_E
    return x / (1.0 + jnp.exp2(x * (k0 + k1 * (x * x))))


def _softplus(x):
    return jnp.maximum(x, 0.0) + jnp.log1p(jnp.exp(-jnp.abs(x)))


def _layer_norm(x, g, b):
    mu = jnp.mean(x, axis=-1, keepdims=True)
    xc = x - mu
    var = jnp.mean(xc * xc, axis=-1, keepdims=True)
    return xc * lax.rsqrt(var + LN_EPS) * g + b


def _inproj_kernel(x_ref, w_ref, bf_ref, q_ref, k_ref, v_ref, kb_ref, vb_ref, lf_ref, xr_ref,
                   gg_ref, sa_ref, sr_ref, *, time_minor_kv):
    xb = x_ref[...].astype(BF16)

    def mm(lo, hi):
        return jnp.dot(xb, w_ref[:, lo:hi], preferred_element_type=F32)

    q_ref[...] = (mm(_C_Q, _C_K) * (ATTN_SCALE * LOG2_E)).astype(BF16)
    k = mm(_C_K, _C_V)
    kb_ref[...] = k.astype(BF16)
    v = mm(_C_V, _C_F)
    vb_ref[...] = v.astype(BF16)
    if time_minor_kv:
        k_ref[0] = jnp.transpose(k)
        v_ref[0] = jnp.transpose(v)
    else:
        k_ref[...] = k
        v_ref[...] = v
    f = jnp.transpose(mm(_C_F, _C_XR))[:N_ATTN_HEADS] + bf_ref[...]
    lf_ref[...] = -_softplus(-f)
    xr_ref[...] = mm(_C_XR, _C_GATE)
    gg_ref[...] = _gelu_tanh(mm(_C_GATE, _C_GA)).astype(BF16)
    sa_ref[...] = _sigmoid(mm(_C_GA, _C_GR)).astype(BF16)
    sr_ref[...] = _sigmoid(mm(_C_GR, _C_END)).astype(BF16)


def _inproj(x2, w_in_p, b_forget, bsz, t, tm):
    n = x2.shape[0]
    row = lambda w: pl.BlockSpec((tm, w), lambda i: (i, 0))
    full = lambda a: pl.BlockSpec(a.shape, lambda i: (0,) * a.ndim)
    time_minor_kv = tm % LANES == 0 and t % tm == 0
    if time_minor_kv:
        kv_shape = jax.ShapeDtypeStruct((bsz, D_ATTN, t), F32)
        kv_spec = pl.BlockSpec((1, D_ATTN, tm), lambda i: (i // (t // tm), 0, i % (t // tm)))
    else:
        kv_shape = jax.ShapeDtypeStruct((n, D_ATTN), F32)
        kv_spec = row(D_ATTN)
    out_shape = (
        jax.ShapeDtypeStruct((n, D_ATTN), BF16),
        kv_shape,
        kv_shape,
        jax.ShapeDtypeStruct((n, D_ATTN), BF16),
        jax.ShapeDtypeStruct((n, D_ATTN), BF16),
        jax.ShapeDtypeStruct((N_ATTN_HEADS, n), F32),
        jax.ShapeDtypeStruct((n, D_RNN), F32),
        jax.ShapeDtypeStruct((n, D_RNN), BF16),
        jax.ShapeDtypeStruct((n, D_MODEL), BF16),
        jax.ShapeDtypeStruct((n, D_MODEL), BF16),
    )
    out_specs = (row(D_ATTN), kv_spec, kv_spec, row(D_ATTN), row(D_ATTN),
                 pl.BlockSpec((N_ATTN_HEADS, tm), lambda i: (0, i)),
                 row(D_RNN), row(D_RNN), row(D_MODEL), row(D_MODEL))
    return pl.pallas_call(
        functools.partial(_inproj_kernel, time_minor_kv=time_minor_kv),
        grid=(n // tm,),
        in_specs=[row(D_MODEL), full(w_in_p), full(b_forget)],
        out_specs=out_specs,
        out_shape=out_shape,
        compiler_params=_params("parallel"),
        name="inproj",
    )(x2, w_in_p, b_forget)


def _cumsum_kernel(x_ref, o_ref):
    x = x_ref[0]
    t = x.shape[1]
    lane = lax.broadcasted_iota(jnp.int32, x.shape, 1)
    d = 1
    while d < t:
        x = x + jnp.where(lane >= d, pltpu.roll(x, d, axis=1), 0.0)
        d *= 2
    o_ref[0] = x * LOG2_E


def _cumsum_time(lf_t):
    b, h, t = lf_t.shape
    spec = pl.BlockSpec((1, h, t), lambda i: (i, 0, 0))
    return pl.pallas_call(
        _cumsum_kernel, grid=(b,), in_specs=[spec], out_specs=spec,
        out_shape=jax.ShapeDtypeStruct(lf_t.shape, F32),
        compiler_params=_params("parallel"), name="logf_cumsum",
    )(lf_t)


def _attn_kernel(q_ref, k_ref, v_ref, fq_ref, fk_ref, o_ref, q_sc, fq_sc, s_sc, p_sc, acc_sc, m_sc, al_sc,
                 lp_sc, *, tq, tk, q_off):
    qi = pl.program_id(2)
    row0 = q_off + qi * tq
    n_full = (row0 + 1) // tk
    heads = range(HEADS_PER_GROUP)
    rb = min(ATTN_ROW_BLOCK, tq)
    lane = lax.broadcasted_iota(jnp.int32, (tq, GROUP_LANES), 1)
    in_head = [(lane >= j * ATTN_HEAD_DIM) & (lane < (j + 1) * ATTN_HEAD_DIM) for j in heads]
    q = q_ref[0]
    for j in heads:
        q_sc[j * tq:(j + 1) * tq, :] = jnp.where(in_head[j], q, jnp.zeros_like(q))
        fq_sc[j * tq:(j + 1) * tq, :] = jnp.broadcast_to(fq_ref[0, 0, :, j:j + 1], (tq, LANES))
    m_sc[...] = jnp.full_like(m_sc, NEG_BIG)
    lp_sc[...] = jnp.zeros_like(lp_sc)
    acc_sc[...] = jnp.zeros_like(acc_sc)
    halves = [slice(0, 2 * tq), slice(2 * tq, 4 * tq)]
    n_lane_tiles = tk // LANES

    def scores(c, s_sc):
        ks = k_ref[0, pl.ds(pl.multiple_of(c * tk, tk), tk), :]
        for hv in halves:
            s_sc[hv, :] = lax.dot_general(q_sc[hv, :], ks, (((1,), (1,)), ((), ())),
                                          preferred_element_type=F32)

    def absorb(c, s_sc, masked):
        start = pl.multiple_of(c * tk, tk)
        fk = fk_ref[0, 0, c]
        vs = v_ref[0, pl.ds(start, tk), :]

        def weighted_values(hv):
            pv = jnp.dot(p_sc[hv, :], vs, preferred_element_type=F32)
            alpha = al_sc[hv, :]
            acc_sc[hv, :] = jnp.concatenate([alpha] * (GROUP_LANES // LANES), axis=1) * acc_sc[hv, :] + pv

        for j in heads:
            for r in range(tq // rb):
                rows = slice(j * tq + r * rb, j * tq + (r + 1) * rb)
                s = s_sc[rows, :] - fk[j:j + 1, :]
                if masked:
                    q_pos = row0 + r * rb + lax.broadcasted_iota(jnp.int32, (rb, tk), 0)
                    k_pos = c * tk + lax.broadcasted_iota(jnp.int32, (rb, tk), 1)
                    s = jnp.where(k_pos <= q_pos, s, NEG_BIG)
                tiles = [s[:, t * LANES:(t + 1) * LANES] for t in range(n_lane_tiles)]
                row_max = jnp.max(functools.reduce(jnp.maximum, tiles), axis=1, keepdims=True)
                fq = fq_sc[rows, :]
                m_old = m_sc[rows, :]
                m_new = jnp.maximum(m_old, jnp.broadcast_to(row_max, (rb, LANES)) + fq)
                alpha = jnp.exp2(m_old - m_new)
                shift = fq - m_new
                p_tiles = [jnp.exp2(tile + shift) for tile in tiles]
                m_sc[rows, :] = m_new
                al_sc[rows, :] = alpha
                lp_sc[rows, :] = alpha * lp_sc[rows, :] + functools.reduce(jnp.add, p_tiles)
                p_sc[rows, :] = jnp.concatenate(p_tiles, axis=1).astype(BF16)
            if j % 2 == 1:
                weighted_values(halves[j // 2])

    def step(c, carry):
        scores(c, s_sc)
        absorb(c, s_sc, masked=False)
        return carry

    lax.fori_loop(0, n_full, step, 0)
    scores(n_full, s_sc)
    absorb(n_full, s_sc, masked=True)
    o_all = acc_sc[...] / jnp.sum(lp_sc[...], axis=1, keepdims=True)
    out = jnp.zeros((tq, GROUP_LANES), F32)
    for j in heads:
        out = jnp.where(in_head[j], o_all[j * tq:(j + 1) * tq], out)
    o_ref[0] = out.astype(BF16)


def _attention(q, k_all, v_all, fq, fk, *, tq, tk, q_off):
    b, t, _ = q.shape
    t_k = k_all.shape[1]
    n_kv = t_k // tk
    assert all((q_off + i * tq) % tk + tq <= tk for i in range(t // tq)), "a query block straddles key chunks"
    kern = functools.partial(_attn_kernel, tq=tq, tk=tk, q_off=q_off)
    rows = HEADS_PER_GROUP * tq
    return pl.pallas_call(
        kern,
        grid=(b, N_HEAD_GROUPS, t // tq),
        in_specs=[
            pl.BlockSpec((1, tq, GROUP_LANES), lambda bi, g, i: (bi, i, g)),
            pl.BlockSpec((1, t_k, GROUP_LANES), lambda bi, g, i: (bi, 0, g)),
            pl.BlockSpec((1, t_k, GROUP_LANES), lambda bi, g, i: (bi, 0, g)),
            pl.BlockSpec((1, 1, tq, HEADS_PER_GROUP), lambda bi, g, i: (bi, g, i, 0)),
            pl.BlockSpec((1, 1, n_kv, HEADS_PER_GROUP, tk), lambda bi, g, i: (bi, g, 0, 0, 0)),
        ],
        out_specs=pl.BlockSpec((1, tq, GROUP_LANES), lambda bi, g, i: (bi, i, g)),
        out_shape=jax.ShapeDtypeStruct((b, t, D_ATTN), BF16),
        scratch_shapes=[
            pltpu.VMEM((rows, GROUP_LANES), BF16),
            pltpu.VMEM((rows, LANES), F32),
            pltpu.VMEM((rows, tk), F32),
            pltpu.VMEM((rows, tk), BF16),
            pltpu.VMEM((rows, GROUP_LANES), F32),
            pltpu.VMEM((rows, LANES), F32),
            pltpu.VMEM((rows, LANES), F32),
            pltpu.VMEM((rows, LANES), F32),
        ],
        compiler_params=_params("parallel", "parallel", "arbitrary"),
        name="fox_attention",
    )(q, k_all, v_all, fq, fk)


def _rnn_kernel(xr_ref, gg_ref, hist_ref, h0_ref, cw_ref, cb_ref, wa_ref, ba_ref, wx_ref, bx_ref,
                lam_ref, out_ref, hl_ref, win_sc, h_sc, *, tb):
    t = pl.program_id(1)

    @pl.when(t == 0)
    def _():
        win_sc[:SUBLANES, :] = hist_ref[0]
        h_sc[...] = h0_ref[0]

    x = xr_ref[0]
    win_sc[SUBLANES:, :] = x
    xc = x * cw_ref[CONV_WIDTH - 1:CONV_WIDTH, :] + cb_ref[...]
    for s in range(1, CONV_WIDTH):
        xc = xc + win_sc[SUBLANES - s:SUBLANES - s + tb, :] * cw_ref[CONV_WIDTH - 1 - s:CONV_WIDTH - s, :]
    win_sc[:SUBLANES, :] = x[tb - SUBLANES:tb]

    xcb = xc.astype(BF16)
    r = _sigmoid(jnp.dot(xcb, wa_ref[...], preferred_element_type=F32) + ba_ref[...])
    ig = _sigmoid(jnp.dot(xcb, wx_ref[...], preferred_element_type=F32) + bx_ref[...])
    log_a = (-LRU_C) * r * _softplus(-lam_ref[...])
    a = jnp.exp(log_a)
    one_minus_a2 = 1.0 - a * a
    scale = jnp.where(one_minus_a2 > 0.0, one_minus_a2 * lax.rsqrt(one_minus_a2), 0.0)
    bterm = scale * ig * xc

    grouped = (tb // SUBLANES, SUBLANES, D_RNN)
    a, bterm = a.reshape(grouped), bterm.reshape(grouped)
    row_in_group = lax.broadcasted_iota(jnp.int32, grouped, 1)
    d = 1
    while d < SUBLANES:
        valid = row_in_group >= d
        a_s = pltpu.roll(a, d, axis=1)
        b_s = pltpu.roll(bterm, d, axis=1)
        bterm = jnp.where(valid, a * b_s + bterm, bterm)
        a = jnp.where(valid, a * a_s, a)
        d *= 2
    h_last = h_sc[...]
    groups = []
    for g in range(tb // SUBLANES):
        h_g = bterm[g] + a[g] * h_last
        h_last = h_g[SUBLANES - 1:SUBLANES]
        groups.append(h_g)
    h = jnp.concatenate(groups, axis=0)
    h_sc[...] = h_last
    hl_ref[0] = h_last
    out_ref[0] = (h * gg_ref[0].astype(F32)).astype(BF16)


def _rnn(xr, gg, hist8, h0, conv_w, conv_b, wa, ba, wx, bx, lam, tb):
    b, t, _ = xr.shape
    blk = pl.BlockSpec((1, tb, D_RNN), lambda bi, ti: (bi, ti, 0))
    full = lambda a: pl.BlockSpec(a.shape, lambda bi, ti: (0,) * a.ndim)
    per_b = lambda r: pl.BlockSpec((1, r, D_RNN), lambda bi, ti: (bi, 0, 0))
    return pl.pallas_call(
        functools.partial(_rnn_kernel, tb=tb),
        grid=(b, t // tb),
        in_specs=[blk, blk, per_b(SUBLANES), per_b(1), full(conv_w), full(conv_b), full(wa), full(ba),
                  full(wx), full(bx), full(lam)],
        out_specs=(blk, per_b(1)),
        out_shape=(jax.ShapeDtypeStruct((b, t, D_RNN), BF16), jax.ShapeDtypeStruct((b, 1, D_RNN), F32)),
        scratch_shapes=[pltpu.VMEM((SUBLANES + tb, D_RNN), F32), pltpu.VMEM((1, D_RNN), F32)],
        compiler_params=_params("parallel", "arbitrary"),
        name="conv_rglru",
    )(xr, gg, hist8, h0, conv_w, conv_b, wa, ba, wx, bx, lam)


def _outproj_kernel(o_ref, r_ref, sa_ref, sr_ref, x_ref, wau_ref, wru_ref, wo_ref, g_ref, b_ref,
                    h_ref, *, alpha):
    up_a = jnp.dot(o_ref[...], wau_ref[...], preferred_element_type=F32)
    up_r = jnp.dot(r_ref[...], wru_ref[...], preferred_element_type=F32)
    merged = sa_ref[...].astype(F32) * up_a + sr_ref[...].astype(F32) * up_r
    mix = jnp.dot(merged.astype(BF16), wo_ref[...], preferred_element_type=F32)
    h_ref[...] = _layer_norm(alpha * x_ref[...] + mix, g_ref[...], b_ref[...])


def _outproj(o2, r2, sa, sr, x2, wau, wru, wo, g, bta, alpha, tm):
    n = x2.shape[0]
    row = lambda w: pl.BlockSpec((tm, w), lambda i: (i, 0))
    full = lambda a: pl.BlockSpec(a.shape, lambda i: (0,) * a.ndim)
    return pl.pallas_call(
        functools.partial(_outproj_kernel, alpha=alpha),
        grid=(n // tm,),
        in_specs=[row(D_ATTN), row(D_RNN), row(D_MODEL), row(D_MODEL), row(D_MODEL), full(wau), full(wru),
                  full(wo), full(g), full(bta)],
        out_specs=row(D_MODEL),
        out_shape=jax.ShapeDtypeStruct((n, D_MODEL), F32),
        compiler_params=_params("parallel"),
        name="outproj_ln1",
    )(o2, r2, sa, sr, x2, wau, wru, wo, g, bta)


ROUTE_TOKENS = 2 * LANES


def _odd_even_merge_sort(n):
    def merge(lo, hi, r):
        step = r * 2
        if step < hi - lo:
            yield from merge(lo, hi, step)
            yield from merge(lo + r, hi, step)
            yield from ((i, i + r) for i in range(lo + r, hi - r, step))
        else:
            yield (lo, lo + r)

    def sort(lo, hi):
        if hi - lo >= 1:
            mid = lo + (hi - lo) // 2
            yield from sort(lo, mid)
            yield from sort(mid + 1, hi)
            yield from merge(lo, hi, 1)

    return tuple(sort(0, n - 1))


_SORT_TOPK = _odd_even_merge_sort(PEER_TOPK)


def _compare_exchange(v, i, j):
    v[i], v[j] = jnp.maximum(v[i], v[j]), jnp.minimum(v[i], v[j])


def _sort_bitonic(v):
    d = PEER_TOPK // 2
    while d >= 1:
        for i in range(PEER_TOPK):
            if i & d == 0:
                _compare_exchange(v, i, i + d)
        d //= 2
    return v


def _top_k_sorted(tiles, presorted=False):
    v = list(tiles)
    if not presorted:
        for i, j in _SORT_TOPK:
            _compare_exchange(v, i, j)
    shift = SUBLANES // 2
    while shift >= 1:
        other = [pltpu.roll(x, shift, axis=0) for x in v]
        v = _sort_bitonic([jnp.maximum(v[i], other[PEER_TOPK - 1 - i]) for i in range(PEER_TOPK)])
        shift //= 2
    return v


def _count_greater(x, t):
    assert len(t) == 16, "the bisection below is written out for 16 entries"
    one = lambda m, w: jnp.where(m, float(w), 0.0)
    b3 = t[7] > x
    b2 = jnp.where(b3, t[11], t[3]) > x
    b1 = jnp.where(b3, jnp.where(b2, t[13], t[9]), jnp.where(b2, t[5], t[1])) > x
    hi = jnp.where(b2, jnp.where(b1, t[14], t[12]), jnp.where(b1, t[10], t[8]))
    lo = jnp.where(b2, jnp.where(b1, t[6], t[4]), jnp.where(b1, t[2], t[0]))
    b0 = jnp.where(b3, hi, lo) > x
    count = one(b3, 8) + one(b2, 4) + one(b1, 2) + one(b0, 1)
    return jnp.where(t[15] > x, float(PEER_TOPK), count)


def _route_kernel(h_ref, wq_ref, k1_ref, k2_ref, ht_ref, cnt_ref, e1_ref, r2_ref, e2_ref,
                  qt_sc, t1_sc, t2_sc, *, tm):
    ht = jnp.transpose(h_ref[...]).astype(BF16)
    ht_ref[...] = ht
    qt_sc[...] = jnp.dot(wq_ref[...], ht, preferred_element_type=F32).astype(BF16)
    key_iota = lax.broadcasted_iota(jnp.int32, (N_KEYS, tm), 0).astype(F32)
    top_iota = lax.broadcasted_iota(jnp.int32, (PEER_TOPK, tm), 0).astype(F32)
    front_rows = SUBLANES

    def scores(hd):
        base = hd * 2 * PEER_HALF
        s1 = jnp.dot(k1_ref[...], qt_sc[base:base + PEER_HALF], preferred_element_type=F32)
        s2 = jnp.dot(k2_ref[...], qt_sc[base + PEER_HALF:base + 2 * PEER_HALF], preferred_element_type=F32)
        return s1, s2

    sub = lax.broadcasted_iota(jnp.int32, (SUBLANES, tm), 0)
    tiles_of = lambda s: [s[r * SUBLANES:(r + 1) * SUBLANES] for r in range(N_KEYS // SUBLANES)]
    sublane_sum = lambda x: jnp.sum(x, axis=0, keepdims=True)

    def route_head_sorted(hd):
        s1, s2 = scores(hd)
        rows1, rows2 = tiles_of(s1), tiles_of(s2)
        t1 = _top_k_sorted(rows1)
        t2 = _top_k_sorted(rows2)
        t1_lo = t1[SUBLANES - 1]
        for a in range(SUBLANES - 2, -1, -1):
            t1_lo = jnp.where(sub == a, t1[a], t1_lo)
        cand = [t1_lo + t2[b] for b in range(PEER_TOPK)]
        top_lo = _top_k_sorted(cand, presorted=True)
        hi_sums = [t1[a] + t2[0] for a in range(SUBLANES, PEER_TOPK)]
        top = _sort_bitonic(top_lo[:SUBLANES] + [jnp.maximum(top_lo[i], hi_sums[PEER_TOPK - 1 - i])
                                                 for i in range(SUBLANES, PEER_TOPK)])
        tau = top[PEER_TOPK - 1]
        cnt_lo = functools.reduce(jnp.add, [jnp.where(c >= tau, 1.0, 0.0) for c in cand])
        cnt = [jnp.broadcast_to(cnt_lo[a:a + 1], (SUBLANES, tm)) for a in range(SUBLANES)]
        cnt += [jnp.where(s >= tau, 1.0, 0.0) for s in hi_sums]
        z = functools.reduce(jnp.add, [jnp.exp(t - top[0]) for t in top])

        cnt1_rows, rank2_rows = [], []
        for r in range(N_KEYS // SUBLANES):
            c = jnp.where(rows1[r] + t2[0] >= tau, 1.0, 0.0)
            c = jnp.where(rows1[r] >= t1[PEER_TOPK - 1], c, 0.0)
            for a in range(SUBLANES - 1, -1, -1):
                c = jnp.where(rows1[r] >= t1[a], cnt[a], c)
            cnt1_rows.append(c)
            g = _count_greater(rows2[r], t2)
            rank2_rows.append(jnp.where(g < float(PEER_TOPK), g, NOT_SELECTED_RANK))
        cnt1 = jnp.concatenate(cnt1_rows, axis=0)
        rank2 = jnp.concatenate(rank2_rows, axis=0)
        cnt_ref[hd] = cnt1
        e1_ref[hd] = jnp.exp(s1 - t1[0][0:1]) / z[0:1]
        r2_ref[hd] = rank2.astype(BF16)
        e2_ref[hd] = jnp.exp(s2 - t2[0][0:1]).astype(BF16)

        gap = lambda t: functools.reduce(jnp.minimum, [t[b] - t[b + 1] for b in range(PEER_TOPK - 1)])[0:1]
        n_cnt = sublane_sum(cnt_lo) + functools.reduce(jnp.add, cnt[SUBLANES:])[0:1]
        n_sel = sublane_sum(functools.reduce(jnp.add, cnt1_rows))
        n_rank = sublane_sum(functools.reduce(
            jnp.add, [jnp.where(x < float(PEER_TOPK), 1.0, 0.0) for x in rank2_rows]))
        off = lambda n: jnp.abs(n - float(PEER_TOPK))
        return (off(n_cnt) + off(n_sel) + off(n_rank)
                + jnp.where(jnp.minimum(gap(t1), gap(t2)) > 0.0, 0.0, 1.0))

    def route_head_exact(hd):
        def pick_one(v, iota, n):
            m = jnp.max(v, axis=0, keepdims=True)
            return m, iota == jnp.min(jnp.where(v == m, iota, float(n)), axis=0, keepdims=True)

        s1, s2 = scores(hd)

        def extract(a, carry):
            v1, r1, v2, r2 = carry
            m1, sel1 = pick_one(v1, key_iota, N_KEYS)
            m2, sel2 = pick_one(v2, key_iota, N_KEYS)
            t1_sc[pl.ds(a, 1), :] = m1
            t2_sc[pl.ds(a, 1), :] = m2
            af = jnp.asarray(a, dtype=F32)
            return (jnp.where(sel1, -jnp.inf, v1), jnp.where(sel1, af, r1),
                    jnp.where(sel2, -jnp.inf, v2), jnp.where(sel2, af, r2))

        no_rank = jnp.full((N_KEYS, tm), NOT_SELECTED_RANK, F32)
        _, rank1, _, rank2 = lax.fori_loop(0, PEER_TOPK, extract, (s1, no_rank, s2, no_rank))
        t1 = t1_sc[...]
        t2 = t2_sc[...]
        top0 = t1[0:1] + t2[0:1]

        def pick(_, carry):
            ptr, front, z = carry
            m, sel = pick_one(front, top_iota, PEER_TOPK)
            ptr = ptr + jnp.where(sel, 1.0, 0.0)
            lo = ptr[:front_rows]
            nxt = jnp.full(lo.shape, -jnp.inf, F32)
            for b in range(1, PEER_TOPK):
                nxt = jnp.where(lo == float(b), t2_sc[b:b + 1, :], nxt)
            nxt = jnp.where(lo == 0.0, t2[0:1], nxt)
            front = jnp.concatenate(
                [t1[:front_rows] + nxt, jnp.where(sel[front_rows:], -jnp.inf, front[front_rows:])], axis=0)
            return ptr, front, z + jnp.exp(m - top0)

        cnt, _, z = lax.fori_loop(
            0, PEER_TOPK, pick,
            (jnp.zeros((PEER_TOPK, tm), F32), t1 + t2[0:1], jnp.zeros((1, tm), F32)))

        cnt1 = jnp.zeros((N_KEYS, tm), F32)
        for a in range(PEER_TOPK):
            cnt1 = jnp.where(rank1 == float(a), cnt[a:a + 1], cnt1)
        cnt_ref[hd] = cnt1
        e1_ref[hd] = jnp.exp(s1 - t1[0:1]) / z
        r2_ref[hd] = rank2.astype(BF16)
        e2_ref[hd] = jnp.exp(s2 - t2[0:1]).astype(BF16)

    doubt = [route_head_sorted(hd) for hd in range(PEER_HEADS)]

    @pl.when(jnp.max(functools.reduce(jnp.maximum, doubt)) > 0.0)
    def _():
        for hd in range(PEER_HEADS):
            @pl.when(jnp.max(doubt[hd]) > 0.0)
            def _():
                route_head_exact(hd)


def _route(h2, wq_t, k1, k2, tm):
    n = h2.shape[0]
    full = lambda a: pl.BlockSpec(a.shape, lambda i: (0,) * a.ndim)
    per_head = pl.BlockSpec((PEER_HEADS, N_KEYS, tm), lambda i: (0, 0, i))
    hshape = lambda dt: jax.ShapeDtypeStruct((PEER_HEADS, N_KEYS, n), dt)
    return pl.pallas_call(
        functools.partial(_route_kernel, tm=tm),
        grid=(n // tm,),
        in_specs=[pl.BlockSpec((tm, D_MODEL), lambda i: (i, 0)), full(wq_t), full(k1), full(k2)],
        out_specs=(pl.BlockSpec((D_MODEL, tm), lambda i: (0, i)), per_head, per_head, per_head, per_head),
        out_shape=(jax.ShapeDtypeStruct((D_MODEL, n), BF16), hshape(F32), hshape(F32), hshape(BF16),
                   hshape(BF16)),
        scratch_shapes=[pltpu.VMEM((PEER_HEADS * 2 * PEER_HALF, tm), BF16),
                        pltpu.VMEM((PEER_TOPK, tm), F32), pltpu.VMEM((PEER_TOPK, tm), F32)],
        compiler_params=_params("parallel"),
        name="peer_route",
    )(h2, wq_t, k1, k2)


ROWS_PER_STEP = 16
EXPERTS_PER_STEP = ROWS_PER_STEP * N_KEYS
ROWS_PER_SUB = 4
EXPERTS_PER_SUB = ROWS_PER_SUB * N_KEYS
DENSE_CHUNK = 2 * LANES


def _bf16_row_tile(row):
    tile = jnp.broadcast_to(row, (BF16_ROWS, row.shape[1])).astype(BF16)
    return jnp.concatenate([tile] * (N_KEYS // BF16_ROWS), axis=0)


def _dense_kernel(ht_ref, u_ref, vt_ref, cnt_ref, e1_ref, r2_ref, e2_ref, h_ref, g_ref, b_ref,
                  y_ref, acc_sc, act_sc, coef_sc, *, alpha):
    e = pl.program_id(1)
    tm = act_sc.shape[1]
    n_sub = ROWS_PER_STEP // ROWS_PER_SUB

    @pl.when(e == 0)
    def _():
        acc_sc[...] = jnp.zeros_like(acc_sc)

    def activations(sub):
        rows = slice(sub * EXPERTS_PER_SUB, (sub + 1) * EXPERTS_PER_SUB)
        act_sc[rows, :] = jnp.dot(u_ref[rows, :], ht_ref[...], preferred_element_type=F32).astype(BF16)

    def tiles(sub):
        chunk = min(DENSE_CHUNK, tm)
        for il in range(sub * ROWS_PER_SUB, (sub + 1) * ROWS_PER_SUB):
            for c in range(tm // chunk):
                yield il, slice(il * N_KEYS, (il + 1) * N_KEYS), slice(c * chunk, (c + 1) * chunk)

    def routing_weights(sub):
        for il, rows, lanes in tiles(sub):
            w = None
            for hd in range(PEER_HEADS):
                cnt = _bf16_row_tile(cnt_ref[hd, il:il + 1, lanes])
                e1 = _bf16_row_tile(e1_ref[hd, il:il + 1, lanes])
                term = jnp.where(r2_ref[hd, :, lanes] < cnt, e2_ref[hd, :, lanes] * e1, jnp.zeros((), BF16))
                w = term if w is None else w + term
            coef_sc[rows, lanes] = w

    def coefficients(sub):
        for _, rows, lanes in tiles(sub):
            coef_sc[rows, lanes] = coef_sc[rows, lanes] * _gelu_tanh(act_sc[rows, lanes])

    routing_weights(0)
    activations(0)
    for sub in range(n_sub):
        if sub + 1 < n_sub:
            routing_weights(sub + 1)
            activations(sub + 1)
        coefficients(sub)
    acc_sc[...] += jnp.dot(vt_ref[...], coef_sc[...], preferred_element_type=F32)

    @pl.when(e == pl.num_programs(1) - 1)
    def _():
        peer = jnp.transpose(acc_sc[...])
        y_ref[...] = _layer_norm(alpha * h_ref[...] + peer, g_ref[...], b_ref[...])


def _dense(ht, u_b, vt_b, cnt1, e1n, rank2, e2, h2, g, bta, alpha, tm):
    n = h2.shape[0]
    n_e = N_EXPERTS // EXPERTS_PER_STEP
    full = lambda a: pl.BlockSpec(a.shape, lambda t, e: (0,) * a.ndim)
    rows_blk = pl.BlockSpec((PEER_HEADS, ROWS_PER_STEP, tm), lambda t, e: (0, e, t))
    cols_blk = pl.BlockSpec((PEER_HEADS, N_KEYS, tm), lambda t, e: (0, 0, t))
    return pl.pallas_call(
        functools.partial(_dense_kernel, alpha=alpha),
        grid=(n // tm, n_e),
        in_specs=[
            pl.BlockSpec((D_MODEL, tm), lambda t, e: (0, t)),
            pl.BlockSpec((EXPERTS_PER_STEP, D_MODEL), lambda t, e: (e, 0)),
            pl.BlockSpec((D_MODEL, EXPERTS_PER_STEP), lambda t, e: (0, e)),
            rows_blk, rows_blk, cols_blk, cols_blk,
            pl.BlockSpec((tm, D_MODEL), lambda t, e: (t, 0)),
            full(g), full(bta),
        ],
        out_specs=pl.BlockSpec((tm, D_MODEL), lambda t, e: (t, 0)),
        out_shape=jax.ShapeDtypeStruct((n, D_MODEL), F32),
        scratch_shapes=[pltpu.VMEM((D_MODEL, tm), F32), pltpu.VMEM((EXPERTS_PER_STEP, tm), BF16),
                        pltpu.VMEM((EXPERTS_PER_STEP, tm), BF16)],
        compiler_params=_params("parallel", "arbitrary"),
        name="peer_dense",
    )(ht, u_b, vt_b, cnt1, e1n, rank2, e2, h2, g, bta)


def _block_diag(w):
    nb, bi, bo = w.shape
    eye = jnp.eye(nb, dtype=w.dtype)
    return (eye[:, None, :, None] * w[:, :, None, :]).reshape(nb * bi, nb * bo)


def _prep_weights(w_in, b_forget, conv_w, conv_b, w_rg_a, b_rg_a, w_rg_x, b_rg_x, lru_lambda,
                  w_attn_up, w_rnn_up, w_out, ln1_g, ln1_b, peer_w_query, peer_keys_1, peer_keys_2,
                  peer_u, peer_v, ln2_g, ln2_b):
    c_f = 3 * D_ATTN
    w_in_p = jnp.concatenate(
        [w_in[:, :c_f], jnp.pad(w_in[:, c_f:c_f + N_ATTN_HEADS], ((0, 0), (0, F_PAD - N_ATTN_HEADS))),
         w_in[:, c_f + N_ATTN_HEADS:]], axis=1).astype(BF16)
    row = lambda a: a.reshape(1, -1).astype(F32)
    return dict(
        w_in=w_in_p, b_forget=b_forget.reshape(-1, 1).astype(F32), conv_w=conv_w.astype(F32), conv_b=row(conv_b),
        wa=_block_diag(w_rg_a).astype(BF16), ba=row(b_rg_a), wx=_block_diag(w_rg_x).astype(BF16),
        bx=row(b_rg_x), lam=row(lru_lambda),
        wau=w_attn_up.astype(BF16), wru=w_rnn_up.astype(BF16), wo=w_out.astype(BF16),
        ln1_g=row(ln1_g), ln1_b=row(ln1_b),
        wq_t=jnp.transpose(peer_w_query).astype(BF16), k1=peer_keys_1.astype(BF16),
        k2=peer_keys_2.astype(BF16), u=peer_u.astype(BF16), vt=jnp.transpose(peer_v).astype(BF16),
        ln2_g=row(ln2_g), ln2_b=row(ln2_b),
    )


def _pick_block(n, target):
    blk = min(n, target)
    assert n % blk == 0, (n, blk)
    return blk


def _trunk_layer(x, past_k, past_v, past_logf, conv_hist, h0, p, alpha):
    bsz, t, _ = x.shape
    n = bsz * t
    x2 = x.reshape(n, D_MODEL)
    q, k, v, kb, vb, lf, xr, gg, sa, sr = _inproj(x2, p["w_in"], p["b_forget"], bsz, t, _pick_block(n, 256))

    def state_layout(a):
        if a.ndim == 3:
            return jnp.transpose(a.reshape(bsz, N_ATTN_HEADS, ATTN_HEAD_DIM, t), (0, 3, 1, 2))
        return a.reshape(bsz, t, N_ATTN_HEADS, ATTN_HEAD_DIM)

    tq = _pick_block(t, ATTN_Q_BLOCK)
    n_past = 0 if past_k is None else past_k.shape[1]
    t_all = n_past + t
    t_lanes = -(-t_all // LANES) * LANES
    tk = ATTN_K_BLOCK if t_lanes % ATTN_K_BLOCK == 0 else t_lanes
    t_pad = -(-t_all // tk) * tk
    lf_bht = jnp.transpose(lf.reshape(N_ATTN_HEADS, bsz, t), (1, 0, 2))
    lf_all = lf_bht
    kb3 = kb.reshape(bsz, t, D_ATTN)
    vb3 = vb.reshape(bsz, t, D_ATTN)
    if past_k is not None:
        lf_all = jnp.concatenate([jnp.transpose(past_logf.astype(F32), (0, 2, 1)), lf_bht], axis=2)
        kb3 = jnp.concatenate([past_k.reshape(bsz, n_past, D_ATTN).astype(BF16), kb3], axis=1)
        vb3 = jnp.concatenate([past_v.reshape(bsz, n_past, D_ATTN).astype(BF16), vb3], axis=1)
    pad = ((0, 0), (0, t_pad - t_all), (0, 0))
    kb3, vb3 = jnp.pad(kb3, pad), jnp.pad(vb3, pad)
    f_t = _cumsum_time(jnp.pad(lf_all, ((0, 0), (0, 0), (0, t_pad - t_all))))
    f_g = f_t.reshape(bsz, N_HEAD_GROUPS, HEADS_PER_GROUP, t_pad)
    fq = jnp.transpose(f_g[:, :, :, n_past:n_past + t], (0, 1, 3, 2))
    fk = jnp.transpose(f_g.reshape(bsz, N_HEAD_GROUPS, HEADS_PER_GROUP, t_pad // tk, tk), (0, 1, 3, 2, 4))
    o = _attention(q.reshape(bsz, t, D_ATTN), kb3, vb3, fq, fk, tq=tq, tk=tk, q_off=n_past)

    hist8 = jnp.pad(conv_hist.astype(F32), ((0, 0), (SUBLANES - (CONV_WIDTH - 1), 0), (0, 0)))
    xr3 = xr.reshape(bsz, t, D_RNN)
    rnn_out, h_last = _rnn(xr3, gg.reshape(bsz, t, D_RNN), hist8, h0.astype(F32).reshape(bsz, 1, D_RNN),
                           p["conv_w"], p["conv_b"], p["wa"], p["ba"], p["wx"], p["bx"], p["lam"],
                           _pick_block(t, 1024))
    new_hist = jnp.concatenate([conv_hist.astype(F32), xr3], axis=1)[:, -(CONV_WIDTH - 1):]

    h = _outproj(o.reshape(n, D_ATTN), rnn_out.reshape(n, D_RNN), sa, sr, x2, p["wau"], p["wru"], p["wo"],
                 p["ln1_g"], p["ln1_b"], alpha, _pick_block(n, 1024))
    ht, cnt1, e1n, rank2, e2 = _route(h, p["wq_t"], p["k1"], p["k2"], _pick_block(n, ROUTE_TOKENS))
    y = _dense(ht, p["u"], p["vt"], cnt1, e1n, rank2, e2, h, p["ln2_g"], p["ln2_b"], alpha,
               _pick_block(n, 512))
    return (y.reshape(bsz, t, D_MODEL), state_layout(k), state_layout(v), jnp.transpose(lf_bht, (0, 2, 1)), new_hist,
            h_last.reshape(bsz, D_RNN))


def kernel(x_prompt, x_sample, cache_k, cache_v, cache_logf, state_conv, state_rnn, w_in, b_forget, conv_w, conv_b, w_rg_a, b_rg_a, w_rg_x, b_rg_x, lru_lambda, w_attn_up, w_rnn_up, w_out, ln1_g, ln1_b, peer_w_query, peer_keys_1, peer_keys_2, peer_u, peer_v, ln2_g, ln2_b):
    depth = w_in.shape[0]
    alpha = (2 * depth) ** 0.25
    layer_weights = (w_in, b_forget, conv_w, conv_b, w_rg_a, b_rg_a, w_rg_x, b_rg_x, lru_lambda, w_attn_up,
                     w_rnn_up, w_out, ln1_g, ln1_b, peer_w_query, peer_keys_1, peer_keys_2, peer_u, peer_v,
                     ln2_g, ln2_b)
    hp, hs = x_prompt, x_sample
    prompt_state, sample_state = [], []
    for l in range(depth):
        p = _prep_weights(*(w[l] for w in layer_weights))
        zero_hist = jnp.zeros((hp.shape[0], CONV_WIDTH - 1, D_RNN), F32)
        zero_h = jnp.zeros((hp.shape[0], D_RNN), F32)
        hp, *st_p = _trunk_layer(hp, None, None, None, zero_hist, zero_h, p, alpha)
        hs, *st_s = _trunk_layer(hs, cache_k[l], cache_v[l], cache_logf[l], state_conv[l], state_rnn[l], p,
                                 alpha)
        prompt_state.append(st_p)
        sample_state.append(st_s)
    stack = lambda states, i: jnp.stack([s[i] for s in states])
    return (hp, hs) + tuple(stack(prompt_state, i) for i in range(5)) + tuple(
        stack(sample_state, i) for i in range(5))
```

```python
import functools
import math

import jax
import jax.numpy as jnp
from jax import lax
from jax.experimental import pallas as pl
from jax.experimental.pallas import tpu as pltpu

F32 = jnp.float32
BF16 = jnp.bfloat16

D_MODEL = 1024
N_ATTN_HEADS = 8
ATTN_HEAD_DIM = 64
D_ATTN = N_ATTN_HEADS * ATTN_HEAD_DIM
ATTN_SCALE = ATTN_HEAD_DIM ** -0.5
D_RNN = 512
CONV_WIDTH = 4
LRU_C = 8.0
N_KEYS = 128
N_EXPERTS = N_KEYS * N_KEYS
PEER_HEADS = 8
PEER_TOPK = 16
PEER_HALF = 128
LN_EPS = 1e-5

LANES = 128
SUBLANES = 8
BF16_ROWS = 2 * SUBLANES
VMEM_LIMIT_BYTES = 56 * 1024 * 1024

HEADS_PER_GROUP = 4
GROUP_LANES = HEADS_PER_GROUP * ATTN_HEAD_DIM
N_HEAD_GROUPS = N_ATTN_HEADS // HEADS_PER_GROUP
ATTN_Q_BLOCK = 512
ATTN_K_BLOCK = 512
ATTN_ROW_BLOCK = 32
F_PAD = LANES
NEG_BIG = -1e30
LOG2_E = math.log2(math.e)
NOT_SELECTED_RANK = 99.0

_C_Q = 0
_C_K = _C_Q + D_ATTN
_C_V = _C_K + D_ATTN
_C_F = _C_V + D_ATTN
_C_XR = _C_F + F_PAD
_C_GATE = _C_XR + D_RNN
_C_GA = _C_GATE + D_RNN
_C_GR = _C_GA + D_MODEL
_C_END = _C_GR + D_MODEL


def _params(*sem):
    return pltpu.CompilerParams(dimension_semantics=sem, vmem_limit_bytes=VMEM_LIMIT_BYTES)


def _sigmoid(x):
    return 1.0 / (1.0 + jnp.exp(-x))


def _gelu_tanh(x):
    k0 = -2.0 * 0.7978845608028654 * LOG2_E
    k1 = -2.0 * 0.035677408136300125 * LOG2_E
    return x / (1.0 + jnp.exp2(x * (k0 + k1 * (x * x))))


def _softplus(x):
    return jnp.maximum(x, 0.0) + jnp.log1p(jnp.exp(-jnp.abs(x)))


def _layer_norm(x, g, b):
    mu = jnp.mean(x, axis=-1, keepdims=True)
    xc = x - mu
    var = jnp.mean(xc * xc, axis=-1, keepdims=True)
    return xc * lax.rsqrt(var + LN_EPS) * g + b


def _inproj_kernel(x_ref, w_ref, bf_ref, q_ref, k_ref, v_ref, kb_ref, vb_ref, lf_ref, xr_ref,
                   gg_ref, sa_ref, sr_ref, *, time_minor_kv):
    xb = x_ref[...].astype(BF16)

    def mm(lo, hi):
        return jnp.dot(xb, w_ref[:, lo:hi], preferred_element_type=F32)

    q_ref[...] = (mm(_C_Q, _C_K) * (ATTN_SCALE * LOG2_E)).astype(BF16)
    k = mm(_C_K, _C_V)
    kb_ref[...] = k.astype(BF16)
    v = mm(_C_V, _C_F)
    vb_ref[...] = v.astype(BF16)
    if time_minor_kv:
        k_ref[0] = jnp.transpose(k)
        v_ref[0] = jnp.transpose(v)
    else:
        k_ref[...] = k
        v_ref[...] = v
    f = jnp.transpose(mm(_C_F, _C_XR))[:N_ATTN_HEADS] + bf_ref[...]
    lf_ref[...] = -_softplus(-f)
    xr_ref[...] = mm(_C_XR, _C_GATE)
    gg_ref[...] = _gelu_tanh(mm(_C_GATE, _C_GA)).astype(BF16)
    sa_ref[...] = _sigmoid(mm(_C_GA, _C_GR)).astype(BF16)
    sr_ref[...] = _sigmoid(mm(_C_GR, _C_END)).astype(BF16)


def _inproj(x2, w_in_p, b_forget, bsz, t, tm):
    n = x2.shape[0]
    row = lambda w: pl.BlockSpec((tm, w), lambda i: (i, 0))
    full = lambda a: pl.BlockSpec(a.shape, lambda i: (0,) * a.ndim)
    time_minor_kv = tm % LANES == 0 and t % tm == 0
    if time_minor_kv:
        kv_shape = jax.ShapeDtypeStruct((bsz, D_ATTN, t), F32)
        kv_spec = pl.BlockSpec((1, D_ATTN, tm), lambda i: (i // (t // tm), 0, i % (t // tm)))
    else:
        kv_shape = jax.ShapeDtypeStruct((n, D_ATTN), F32)
        kv_spec = row(D_ATTN)
    out_shape = (
        jax.ShapeDtypeStruct((n, D_ATTN), BF16),
        kv_shape,
        kv_shape,
        jax.ShapeDtypeStruct((n, D_ATTN), BF16),
        jax.ShapeDtypeStruct((n, D_ATTN), BF16),
        jax.ShapeDtypeStruct((N_ATTN_HEADS, n), F32),
        jax.ShapeDtypeStruct((n, D_RNN), F32),
        jax.ShapeDtypeStruct((n, D_RNN), BF16),
        jax.ShapeDtypeStruct((n, D_MODEL), BF16),
        jax.ShapeDtypeStruct((n, D_MODEL), BF16),
    )
    out_specs = (row(D_ATTN), kv_spec, kv_spec, row(D_ATTN), row(D_ATTN),
                 pl.BlockSpec((N_ATTN_HEADS, tm), lambda i: (0, i)),
                 row(D_RNN), row(D_RNN), row(D_MODEL), row(D_MODEL))
    return pl.pallas_call(
        functools.partial(_inproj_kernel, time_minor_kv=time_minor_kv),
        grid=(n // tm,),
        in_specs=[row(D_MODEL), full(w_in_p), full(b_forget)],
        out_specs=out_specs,
        out_shape=out_shape,
        compiler_params=_params("parallel"),
        name="inproj",
    )(x2, w_in_p, b_forget)


def _cumsum_kernel(x_ref, o_ref):
    x = x_ref[0]
    t = x.shape[1]
    lane = lax.broadcasted_iota(jnp.int32, x.shape, 1)
    d = 1
    while d < t:
        x = x + jnp.where(lane >= d, pltpu.roll(x, d, axis=1), 0.0)
        d *= 2
    o_ref[0] = x * LOG2_E


def _cumsum_time(lf_t):
    b, h, t = lf_t.shape
    spec = pl.BlockSpec((1, h, t), lambda i: (i, 0, 0))
    return pl.pallas_call(
        _cumsum_kernel, grid=(b,), in_specs=[spec], out_specs=spec,
        out_shape=jax.ShapeDtypeStruct(lf_t.shape, F32),
        compiler_params=_params("parallel"), name="logf_cumsum",
    )(lf_t)


def _attn_kernel(q_ref, k_ref, v_ref, fq_ref, fk_ref, o_ref, q_sc, fq_sc, s_sc, p_sc, acc_sc, m_sc, al_sc,
                 lp_sc, *, tq, tk, q_off):
    qi = pl.program_id(2)
    row0 = q_off + qi * tq
    n_full = (row0 + 1) // tk
    heads = range(HEADS_PER_GROUP)
    rb = min(ATTN_ROW_BLOCK, tq)
    lane = lax.broadcasted_iota(jnp.int32, (tq, GROUP_LANES), 1)
    in_head = [(lane >= j * ATTN_HEAD_DIM) & (lane < (j + 1) * ATTN_HEAD_DIM) for j in heads]
    q = q_ref[0]
    for j in heads:
        q_sc[j * tq:(j + 1) * tq, :] = jnp.where(in_head[j], q, jnp.zeros_like(q))
        fq_sc[j * tq:(j + 1) * tq, :] = jnp.broadcast_to(fq_ref[0, 0, :, j:j + 1], (tq, LANES))
    m_sc[...] = jnp.full_like(m_sc, NEG_BIG)
    lp_sc[...] = jnp.zeros_like(lp_sc)
    acc_sc[...] = jnp.zeros_like(acc_sc)
    halves = [slice(0, 2 * tq), slice(2 * tq, 4 * tq)]
    n_lane_tiles = tk // LANES

    def scores(c, s_sc):
        ks = k_ref[0, pl.ds(pl.multiple_of(c * tk, tk), tk), :]
        for hv in halves:
            s_sc[hv, :] = lax.dot_general(q_sc[hv, :], ks, (((1,), (1,)), ((), ())),
                                          preferred_element_type=F32)

    def absorb(c, s_sc, masked):
        start = pl.multiple_of(c * tk, tk)
        fk = fk_ref[0, 0, c]
        vs = v_ref[0, pl.ds(start, tk), :]

        def weighted_values(hv):
            pv = jnp.dot(p_sc[hv, :], vs, preferred_element_type=F32)
            alpha = al_sc[hv, :]
            acc_sc[hv, :] = jnp.concatenate([alpha] * (GROUP_LANES // LANES), axis=1) * acc_sc[hv, :] + pv

        for j in heads:
            for r in range(tq // rb):
                rows = slice(j * tq + r * rb, j * tq + (r + 1) * rb)
                s = s_sc[rows, :] - fk[j:j + 1, :]
                if masked:
                    q_pos = row0 + r * rb + lax.broadcasted_iota(jnp.int32, (rb, tk), 0)
                    k_pos = c * tk + lax.broadcasted_iota(jnp.int32, (rb, tk), 1)
                    s = jnp.where(k_pos <= q_pos, s, NEG_BIG)
                tiles = [s[:, t * LANES:(t + 1) * LANES] for t in range(n_lane_tiles)]
                row_max = jnp.max(functools.reduce(jnp.maximum, tiles), axis=1, keepdims=True)
                fq = fq_sc[rows, :]
                m_old = m_sc[rows, :]
                m_new = jnp.maximum(m_old, jnp.broadcast_to(row_max, (rb, LANES)) + fq)
                alpha = jnp.exp2(m_old - m_new)
                shift = fq - m_new
                p_tiles = [jnp.exp2(tile + shift) for tile in tiles]
                m_sc[rows, :] = m_new
                al_sc[rows, :] = alpha
                lp_sc[rows, :] = alpha * lp_sc[rows, :] + functools.reduce(jnp.add, p_tiles)
                p_sc[rows, :] = jnp.concatenate(p_tiles, axis=1).astype(BF16)
            if j % 2 == 1:
                weighted_values(halves[j // 2])

    def step(c, carry):
        scores(c, s_sc)
        absorb(c, s_sc, masked=False)
        return carry

    lax.fori_loop(0, n_full, step, 0)
    scores(n_full, s_sc)
    absorb(n_full, s_sc, masked=True)
    o_all = acc_sc[...] / jnp.sum(lp_sc[...], axis=1, keepdims=True)
    out = jnp.zeros((tq, GROUP_LANES), F32)
    for j in heads:
        out = jnp.where(in_head[j], o_all[j * tq:(j + 1) * tq], out)
    o_ref[0] = out.astype(BF16)


def _attention(q, k_all, v_all, fq, fk, *, tq, tk, q_off):
    b, t, _ = q.shape
    t_k = k_all.shape[1]
    n_kv = t_k // tk
    assert all((q_off + i * tq) % tk + tq <= tk for i in range(t // tq)), "a query block straddles key chunks"
    kern = functools.partial(_attn_kernel, tq=tq, tk=tk, q_off=q_off)
    rows = HEADS_PER_GROUP * tq
    return pl.pallas_call(
        kern,
        grid=(b, N_HEAD_GROUPS, t // tq),
        in_specs=[
            pl.BlockSpec((1, tq, GROUP_LANES), lambda bi, g, i: (bi, i, g)),
            pl.BlockSpec((1, t_k, GROUP_LANES), lambda bi, g, i: (bi, 0, g)),
            pl.BlockSpec((1, t_k, GROUP_LANES), lambda bi, g, i: (bi, 0, g)),
            pl.BlockSpec((1, 1, tq, HEADS_PER_GROUP), lambda bi, g, i: (bi, g, i, 0)),
            pl.BlockSpec((1, 1, n_kv, HEADS_PER_GROUP, tk), lambda bi, g, i: (bi, g, 0, 0, 0)),
        ],
        out_specs=pl.BlockSpec((1, tq, GROUP_LANES), lambda bi, g, i: (bi, i, g)),
        out_shape=jax.ShapeDtypeStruct((b, t, D_ATTN), BF16),
        scratch_shapes=[
            pltpu.VMEM((rows, GROUP_LANES), BF16),
            pltpu.VMEM((rows, LANES), F32),
            pltpu.VMEM((rows, tk), F32),
            pltpu.VMEM((rows, tk), BF16),
            pltpu.VMEM((rows, GROUP_LANES), F32),
            pltpu.VMEM((rows, LANES), F32),
            pltpu.VMEM((rows, LANES), F32),
            pltpu.VMEM((rows, LANES), F32),
        ],
        compiler_params=_params("parallel", "parallel", "arbitrary"),
        name="fox_attention",
    )(q, k_all, v_all, fq, fk)


def _rnn_kernel(xr_ref, gg_ref, hist_ref, h0_ref, cw_ref, cb_ref, wa_ref, ba_ref, wx_ref, bx_ref,
                lam_ref, out_ref, hl_ref, win_sc, h_sc, *, tb):
    t = pl.program_id(1)

    @pl.when(t == 0)
    def _():
        win_sc[:SUBLANES, :] = hist_ref[0]
        h_sc[...] = h0_ref[0]

    x = xr_ref[0]
    win_sc[SUBLANES:, :] = x
    xc = x * cw_ref[CONV_WIDTH - 1:CONV_WIDTH, :] + cb_ref[...]
    for s in range(1, CONV_WIDTH):
        xc = xc + win_sc[SUBLANES - s:SUBLANES - s + tb, :] * cw_ref[CONV_WIDTH - 1 - s:CONV_WIDTH - s, :]
    win_sc[:SUBLANES, :] = x[tb - SUBLANES:tb]

    xcb = xc.astype(BF16)
    r = _sigmoid(jnp.dot(xcb, wa_ref[...], preferred_element_type=F32) + ba_ref[...])
    ig = _sigmoid(jnp.dot(xcb, wx_ref[...], preferred_element_type=F32) + bx_ref[...])
    log_a = (-LRU_C) * r * _softplus(-lam_ref[...])
    a = jnp.exp(log_a)
    one_minus_a2 = 1.0 - a * a
    scale = jnp.where(one_minus_a2 > 0.0, one_minus_a2 * lax.rsqrt(one_minus_a2), 0.0)
    bterm = scale * ig * xc

    grouped = (tb // SUBLANES, SUBLANES, D_RNN)
    a, bterm = a.reshape(grouped), bterm.reshape(grouped)
    row_in_group = lax.broadcasted_iota(jnp.int32, grouped, 1)
    d = 1
    while d < SUBLANES:
        valid = row_in_group >= d
        a_s = pltpu.roll(a, d, axis=1)
        b_s = pltpu.roll(bterm, d, axis=1)
        bterm = jnp.where(valid, a * b_s + bterm, bterm)
        a = jnp.where(valid, a * a_s, a)
        d *= 2
    h_last = h_sc[...]
    groups = []
    for g in range(tb // SUBLANES):
        h_g = bterm[g] + a[g] * h_last
        h_last = h_g[SUBLANES - 1:SUBLANES]
        groups.append(h_g)
    h = jnp.concatenate(groups, axis=0)
    h_sc[...] = h_last
    hl_ref[0] = h_last
    out_ref[0] = (h * gg_ref[0].astype(F32)).astype(BF16)


def _rnn(xr, gg, hist8, h0, conv_w, conv_b, wa, ba, wx, bx, lam, tb):
    b, t, _ = xr.shape
    blk = pl.BlockSpec((1, tb, D_RNN), lambda bi, ti: (bi, ti, 0))
    full = lambda a: pl.BlockSpec(a.shape, lambda bi, ti: (0,) * a.ndim)
    per_b = lambda r: pl.BlockSpec((1, r, D_RNN), lambda bi, ti: (bi, 0, 0))
    return pl.pallas_call(
        functools.partial(_rnn_kernel, tb=tb),
        grid=(b, t // tb),
        in_specs=[blk, blk, per_b(SUBLANES), per_b(1), full(conv_w), full(conv_b), full(wa), full(ba),
                  full(wx), full(bx), full(lam)],
        out_specs=(blk, per_b(1)),
        out_shape=(jax.ShapeDtypeStruct((b, t, D_RNN), BF16), jax.ShapeDtypeStruct((b, 1, D_RNN), F32)),
        scratch_shapes=[pltpu.VMEM((SUBLANES + tb, D_RNN), F32), pltpu.VMEM((1, D_RNN), F32)],
        compiler_params=_params("parallel", "arbitrary"),
        name="conv_rglru",
    )(xr, gg, hist8, h0, conv_w, conv_b, wa, ba, wx, bx, lam)


def _outproj_kernel(o_ref, r_ref, sa_ref, sr_ref, x_ref, wau_ref, wru_ref, wo_ref, g_ref, b_ref,
                    h_ref, *, alpha):
    up_a = jnp.dot(o_ref[...], wau_ref[...], preferred_element_type=F32)
    up_r = jnp.dot(r_ref[...], wru_ref[...], preferred_element_type=F32)
    merged = sa_ref[...].astype(F32) * up_a + sr_ref[...].astype(F32) * up_r
    mix = jnp.dot(merged.astype(BF16), wo_ref[...], preferred_element_type=F32)
    h_ref[...] = _layer_norm(alpha * x_ref[...] + mix, g_ref[...], b_ref[...])


def _outproj(o2, r2, sa, sr, x2, wau, wru, wo, g, bta, alpha, tm):
    n = x2.shape[0]
    row = lambda w: pl.BlockSpec((tm, w), lambda i: (i, 0))
    full = lambda a: pl.BlockSpec(a.shape, lambda i: (0,) * a.ndim)
    return pl.pallas_call(
        functools.partial(_outproj_kernel, alpha=alpha),
        grid=(n // tm,),
        in_specs=[row(D_ATTN), row(D_RNN), row(D_MODEL), row(D_MODEL), row(D_MODEL), full(wau), full(wru),
                  full(wo), full(g), full(bta)],
        out_specs=row(D_MODEL),
        out_shape=jax.ShapeDtypeStruct((n, D_MODEL), F32),
        compiler_params=_params("parallel"),
        name="outproj_ln1",
    )(o2, r2, sa, sr, x2, wau, wru, wo, g, bta)


ROUTE_TOKENS = 2 * LANES


def _odd_even_merge_sort(n):
    def merge(lo, hi, r):
        step = r * 2
        if step < hi - lo:
            yield from merge(lo, hi, step)
            yield from merge(lo + r, hi, step)
            yield from ((i, i + r) for i in range(lo + r, hi - r, step))
        else:
            yield (lo, lo + r)

    def sort(lo, hi):
        if hi - lo >= 1:
            mid = lo + (hi - lo) // 2
            yield from sort(lo, mid)
            yield from sort(mid + 1, hi)
            yield from merge(lo, hi, 1)

    return tuple(sort(0, n - 1))


_SORT_TOPK = _odd_even_merge_sort(PEER_TOPK)


def _compare_exchange(v, i, j):
    v[i], v[j] = jnp.maximum(v[i], v[j]), jnp.minimum(v[i], v[j])


def _sort_bitonic(v):
    d = PEER_TOPK // 2
    while d >= 1:
        for i in range(PEER_TOPK):
            if i & d == 0:
                _compare_exchange(v, i, i + d)
        d //= 2
    return v


def _top_k_sorted(tiles, presorted=False):
    v = list(tiles)
    if not presorted:
        for i, j in _SORT_TOPK:
            _compare_exchange(v, i, j)
    shift = SUBLANES // 2
    while shift >= 1:
        other = [pltpu.roll(x, shift, axis=0) for x in v]
        v = _sort_bitonic([jnp.maximum(v[i], other[PEER_TOPK - 1 - i]) for i in range(PEER_TOPK)])
        shift //= 2
    return v


def _count_greater(x, t):
    assert len(t) == 16, "the bisection below is written out for 16 entries"
    one = lambda m, w: jnp.where(m, float(w), 0.0)
    b3 = t[7] > x
    b2 = jnp.where(b3, t[11], t[3]) > x
    b1 = jnp.where(b3, jnp.where(b2, t[13], t[9]), jnp.where(b2, t[5], t[1])) > x
    hi = jnp.where(b2, jnp.where(b1, t[14], t[12]), jnp.where(b1, t[10], t[8]))
    lo = jnp.where(b2, jnp.where(b1, t[6], t[4]), jnp.where(b1, t[2], t[0]))
    b0 = jnp.where(b3, hi, lo) > x
    count = one(b3, 8) + one(b2, 4) + one(b1, 2) + one(b0, 1)
    return jnp.where(t[15] > x, float(PEER_TOPK), count)


def _route_kernel(h_ref, wq_ref, k1_ref, k2_ref, ht_ref, cnt_ref, e1_ref, r2_ref, e2_ref,
                  qt_sc, t1_sc, t2_sc, *, tm):
    ht = jnp.transpose(h_ref[...]).astype(BF16)
    ht_ref[...] = ht
    qt_sc[...] = jnp.dot(wq_ref[...], ht, preferred_element_type=F32).astype(BF16)
    key_iota = lax.broadcasted_iota(jnp.int32, (N_KEYS, tm), 0).astype(F32)
    top_iota = lax.broadcasted_iota(jnp.int32, (PEER_TOPK, tm), 0).astype(F32)
    front_rows = SUBLANES

    def scores(hd):
        base = hd * 2 * PEER_HALF
        s1 = jnp.dot(k1_ref[...], qt_sc[base:base + PEER_HALF], preferred_element_type=F32)
        s2 = jnp.dot(k2_ref[...], qt_sc[base + PEER_HALF:base + 2 * PEER_HALF], preferred_element_type=F32)
        return s1, s2

    sub = lax.broadcasted_iota(jnp.int32, (SUBLANES, tm), 0)
    tiles_of = lambda s: [s[r * SUBLANES:(r + 1) * SUBLANES] for r in range(N_KEYS // SUBLANES)]
    sublane_sum = lambda x: jnp.sum(x, axis=0, keepdims=True)

    def route_head_sorted(hd):
        s1, s2 = scores(hd)
        rows1, rows2 = tiles_of(s1), tiles_of(s2)
        t1 = _top_k_sorted(rows1)
        t2 = _top_k_sorted(rows2)
        t1_lo = t1[SUBLANES - 1]
        for a in range(SUBLANES - 2, -1, -1):
            t1_lo = jnp.where(sub == a, t1[a], t1_lo)
        cand = [t1_lo + t2[b] for b in range(PEER_TOPK)]
        top_lo = _top_k_sorted(cand, presorted=True)
        hi_sums = [t1[a] + t2[0] for a in range(SUBLANES, PEER_TOPK)]
        top = _sort_bitonic(top_lo[:SUBLANES] + [jnp.maximum(top_lo[i], hi_sums[PEER_TOPK - 1 - i])
                                                 for i in range(SUBLANES, PEER_TOPK)])
        tau = top[PEER_TOPK - 1]
        cnt_lo = functools.reduce(jnp.add, [jnp.where(c >= tau, 1.0, 0.0) for c in cand])
        cnt = [jnp.broadcast_to(cnt_lo[a:a + 1], (SUBLANES, tm)) for a in range(SUBLANES)]
        cnt += [jnp.where(s >= tau, 1.0, 0.0) for s in hi_sums]
        z = functools.reduce(jnp.add, [jnp.exp(t - top[0]) for t in top])

        cnt1_rows, rank2_rows = [], []
        for r in range(N_KEYS // SUBLANES):
            c = jnp.where(rows1[r] + t2[0] >= tau, 1.0, 0.0)
            c = jnp.where(rows1[r] >= t1[PEER_TOPK - 1], c, 0.0)
            for a in range(SUBLANES - 1, -1, -1):
                c = jnp.where(rows1[r] >= t1[a], cnt[a], c)
            cnt1_rows.append(c)
            g = _count_greater(rows2[r], t2)
            rank2_rows.append(jnp.where(g < float(PEER_TOPK), g, NOT_SELECTED_RANK))
        cnt1 = jnp.concatenate(cnt1_rows, axis=0)
        rank2 = jnp.concatenate(rank2_rows, axis=0)
        cnt_ref[hd] = cnt1
        e1_ref[hd] = jnp.exp(s1 - t1[0][0:1]) / z[0:1]
        r2_ref[hd] = rank2.astype(BF16)
        e2_ref[hd] = jnp.exp(s2 - t2[0][0:1]).astype(BF16)

        gap = lambda t: functools.reduce(jnp.minimum, [t[b] - t[b + 1] for b in range(PEER_TOPK - 1)])[0:1]
        n_cnt = sublane_sum(cnt_lo) + functools.reduce(jnp.add, cnt[SUBLANES:])[0:1]
        n_sel = sublane_sum(functools.reduce(jnp.add, cnt1_rows))
        n_rank = sublane_sum(functools.reduce(
            jnp.add, [jnp.where(x < float(PEER_TOPK), 1.0, 0.0) for x in rank2_rows]))
        off = lambda n: jnp.abs(n - float(PEER_TOPK))
        return (off(n_cnt) + off(n_sel) + off(n_rank)
                + jnp.where(jnp.minimum(gap(t1), gap(t2)) > 0.0, 0.0, 1.0))

    def route_head_exact(hd):
        def pick_one(v, iota, n):
            m = jnp.max(v, axis=0, keepdims=True)
            return m, iota == jnp.min(jnp.where(v == m, iota, float(n)), axis=0, keepdims=True)

        s1, s2 = scores(hd)

        def extract(a, carry):
            v1, r1, v2, r2 = carry
            m1, sel1 = pick_one(v1, key_iota, N_KEYS)
            m2, sel2 = pick_one(v2, key_iota, N_KEYS)
            t1_sc[pl.ds(a, 1), :] = m1
            t2_sc[pl.ds(a, 1), :] = m2
            af = jnp.asarray(a, dtype=F32)
            return (jnp.where(sel1, -jnp.inf, v1), jnp.where(sel1, af, r1),
                    jnp.where(sel2, -jnp.inf, v2), jnp.where(sel2, af, r2))

        no_rank = jnp.full((N_KEYS, tm), NOT_SELECTED_RANK, F32)
        _, rank1, _, rank2 = lax.fori_loop(0, PEER_TOPK, extract, (s1, no_rank, s2, no_rank))
        t1 = t1_sc[...]
        t2 = t2_sc[...]
        top0 = t1[0:1] + t2[0:1]

        def pick(_, carry):
            ptr, front, z = carry
            m, sel = pick_one(front, top_iota, PEER_TOPK)
            ptr = ptr + jnp.where(sel, 1.0, 0.0)
            lo = ptr[:front_rows]
            nxt = jnp.full(lo.shape, -jnp.inf, F32)
            for b in range(1, PEER_TOPK):
                nxt = jnp.where(lo == float(b), t2_sc[b:b + 1, :], nxt)
            nxt = jnp.where(lo == 0.0, t2[0:1], nxt)
            front = jnp.concatenate(
                [t1[:front_rows] + nxt, jnp.where(sel[front_rows:], -jnp.inf, front[front_rows:])], axis=0)
            return ptr, front, z + jnp.exp(m - top0)

        cnt, _, z = lax.fori_loop(
            0, PEER_TOPK, pick,
            (jnp.zeros((PEER_TOPK, tm), F32), t1 + t2[0:1], jnp.zeros((1, tm), F32)))

        cnt1 = jnp.zeros((N_KEYS, tm), F32)
        for a in range(PEER_TOPK):
            cnt1 = jnp.where(rank1 == float(a), cnt[a:a + 1], cnt1)
        cnt_ref[hd] = cnt1
        e1_ref[hd] = jnp.exp(s1 - t1[0:1]) / z
        r2_ref[hd] = rank2.astype(BF16)
        e2_ref[hd] = jnp.exp(s2 - t2[0:1]).astype(BF16)

    doubt = [route_head_sorted(hd) for hd in range(PEER_HEADS)]

    @pl.when(jnp.max(functools.reduce(jnp.maximum, doubt)) > 0.0)
    def _():
        for hd in range(PEER_HEADS):
            @pl.when(jnp.max(doubt[hd]) > 0.0)
            def _():
                route_head_exact(hd)


def _route(h2, wq_t, k1, k2, tm):
    n = h2.shape[0]
    full = lambda a: pl.BlockSpec(a.shape, lambda i: (0,) * a.ndim)
    per_head = pl.BlockSpec((PEER_HEADS, N_KEYS, tm), lambda i: (0, 0, i))
    hshape = lambda dt: jax.ShapeDtypeStruct((PEER_HEADS, N_KEYS, n), dt)
    return pl.pallas_call(
        functools.partial(_route_kernel, tm=tm),
        grid=(n // tm,),
        in_specs=[pl.BlockSpec((tm, D_MODEL), lambda i: (i, 0)), full(wq_t), full(k1), full(k2)],
        out_specs=(pl.BlockSpec((D_MODEL, tm), lambda i: (0, i)), per_head, per_head, per_head, per_head),
        out_shape=(jax.ShapeDtypeStruct((D_MODEL, n), BF16), hshape(F32), hshape(F32), hshape(BF16),
                   hshape(BF16)),
        scratch_shapes=[pltpu.VMEM((PEER_HEADS * 2 * PEER_HALF, tm), BF16),
                        pltpu.VMEM((PEER_TOPK, tm), F32), pltpu.VMEM((PEER_TOPK, tm), F32)],
        compiler_params=_params("parallel"),
        name="peer_route",
    )(h2, wq_t, k1, k2)


ROWS_PER_STEP = 8
EXPERTS_PER_STEP = ROWS_PER_STEP * N_KEYS
ROWS_PER_SUB = 4
EXPERTS_PER_SUB = ROWS_PER_SUB * N_KEYS
DENSE_CHUNK = 2 * LANES


def _bf16_row_tile(row):
    tile = jnp.broadcast_to(row, (BF16_ROWS, row.shape[1])).astype(BF16)
    return jnp.concatenate([tile] * (N_KEYS // BF16_ROWS), axis=0)


def _dense_kernel(ht_ref, u_ref, vt_ref, cnt_ref, e1_ref, r2_ref, e2_ref, h_ref, g_ref, b_ref,
                  y_ref, acc_sc, act_sc, coef_sc, *, alpha):
    e = pl.program_id(1)
    tm = act_sc.shape[1]
    n_sub = ROWS_PER_STEP // ROWS_PER_SUB

    @pl.when(e == 0)
    def _():
        acc_sc[...] = jnp.zeros_like(acc_sc)

    def activations(sub):
        rows = slice(sub * EXPERTS_PER_SUB, (sub + 1) * EXPERTS_PER_SUB)
        act_sc[rows, :] = jnp.dot(u_ref[rows, :], ht_ref[...], preferred_element_type=F32).astype(BF16)

    def tiles(sub):
        chunk = min(DENSE_CHUNK, tm)
        for il in range(sub * ROWS_PER_SUB, (sub + 1) * ROWS_PER_SUB):
            for c in range(tm // chunk):
                yield il, slice(il * N_KEYS, (il + 1) * N_KEYS), slice(c * chunk, (c + 1) * chunk)

    def routing_weights(sub):
        for il, rows, lanes in tiles(sub):
            w = None
            for hd in range(PEER_HEADS):
                cnt = _bf16_row_tile(cnt_ref[hd, il:il + 1, lanes])
                e1 = _bf16_row_tile(e1_ref[hd, il:il + 1, lanes])
                term = jnp.where(r2_ref[hd, :, lanes] < cnt, e2_ref[hd, :, lanes] * e1, jnp.zeros((), BF16))
                w = term if w is None else w + term
            coef_sc[rows, lanes] = w

    def coefficients(sub):
        for _, rows, lanes in tiles(sub):
            coef_sc[rows, lanes] = coef_sc[rows, lanes] * _gelu_tanh(act_sc[rows, lanes])

    routing_weights(0)
    activations(0)
    for sub in range(n_sub):
        if sub + 1 < n_sub:
            routing_weights(sub + 1)
            activations(sub + 1)
        coefficients(sub)
    acc_sc[...] += jnp.dot(vt_ref[...], coef_sc[...], preferred_element_type=F32)

    @pl.when(e == pl.num_programs(1) - 1)
    def _():
        peer = jnp.transpose(acc_sc[...])
        y_ref[...] = _layer_norm(alpha * h_ref[...] + peer, g_ref[...], b_ref[...])


def _dense(ht, u_b, vt_b, cnt1, e1n, rank2, e2, h2, g, bta, alpha, tm):
    n = h2.shape[0]
    n_e = N_EXPERTS // EXPERTS_PER_STEP
    full = lambda a: pl.BlockSpec(a.shape, lambda t, e: (0,) * a.ndim)
    rows_blk = pl.BlockSpec((PEER_HEADS, ROWS_PER_STEP, tm), lambda t, e: (0, e, t))
    cols_blk = pl.BlockSpec((PEER_HEADS, N_KEYS, tm), lambda t, e: (0, 0, t))
    return pl.pallas_call(
        functools.partial(_dense_kernel, alpha=alpha),
        grid=(n // tm, n_e),
        in_specs=[
            pl.BlockSpec((D_MODEL, tm), lambda t, e: (0, t)),
            pl.BlockSpec((EXPERTS_PER_STEP, D_MODEL), lambda t, e: (e, 0)),
            pl.BlockSpec((D_MODEL, EXPERTS_PER_STEP), lambda t, e: (0, e)),
            rows_blk, rows_blk, cols_blk, cols_blk,
            pl.BlockSpec((tm, D_MODEL), lambda t, e: (t, 0)),
            full(g), full(bta),
        ],
        out_specs=pl.BlockSpec((tm, D_MODEL), lambda t, e: (t, 0)),
        out_shape=jax.ShapeDtypeStruct((n, D_MODEL), F32),
        scratch_shapes=[pltpu.VMEM((D_MODEL, tm), F32), pltpu.VMEM((EXPERTS_PER_STEP, tm), BF16),
                        pltpu.VMEM((EXPERTS_PER_STEP, tm), BF16)],
        compiler_params=_params("parallel", "arbitrary"),
        name="peer_dense",
    )(ht, u_b, vt_b, cnt1, e1n, rank2, e2, h2, g, bta)


def _block_diag(w):
    nb, bi, bo = w.shape
    eye = jnp.eye(nb, dtype=w.dtype)
    return (eye[:, None, :, None] * w[:, :, None, :]).reshape(nb * bi, nb * bo)


def _prep_weights(w_in, b_forget, conv_w, conv_b, w_rg_a, b_rg_a, w_rg_x, b_rg_x, lru_lambda,
                  w_attn_up, w_rnn_up, w_out, ln1_g, ln1_b, peer_w_query, peer_keys_1, peer_keys_2,
                  peer_u, peer_v, ln2_g, ln2_b):
    c_f = 3 * D_ATTN
    w_in_p = jnp.concatenate(
        [w_in[:, :c_f], jnp.pad(w_in[:, c_f:c_f + N_ATTN_HEADS], ((0, 0), (0, F_PAD - N_ATTN_HEADS))),
         w_in[:, c_f + N_ATTN_HEADS:]], axis=1).astype(BF16)
    row = lambda a: a.reshape(1, -1).astype(F32)
    return dict(
        w_in=w_in_p, b_forget=b_forget.reshape(-1, 1).astype(F32), conv_w=conv_w.astype(F32), conv_b=row(conv_b),
        wa=_block_diag(w_rg_a).astype(BF16), ba=row(b_rg_a), wx=_block_diag(w_rg_x).astype(BF16),
        bx=row(b_rg_x), lam=row(lru_lambda),
        wau=w_attn_up.astype(BF16), wru=w_rnn_up.astype(BF16), wo=w_out.astype(BF16),
        ln1_g=row(ln1_g), ln1_b=row(ln1_b),
        wq_t=jnp.transpose(peer_w_query).astype(BF16), k1=peer_keys_1.astype(BF16),
        k2=peer_keys_2.astype(BF16), u=peer_u.astype(BF16), vt=jnp.transpose(peer_v).astype(BF16),
        ln2_g=row(ln2_g), ln2_b=row(ln2_b),
    )


def _pick_block(n, target):
    blk = min(n, target)
    assert n % blk == 0, (n, blk)
    return blk


def _trunk_layer(x, past_k, past_v, past_logf, conv_hist, h0, p, alpha):
    bsz, t, _ = x.shape
    n = bsz * t
    x2 = x.reshape(n, D_MODEL)
    q, k, v, kb, vb, lf, xr, gg, sa, sr = _inproj(x2, p["w_in"], p["b_forget"], bsz, t, _pick_block(n, 256))

    def state_layout(a):
        if a.ndim == 3:
            return jnp.transpose(a.reshape(bsz, N_ATTN_HEADS, ATTN_HEAD_DIM, t), (0, 3, 1, 2))
        return a.reshape(bsz, t, N_ATTN_HEADS, ATTN_HEAD_DIM)

    tq = _pick_block(t, ATTN_Q_BLOCK)
    n_past = 0 if past_k is None else past_k.shape[1]
    t_all = n_past + t
    t_lanes = -(-t_all // LANES) * LANES
    tk = ATTN_K_BLOCK if t_lanes % ATTN_K_BLOCK == 0 else t_lanes
    t_pad = -(-t_all // tk) * tk
    lf_bht = jnp.transpose(lf.reshape(N_ATTN_HEADS, bsz, t), (1, 0, 2))
    lf_all = lf_bht
    kb3 = kb.reshape(bsz, t, D_ATTN)
    vb3 = vb.reshape(bsz, t, D_ATTN)
    if past_k is not None:
        lf_all = jnp.concatenate([jnp.transpose(past_logf.astype(F32), (0, 2, 1)), lf_bht], axis=2)
        kb3 = jnp.concatenate([past_k.reshape(bsz, n_past, D_ATTN).astype(BF16), kb3], axis=1)
        vb3 = jnp.concatenate([past_v.reshape(bsz, n_past, D_ATTN).astype(BF16), vb3], axis=1)
    pad = ((0, 0), (0, t_pad - t_all), (0, 0))
    kb3, vb3 = jnp.pad(kb3, pad), jnp.pad(vb3, pad)
    f_t = _cumsum_time(jnp.pad(lf_all, ((0, 0), (0, 0), (0, t_pad - t_all))))
    f_g = f_t.reshape(bsz, N_HEAD_GROUPS, HEADS_PER_GROUP, t_pad)
    fq = jnp.transpose(f_g[:, :, :, n_past:n_past + t], (0, 1, 3, 2))
    fk = jnp.transpose(f_g.reshape(bsz, N_HEAD_GROUPS, HEADS_PER_GROUP, t_pad // tk, tk), (0, 1, 3, 2, 4))
    o = _attention(q.reshape(bsz, t, D_ATTN), kb3, vb3, fq, fk, tq=tq, tk=tk, q_off=n_past)

    hist8 = jnp.pad(conv_hist.astype(F32), ((0, 0), (SUBLANES - (CONV_WIDTH - 1), 0), (0, 0)))
    xr3 = xr.reshape(bsz, t, D_RNN)
    rnn_out, h_last = _rnn(xr3, gg.reshape(bsz, t, D_RNN), hist8, h0.astype(F32).reshape(bsz, 1, D_RNN),
                           p["conv_w"], p["conv_b"], p["wa"], p["ba"], p["wx"], p["bx"], p["lam"],
                           _pick_block(t, 1024))
    new_hist = jnp.concatenate([conv_hist.astype(F32), xr3], axis=1)[:, -(CONV_WIDTH - 1):]

    h = _outproj(o.reshape(n, D_ATTN), rnn_out.reshape(n, D_RNN), sa, sr, x2, p["wau"], p["wru"], p["wo"],
                 p["ln1_g"], p["ln1_b"], alpha, _pick_block(n, 1024))
    ht, cnt1, e1n, rank2, e2 = _route(h, p["wq_t"], p["k1"], p["k2"], _pick_block(n, ROUTE_TOKENS))
    y = _dense(ht, p["u"], p["vt"], cnt1, e1n, rank2, e2, h, p["ln2_g"], p["ln2_b"], alpha,
               _pick_block(n, 1024))
    return (y.reshape(bsz, t, D_MODEL), state_layout(k), state_layout(v), jnp.transpose(lf_bht, (0, 2, 1)), new_hist,
            h_last.reshape(bsz, D_RNN))


def kernel(x_prompt, x_sample, cache_k, cache_v, cache_logf, state_conv, state_rnn, w_in, b_forget, conv_w, conv_b, w_rg_a, b_rg_a, w_rg_x, b_rg_x, lru_lambda, w_attn_up, w_rnn_up, w_out, ln1_g, ln1_b, peer_w_query, peer_keys_1, peer_keys_2, peer_u, peer_v, ln2_g, ln2_b):
    depth = w_in.shape[0]
    alpha = (2 * depth) ** 0.25
    layer_weights = (w_in, b_forget, conv_w, conv_b, w_rg_a, b_rg_a, w_rg_x, b_rg_x, lru_lambda, w_attn_up,
                     w_rnn_up, w_out, ln1_g, ln1_b, peer_w_query, peer_keys_1, peer_keys_2, peer_u, peer_v,
                     ln2_g, ln2_b)
    hp, hs = x_prompt, x_sample
    prompt_state, sample_state = [], []
    for l in range(depth):
        p = _prep_weights(*(w[l] for w in layer_weights))
        zero_hist = jnp.zeros((hp.shape[0], CONV_WIDTH - 1, D_RNN), F32)
        zero_h = jnp.zeros((hp.shape[0], D_RNN), F32)
        hp, *st_p = _trunk_layer(hp, None, None, None, zero_hist, zero_h, p, alpha)
        hs, *st_s = _trunk_layer(hs, cache_k[l], cache_v[l], cache_logf[l], state_conv[l], state_rnn[l], p,
                                 alpha)
        prompt_state.append(st_p)
        sample_state.append(st_s)
    stack = lambda states, i: jnp.stack([s[i] for s in states])
    return (hp, hs) + tuple(stack(prompt_state, i) for i in range(5)) + tuple(
        stack(sample_state, i) for i in range(5))
```

```python
import functools
import math

import jax
import jax.numpy as jnp
from jax import lax
from jax.experimental import pallas as pl
from jax.experimental.pallas import tpu as pltpu

F32 = jnp.float32
BF16 = jnp.bfloat16

D_MODEL = 1024
N_ATTN_HEADS = 8
ATTN_HEAD_DIM = 64
D_ATTN = N_ATTN_HEADS * ATTN_HEAD_DIM
ATTN_SCALE = ATTN_HEAD_DIM ** -0.5
D_RNN = 512
CONV_WIDTH = 4
LRU_C = 8.0
N_KEYS = 128
N_EXPERTS = N_KEYS * N_KEYS
PEER_HEADS = 8
PEER_TOPK = 16
PEER_HALF = 128
LN_EPS = 1e-5

LANES = 128
SUBLANES = 8
BF16_ROWS = 2 * SUBLANES
VMEM_LIMIT_BYTES = 56 * 1024 * 1024

HEADS_PER_GROUP = 4
GROUP_LANES = HEADS_PER_GROUP * ATTN_HEAD_DIM
N_HEAD_GROUPS = N_ATTN_HEADS // HEADS_PER_GROUP
ATTN_Q_BLOCK = 512
ATTN_K_BLOCK = 512
ATTN_ROW_BLOCK = 32
F_PAD = LANES
NEG_BIG = -1e30
LOG2_E = math.log2(math.e)
NOT_SELECTED_RANK = 99.0

_C_Q = 0
_C_K = _C_Q + D_ATTN
_C_V = _C_K + D_ATTN
_C_F = _C_V + D_ATTN
_C_XR = _C_F + F_PAD
_C_GATE = _C_XR + D_RNN
_C_GA = _C_GATE + D_RNN
_C_GR = _C_GA + D_MODEL
_C_END = _C_GR + D_MODEL


def _params(*sem):
    return pltpu.CompilerParams(dimension_semantics=sem, vmem_limit_bytes=VMEM_LIMIT_BYTES)


def _sigmoid(x):
    return 1.0 / (1.0 + jnp.exp(-x))


def _gelu_tanh(x):
    k0 = -2.0 * 0.7978845608028654 * LOG2_E
    k1 = -2.0 * 0.035677408136300125 * LOG2_E
    return x / (1.0 + jnp.exp2(x * (k0 + k1 * (x * x))))


def _softplus(x):
    return jnp.maximum(x, 0.0) + jnp.log1p(jnp.exp(-jnp.abs(x)))


def _layer_norm(x, g, b):
    mu = jnp.mean(x, axis=-1, keepdims=True)
    xc = x - mu
    var = jnp.mean(xc * xc, axis=-1, keepdims=True)
    return xc * lax.rsqrt(var + LN_EPS) * g + b


def _inproj_kernel(x_ref, w_ref, bf_ref, q_ref, k_ref, v_ref, kb_ref, vb_ref, lf_ref, xr_ref,
                   gg_ref, sa_ref, sr_ref, *, time_minor_kv):
    xb = x_ref[...].astype(BF16)

    def mm(lo, hi):
        return jnp.dot(xb, w_ref[:, lo:hi], preferred_element_type=F32)

    q_ref[...] = (mm(_C_Q, _C_K) * (ATTN_SCALE * LOG2_E)).astype(BF16)
    k = mm(_C_K, _C_V)
    kb_ref[...] = k.astype(BF16)
    v = mm(_C_V, _C_F)
    vb_ref[...] = v.astype(BF16)
    if time_minor_kv:
        k_ref[0] = jnp.transpose(k)
        v_ref[0] = jnp.transpose(v)
    else:
        k_ref[...] = k
        v_ref[...] = v
    f = jnp.transpose(mm(_C_F, _C_XR))[:N_ATTN_HEADS] + bf_ref[...]
    lf_ref[...] = -_softplus(-f)
    xr_ref[...] = mm(_C_XR, _C_GATE)
    gg_ref[...] = _gelu_tanh(mm(_C_GATE, _C_GA)).astype(BF16)
    sa_ref[...] = _sigmoid(mm(_C_GA, _C_GR)).astype(BF16)
    sr_ref[...] = _sigmoid(mm(_C_GR, _C_END)).astype(BF16)


def _inproj(x2, w_in_p, b_forget, bsz, t, tm):
    n = x2.shape[0]
    row = lambda w: pl.BlockSpec((tm, w), lambda i: (i, 0))
    full = lambda a: pl.BlockSpec(a.shape, lambda i: (0,) * a.ndim)
    time_minor_kv = tm % LANES == 0 and t % tm == 0
    if time_minor_kv:
        kv_shape = jax.ShapeDtypeStruct((bsz, D_ATTN, t), F32)
        kv_spec = pl.BlockSpec((1, D_ATTN, tm), lambda i: (i // (t // tm), 0, i % (t // tm)))
    else:
        kv_shape = jax.ShapeDtypeStruct((n, D_ATTN), F32)
        kv_spec = row(D_ATTN)
    out_shape = (
        jax.ShapeDtypeStruct((n, D_ATTN), BF16),
        kv_shape,
        kv_shape,
        jax.ShapeDtypeStruct((n, D_ATTN), BF16),
        jax.ShapeDtypeStruct((n, D_ATTN), BF16),
        jax.ShapeDtypeStruct((N_ATTN_HEADS, n), F32),
        jax.ShapeDtypeStruct((n, D_RNN), F32),
        jax.ShapeDtypeStruct((n, D_RNN), BF16),
        jax.ShapeDtypeStruct((n, D_MODEL), BF16),
        jax.ShapeDtypeStruct((n, D_MODEL), BF16),
    )
    out_specs = (row(D_ATTN), kv_spec, kv_spec, row(D_ATTN), row(D_ATTN),
                 pl.BlockSpec((N_ATTN_HEADS, tm), lambda i: (0, i)),
                 row(D_RNN), row(D_RNN), row(D_MODEL), row(D_MODEL))
    return pl.pallas_call(
        functools.partial(_inproj_kernel, time_minor_kv=time_minor_kv),
        grid=(n // tm,),
        in_specs=[row(D_MODEL), full(w_in_p), full(b_forget)],
        out_specs=out_specs,
        out_shape=out_shape,
        compiler_params=_params("parallel"),
        name="inproj",
    )(x2, w_in_p, b_forget)


def _cumsum_kernel(x_ref, o_ref):
    x = x_ref[0]
    t = x.shape[1]
    lane = lax.broadcasted_iota(jnp.int32, x.shape, 1)
    d = 1
    while d < t:
        x = x + jnp.where(lane >= d, pltpu.roll(x, d, axis=1), 0.0)
        d *= 2
    o_ref[0] = x * LOG2_E


def _cumsum_time(lf_t):
    b, h, t = lf_t.shape
    spec = pl.BlockSpec((1, h, t), lambda i: (i, 0, 0))
    return pl.pallas_call(
        _cumsum_kernel, grid=(b,), in_specs=[spec], out_specs=spec,
        out_shape=jax.ShapeDtypeStruct(lf_t.shape, F32),
        compiler_params=_params("parallel"), name="logf_cumsum",
    )(lf_t)


def _attn_kernel(q_ref, k_ref, v_ref, fq_ref, fk_ref, o_ref, q_sc, fq_sc, s_sc, p_sc, acc_sc, m_sc, al_sc,
                 lp_sc, *, tq, tk, q_off):
    qi = pl.program_id(2)
    row0 = q_off + qi * tq
    n_full = (row0 + 1) // tk
    heads = range(HEADS_PER_GROUP)
    rb = min(ATTN_ROW_BLOCK, tq)
    lane = lax.broadcasted_iota(jnp.int32, (tq, GROUP_LANES), 1)
    in_head = [(lane >= j * ATTN_HEAD_DIM) & (lane < (j + 1) * ATTN_HEAD_DIM) for j in heads]
    q = q_ref[0]
    for j in heads:
        q_sc[j * tq:(j + 1) * tq, :] = jnp.where(in_head[j], q, jnp.zeros_like(q))
        fq_sc[j * tq:(j + 1) * tq, :] = jnp.broadcast_to(fq_ref[0, 0, :, j:j + 1], (tq, LANES))
    m_sc[...] = jnp.full_like(m_sc, NEG_BIG)
    lp_sc[...] = jnp.zeros_like(lp_sc)
    acc_sc[...] = jnp.zeros_like(acc_sc)
    halves = [slice(0, 2 * tq), slice(2 * tq, 4 * tq)]
    n_lane_tiles = tk // LANES

    def scores(c, s_sc):
        ks = k_ref[0, pl.ds(pl.multiple_of(c * tk, tk), tk), :]
        for hv in halves:
            s_sc[hv, :] = lax.dot_general(q_sc[hv, :], ks, (((1,), (1,)), ((), ())),
                                          preferred_element_type=F32)

    def absorb(c, s_sc, masked):
        start = pl.multiple_of(c * tk, tk)
        fk = fk_ref[0, 0, c]
        vs = v_ref[0, pl.ds(start, tk), :]

        def weighted_values(half):
            hv = halves[half]
            pv = jnp.dot(p_sc[hv, :], vs[:, half * LANES:(half + 1) * LANES], preferred_element_type=F32)
            acc_sc[hv, :] = al_sc[hv, :] * acc_sc[hv, :] + pv

        for j in heads:
            for r in range(tq // rb):
                rows = slice(j * tq + r * rb, j * tq + (r + 1) * rb)
                s = s_sc[rows, :] - fk[j:j + 1, :]
                if masked:
                    q_pos = row0 + r * rb + lax.broadcasted_iota(jnp.int32, (rb, tk), 0)
                    k_pos = c * tk + lax.broadcasted_iota(jnp.int32, (rb, tk), 1)
                    s = jnp.where(k_pos <= q_pos, s, NEG_BIG)
                tiles = [s[:, t * LANES:(t + 1) * LANES] for t in range(n_lane_tiles)]
                row_max = jnp.max(functools.reduce(jnp.maximum, tiles), axis=1, keepdims=True)
                fq = fq_sc[rows, :]
                m_old = m_sc[rows, :]
                m_new = jnp.maximum(m_old, jnp.broadcast_to(row_max, (rb, LANES)) + fq)
                alpha = jnp.exp2(m_old - m_new)
                shift = fq - m_new
                p_tiles = [jnp.exp2(tile + shift) for tile in tiles]
                m_sc[rows, :] = m_new
                al_sc[rows, :] = alpha
                lp_sc[rows, :] = alpha * lp_sc[rows, :] + functools.reduce(jnp.add, p_tiles)
                p_sc[rows, :] = jnp.concatenate(p_tiles, axis=1).astype(BF16)
            if j % 2 == 1:
                weighted_values(j // 2)

    def step(c, carry):
        scores(c, s_sc)
        absorb(c, s_sc, masked=False)
        return carry

    lax.fori_loop(0, n_full, step, 0)
    scores(n_full, s_sc)
    absorb(n_full, s_sc, masked=True)
    o_all = acc_sc[...] / jnp.sum(lp_sc[...], axis=1, keepdims=True)
    first_of_pair = lax.broadcasted_iota(jnp.int32, (tq, LANES), 1) < ATTN_HEAD_DIM
    tiles = [jnp.where(first_of_pair, o_all[2 * g * tq:(2 * g + 1) * tq], o_all[(2 * g + 1) * tq:(2 * g + 2) * tq])
             for g in range(HEADS_PER_GROUP // 2)]
    o_ref[0] = jnp.concatenate(tiles, axis=1).astype(BF16)


def _attention(q, k_all, v_all, fq, fk, *, tq, tk, q_off):
    b, t, _ = q.shape
    t_k = k_all.shape[1]
    n_kv = t_k // tk
    assert all((q_off + i * tq) % tk + tq <= tk for i in range(t // tq)), "a query block straddles key chunks"
    kern = functools.partial(_attn_kernel, tq=tq, tk=tk, q_off=q_off)
    rows = HEADS_PER_GROUP * tq
    return pl.pallas_call(
        kern,
        grid=(b, N_HEAD_GROUPS, t // tq),
        in_specs=[
            pl.BlockSpec((1, tq, GROUP_LANES), lambda bi, g, i: (bi, i, g)),
            pl.BlockSpec((1, t_k, GROUP_LANES), lambda bi, g, i: (bi, 0, g)),
            pl.BlockSpec((1, t_k, GROUP_LANES), lambda bi, g, i: (bi, 0, g)),
            pl.BlockSpec((1, 1, tq, HEADS_PER_GROUP), lambda bi, g, i: (bi, g, i, 0)),
            pl.BlockSpec((1, 1, n_kv, HEADS_PER_GROUP, tk), lambda bi, g, i: (bi, g, 0, 0, 0)),
        ],
        out_specs=pl.BlockSpec((1, tq, GROUP_LANES), lambda bi, g, i: (bi, i, g)),
        out_shape=jax.ShapeDtypeStruct((b, t, D_ATTN), BF16),
        scratch_shapes=[
            pltpu.VMEM((rows, GROUP_LANES), BF16),
            pltpu.VMEM((rows, LANES), F32),
            pltpu.VMEM((rows, tk), F32),
            pltpu.VMEM((rows, tk), BF16),
            pltpu.VMEM((rows, LANES), F32),
            pltpu.VMEM((rows, LANES), F32),
            pltpu.VMEM((rows, LANES), F32),
            pltpu.VMEM((rows, LANES), F32),
        ],
        compiler_params=_params("parallel", "parallel", "arbitrary"),
        name="fox_attention",
    )(q, k_all, v_all, fq, fk)


def _rnn_kernel(xr_ref, gg_ref, hist_ref, h0_ref, cw_ref, cb_ref, wa_ref, ba_ref, wx_ref, bx_ref,
                lam_ref, out_ref, hl_ref, win_sc, h_sc, *, tb):
    t = pl.program_id(1)

    @pl.when(t == 0)
    def _():
        win_sc[:SUBLANES, :] = hist_ref[0]
        h_sc[...] = h0_ref[0]

    x = xr_ref[0]
    win_sc[SUBLANES:, :] = x
    xc = x * cw_ref[CONV_WIDTH - 1:CONV_WIDTH, :] + cb_ref[...]
    for s in range(1, CONV_WIDTH):
        xc = xc + win_sc[SUBLANES - s:SUBLANES - s + tb, :] * cw_ref[CONV_WIDTH - 1 - s:CONV_WIDTH - s, :]
    win_sc[:SUBLANES, :] = x[tb - SUBLANES:tb]

    xcb = xc.astype(BF16)
    r = _sigmoid(jnp.dot(xcb, wa_ref[...], preferred_element_type=F32) + ba_ref[...])
    ig = _sigmoid(jnp.dot(xcb, wx_ref[...], preferred_element_type=F32) + bx_ref[...])
    log_a = (-LRU_C) * r * _softplus(-lam_ref[...])
    a = jnp.exp(log_a)
    one_minus_a2 = 1.0 - a * a
    scale = jnp.where(one_minus_a2 > 0.0, one_minus_a2 * lax.rsqrt(one_minus_a2), 0.0)
    bterm = scale * ig * xc

    grouped = (tb // SUBLANES, SUBLANES, D_RNN)
    a, bterm = a.reshape(grouped), bterm.reshape(grouped)
    row_in_group = lax.broadcasted_iota(jnp.int32, grouped, 1)
    d = 1
    while d < SUBLANES:
        valid = row_in_group >= d
        a_s = pltpu.roll(a, d, axis=1)
        b_s = pltpu.roll(bterm, d, axis=1)
        bterm = jnp.where(valid, a * b_s + bterm, bterm)
        a = jnp.where(valid, a * a_s, a)
        d *= 2
    h_last = h_sc[...]
    groups = []
    for g in range(tb // SUBLANES):
        h_g = bterm[g] + a[g] * h_last
        h_last = h_g[SUBLANES - 1:SUBLANES]
        groups.append(h_g)
    h = jnp.concatenate(groups, axis=0)
    h_sc[...] = h_last
    hl_ref[0] = h_last
    out_ref[0] = (h * gg_ref[0].astype(F32)).astype(BF16)


def _rnn(xr, gg, hist8, h0, conv_w, conv_b, wa, ba, wx, bx, lam, tb):
    b, t, _ = xr.shape
    blk = pl.BlockSpec((1, tb, D_RNN), lambda bi, ti: (bi, ti, 0))
    full = lambda a: pl.BlockSpec(a.shape, lambda bi, ti: (0,) * a.ndim)
    per_b = lambda r: pl.BlockSpec((1, r, D_RNN), lambda bi, ti: (bi, 0, 0))
    return pl.pallas_call(
        functools.partial(_rnn_kernel, tb=tb),
        grid=(b, t // tb),
        in_specs=[blk, blk, per_b(SUBLANES), per_b(1), full(conv_w), full(conv_b), full(wa), full(ba),
                  full(wx), full(bx), full(lam)],
        out_specs=(blk, per_b(1)),
        out_shape=(jax.ShapeDtypeStruct((b, t, D_RNN), BF16), jax.ShapeDtypeStruct((b, 1, D_RNN), F32)),
        scratch_shapes=[pltpu.VMEM((SUBLANES + tb, D_RNN), F32), pltpu.VMEM((1, D_RNN), F32)],
        compiler_params=_params("parallel", "arbitrary"),
        name="conv_rglru",
    )(xr, gg, hist8, h0, conv_w, conv_b, wa, ba, wx, bx, lam)


def _outproj_kernel(o_ref, r_ref, sa_ref, sr_ref, x_ref, wau_ref, wru_ref, wo_ref, g_ref, b_ref,
                    h_ref, *, alpha):
    up_a = jnp.dot(o_ref[...], wau_ref[...], preferred_element_type=F32)
    up_r = jnp.dot(r_ref[...], wru_ref[...], preferred_element_type=F32)
    merged = sa_ref[...].astype(F32) * up_a + sr_ref[...].astype(F32) * up_r
    mix = jnp.dot(merged.astype(BF16), wo_ref[...], preferred_element_type=F32)
    h_ref[...] = _layer_norm(alpha * x_ref[...] + mix, g_ref[...], b_ref[...])


def _outproj(o2, r2, sa, sr, x2, wau, wru, wo, g, bta, alpha, tm):
    n = x2.shape[0]
    row = lambda w: pl.BlockSpec((tm, w), lambda i: (i, 0))
    full = lambda a: pl.BlockSpec(a.shape, lambda i: (0,) * a.ndim)
    return pl.pallas_call(
        functools.partial(_outproj_kernel, alpha=alpha),
        grid=(n // tm,),
        in_specs=[row(D_ATTN), row(D_RNN), row(D_MODEL), row(D_MODEL), row(D_MODEL), full(wau), full(wru),
                  full(wo), full(g), full(bta)],
        out_specs=row(D_MODEL),
        out_shape=jax.ShapeDtypeStruct((n, D_MODEL), F32),
        compiler_params=_params("parallel"),
        name="outproj_ln1",
    )(o2, r2, sa, sr, x2, wau, wru, wo, g, bta)


ROUTE_TOKENS = 2 * LANES


def _odd_even_merge_sort(n):
    def merge(lo, hi, r):
        step = r * 2
        if step < hi - lo:
            yield from merge(lo, hi, step)
            yield from merge(lo + r, hi, step)
            yield from ((i, i + r) for i in range(lo + r, hi - r, step))
        else:
            yield (lo, lo + r)

    def sort(lo, hi):
        if hi - lo >= 1:
            mid = lo + (hi - lo) // 2
            yield from sort(lo, mid)
            yield from sort(mid + 1, hi)
            yield from merge(lo, hi, 1)

    return tuple(sort(0, n - 1))


_SORT_TOPK = _odd_even_merge_sort(PEER_TOPK)


def _compare_exchange(v, i, j):
    v[i], v[j] = jnp.maximum(v[i], v[j]), jnp.minimum(v[i], v[j])


def _sort_bitonic(v):
    d = PEER_TOPK // 2
    while d >= 1:
        for i in range(PEER_TOPK):
            if i & d == 0:
                _compare_exchange(v, i, i + d)
        d //= 2
    return v


def _top_k_sorted(tiles, presorted=False):
    v = list(tiles)
    if not presorted:
        for i, j in _SORT_TOPK:
            _compare_exchange(v, i, j)
    shift = SUBLANES // 2
    while shift >= 1:
        other = [pltpu.roll(x, shift, axis=0) for x in v]
        v = _sort_bitonic([jnp.maximum(v[i], other[PEER_TOPK - 1 - i]) for i in range(PEER_TOPK)])
        shift //= 2
    return v


def _count_greater(x, t):
    assert len(t) == 16, "the bisection below is written out for 16 entries"
    one = lambda m, w: jnp.where(m, float(w), 0.0)
    b3 = t[7] > x
    b2 = jnp.where(b3, t[11], t[3]) > x
    b1 = jnp.where(b3, jnp.where(b2, t[13], t[9]), jnp.where(b2, t[5], t[1])) > x
    hi = jnp.where(b2, jnp.where(b1, t[14], t[12]), jnp.where(b1, t[10], t[8]))
    lo = jnp.where(b2, jnp.where(b1, t[6], t[4]), jnp.where(b1, t[2], t[0]))
    b0 = jnp.where(b3, hi, lo) > x
    count = one(b3, 8) + one(b2, 4) + one(b1, 2) + one(b0, 1)
    return jnp.where(t[15] > x, float(PEER_TOPK), count)


def _route_kernel(h_ref, wq_ref, k1_ref, k2_ref, ht_ref, cnt_ref, e1_ref, r2_ref, e2_ref,
                  qt_sc, t1_sc, t2_sc, *, tm):
    ht = jnp.transpose(h_ref[...]).astype(BF16)
    ht_ref[...] = ht
    qt_sc[...] = jnp.dot(wq_ref[...], ht, preferred_element_type=F32).astype(BF16)
    key_iota = lax.broadcasted_iota(jnp.int32, (N_KEYS, tm), 0).astype(F32)
    top_iota = lax.broadcasted_iota(jnp.int32, (PEER_TOPK, tm), 0).astype(F32)
    front_rows = SUBLANES

    def scores(hd):
        base = hd * 2 * PEER_HALF
        s1 = jnp.dot(k1_ref[...], qt_sc[base:base + PEER_HALF], preferred_element_type=F32)
        s2 = jnp.dot(k2_ref[...], qt_sc[base + PEER_HALF:base + 2 * PEER_HALF], preferred_element_type=F32)
        return s1, s2

    sub = lax.broadcasted_iota(jnp.int32, (SUBLANES, tm), 0)
    tiles_of = lambda s: [s[r * SUBLANES:(r + 1) * SUBLANES] for r in range(N_KEYS // SUBLANES)]
    sublane_sum = lambda x: jnp.sum(x, axis=0, keepdims=True)

    def route_head_sorted(hd):
        s1, s2 = scores(hd)
        rows1, rows2 = tiles_of(s1), tiles_of(s2)
        t1 = _top_k_sorted(rows1)
        t2 = _top_k_sorted(rows2)
        t1_lo = t1[SUBLANES - 1]
        for a in range(SUBLANES - 2, -1, -1):
            t1_lo = jnp.where(sub == a, t1[a], t1_lo)
        cand = [t1_lo + t2[b] for b in range(PEER_TOPK)]
        top_lo = _top_k_sorted(cand, presorted=True)
        hi_sums = [t1[a] + t2[0] for a in range(SUBLANES, PEER_TOPK)]
        top = _sort_bitonic(top_lo[:SUBLANES] + [jnp.maximum(top_lo[i], hi_sums[PEER_TOPK - 1 - i])
                                                 for i in range(SUBLANES, PEER_TOPK)])
        tau = top[PEER_TOPK - 1]
        cnt_lo = functools.reduce(jnp.add, [jnp.where(c >= tau, 1.0, 0.0) for c in cand])
        cnt = [jnp.broadcast_to(cnt_lo[a:a + 1], (SUBLANES, tm)) for a in range(SUBLANES)]
        cnt += [jnp.where(s >= tau, 1.0, 0.0) for s in hi_sums]
        z = functools.reduce(jnp.add, [jnp.exp(t - top[0]) for t in top])

        cnt1_rows, rank2_rows = [], []
        for r in range(N_KEYS // SUBLANES):
            c = jnp.where(rows1[r] + t2[0] >= tau, 1.0, 0.0)
            c = jnp.where(rows1[r] >= t1[PEER_TOPK - 1], c, 0.0)
            for a in range(SUBLANES - 1, -1, -1):
                c = jnp.where(rows1[r] >= t1[a], cnt[a], c)
            cnt1_rows.append(c)
            g = _count_greater(rows2[r], t2)
            rank2_rows.append(jnp.where(g < float(PEER_TOPK), g, NOT_SELECTED_RANK))
        cnt1 = jnp.concatenate(cnt1_rows, axis=0)
        rank2 = jnp.concatenate(rank2_rows, axis=0)
        cnt_ref[hd] = cnt1
        e1_ref[hd] = jnp.exp(s1 - t1[0][0:1]) / z[0:1]
        r2_ref[hd] = rank2.astype(BF16)
        e2_ref[hd] = jnp.exp(s2 - t2[0][0:1]).astype(BF16)

        gap = lambda t: functools.reduce(jnp.minimum, [t[b] - t[b + 1] for b in range(PEER_TOPK - 1)])[0:1]
        n_cnt = sublane_sum(cnt_lo) + functools.reduce(jnp.add, cnt[SUBLANES:])[0:1]
        n_sel = sublane_sum(functools.reduce(jnp.add, cnt1_rows))
        n_rank = sublane_sum(functools.reduce(
            jnp.add, [jnp.where(x < float(PEER_TOPK), 1.0, 0.0) for x in rank2_rows]))
        off = lambda n: jnp.abs(n - float(PEER_TOPK))
        return (off(n_cnt) + off(n_sel) + off(n_rank)
                + jnp.where(jnp.minimum(gap(t1), gap(t2)) > 0.0, 0.0, 1.0))

    def route_head_exact(hd):
        def pick_one(v, iota, n):
            m = jnp.max(v, axis=0, keepdims=True)
            return m, iota == jnp.min(jnp.where(v == m, iota, float(n)), axis=0, keepdims=True)

        s1, s2 = scores(hd)

        def extract(a, carry):
            v1, r1, v2, r2 = carry
            m1, sel1 = pick_one(v1, key_iota, N_KEYS)
            m2, sel2 = pick_one(v2, key_iota, N_KEYS)
            t1_sc[pl.ds(a, 1), :] = m1
            t2_sc[pl.ds(a, 1), :] = m2
            af = jnp.asarray(a, dtype=F32)
            return (jnp.where(sel1, -jnp.inf, v1), jnp.where(sel1, af, r1),
                    jnp.where(sel2, -jnp.inf, v2), jnp.where(sel2, af, r2))

        no_rank = jnp.full((N_KEYS, tm), NOT_SELECTED_RANK, F32)
        _, rank1, _, rank2 = lax.fori_loop(0, PEER_TOPK, extract, (s1, no_rank, s2, no_rank))
        t1 = t1_sc[...]
        t2 = t2_sc[...]
        top0 = t1[0:1] + t2[0:1]

        def pick(_, carry):
            ptr, front, z = carry
            m, sel = pick_one(front, top_iota, PEER_TOPK)
            ptr = ptr + jnp.where(sel, 1.0, 0.0)
            lo = ptr[:front_rows]
            nxt = jnp.full(lo.shape, -jnp.inf, F32)
            for b in range(1, PEER_TOPK):
                nxt = jnp.where(lo == float(b), t2_sc[b:b + 1, :], nxt)
            nxt = jnp.where(lo == 0.0, t2[0:1], nxt)
            front = jnp.concatenate(
                [t1[:front_rows] + nxt, jnp.where(sel[front_rows:], -jnp.inf, front[front_rows:])], axis=0)
            return ptr, front, z + jnp.exp(m - top0)

        cnt, _, z = lax.fori_loop(
            0, PEER_TOPK, pick,
            (jnp.zeros((PEER_TOPK, tm), F32), t1 + t2[0:1], jnp.zeros((1, tm), F32)))

        cnt1 = jnp.zeros((N_KEYS, tm), F32)
        for a in range(PEER_TOPK):
            cnt1 = jnp.where(rank1 == float(a), cnt[a:a + 1], cnt1)
        cnt_ref[hd] = cnt1
        e1_ref[hd] = jnp.exp(s1 - t1[0:1]) / z
        r2_ref[hd] = rank2.astype(BF16)
        e2_ref[hd] = jnp.exp(s2 - t2[0:1]).astype(BF16)

    doubt = [route_head_sorted(hd) for hd in range(PEER_HEADS)]

    @pl.when(jnp.max(functools.reduce(jnp.maximum, doubt)) > 0.0)
    def _():
        for hd in range(PEER_HEADS):
            @pl.when(jnp.max(doubt[hd]) > 0.0)
            def _():
                route_head_exact(hd)


def _route(h2, wq_t, k1, k2, tm):
    n = h2.shape[0]
    full = lambda a: pl.BlockSpec(a.shape, lambda i: (0,) * a.ndim)
    per_head = pl.BlockSpec((PEER_HEADS, N_KEYS, tm), lambda i: (0, 0, i))
    hshape = lambda dt: jax.ShapeDtypeStruct((PEER_HEADS, N_KEYS, n), dt)
    return pl.pallas_call(
        functools.partial(_route_kernel, tm=tm),
        grid=(n // tm,),
        in_specs=[pl.BlockSpec((tm, D_MODEL), lambda i: (i, 0)), full(wq_t), full(k1), full(k2)],
        out_specs=(pl.BlockSpec((D_MODEL, tm), lambda i: (0, i)), per_head, per_head, per_head, per_head),
        out_shape=(jax.ShapeDtypeStruct((D_MODEL, n), BF16), hshape(F32), hshape(F32), hshape(BF16),
                   hshape(BF16)),
        scratch_shapes=[pltpu.VMEM((PEER_HEADS * 2 * PEER_HALF, tm), BF16),
                        pltpu.VMEM((PEER_TOPK, tm), F32), pltpu.VMEM((PEER_TOPK, tm), F32)],
        compiler_params=_params("parallel"),
        name="peer_route",
    )(h2, wq_t, k1, k2)


ROWS_PER_STEP = 16
EXPERTS_PER_STEP = ROWS_PER_STEP * N_KEYS
ROWS_PER_SUB = 4
EXPERTS_PER_SUB = ROWS_PER_SUB * N_KEYS
DENSE_CHUNK = 2 * LANES


def _bf16_row_tile(row):
    tile = jnp.broadcast_to(row, (BF16_ROWS, row.shape[1])).astype(BF16)
    return jnp.concatenate([tile] * (N_KEYS // BF16_ROWS), axis=0)


def _dense_kernel(ht_ref, u_ref, vt_ref, cnt_ref, e1_ref, r2_ref, e2_ref, h_ref, g_ref, b_ref,
                  y_ref, acc_sc, act_sc, coef_sc, *, alpha):
    e = pl.program_id(1)
    tm = act_sc.shape[1]
    n_sub = ROWS_PER_STEP // ROWS_PER_SUB

    @pl.when(e == 0)
    def _():
        acc_sc[...] = jnp.zeros_like(acc_sc)

    def activations(sub):
        rows = slice(sub * EXPERTS_PER_SUB, (sub + 1) * EXPERTS_PER_SUB)
        act_sc[rows, :] = jnp.dot(u_ref[rows, :], ht_ref[...], preferred_element_type=F32).astype(BF16)

    def tiles(sub):
        chunk = min(DENSE_CHUNK, tm)
        for il in range(sub * ROWS_PER_SUB, (sub + 1) * ROWS_PER_SUB):
            for c in range(tm // chunk):
                yield il, slice(il * N_KEYS, (il + 1) * N_KEYS), slice(c * chunk, (c + 1) * chunk)

    def routing_weights(sub):
        for il, rows, lanes in tiles(sub):
            w = None
            for hd in range(PEER_HEADS):
                cnt = _bf16_row_tile(cnt_ref[hd, il:il + 1, lanes])
                e1 = _bf16_row_tile(e1_ref[hd, il:il + 1, lanes])
                term = jnp.where(r2_ref[hd, :, lanes] < cnt, e2_ref[hd, :, lanes] * e1, jnp.zeros((), BF16))
                w = term if w is None else w + term
            coef_sc[rows, lanes] = w

    def coefficients(sub):
        for _, rows, lanes in tiles(sub):
            coef_sc[rows, lanes] = coef_sc[rows, lanes] * _gelu_tanh(act_sc[rows, lanes])

    routing_weights(0)
    activations(0)
    for sub in range(n_sub):
        if sub + 1 < n_sub:
            routing_weights(sub + 1)
            activations(sub + 1)
        coefficients(sub)
    acc_sc[...] += jnp.dot(vt_ref[...], coef_sc[...], preferred_element_type=F32)

    @pl.when(e == pl.num_programs(1) - 1)
    def _():
        peer = jnp.transpose(acc_sc[...])
        y_ref[...] = _layer_norm(alpha * h_ref[...] + peer, g_ref[...], b_ref[...])


def _dense(ht, u_b, vt_b, cnt1, e1n, rank2, e2, h2, g, bta, alpha, tm):
    n = h2.shape[0]
    n_e = N_EXPERTS // EXPERTS_PER_STEP
    full = lambda a: pl.BlockSpec(a.shape, lambda t, e: (0,) * a.ndim)
    rows_blk = pl.BlockSpec((PEER_HEADS, ROWS_PER_STEP, tm), lambda t, e: (0, e, t))
    cols_blk = pl.BlockSpec((PEER_HEADS, N_KEYS, tm), lambda t, e: (0, 0, t))
    return pl.pallas_call(
        functools.partial(_dense_kernel, alpha=alpha),
        grid=(n // tm, n_e),
        in_specs=[
            pl.BlockSpec((D_MODEL, tm), lambda t, e: (0, t)),
            pl.BlockSpec((EXPERTS_PER_STEP, D_MODEL), lambda t, e: (e, 0)),
            pl.BlockSpec((D_MODEL, EXPERTS_PER_STEP), lambda t, e: (0, e)),
            rows_blk, rows_blk, cols_blk, cols_blk,
            pl.BlockSpec((tm, D_MODEL), lambda t, e: (t, 0)),
            full(g), full(bta),
        ],
        out_specs=pl.BlockSpec((tm, D_MODEL), lambda t, e: (t, 0)),
        out_shape=jax.ShapeDtypeStruct((n, D_MODEL), F32),
        scratch_shapes=[pltpu.VMEM((D_MODEL, tm), F32), pltpu.VMEM((EXPERTS_PER_STEP, tm), BF16),
                        pltpu.VMEM((EXPERTS_PER_STEP, tm), BF16)],
        compiler_params=_params("parallel", "arbitrary"),
        name="peer_dense",
    )(ht, u_b, vt_b, cnt1, e1n, rank2, e2, h2, g, bta)


def _block_diag(w):
    nb, bi, bo = w.shape
    eye = jnp.eye(nb, dtype=w.dtype)
    return (eye[:, None, :, None] * w[:, :, None, :]).reshape(nb * bi, nb * bo)


def _prep_weights(w_in, b_forget, conv_w, conv_b, w_rg_a, b_rg_a, w_rg_x, b_rg_x, lru_lambda,
                  w_attn_up, w_rnn_up, w_out, ln1_g, ln1_b, peer_w_query, peer_keys_1, peer_keys_2,
                  peer_u, peer_v, ln2_g, ln2_b):
    c_f = 3 * D_ATTN
    w_in_p = jnp.concatenate(
        [w_in[:, :c_f], jnp.pad(w_in[:, c_f:c_f + N_ATTN_HEADS], ((0, 0), (0, F_PAD - N_ATTN_HEADS))),
         w_in[:, c_f + N_ATTN_HEADS:]], axis=1).astype(BF16)
    row = lambda a: a.reshape(1, -1).astype(F32)
    return dict(
        w_in=w_in_p, b_forget=b_forget.reshape(-1, 1).astype(F32), conv_w=conv_w.astype(F32), conv_b=row(conv_b),
        wa=_block_diag(w_rg_a).astype(BF16), ba=row(b_rg_a), wx=_block_diag(w_rg_x).astype(BF16),
        bx=row(b_rg_x), lam=row(lru_lambda),
        wau=w_attn_up.astype(BF16), wru=w_rnn_up.astype(BF16), wo=w_out.astype(BF16),
        ln1_g=row(ln1_g), ln1_b=row(ln1_b),
        wq_t=jnp.transpose(peer_w_query).astype(BF16), k1=peer_keys_1.astype(BF16),
        k2=peer_keys_2.astype(BF16), u=peer_u.astype(BF16), vt=jnp.transpose(peer_v).astype(BF16),
        ln2_g=row(ln2_g), ln2_b=row(ln2_b),
    )


def _pick_block(n, target):
    blk = min(n, target)
    assert n % blk == 0, (n, blk)
    return blk


def _trunk_layer(x, past_k, past_v, past_logf, conv_hist, h0, p, alpha):
    bsz, t, _ = x.shape
    n = bsz * t
    x2 = x.reshape(n, D_MODEL)
    q, k, v, kb, vb, lf, xr, gg, sa, sr = _inproj(x2, p["w_in"], p["b_forget"], bsz, t, _pick_block(n, 256))

    def state_layout(a):
        if a.ndim == 3:
            return jnp.transpose(a.reshape(bsz, N_ATTN_HEADS, ATTN_HEAD_DIM, t), (0, 3, 1, 2))
        return a.reshape(bsz, t, N_ATTN_HEADS, ATTN_HEAD_DIM)

    tq = _pick_block(t, ATTN_Q_BLOCK)
    n_past = 0 if past_k is None else past_k.shape[1]
    t_all = n_past + t
    t_lanes = -(-t_all // LANES) * LANES
    tk = ATTN_K_BLOCK if t_lanes % ATTN_K_BLOCK == 0 else t_lanes
    t_pad = -(-t_all // tk) * tk
    lf_bht = jnp.transpose(lf.reshape(N_ATTN_HEADS, bsz, t), (1, 0, 2))
    lf_all = lf_bht
    kb3 = kb.reshape(bsz, t, D_ATTN)
    vb3 = vb.reshape(bsz, t, D_ATTN)
    if past_k is not None:
        lf_all = jnp.concatenate([jnp.transpose(past_logf.astype(F32), (0, 2, 1)), lf_bht], axis=2)
        kb3 = jnp.concatenate([past_k.reshape(bsz, n_past, D_ATTN).astype(BF16), kb3], axis=1)
        vb3 = jnp.concatenate([past_v.reshape(bsz, n_past, D_ATTN).astype(BF16), vb3], axis=1)
    pad = ((0, 0), (0, t_pad - t_all), (0, 0))
    kb3, vb3 = jnp.pad(kb3, pad), jnp.pad(vb3, pad)
    f_t = _cumsum_time(jnp.pad(lf_all, ((0, 0), (0, 0), (0, t_pad - t_all))))
    f_g = f_t.reshape(bsz, N_HEAD_GROUPS, HEADS_PER_GROUP, t_pad)
    fq = jnp.transpose(f_g[:, :, :, n_past:n_past + t], (0, 1, 3, 2))
    fk = jnp.transpose(f_g.reshape(bsz, N_HEAD_GROUPS, HEADS_PER_GROUP, t_pad // tk, tk), (0, 1, 3, 2, 4))
    o = _attention(q.reshape(bsz, t, D_ATTN), kb3, vb3, fq, fk, tq=tq, tk=tk, q_off=n_past)

    hist8 = jnp.pad(conv_hist.astype(F32), ((0, 0), (SUBLANES - (CONV_WIDTH - 1), 0), (0, 0)))
    xr3 = xr.reshape(bsz, t, D_RNN)
    rnn_out, h_last = _rnn(xr3, gg.reshape(bsz, t, D_RNN), hist8, h0.astype(F32).reshape(bsz, 1, D_RNN),
                           p["conv_w"], p["conv_b"], p["wa"], p["ba"], p["wx"], p["bx"], p["lam"],
                           _pick_block(t, 1024))
    new_hist = jnp.concatenate([conv_hist.astype(F32), xr3], axis=1)[:, -(CONV_WIDTH - 1):]

    h = _outproj(o.reshape(n, D_ATTN), rnn_out.reshape(n, D_RNN), sa, sr, x2, p["wau"], p["wru"], p["wo"],
                 p["ln1_g"], p["ln1_b"], alpha, _pick_block(n, 1024))
    ht, cnt1, e1n, rank2, e2 = _route(h, p["wq_t"], p["k1"], p["k2"], _pick_block(n, ROUTE_TOKENS))
    y = _dense(ht, p["u"], p["vt"], cnt1, e1n, rank2, e2, h, p["ln2_g"], p["ln2_b"], alpha,
               _pick_block(n, 512))
    return (y.reshape(bsz, t, D_MODEL), state_layout(k), state_layout(v), jnp.transpose(lf_bht, (0, 2, 1)), new_hist,
            h_last.reshape(bsz, D_RNN))


def kernel(x_prompt, x_sample, cache_k, cache_v, cache_logf, state_conv, state_rnn, w_in, b_forget, conv_w, conv_b, w_rg_a, b_rg_a, w_rg_x, b_rg_x, lru_lambda, w_attn_up, w_rnn_up, w_out, ln1_g, ln1_b, peer_w_query, peer_keys_1, peer_keys_2, peer_u, peer_v, ln2_g, ln2_b):
    depth = w_in.shape[0]
    alpha = (2 * depth) ** 0.25
    layer_weights = (w_in, b_forget, conv_w, conv_b, w_rg_a, b_rg_a, w_rg_x, b_rg_x, lru_lambda, w_attn_up,
                     w_rnn_up, w_out, ln1_g, ln1_b, peer_w_query, peer_keys_1, peer_keys_2, peer_u, peer_v,
                     ln2_g, ln2_b)
    hp, hs = x_prompt, x_sample
    prompt_state, sample_state = [], []
    for l in range(depth):
        p = _prep_weights(*(w[l] for w in layer_weights))
        zero_hist = jnp.zeros((hp.shape[0], CONV_WIDTH - 1, D_RNN), F32)
        zero_h = jnp.zeros((hp.shape[0], D_RNN), F32)
        hp, *st_p = _trunk_layer(hp, None, None, None, zero_hist, zero_h, p, alpha)
        hs, *st_s = _trunk_layer(hs, cache_k[l], cache_v[l], cache_logf[l], state_conv[l], state_rnn[l], p,
                                 alpha)
        prompt_state.append(st_p)
        sample_state.append(st_s)
    stack = lambda states, i: jnp.stack([s[i] for s in states])
    return (hp, hs) + tuple(stack(prompt_state, i) for i in range(5)) + tuple(
        stack(sample_state, i) for i in range(5))
```

```python
import functools
import math

import jax
import jax.numpy as jnp
from jax import lax
from jax.experimental import pallas as pl
from jax.experimental.pallas import tpu as pltpu

F32 = jnp.float32
BF16 = jnp.bfloat16

D_MODEL = 1024
N_ATTN_HEADS = 8
ATTN_HEAD_DIM = 64
D_ATTN = N_ATTN_HEADS * ATTN_HEAD_DIM
ATTN_SCALE = ATTN_HEAD_DIM ** -0.5
D_RNN = 512
CONV_WIDTH = 4
LRU_C = 8.0
N_KEYS = 128
N_EXPERTS = N_KEYS * N_KEYS
PEER_HEADS = 8
PEER_TOPK = 16
PEER_HALF = 128
LN_EPS = 1e-5

LANES = 128
SUBLANES = 8
BF16_ROWS = 2 * SUBLANES
VMEM_LIMIT_BYTES = 56 * 1024 * 1024

HEADS_PER_GROUP = 4
GROUP_LANES = HEADS_PER_GROUP * ATTN_HEAD_DIM
N_HEAD_GROUPS = N_ATTN_HEADS // HEADS_PER_GROUP
ATTN_Q_BLOCK = 512
ATTN_K_BLOCK = 512
ATTN_ROW_BLOCK = 32
F_PAD = LANES
NEG_BIG = -1e30
LOG2_E = math.log2(math.e)
NOT_SELECTED_RANK = 99.0

_C_Q = 0
_C_K = _C_Q + D_ATTN
_C_V = _C_K + D_ATTN
_C_F = _C_V + D_ATTN
_C_XR = _C_F + F_PAD
_C_GATE = _C_XR + D_RNN
_C_GA = _C_GATE + D_RNN
_C_GR = _C_GA + D_MODEL
_C_END = _C_GR + D_MODEL


def _params(*sem):
    return pltpu.CompilerParams(dimension_semantics=sem, vmem_limit_bytes=VMEM_LIMIT_BYTES)


def _sigmoid(x):
    return 1.0 / (1.0 + jnp.exp(-x))


def _gelu_tanh(x):
    k0 = -2.0 * 0.7978845608028654 * LOG2_E
    k1 = -2.0 * 0.035677408136300125 * LOG2_E
    return x / (1.0 + jnp.exp2(x * (k0 + k1 * (x * x))))


def _softplus(x):
    return jnp.maximum(x, 0.0) + jnp.log1p(jnp.exp(-jnp.abs(x)))


def _layer_norm(x, g, b):
    mu = jnp.mean(x, axis=-1, keepdims=True)
    xc = x - mu
    var = jnp.mean(xc * xc, axis=-1, keepdims=True)
    return xc * lax.rsqrt(var + LN_EPS) * g + b


def _inproj_kernel(x_ref, w_ref, bf_ref, q_ref, k_ref, v_ref, kb_ref, vb_ref, lf_ref, xr_ref,
                   gg_ref, sa_ref, sr_ref, *, time_minor_kv):
    xb = x_ref[...].astype(BF16)

    def mm(lo, hi):
        return jnp.dot(xb, w_ref[:, lo:hi], preferred_element_type=F32)

    q_ref[...] = (mm(_C_Q, _C_K) * (ATTN_SCALE * LOG2_E)).astype(BF16)
    k = mm(_C_K, _C_V)
    kb_ref[...] = k.astype(BF16)
    v = mm(_C_V, _C_F)
    vb_ref[...] = v.astype(BF16)
    if time_minor_kv:
        k_ref[0] = jnp.transpose(k)
        v_ref[0] = jnp.transpose(v)
    else:
        k_ref[...] = k
        v_ref[...] = v
    f = jnp.transpose(mm(_C_F, _C_XR))[:N_ATTN_HEADS] + bf_ref[...]
    lf_ref[...] = -_softplus(-f)
    xr_ref[...] = mm(_C_XR, _C_GATE)
    gg_ref[...] = _gelu_tanh(mm(_C_GATE, _C_GA)).astype(BF16)
    sa_ref[...] = _sigmoid(mm(_C_GA, _C_GR)).astype(BF16)
    sr_ref[...] = _sigmoid(mm(_C_GR, _C_END)).astype(BF16)


def _inproj(x2, w_in_p, b_forget, bsz, t, tm):
    n = x2.shape[0]
    row = lambda w: pl.BlockSpec((tm, w), lambda i: (i, 0))
    full = lambda a: pl.BlockSpec(a.shape, lambda i: (0,) * a.ndim)
    time_minor_kv = tm % LANES == 0 and t % tm == 0
    if time_minor_kv:
        kv_shape = jax.ShapeDtypeStruct((bsz, D_ATTN, t), F32)
        kv_spec = pl.BlockSpec((1, D_ATTN, tm), lambda i: (i // (t // tm), 0, i % (t // tm)))
    else:
        kv_shape = jax.ShapeDtypeStruct((n, D_ATTN), F32)
        kv_spec = row(D_ATTN)
    out_shape = (
        jax.ShapeDtypeStruct((n, D_ATTN), BF16),
        kv_shape,
        kv_shape,
        jax.ShapeDtypeStruct((n, D_ATTN), BF16),
        jax.ShapeDtypeStruct((n, D_ATTN), BF16),
        jax.ShapeDtypeStruct((N_ATTN_HEADS, n), F32),
        jax.ShapeDtypeStruct((n, D_RNN), F32),
        jax.ShapeDtypeStruct((n, D_RNN), BF16),
        jax.ShapeDtypeStruct((n, D_MODEL), BF16),
        jax.ShapeDtypeStruct((n, D_MODEL), BF16),
    )
    out_specs = (row(D_ATTN), kv_spec, kv_spec, row(D_ATTN), row(D_ATTN),
                 pl.BlockSpec((N_ATTN_HEADS, tm), lambda i: (0, i)),
                 row(D_RNN), row(D_RNN), row(D_MODEL), row(D_MODEL))
    return pl.pallas_call(
        functools.partial(_inproj_kernel, time_minor_kv=time_minor_kv),
        grid=(n // tm,),
        in_specs=[row(D_MODEL), full(w_in_p), full(b_forget)],
        out_specs=out_specs,
        out_shape=out_shape,
        compiler_params=_params("parallel"),
        name="inproj",
    )(x2, w_in_p, b_forget)


def _cumsum_kernel(x_ref, o_ref):
    x = x_ref[0]
    t = x.shape[1]
    lane = lax.broadcasted_iota(jnp.int32, x.shape, 1)
    d = 1
    while d < t:
        x = x + jnp.where(lane >= d, pltpu.roll(x, d, axis=1), 0.0)
        d *= 2
    o_ref[0] = x * LOG2_E


def _cumsum_time(lf_t):
    b, h, t = lf_t.shape
    spec = pl.BlockSpec((1, h, t), lambda i: (i, 0, 0))
    return pl.pallas_call(
        _cumsum_kernel, grid=(b,), in_specs=[spec], out_specs=spec,
        out_shape=jax.ShapeDtypeStruct(lf_t.shape, F32),
        compiler_params=_params("parallel"), name="logf_cumsum",
    )(lf_t)


def _attn_kernel(q_ref, k_ref, v_ref, fq_ref, fk_ref, o_ref, q_sc, fq_sc, s_sc, p_sc, acc_sc, m_sc, al_sc,
                 lp_sc, *, tq, tk, q_off):
    qi = pl.program_id(2)
    row0 = q_off + qi * tq
    n_full = (row0 + 1) // tk
    heads = range(HEADS_PER_GROUP)
    rb = min(ATTN_ROW_BLOCK, tq)
    first_of_pair = lax.broadcasted_iota(jnp.int32, (tq, LANES), 1) < ATTN_HEAD_DIM
    for j in heads:
        q_tile = q_ref[0, :, (j // 2) * LANES:(j // 2 + 1) * LANES]
        own = first_of_pair if j % 2 == 0 else jnp.logical_not(first_of_pair)
        q_sc[j * tq:(j + 1) * tq, :] = jnp.where(own, q_tile, jnp.zeros_like(q_tile))
        fq_sc[j * tq:(j + 1) * tq, :] = jnp.broadcast_to(fq_ref[0, 0, :, j:j + 1], (tq, LANES))
    m_sc[...] = jnp.full_like(m_sc, NEG_BIG)
    lp_sc[...] = jnp.zeros_like(lp_sc)
    acc_sc[...] = jnp.zeros_like(acc_sc)
    halves = [slice(0, 2 * tq), slice(2 * tq, 4 * tq)]
    n_lane_tiles = tk // LANES

    def scores(c, s_sc):
        ks = k_ref[0, pl.ds(pl.multiple_of(c * tk, tk), tk), :]
        for half, hv in enumerate(halves):
            s_sc[hv, :] = lax.dot_general(q_sc[hv, :], ks[:, half * LANES:(half + 1) * LANES],
                                          (((1,), (1,)), ((), ())), preferred_element_type=F32)

    def absorb(c, s_sc, masked):
        start = pl.multiple_of(c * tk, tk)
        fk = fk_ref[0, 0, c]
        vs = v_ref[0, pl.ds(start, tk), :]

        def weighted_values(half):
            hv = halves[half]
            pv = jnp.dot(p_sc[hv, :], vs[:, half * LANES:(half + 1) * LANES], preferred_element_type=F32)
            acc_sc[hv, :] = al_sc[hv, :] * acc_sc[hv, :] + pv

        for j in heads:
            for r in range(tq // rb):
                rows = slice(j * tq + r * rb, j * tq + (r + 1) * rb)
                s = s_sc[rows, :] - fk[j:j + 1, :]
                if masked:
                    q_pos = row0 + r * rb + lax.broadcasted_iota(jnp.int32, (rb, tk), 0)
                    k_pos = c * tk + lax.broadcasted_iota(jnp.int32, (rb, tk), 1)
                    s = jnp.where(k_pos <= q_pos, s, NEG_BIG)
                tiles = [s[:, t * LANES:(t + 1) * LANES] for t in range(n_lane_tiles)]
                row_max = jnp.max(functools.reduce(jnp.maximum, tiles), axis=1, keepdims=True)
                fq = fq_sc[rows, :]
                m_old = m_sc[rows, :]
                m_new = jnp.maximum(m_old, jnp.broadcast_to(row_max, (rb, LANES)) + fq)
                alpha = jnp.exp2(m_old - m_new)
                shift = fq - m_new
                p_tiles = [jnp.exp2(tile + shift) for tile in tiles]
                m_sc[rows, :] = m_new
                al_sc[rows, :] = alpha
                lp_sc[rows, :] = alpha * lp_sc[rows, :] + functools.reduce(jnp.add, p_tiles)
                p_sc[rows, :] = jnp.concatenate(p_tiles, axis=1).astype(BF16)
            if j % 2 == 1:
                weighted_values(j // 2)

    def step(c, carry):
        scores(c, s_sc)
        absorb(c, s_sc, masked=False)
        return carry

    lax.fori_loop(0, n_full, step, 0)
    scores(n_full, s_sc)
    absorb(n_full, s_sc, masked=True)
    o_all = acc_sc[...] / jnp.sum(lp_sc[...], axis=1, keepdims=True)
    tiles =[jnp.where(first_of_pair, o_all[2 * g * tq:(2 * g + 1) * tq], o_all[(2 * g + 1) * tq:(2 * g + 2) * tq])
             for g in range(HEADS_PER_GROUP // 2)]
    o_ref[0] = jnp.concatenate(tiles, axis=1).astype(BF16)


def _attention(q, k_all, v_all, fq, fk, *, tq, tk, q_off):
    b, t, _ = q.shape
    t_k = k_all.shape[1]
    n_kv = t_k // tk
    assert all((q_off + i * tq) % tk + tq <= tk for i in range(t // tq)), "a query block straddles key chunks"
    kern = functools.partial(_attn_kernel, tq=tq, tk=tk, q_off=q_off)
    rows = HEADS_PER_GROUP * tq
    return pl.pallas_call(
        kern,
        grid=(b, N_HEAD_GROUPS, t // tq),
        in_specs=[
            pl.BlockSpec((1, tq, GROUP_LANES), lambda bi, g, i: (bi, i, g)),
            pl.BlockSpec((1, t_k, GROUP_LANES), lambda bi, g, i: (bi, 0, g)),
            pl.BlockSpec((1, t_k, GROUP_LANES), lambda bi, g, i: (bi, 0, g)),
            pl.BlockSpec((1, 1, tq, HEADS_PER_GROUP), lambda bi, g, i: (bi, g, i, 0)),
            pl.BlockSpec((1, 1, n_kv, HEADS_PER_GROUP, tk), lambda bi, g, i: (bi, g, 0, 0, 0)),
        ],
        out_specs=pl.BlockSpec((1, tq, GROUP_LANES), lambda bi, g, i: (bi, i, g)),
        out_shape=jax.ShapeDtypeStruct((b, t, D_ATTN), BF16),
        scratch_shapes=[
            pltpu.VMEM((rows, LANES), BF16),
            pltpu.VMEM((rows, LANES), F32),
            pltpu.VMEM((rows, tk), F32),
            pltpu.VMEM((rows, tk), BF16),
            pltpu.VMEM((rows, LANES), F32),
            pltpu.VMEM((rows, LANES), F32),
            pltpu.VMEM((rows, LANES), F32),
            pltpu.VMEM((rows, LANES), F32),
        ],
        compiler_params=_params("parallel", "parallel", "arbitrary"),
        name="fox_attention",
    )(q, k_all, v_all, fq, fk)


def _rnn_kernel(xr_ref, gg_ref, hist_ref, h0_ref, cw_ref, cb_ref, wa_ref, ba_ref, wx_ref, bx_ref,
                lam_ref, out_ref, hl_ref, win_sc, h_sc, *, tb):
    t = pl.program_id(1)

    @pl.when(t == 0)
    def _():
        win_sc[:SUBLANES, :] = hist_ref[0]
        h_sc[...] = h0_ref[0]

    x = xr_ref[0]
    win_sc[SUBLANES:, :] = x
    xc = x * cw_ref[CONV_WIDTH - 1:CONV_WIDTH, :] + cb_ref[...]
    for s in range(1, CONV_WIDTH):
        xc = xc + win_sc[SUBLANES - s:SUBLANES - s + tb, :] * cw_ref[CONV_WIDTH - 1 - s:CONV_WIDTH - s, :]
    win_sc[:SUBLANES, :] = x[tb - SUBLANES:tb]

    xcb = xc.astype(BF16)
    r = _sigmoid(jnp.dot(xcb, wa_ref[...], preferred_element_type=F32) + ba_ref[...])
    ig = _sigmoid(jnp.dot(xcb, wx_ref[...], preferred_element_type=F32) + bx_ref[...])
    log_a = (-LRU_C) * r * _softplus(-lam_ref[...])
    a = jnp.exp(log_a)
    one_minus_a2 = 1.0 - a * a
    scale = jnp.where(one_minus_a2 > 0.0, one_minus_a2 * lax.rsqrt(one_minus_a2), 0.0)
    bterm = scale * ig * xc

    grouped = (tb // SUBLANES, SUBLANES, D_RNN)
    a, bterm = a.reshape(grouped), bterm.reshape(grouped)
    row_in_group = lax.broadcasted_iota(jnp.int32, grouped, 1)
    d = 1
    while d < SUBLANES:
        valid = row_in_group >= d
        a_s = pltpu.roll(a, d, axis=1)
        b_s = pltpu.roll(bterm, d, axis=1)
        bterm = jnp.where(valid, a * b_s + bterm, bterm)
        a = jnp.where(valid, a * a_s, a)
        d *= 2
    h_last = h_sc[...]
    groups = []
    for g in range(tb // SUBLANES):
        h_g = bterm[g] + a[g] * h_last
        h_last = h_g[SUBLANES - 1:SUBLANES]
        groups.append(h_g)
    h = jnp.concatenate(groups, axis=0)
    h_sc[...] = h_last
    hl_ref[0] = h_last
    out_ref[0] = (h * gg_ref[0].astype(F32)).astype(BF16)


def _rnn(xr, gg, hist8, h0, conv_w, conv_b, wa, ba, wx, bx, lam, tb):
    b, t, _ = xr.shape
    blk = pl.BlockSpec((1, tb, D_RNN), lambda bi, ti: (bi, ti, 0))
    full = lambda a: pl.BlockSpec(a.shape, lambda bi, ti: (0,) * a.ndim)
    per_b = lambda r: pl.BlockSpec((1, r, D_RNN), lambda bi, ti: (bi, 0, 0))
    return pl.pallas_call(
        functools.partial(_rnn_kernel, tb=tb),
        grid=(b, t // tb),
        in_specs=[blk, blk, per_b(SUBLANES), per_b(1), full(conv_w), full(conv_b), full(wa), full(ba),
                  full(wx), full(bx), full(lam)],
        out_specs=(blk, per_b(1)),
        out_shape=(jax.ShapeDtypeStruct((b, t, D_RNN), BF16), jax.ShapeDtypeStruct((b, 1, D_RNN), F32)),
        scratch_shapes=[pltpu.VMEM((SUBLANES + tb, D_RNN), F32), pltpu.VMEM((1, D_RNN), F32)],
        compiler_params=_params("parallel", "arbitrary"),
        name="conv_rglru",
    )(xr, gg, hist8, h0, conv_w, conv_b, wa, ba, wx, bx, lam)


def _outproj_kernel(o_ref, r_ref, sa_ref, sr_ref, x_ref, wau_ref, wru_ref, wo_ref, g_ref, b_ref,
                    h_ref, *, alpha):
    up_a = jnp.dot(o_ref[...], wau_ref[...], preferred_element_type=F32)
    up_r = jnp.dot(r_ref[...], wru_ref[...], preferred_element_type=F32)
    merged = sa_ref[...].astype(F32) * up_a + sr_ref[...].astype(F32) * up_r
    mix = jnp.dot(merged.astype(BF16), wo_ref[...], preferred_element_type=F32)
    h_ref[...] = _layer_norm(alpha * x_ref[...] + mix, g_ref[...], b_ref[...])


def _outproj(o2, r2, sa, sr, x2, wau, wru, wo, g, bta, alpha, tm):
    n = x2.shape[0]
    row = lambda w: pl.BlockSpec((tm, w), lambda i: (i, 0))
    full = lambda a: pl.BlockSpec(a.shape, lambda i: (0,) * a.ndim)
    return pl.pallas_call(
        functools.partial(_outproj_kernel, alpha=alpha),
        grid=(n // tm,),
        in_specs=[row(D_ATTN), row(D_RNN), row(D_MODEL), row(D_MODEL), row(D_MODEL), full(wau), full(wru),
                  full(wo), full(g), full(bta)],
        out_specs=row(D_MODEL),
        out_shape=jax.ShapeDtypeStruct((n, D_MODEL), F32),
        compiler_params=_params("parallel"),
        name="outproj_ln1",
    )(o2, r2, sa, sr, x2, wau, wru, wo, g, bta)


ROUTE_TOKENS = 2 * LANES


def _odd_even_merge_sort(n):
    def merge(lo, hi, r):
        step = r * 2
        if step < hi - lo:
            yield from merge(lo, hi, step)
            yield from merge(lo + r, hi, step)
            yield from ((i, i + r) for i in range(lo + r, hi - r, step))
        else:
            yield (lo, lo + r)

    def sort(lo, hi):
        if hi - lo >= 1:
            mid = lo + (hi - lo) // 2
            yield from sort(lo, mid)
            yield from sort(mid + 1, hi)
            yield from merge(lo, hi, 1)

    return tuple(sort(0, n - 1))


_SORT_TOPK = _odd_even_merge_sort(PEER_TOPK)


def _compare_exchange(v, i, j):
    v[i], v[j] = jnp.maximum(v[i], v[j]), jnp.minimum(v[i], v[j])


def _sort_bitonic(v):
    d = PEER_TOPK // 2
    while d >= 1:
        for i in range(PEER_TOPK):
            if i & d == 0:
                _compare_exchange(v, i, i + d)
        d //= 2
    return v


def _top_k_sorted(tiles, presorted=False):
    v = list(tiles)
    if not presorted:
        for i, j in _SORT_TOPK:
            _compare_exchange(v, i, j)
    shift = SUBLANES // 2
    while shift >= 1:
        other = [pltpu.roll(x, shift, axis=0) for x in v]
        v = _sort_bitonic([jnp.maximum(v[i], other[PEER_TOPK - 1 - i]) for i in range(PEER_TOPK)])
        shift //= 2
    return v


def _count_greater(x, t):
    assert len(t) == 16, "the bisection below is written out for 16 entries"
    one = lambda m, w: jnp.where(m, float(w), 0.0)
    b3 = t[7] > x
    b2 = jnp.where(b3, t[11], t[3]) > x
    b1 = jnp.where(b3, jnp.where(b2, t[13], t[9]), jnp.where(b2, t[5], t[1])) > x
    hi = jnp.where(b2, jnp.where(b1, t[14], t[12]), jnp.where(b1, t[10], t[8]))
    lo = jnp.where(b2, jnp.where(b1, t[6], t[4]), jnp.where(b1, t[2], t[0]))
    b0 = jnp.where(b3, hi, lo) > x
    count = one(b3, 8) + one(b2, 4) + one(b1, 2) + one(b0, 1)
    return jnp.where(t[15] > x, float(PEER_TOPK), count)


def _route_kernel(h_ref, wq_ref, k1_ref, k2_ref, ht_ref, cnt_ref, e1_ref, r2_ref, e2_ref,
                  qt_sc, t1_sc, t2_sc, *, tm):
    ht = jnp.transpose(h_ref[...]).astype(BF16)
    ht_ref[...] = ht
    qt_sc[...] = jnp.dot(wq_ref[...], ht, preferred_element_type=F32).astype(BF16)
    key_iota = lax.broadcasted_iota(jnp.int32, (N_KEYS, tm), 0).astype(F32)
    top_iota = lax.broadcasted_iota(jnp.int32, (PEER_TOPK, tm), 0).astype(F32)
    front_rows = SUBLANES

    def scores(hd):
        base = hd * 2 * PEER_HALF
        s1 = jnp.dot(k1_ref[...], qt_sc[base:base + PEER_HALF], preferred_element_type=F32)
        s2 = jnp.dot(k2_ref[...], qt_sc[base + PEER_HALF:base + 2 * PEER_HALF], preferred_element_type=F32)
        return s1, s2

    sub = lax.broadcasted_iota(jnp.int32, (SUBLANES, tm), 0)
    tiles_of = lambda s: [s[r * SUBLANES:(r + 1) * SUBLANES] for r in range(N_KEYS // SUBLANES)]
    sublane_sum = lambda x: jnp.sum(x, axis=0, keepdims=True)

    def route_head_sorted(hd):
        s1, s2 = scores(hd)
        rows1, rows2 = tiles_of(s1), tiles_of(s2)
        t1 = _top_k_sorted(rows1)
        t2 = _top_k_sorted(rows2)
        t1_lo = t1[SUBLANES - 1]
        for a in range(SUBLANES - 2, -1, -1):
            t1_lo = jnp.where(sub == a, t1[a], t1_lo)
        cand = [t1_lo + t2[b] for b in range(PEER_TOPK)]
        top_lo = _top_k_sorted(cand, presorted=True)
        hi_sums = [t1[a] + t2[0] for a in range(SUBLANES, PEER_TOPK)]
        top = _sort_bitonic(top_lo[:SUBLANES] + [jnp.maximum(top_lo[i], hi_sums[PEER_TOPK - 1 - i])
                                                 for i in range(SUBLANES, PEER_TOPK)])
        tau = top[PEER_TOPK - 1]
        cnt_lo = functools.reduce(jnp.add, [jnp.where(c >= tau, 1.0, 0.0) for c in cand])
        cnt = [jnp.broadcast_to(cnt_lo[a:a + 1], (SUBLANES, tm)) for a in range(SUBLANES)]
        cnt += [jnp.where(s >= tau, 1.0, 0.0) for s in hi_sums]
        z = functools.reduce(jnp.add, [jnp.exp(t - top[0]) for t in top])

        cnt1_rows, rank2_rows = [], []
        for r in range(N_KEYS // SUBLANES):
            c = jnp.where(rows1[r] + t2[0] >= tau, 1.0, 0.0)
            c = jnp.where(rows1[r] >= t1[PEER_TOPK - 1], c, 0.0)
            for a in range(SUBLANES - 1, -1, -1):
                c = jnp.where(rows1[r] >= t1[a], cnt[a], c)
            cnt1_rows.append(c)
            g = _count_greater(rows2[r], t2)
            rank2_rows.append(jnp.where(g < float(PEER_TOPK), g, NOT_SELECTED_RANK))
        cnt1 = jnp.concatenate(cnt1_rows, axis=0)
        rank2 = jnp.concatenate(rank2_rows, axis=0)
        cnt_ref[hd] = cnt1
        e1_ref[hd] = jnp.exp(s1 - t1[0][0:1]) / z[0:1]
        r2_ref[hd] = rank2.astype(BF16)
        e2_ref[hd] = jnp.exp(s2 - t2[0][0:1]).astype(BF16)

        gap = lambda t: functools.reduce(jnp.minimum, [t[b] - t[b + 1] for b in range(PEER_TOPK - 1)])[0:1]
        n_cnt = sublane_sum(cnt_lo) + functools.reduce(jnp.add, cnt[SUBLANES:])[0:1]
        n_sel = sublane_sum(functools.reduce(jnp.add, cnt1_rows))
        n_rank = sublane_sum(functools.reduce(
            jnp.add, [jnp.where(x < float(PEER_TOPK), 1.0, 0.0) for x in rank2_rows]))
        off = lambda n: jnp.abs(n - float(PEER_TOPK))
        return (off(n_cnt) + off(n_sel) + off(n_rank)
                + jnp.where(jnp.minimum(gap(t1), gap(t2)) > 0.0, 0.0, 1.0))

    def route_head_exact(hd):
        def pick_one(v, iota, n):
            m = jnp.max(v, axis=0, keepdims=True)
            return m, iota == jnp.min(jnp.where(v == m, iota, float(n)), axis=0, keepdims=True)

        s1, s2 = scores(hd)

        def extract(a, carry):
            v1, r1, v2, r2 = carry
            m1, sel1 = pick_one(v1, key_iota, N_KEYS)
            m2, sel2 = pick_one(v2, key_iota, N_KEYS)
            t1_sc[pl.ds(a, 1), :] = m1
            t2_sc[pl.ds(a, 1), :] = m2
            af = jnp.asarray(a, dtype=F32)
            return (jnp.where(sel1, -jnp.inf, v1), jnp.where(sel1, af, r1),
                    jnp.where(sel2, -jnp.inf, v2), jnp.where(sel2, af, r2))

        no_rank = jnp.full((N_KEYS, tm), NOT_SELECTED_RANK, F32)
        _, rank1, _, rank2 = lax.fori_loop(0, PEER_TOPK, extract, (s1, no_rank, s2, no_rank))
        t1 = t1_sc[...]
        t2 = t2_sc[...]
        top0 = t1[0:1] + t2[0:1]

        def pick(_, carry):
            ptr, front, z = carry
            m, sel = pick_one(front, top_iota, PEER_TOPK)
            ptr = ptr + jnp.where(sel, 1.0, 0.0)
            lo = ptr[:front_rows]
            nxt = jnp.full(lo.shape, -jnp.inf, F32)
            for b in range(1, PEER_TOPK):
                nxt = jnp.where(lo == float(b), t2_sc[b:b + 1, :], nxt)
            nxt = jnp.where(lo == 0.0, t2[0:1], nxt)
            front = jnp.concatenate(
                [t1[:front_rows] + nxt, jnp.where(sel[front_rows:], -jnp.inf, front[front_rows:])], axis=0)
            return ptr, front, z + jnp.exp(m - top0)

        cnt, _, z = lax.fori_loop(
            0, PEER_TOPK, pick,
            (jnp.zeros((PEER_TOPK, tm), F32), t1 + t2[0:1], jnp.zeros((1, tm), F32)))

        cnt1 = jnp.zeros((N_KEYS, tm), F32)
        for a in range(PEER_TOPK):
            cnt1 = jnp.where(rank1 == float(a), cnt[a:a + 1], cnt1)
        cnt_ref[hd] = cnt1
        e1_ref[hd] = jnp.exp(s1 - t1[0:1]) / z
        r2_ref[hd] = rank2.astype(BF16)
        e2_ref[hd] = jnp.exp(s2 - t2[0:1]).astype(BF16)

    doubt = [route_head_sorted(hd) for hd in range(PEER_HEADS)]

    @pl.when(jnp.max(functools.reduce(jnp.maximum, doubt)) > 0.0)
    def _():
        for hd in range(PEER_HEADS):
            @pl.when(jnp.max(doubt[hd]) > 0.0)
            def _():
                route_head_exact(hd)


def _route(h2, wq_t, k1, k2, tm):
    n = h2.shape[0]
    full = lambda a: pl.BlockSpec(a.shape, lambda i: (0,) * a.ndim)
    per_head = pl.BlockSpec((PEER_HEADS, N_KEYS, tm), lambda i: (0, 0, i))
    hshape = lambda dt: jax.ShapeDtypeStruct((PEER_HEADS, N_KEYS, n), dt)
    return pl.pallas_call(
        functools.partial(_route_kernel, tm=tm),
        grid=(n // tm,),
        in_specs=[pl.BlockSpec((tm, D_MODEL), lambda i: (i, 0)), full(wq_t), full(k1), full(k2)],
        out_specs=(pl.BlockSpec((D_MODEL, tm), lambda i: (0, i)), per_head, per_head, per_head, per_head),
        out_shape=(jax.ShapeDtypeStruct((D_MODEL, n), BF16), hshape(F32), hshape(F32), hshape(BF16),
                   hshape(BF16)),
        scratch_shapes=[pltpu.VMEM((PEER_HEADS * 2 * PEER_HALF, tm), BF16),
                        pltpu.VMEM((PEER_TOPK, tm), F32), pltpu.VMEM((PEER_TOPK, tm), F32)],
        compiler_params=_params("parallel"),
        name="peer_route",
    )(h2, wq_t, k1, k2)


ROWS_PER_STEP = 16
EXPERTS_PER_STEP = ROWS_PER_STEP * N_KEYS
ROWS_PER_SUB = 4
EXPERTS_PER_SUB = ROWS_PER_SUB * N_KEYS
DENSE_CHUNK = 2 * LANES


def _bf16_row_tile(row):
    tile = jnp.broadcast_to(row, (BF16_ROWS, row.shape[1])).astype(BF16)
    return jnp.concatenate([tile] * (N_KEYS // BF16_ROWS), axis=0)


def _dense_kernel(ht_ref, u_ref, vt_ref, cnt_ref, e1_ref, r2_ref, e2_ref, h_ref, g_ref, b_ref,
                  y_ref, acc_sc, act_sc, coef_sc, *, alpha):
    e = pl.program_id(1)
    tm = act_sc.shape[1]
    n_sub = ROWS_PER_STEP // ROWS_PER_SUB

    @pl.when(e == 0)
    def _():
        acc_sc[...] = jnp.zeros_like(acc_sc)

    def activations(sub):
        rows = slice(sub * EXPERTS_PER_SUB, (sub + 1) * EXPERTS_PER_SUB)
        act_sc[rows, :] = jnp.dot(u_ref[rows, :], ht_ref[...], preferred_element_type=F32).astype(BF16)

    def tiles(sub):
        chunk = min(DENSE_CHUNK, tm)
        for il in range(sub * ROWS_PER_SUB, (sub + 1) * ROWS_PER_SUB):
            for c in range(tm // chunk):
                yield il, slice(il * N_KEYS, (il + 1) * N_KEYS), slice(c * chunk, (c + 1) * chunk)

    def routing_weights(sub):
        for il, rows, lanes in tiles(sub):
            w = None
            for hd in range(PEER_HEADS):
                cnt = _bf16_row_tile(cnt_ref[hd, il:il + 1, lanes])
                e1 = _bf16_row_tile(e1_ref[hd, il:il + 1, lanes])
                term = jnp.where(r2_ref[hd, :, lanes] < cnt, e2_ref[hd, :, lanes] * e1, jnp.zeros((), BF16))
                w = term if w is None else w + term
            coef_sc[rows, lanes] = w

    def coefficients(sub):
        for _, rows, lanes in tiles(sub):
            coef_sc[rows, lanes] = coef_sc[rows, lanes] * _gelu_tanh(act_sc[rows, lanes])

    routing_weights(0)
    activations(0)
    for sub in range(n_sub):
        if sub + 1 < n_sub:
            routing_weights(sub + 1)
            activations(sub + 1)
        coefficients(sub)
    acc_sc[...] += jnp.dot(vt_ref[...], coef_sc[...], preferred_element_type=F32)

    @pl.when(e == pl.num_programs(1) - 1)
    def _():
        peer = jnp.transpose(acc_sc[...])
        y_ref[...] = _layer_norm(alpha * h_ref[...] + peer, g_ref[...], b_ref[...])


def _dense(ht, u_b, vt_b, cnt1, e1n, rank2, e2, h2, g, bta, alpha, tm):
    n = h2.shape[0]
    n_e = N_EXPERTS // EXPERTS_PER_STEP
    full = lambda a: pl.BlockSpec(a.shape, lambda t, e: (0,) * a.ndim)
    rows_blk = pl.BlockSpec((PEER_HEADS, ROWS_PER_STEP, tm), lambda t, e: (0, e, t))
    cols_blk = pl.BlockSpec((PEER_HEADS, N_KEYS, tm), lambda t, e: (0, 0, t))
    return pl.pallas_call(
        functools.partial(_dense_kernel, alpha=alpha),
        grid=(n // tm, n_e),
        in_specs=[
            pl.BlockSpec((D_MODEL, tm), lambda t, e: (0, t)),
            pl.BlockSpec((EXPERTS_PER_STEP, D_MODEL), lambda t, e: (e, 0)),
            pl.BlockSpec((D_MODEL, EXPERTS_PER_STEP), lambda t, e: (0, e)),
            rows_blk, rows_blk, cols_blk, cols_blk,
            pl.BlockSpec((tm, D_MODEL), lambda t, e: (t, 0)),
            full(g), full(bta),
        ],
        out_specs=pl.BlockSpec((tm, D_MODEL), lambda t, e: (t, 0)),
        out_shape=jax.ShapeDtypeStruct((n, D_MODEL), F32),
        scratch_shapes=[pltpu.VMEM((D_MODEL, tm), F32), pltpu.VMEM((EXPERTS_PER_STEP, tm), BF16),
                        pltpu.VMEM((EXPERTS_PER_STEP, tm), BF16)],
        compiler_params=_params("parallel", "arbitrary"),
        name="peer_dense",
    )(ht, u_b, vt_b, cnt1, e1n, rank2, e2, h2, g, bta)


def _block_diag(w):
    nb, bi, bo = w.shape
    eye = jnp.eye(nb, dtype=w.dtype)
    return (eye[:, None, :, None] * w[:, :, None, :]).reshape(nb * bi, nb * bo)


def _prep_weights(w_in, b_forget, conv_w, conv_b, w_rg_a, b_rg_a, w_rg_x, b_rg_x, lru_lambda,
                  w_attn_up, w_rnn_up, w_out, ln1_g, ln1_b, peer_w_query, peer_keys_1, peer_keys_2,
                  peer_u, peer_v, ln2_g, ln2_b):
    c_f = 3 * D_ATTN
    w_in_p = jnp.concatenate(
        [w_in[:, :c_f], jnp.pad(w_in[:, c_f:c_f + N_ATTN_HEADS], ((0, 0), (0, F_PAD - N_ATTN_HEADS))),
         w_in[:, c_f + N_ATTN_HEADS:]], axis=1).astype(BF16)
    row = lambda a: a.reshape(1, -1).astype(F32)
    return dict(
        w_in=w_in_p, b_forget=b_forget.reshape(-1, 1).astype(F32), conv_w=conv_w.astype(F32), conv_b=row(conv_b),
        wa=_block_diag(w_rg_a).astype(BF16), ba=row(b_rg_a), wx=_block_diag(w_rg_x).astype(BF16),
        bx=row(b_rg_x), lam=row(lru_lambda),
        wau=w_attn_up.astype(BF16), wru=w_rnn_up.astype(BF16), wo=w_out.astype(BF16),
        ln1_g=row(ln1_g), ln1_b=row(ln1_b),
        wq_t=jnp.transpose(peer_w_query).astype(BF16), k1=peer_keys_1.astype(BF16),
        k2=peer_keys_2.astype(BF16), u=peer_u.astype(BF16), vt=jnp.transpose(peer_v).astype(BF16),
        ln2_g=row(ln2_g), ln2_b=row(ln2_b),
    )


def _pick_block(n, target):
    blk = min(n, target)
    assert n % blk == 0, (n, blk)
    return blk


def _trunk_layer(x, past_k, past_v, past_logf, conv_hist, h0, p, alpha):
    bsz, t, _ = x.shape
    n = bsz * t
    x2 = x.reshape(n, D_MODEL)
    q, k, v, kb, vb, lf, xr, gg, sa, sr = _inproj(x2, p["w_in"], p["b_forget"], bsz, t, _pick_block(n, 256))

    def state_layout(a):
        if a.ndim == 3:
            return jnp.transpose(a.reshape(bsz, N_ATTN_HEADS, ATTN_HEAD_DIM, t), (0, 3, 1, 2))
        return a.reshape(bsz, t, N_ATTN_HEADS, ATTN_HEAD_DIM)

    tq = _pick_block(t, ATTN_Q_BLOCK)
    n_past = 0 if past_k is None else past_k.shape[1]
    t_all = n_past + t
    t_lanes = -(-t_all // LANES) * LANES
    tk = ATTN_K_BLOCK if t_lanes % ATTN_K_BLOCK == 0 else t_lanes
    t_pad = -(-t_all // tk) * tk
    lf_bht = jnp.transpose(lf.reshape(N_ATTN_HEADS, bsz, t), (1, 0, 2))
    lf_all = lf_bht
    kb3 = kb.reshape(bsz, t, D_ATTN)
    vb3 = vb.reshape(bsz, t, D_ATTN)
    if past_k is not None:
        lf_all = jnp.concatenate([jnp.transpose(past_logf.astype(F32), (0, 2, 1)), lf_bht], axis=2)
        kb3 = jnp.concatenate([past_k.reshape(bsz, n_past, D_ATTN).astype(BF16), kb3], axis=1)
        vb3 = jnp.concatenate([past_v.reshape(bsz, n_past, D_ATTN).astype(BF16), vb3], axis=1)
    pad = ((0, 0), (0, t_pad - t_all), (0, 0))
    kb3, vb3 = jnp.pad(kb3, pad), jnp.pad(vb3, pad)
    f_t = _cumsum_time(jnp.pad(lf_all, ((0, 0), (0, 0), (0, t_pad - t_all))))
    f_g = f_t.reshape(bsz, N_HEAD_GROUPS, HEADS_PER_GROUP, t_pad)
    fq = jnp.transpose(f_g[:, :, :, n_past:n_past + t], (0, 1, 3, 2))
    fk = jnp.transpose(f_g.reshape(bsz, N_HEAD_GROUPS, HEADS_PER_GROUP, t_pad // tk, tk), (0, 1, 3, 2, 4))
    o = _attention(q.reshape(bsz, t, D_ATTN), kb3, vb3, fq, fk, tq=tq, tk=tk, q_off=n_past)

    hist8 = jnp.pad(conv_hist.astype(F32), ((0, 0), (SUBLANES - (CONV_WIDTH - 1), 0), (0, 0)))
    xr3 = xr.reshape(bsz, t, D_RNN)
    rnn_out, h_last = _rnn(xr3, gg.reshape(bsz, t, D_RNN), hist8, h0.astype(F32).reshape(bsz, 1, D_RNN),
                           p["conv_w"], p["conv_b"], p["wa"], p["ba"], p["wx"], p["bx"], p["lam"],
                           _pick_block(t, 1024))
    new_hist = jnp.concatenate([conv_hist.astype(F32), xr3], axis=1)[:, -(CONV_WIDTH - 1):]

    h = _outproj(o.reshape(n, D_ATTN), rnn_out.reshape(n, D_RNN), sa, sr, x2, p["wau"], p["wru"], p["wo"],
                 p["ln1_g"], p["ln1_b"], alpha, _pick_block(n, 1024))
    ht, cnt1, e1n, rank2, e2 = _route(h, p["wq_t"], p["k1"], p["k2"], _pick_block(n, ROUTE_TOKENS))
    y = _dense(ht, p["u"], p["vt"], cnt1, e1n, rank2, e2, h, p["ln2_g"], p["ln2_b"], alpha,
               _pick_block(n, 512))
    return (y.reshape(bsz, t, D_MODEL), state_layout(k), state_layout(v), jnp.transpose(lf_bht, (0, 2, 1)), new_hist,
            h_last.reshape(bsz, D_RNN))


def kernel(x_prompt, x_sample, cache_k, cache_v, cache_logf, state_conv, state_rnn, w_in, b_forget, conv_w, conv_b, w_rg_a, b_rg_a, w_rg_x, b_rg_x, lru_lambda, w_attn_up, w_rnn_up, w_out, ln1_g, ln1_b, peer_w_query, peer_keys_1, peer_keys_2, peer_u, peer_v, ln2_g, ln2_b):
    depth = w_in.shape[0]
    alpha = (2 * depth) ** 0.25
    layer_weights = (w_in, b_forget, conv_w, conv_b, w_rg_a, b_rg_a, w_rg_x, b_rg_x, lru_lambda, w_attn_up,
                     w_rnn_up, w_out, ln1_g, ln1_b, peer_w_query, peer_keys_1, peer_keys_2, peer_u, peer_v,
                     ln2_g, ln2_b)
    hp, hs = x_prompt, x_sample
    prompt_state, sample_state = [], []
    for l in range(depth):
        p = _prep_weights(*(w[l] for w in layer_weights))
        zero_hist = jnp.zeros((hp.shape[0], CONV_WIDTH - 1, D_RNN), F32)
        zero_h = jnp.zeros((hp.shape[0], D_RNN), F32)
        hp, *st_p = _trunk_layer(hp, None, None, None, zero_hist, zero_h, p, alpha)
        hs, *st_s = _trunk_layer(hs, cache_k[l], cache_v[l], cache_logf[l], state_conv[l], state_rnn[l], p,
                                 alpha)
        prompt_state.append(st_p)
        sample_state.append(st_s)
    stack = lambda states, i: jnp.stack([s[i] for s in states])
    return (hp, hs) + tuple(stack(prompt_state, i) for i in range(5)) + tuple(
        stack(sample_state, i) for i in range(5))
```
